```python
import jax, jax.numpy as jnp
from jax import lax
import numpy as np

D_MODEL = 1024
BATCH = 8
SEQ = 2048
DEPTH = 1
DEC_BATCH = 128
DEC_SEQ = 4
PAST_LEN = 16384
PAGE_SIZE = 128

MIX_WIDTH = D_MODEL
GLA_WIDTH = MIX_WIDTH // 2
GMLP_WIDTH = MIX_WIDTH - GLA_WIDTH
GLA_HEADS = 4
GLA_DV = GLA_WIDTH // GLA_HEADS
GLA_KEY = GLA_WIDTH // 2
GLA_DK = GLA_KEY // GLA_HEADS
GLA_GATE_RANK = 16
GLA_TAU = 16.0
GLA_CHUNK = 64
GMLP_HEADS = 4
GMLP_DC = GMLP_WIDTH // GMLP_HEADS
GMLP_CHUNK = 128
D_FF = 2816
CONV_W = 3
PLE_DIM = 256
EPS = 1e-6
IN_SIZES = (GLA_KEY, GLA_KEY, GLA_WIDTH, GLA_WIDTH, GLA_GATE_RANK, GMLP_WIDTH, GMLP_WIDTH)
IN_COLS = sum(IN_SIZES)
SPLIT_POINTS = tuple(int(s) for s in np.cumsum(IN_SIZES)[:-1])

kernel_name = 'hymba_gla_sgu_convffn_step'


def _rmsnorm(x, g):
    xf = x.astype(jnp.float32)
    y = xf * lax.rsqrt(jnp.mean(xf * xf, axis=-1, keepdims=True) + EPS)
    return (y * g.astype(jnp.float32)).astype(x.dtype)


def _gla_chunk(S, q, k, v, g):
    b = jnp.cumsum(g, axis=2)
    b_last = b[:, :, -1, :]
    o_inter = jnp.einsum('nhld,nhde->nhle', q * jnp.exp(b), S)
    c = q.shape[2]
    causal = jnp.tril(jnp.ones((c, c), dtype=bool))
    diff = b[:, :, :, None, :] - b[:, :, None, :, :]
    decay = jnp.exp(jnp.where(causal[:, :, None], diff, -jnp.inf))
    scores = jnp.einsum('nhid,nhjd,nhijd->nhij', q, k, decay)
    o_intra = jnp.einsum('nhij,nhje->nhie', scores, v)
    k_dec = k * jnp.exp(b_last[:, :, None, :] - b)
    S_new = jnp.exp(b_last)[..., None] * S + jnp.einsum('nhld,nhle->nhde', k_dec, v)
    return S_new, o_inter + o_intra


def _to_chunks(t, c):
    n, h, l, d = t.shape
    return jnp.moveaxis(t.reshape(n, h, l // c, c, d), 2, 0)


def _gla(q, k, v, g, S0):
    q, k, v, g = (jnp.transpose(t, (0, 2, 1, 3)) for t in (q, k, v, g))
    n, h, l, dv = v.shape
    c = min(GLA_CHUNK, l)
    xs = (_to_chunks(q, c), _to_chunks(k, c), _to_chunks(v, c), _to_chunks(g, c))
    S, o = lax.scan(lambda s, inp: _gla_chunk(s, *inp), S0, xs)
    o = jnp.moveaxis(o, 0, 2).reshape(n, h, l, dv)
    return jnp.transpose(o, (0, 2, 1, 3)), S


def _sgu(u, v, g_ln, b_ln, w_s, b_s):
    n, l, _ = v.shape
    vh = v.reshape(n, l, GMLP_HEADS, GMLP_DC).astype(jnp.float32)
    mu = jnp.mean(vh, axis=-1, keepdims=True)
    var = jnp.mean(jnp.square(vh - mu), axis=-1, keepdims=True)
    vn = ((vh - mu) * lax.rsqrt(var + EPS)).reshape(n, l, GMLP_WIDTH)
    vn = (vn * g_ln.astype(jnp.float32) + b_ln.astype(jnp.float32)).astype(v.dtype)
    c = min(GMLP_CHUNK, l)
    w = w_s[:, :c, :c] * jnp.tril(jnp.ones((c, c), dtype=w_s.dtype))
    vc = vn.reshape(n, l // c, c, GMLP_HEADS, GMLP_DC)
    mixed = jnp.einsum('hts,bnshd->bnthd', w, vc) + jnp.transpose(b_s[:, :c])[:, :, None]
    return u * mixed.reshape(n, l, GMLP_WIDTH), vn


def _conv_ffn(h, prefix, w_up, w_conv, b_conv, w_down):
    up = h @ w_up
    full = jnp.concatenate([prefix.astype(up.dtype), up], axis=1)
    l = up.shape[1]
    conv = b_conv
    for j in range(CONV_W):
        conv = conv + full[:, j:j + l] * w_conv[j]
    gate, val = jnp.split(conv, 2, axis=-1)
    y = (jax.nn.gelu(gate) * val) @ w_down
    return y, full[:, -(CONV_W - 1):]


def _layer(x, p, S0, conv_prefix, lw):
    n, l, _ = x.shape
    h = _rmsnorm(x, lw['g_mix_pre'])
    proj = h @ lw['w_in']
    q, k, v, r, glr, u, vs = jnp.split(proj, SPLIT_POINTS, axis=-1)
    gate_logit = (glr @ lw['w_gla_gate'] + lw['b_gla_gate']).astype(jnp.float32)
    g = jax.nn.log_sigmoid(gate_logit) / GLA_TAU
    qh = q.astype(jnp.float32).reshape(n, l, GLA_HEADS, GLA_DK) * (GLA_DK ** -0.5)
    kh = k.astype(jnp.float32).reshape(n, l, GLA_HEADS, GLA_DK)
    vh = v.astype(jnp.float32).reshape(n, l, GLA_HEADS, GLA_DV)
    gh = g.reshape(n, l, GLA_HEADS, GLA_DK)
    o, S_new = _gla(qh, kh, vh, gh, S0.astype(jnp.float32))
    o = _rmsnorm(o, lw['g_gla_out']).reshape(n, l, GLA_WIDTH).astype(x.dtype) * jax.nn.silu(r)
    s_out, v_rows = _sgu(jax.nn.gelu(u), jax.nn.gelu(vs), lw['g_sgu_ln'], lw['b_sgu_ln'],
                         lw['w_spatial'], lw['b_spatial'])
    mix = jnp.concatenate([o, s_out], axis=-1) @ lw['w_out']
    x = x + _rmsnorm(mix, lw['g_mix_post'])
    f, conv_new = _conv_ffn(_rmsnorm(x, lw['g_ffn_pre']), conv_prefix, lw['w_up'], lw['w_conv'],
                            lw['b_conv'], lw['w_down'])
    x = x + _rmsnorm(f, lw['g_ffn_post'])
    gate = jax.nn.sigmoid(_rmsnorm(x, lw['g_ple_in']) @ lw['w_ple_gate'])
    x = x + _rmsnorm((p @ lw['w_ple']) * gate, lw['g_ple_post'])
    return x, S_new, conv_new, v_rows


def setup_inputs(seed: int = 0) -> dict:
    key = jax.random.key(seed)
    ks = jax.random.split(key, 32)
    f32 = jnp.float32
    nrm = lambda k, shape, scale: jax.random.normal(k, shape, f32) * scale
    gain = lambda k, dim: 1.0 + 0.02 * jax.random.normal(k, (DEPTH, dim), f32)
    return {
        'x_prompt': nrm(ks[0], (BATCH, SEQ, D_MODEL), 1.0),
        'x_sample': nrm(ks[1], (DEC_BATCH, DEC_SEQ, D_MODEL), 1.0),
        'state_gla': nrm(ks[2], (DEPTH, DEC_BATCH, GLA_HEADS, GLA_DK, GLA_DV), 1.0),
        'state_ffn_conv': nrm(ks[3], (DEPTH, DEC_BATCH, CONV_W - 1, 2 * D_FF), 1.0),
        'p_prompt': nrm(ks[4], (DEPTH, BATCH, SEQ, PLE_DIM), 1.0),
        'p_sample': nrm(ks[5], (DEPTH, DEC_BATCH, DEC_SEQ, PLE_DIM), 1.0),
        'g_mix_pre': gain(ks[6], D_MODEL),
        'w_in': nrm(ks[7], (DEPTH, D_MODEL, IN_COLS), D_MODEL ** -0.5),
        'w_gla_gate': nrm(ks[8], (DEPTH, GLA_GATE_RANK, GLA_KEY), GLA_GATE_RANK ** -0.5),
        'b_gla_gate': nrm(ks[9], (DEPTH, GLA_KEY), 0.1),
        'g_gla_out': gain(ks[10], GLA_DV),
        'g_sgu_ln': gain(ks[11], GMLP_WIDTH),
        'b_sgu_ln': nrm(ks[12], (DEPTH, GMLP_WIDTH), 0.02),
        'w_spatial': nrm(ks[13], (DEPTH, GMLP_HEADS, GMLP_CHUNK, GMLP_CHUNK), GMLP_CHUNK ** -0.5),
        'b_spatial': 1.0 + nrm(ks[14], (DEPTH, GMLP_HEADS, GMLP_CHUNK), 0.02),
        'w_out': nrm(ks[15], (DEPTH, MIX_WIDTH, D_MODEL), MIX_WIDTH ** -0.5),
        'g_mix_post': gain(ks[16], D_MODEL),
        'g_ffn_pre': gain(ks[17], D_MODEL),
        'w_up': nrm(ks[18], (DEPTH, D_MODEL, 2 * D_FF), D_MODEL ** -0.5),
        'w_conv': nrm(ks[19], (DEPTH, CONV_W, 2 * D_FF), CONV_W ** -0.5),
        'b_conv': nrm(ks[20], (DEPTH, 2 * D_FF), 0.02),
        'w_down': nrm(ks[21], (DEPTH, D_FF, D_MODEL), D_FF ** -0.5),
        'g_ffn_post': gain(ks[22], D_MODEL),
        'g_ple_in': gain(ks[23], D_MODEL),
        'w_ple_gate': nrm(ks[24], (DEPTH, D_MODEL, D_MODEL), D_MODEL ** -0.5),
        'w_ple': nrm(ks[25], (DEPTH, PLE_DIM, D_MODEL), PLE_DIM ** -0.5),
        'g_ple_post': gain(ks[26], D_MODEL),
    }


def reference(x_prompt, x_sample, state_gla, state_ffn_conv, p_prompt, p_sample,
              g_mix_pre, w_in, w_gla_gate, b_gla_gate, g_gla_out, g_sgu_ln, b_sgu_ln,
              w_spatial, b_spatial, w_out, g_mix_post, g_ffn_pre, w_up, w_conv, b_conv,
              w_down, g_ffn_post, g_ple_in, w_ple_gate, w_ple, g_ple_post):
    yp, ys = x_prompt, x_sample
    nb = x_prompt.shape[0]
    gla_p, gla_s, conv_p, conv_s, vrows_s = [], [], [], [], []
    for i in range(DEPTH):
        lw = {
            'g_mix_pre': g_mix_pre[i], 'w_in': w_in[i], 'w_gla_gate': w_gla_gate[i],
            'b_gla_gate': b_gla_gate[i], 'g_gla_out': g_gla_out[i], 'g_sgu_ln': g_sgu_ln[i],
            'b_sgu_ln': b_sgu_ln[i], 'w_spatial': w_spatial[i], 'b_spatial': b_spatial[i],
            'w_out': w_out[i], 'g_mix_post': g_mix_post[i], 'g_ffn_pre': g_ffn_pre[i],
            'w_up': w_up[i], 'w_conv': w_conv[i], 'b_conv': b_conv[i], 'w_down': w_down[i],
            'g_ffn_post': g_ffn_post[i], 'g_ple_in': g_ple_in[i], 'w_ple_gate': w_ple_gate[i],
            'w_ple': w_ple[i], 'g_ple_post': g_ple_post[i],
        }
        S0p = jnp.zeros((nb, GLA_HEADS, GLA_DK, GLA_DV), jnp.float32)
        c0p = jnp.zeros((nb, CONV_W - 1, 2 * D_FF), yp.dtype)
        yp, Sp, cp, _ = _layer(yp, p_prompt[i], S0p, c0p, lw)
        ys, Ss, cs, vr = _layer(ys, p_sample[i], state_gla[i], state_ffn_conv[i], lw)
        gla_p.append(Sp)
        gla_s.append(Ss)
        conv_p.append(cp)
        conv_s.append(cs)
        vrows_s.append(vr)
    new_state_gla_prompt = jnp.stack(gla_p)
    new_state_gla_sample = jnp.stack(gla_s)
    new_state_ffn_conv_prompt = jnp.stack(conv_p)
    new_state_ffn_conv_sample = jnp.stack(conv_s)
    new_state_sgu_v_sample = jnp.stack(vrows_s)
    return (yp, ys, new_state_gla_prompt, new_state_gla_sample, new_state_ffn_conv_prompt,
            new_state_ffn_conv_sample, new_state_sgu_v_sample)
```

```python
import functools

import numpy as np
import jax
import jax.numpy as jnp
from jax import lax
from jax.experimental import pallas as pl
from jax.experimental.pallas import tpu as pltpu

D_MODEL = 1024
GLA_HEADS = 4
GLA_DK = 64
GLA_DV = 128
GLA_KEY = GLA_HEADS * GLA_DK
GLA_WIDTH = GLA_HEADS * GLA_DV
GMLP_HEADS = 4
GMLP_DC = 128
GMLP_WIDTH = GMLP_HEADS * GMLP_DC
GLA_GATE_RANK = 16
GLA_TAU = 16.0
GLA_CHUNK = 64
GMLP_CHUNK = 128
D_FF = 2816
CONV_W = 3
PLE_DIM = 256
EPS = 1e-6

LANES = 128
SUBLANES = 8
HEAD_PAIRS = GLA_HEADS // 2

COL_QK = 0
COL_V = COL_QK + 2 * GLA_KEY
COL_R = COL_V + GLA_WIDTH
COL_U = COL_R + GLA_WIDTH
COL_VS = COL_U + GMLP_WIDTH
COL_GLR = COL_VS + GMLP_WIDTH
IN_COLS_PADDED = COL_GLR + LANES

FF_BLOCK = 256
N_FF_BLOCKS = D_FF // FF_BLOCK

PROMPT_TILE = 256
SAMPLE_TILE = 128

VMEM_LIMIT_BYTES = 56 * 1024 * 1024

BF16 = jnp.bfloat16
F32 = jnp.float32


def _dot(a, b):
    return jnp.dot(a, b, preferred_element_type=F32)


def _dot_nt(a, b):
    return lax.dot_general(a, b, (((1,), (1,)), ((), ())), preferred_element_type=F32)


def _dot_tn(a, b):
    return lax.dot_general(a, b, (((0,), (0,)), ((), ())), preferred_element_type=F32)


def _rms(x, g):
    return x * lax.rsqrt(jnp.mean(x * x, axis=-1, keepdims=True) + EPS) * g


def _gelu(x):
    c = np.sqrt(2.0 / np.pi).astype(np.float32)
    return x * (0.5 * (1.0 + jnp.tanh(c * (x + 0.044715 * (x * x * x)))))


def _sigmoid(x):
    return 1.0 / (1.0 + jnp.exp(-x))


def _log_sigmoid(x):
    return jnp.minimum(x, 0.0) - jnp.log1p(jnp.exp(-jnp.abs(x)))


def _split_bf16(x):
    hi = x.astype(BF16)
    lo = (x - hi.astype(F32)).astype(BF16)
    return hi, lo


def _mixer_kernel(*refs, rows, sample):
    if sample:
        (x_ref, sin_ref, g_pre_ref, w_in_ref, w_gate_ref, b_gate_ref, g_gla_ref, g_ln_ref, b_ln_ref,
         w_sp_ref, b_sp_ref, w_out_ref, g_post_ref, cum_ref, tot_ref, causal_ref,
         y_ref, sout_ref, vn_ref,
         q_s, k_s, g_s, v_s, o_s) = refs
        st_s = None
    else:
        (x_ref, g_pre_ref, w_in_ref, w_gate_ref, b_gate_ref, g_gla_ref, g_ln_ref, b_ln_ref,
         w_sp_ref, b_sp_ref, w_out_ref, g_post_ref, cum_ref, tot_ref, causal_ref,
         y_ref, sout_ref,
         q_s, k_s, g_s, v_s, o_s, st_s) = refs
        t = pl.program_id(1)

        @pl.when(t == 0)
        def _():
            st_s[...] = jnp.zeros_like(st_s)

    x = x_ref[...]
    h = _rms(x, g_pre_ref[...]).astype(BF16)

    qk = _dot(h, w_in_ref[:, COL_QK:COL_V])
    q_s[...] = qk[:, :GLA_KEY] * (GLA_DK ** -0.5)
    k_s[...] = qk[:, GLA_KEY:]
    v_s[...] = _dot(h, w_in_ref[:, COL_V:COL_R])
    glr = _dot(h, w_in_ref[:, COL_GLR:IN_COLS_PADDED])
    logit = _dot(glr.astype(BF16), w_gate_ref[...]) + b_gate_ref[...]
    g_s[...] = _log_sigmoid(logit) * (1.0 / GLA_TAU)

    lane = lax.broadcasted_iota(jnp.int32, (GLA_CHUNK, LANES), 1)
    head_lane_masks = (lane < GLA_DK, lane >= GLA_DK)

    def gla_chunk(c, carry):
        r0 = pl.multiple_of(c * GLA_CHUNK, GLA_CHUNK)
        rws = pl.ds(r0, GLA_CHUNK)
        q = q_s[rws, :]
        k = k_s[rws, :]
        g_hi, g_lo = _split_bf16(g_s[rws, :])
        cum = cum_ref[...]
        tot = tot_ref[...]
        b = _dot(cum, g_hi) + _dot(cum, g_lo)
        bl = _dot(tot, g_hi) + _dot(tot, g_lo)
        qb = q * jnp.exp(b)
        kn = k * jnp.exp(-b)
        kd = k * jnp.exp(bl - b)
        causal = causal_ref[...]
        for p in range(HEAD_PAIRS):
            ls = slice(p * LANES, (p + 1) * LANES)
            qbp = qb[:, ls]
            knp = kn[:, ls].astype(BF16)
            if not sample:
                st = st_s[p]
                o_inter = _dot_nt(qbp.astype(BF16), st.astype(BF16))
            for hh in range(2):
                hd = 2 * p + hh
                vs_ = slice(hd * GLA_DV, (hd + 1) * GLA_DV)
                qm = jnp.where(head_lane_masks[hh], qbp, 0.0).astype(BF16)
                s = _dot_nt(qm, knp) * causal
                o = _dot(s.astype(BF16), v_s[rws, vs_].astype(BF16))
                if not sample:
                    o = o + o_inter[:, hh * GLA_DV:(hh + 1) * GLA_DV]
                o_s[rws, vs_] = o
            if not sample:
                vp = v_s[rws, 2 * p * GLA_DV:(2 * p + 2) * GLA_DV]
                upd = _dot_tn(vp.astype(BF16), kd[:, ls].astype(BF16))
                ri = lax.broadcasted_iota(jnp.int32, upd.shape, 0)
                ci = lax.broadcasted_iota(jnp.int32, upd.shape, 1)
                same_head = (ri < GLA_DV) == (ci < GLA_DK)
                st_s[p] = st * jnp.exp(bl[0:1, ls]) + jnp.where(same_head, upd, 0.0)
            else:
                seq_len = 4
                seqs_per_grp = SUBLANES // seq_len
                kdp_t = kd[:, ls].T
                dec_t = jnp.exp(bl[:, ls]).T
                sub = lax.broadcasted_iota(jnp.int32, (SUBLANES, LANES), 0)
                crow = lax.broadcasted_iota(jnp.int32, (GLA_CHUNK, LANES), 0)
                for grp in range(GLA_CHUNK // SUBLANES):
                    g0 = grp * SUBLANES
                    grws = pl.ds(r0 + g0, SUBLANES)
                    qg = qbp[g0:g0 + SUBLANES, :]
                    o_acc = [jnp.zeros((SUBLANES, GLA_DV), F32) for _ in range(2)]
                    for j in range(seqs_per_grp):
                        sidx = c * (GLA_CHUNK // seq_len) + grp * seqs_per_grp + j
                        s_pair = sin_ref[sidx, p]
                        s_bf = s_pair.astype(BF16)
                        in_seq = (sub >= j * seq_len) & (sub < (j + 1) * seq_len)
                        in_seq_c = (crow >= g0 + j * seq_len) & (crow < g0 + (j + 1) * seq_len)
                        new_rows = []
                        for hh in range(2):
                            hd = 2 * p + hh
                            vs_ = slice(hd * GLA_DV, (hd + 1) * GLA_DV)
                            qsel = jnp.where(in_seq & head_lane_masks[hh][:SUBLANES], qg, 0.0).astype(BF16)
                            o_acc[hh] = o_acc[hh] + _dot(qsel, s_bf)
                            vsel = jnp.where(in_seq_c, v_s[rws, vs_], 0.0).astype(BF16)
                            hrows = slice(hh * GLA_DK, (hh + 1) * GLA_DK)
                            new_rows.append(_dot(kdp_t[hrows, :].astype(BF16), vsel))
                        col = g0 + j * seq_len
                        dcol = dec_t[:, col:col + 1]
                        sout_ref[sidx, p] = s_pair * dcol + jnp.concatenate(new_rows, axis=0)
                    for hh in range(2):
                        hd = 2 * p + hh
                        vs_ = slice(hd * GLA_DV, (hd + 1) * GLA_DV)
                        o_s[grws, vs_] = o_s[grws, vs_] + o_acc[hh]
        return carry

    lax.fori_loop(0, rows // GLA_CHUNK, gla_chunk, 0)

    if not sample:
        @pl.when(t == pl.num_programs(1) - 1)
        def _():
            for p in range(HEAD_PAIRS):
                st = st_s[p]
                sout_ref[2 * p] = st[:GLA_DV, :].T[:GLA_DK, :]
                sout_ref[2 * p + 1] = st[GLA_DV:, :].T[GLA_DK:, :]

    r = _dot(h, w_in_ref[:, COL_R:COL_U])
    o = o_s[...]
    og = jnp.concatenate(
        [_rms(o[:, i * GLA_DV:(i + 1) * GLA_DV], 1.0) for i in range(GLA_HEADS)], axis=-1)
    og = og * g_gla_ref[...] * (r * _sigmoid(r))
    mix = _dot(og.astype(BF16), w_out_ref[:GLA_WIDTH, :])

    u = _gelu(_dot(h, w_in_ref[:, COL_U:COL_VS]))
    vs = _gelu(_dot(h, w_in_ref[:, COL_VS:COL_GLR]))
    vn_parts = []
    for i in range(GMLP_HEADS):
        vh = vs[:, i * GMLP_DC:(i + 1) * GMLP_DC]
        mu = jnp.mean(vh, axis=-1, keepdims=True)
        d = vh - mu
        var = jnp.mean(d * d, axis=-1, keepdims=True)
        vn_parts.append(d * lax.rsqrt(var + EPS))
    vn = jnp.concatenate(vn_parts, axis=-1) * g_ln_ref[...] + b_ln_ref[...]
    if sample:
        vn_ref[...] = vn
    vn_bf = vn.astype(BF16)
    mixed_parts = []
    for cch in range(rows // GMLP_CHUNK):
        crows = slice(cch * GMLP_CHUNK, (cch + 1) * GMLP_CHUNK)
        mixed_parts.append(jnp.concatenate(
            [_dot(w_sp_ref[i], vn_bf[crows, i * GMLP_DC:(i + 1) * GMLP_DC]) for i in range(GMLP_HEADS)],
            axis=-1) + b_sp_ref[...])
    so = u * jnp.concatenate(mixed_parts, axis=0)
    mix = mix + _dot(so.astype(BF16), w_out_ref[GLA_WIDTH:, :])

    y_ref[...] = x + _rms(mix, g_post_ref[...])


def _ffn_tail(x1, f, p, g_ffn_post, g_ple_in, w_pg_ref, w_ple_ref, g_ple_post):
    x2 = x1 + _rms(f, g_ffn_post)
    hp = _rms(x2, g_ple_in).astype(BF16)
    gate = _sigmoid(_dot(hp, w_pg_ref[...]))
    pe = _dot(p.astype(BF16), w_ple_ref[...])
    return x2 + _rms(pe * gate, g_ple_post)


def _ffn_prompt_kernel(x_ref, p_ref, g_pre_ref, w_up_ref, w_conv_ref, b_conv_ref, w_down_ref,
                       g_post_ref, g_ple_in_ref, w_pg_ref, w_ple_ref, g_ple_post_ref,
                       y_ref, cs_ref,
                       carry_s, up_s, *, rows):
    t = pl.program_id(1)

    @pl.when(t == 0)
    def _():
        carry_s[...] = jnp.zeros_like(carry_s)

    x1 = x_ref[...]
    h = _rms(x1, g_pre_ref[...]).astype(BF16)
    hdr = SUBLANES

    def conv_half(col0):
        cols = slice(col0, col0 + FF_BLOCK)
        up_s[0:hdr, :] = carry_s[:, cols]
        up_s[hdr:hdr + rows, :] = _dot(h, w_up_ref[:, cols])
        carry_s[:, cols] = up_s[rows:rows + hdr, :]
        w = w_conv_ref[:, cols]
        return (b_conv_ref[:, cols]
                + up_s[hdr - 2:hdr - 2 + rows, :] * w[0:1, :]
                + up_s[hdr - 1:hdr - 1 + rows, :] * w[1:2, :]
                + up_s[hdr:hdr + rows, :] * w[2:3, :])

    f = jnp.zeros((rows, D_MODEL), F32)
    for j in range(N_FF_BLOCKS):
        gate = conv_half(j * FF_BLOCK)
        val = conv_half(D_FF + j * FF_BLOCK)
        act = (_gelu(gate) * val).astype(BF16)
        f = f + _dot(act, w_down_ref[j * FF_BLOCK:(j + 1) * FF_BLOCK, :])

    @pl.when(t == pl.num_programs(1) - 1)
    def _():
        cs_ref[...] = carry_s[SUBLANES - (CONV_W - 1):, :]

    y_ref[...] = _ffn_tail(x1, f, p_ref[...], g_post_ref[...], g_ple_in_ref[...], w_pg_ref,
                           w_ple_ref, g_ple_post_ref[...])


def _ffn_sample_kernel(x_ref, p_ref, cin_ref, g_pre_ref, w_up_ref, w_conv_ref, b_conv_ref, w_down_ref,
                       g_post_ref, g_ple_in_ref, w_pg_ref, w_ple_ref, g_ple_post_ref,
                       y_ref, cs_ref, *, nseq, seq_len):
    x1 = jnp.concatenate([x_ref[:, j * D_MODEL:(j + 1) * D_MODEL] for j in range(seq_len)], axis=0)
    p = jnp.concatenate([p_ref[:, j * PLE_DIM:(j + 1) * PLE_DIM] for j in range(seq_len)], axis=0)
    h = _rms(x1, g_pre_ref[...]).astype(BF16)

    def conv_half(col0):
        cols = slice(col0, col0 + FF_BLOCK)
        up = _dot(h, w_up_ref[:, cols])
        blocks = [cin_ref[:, col0:col0 + FF_BLOCK],
                  cin_ref[:, 2 * D_FF + col0:2 * D_FF + col0 + FF_BLOCK]]
        blocks += [up[j * nseq:(j + 1) * nseq, :] for j in range(seq_len)]
        w = w_conv_ref[:, cols]
        b = b_conv_ref[:, cols]
        conv = jnp.concatenate(
            [b + blocks[j] * w[0:1, :] + blocks[j + 1] * w[1:2, :] + blocks[j + 2] * w[2:3, :]
             for j in range(seq_len)], axis=0)
        cs_ref[:, col0:col0 + FF_BLOCK] = blocks[seq_len]
        cs_ref[:, 2 * D_FF + col0:2 * D_FF + col0 + FF_BLOCK] = blocks[seq_len + 1]
        return conv

    f = jnp.zeros((nseq * seq_len, D_MODEL), F32)
    for j in range(N_FF_BLOCKS):
        gate = conv_half(j * FF_BLOCK)
        val = conv_half(D_FF + j * FF_BLOCK)
        act = (_gelu(gate) * val).astype(BF16)
        f = f + _dot(act, w_down_ref[j * FF_BLOCK:(j + 1) * FF_BLOCK, :])

    y = _ffn_tail(x1, f, p, g_post_ref[...], g_ple_in_ref[...], w_pg_ref, w_ple_ref,
                  g_ple_post_ref[...])
    for j in range(seq_len):
        y_ref[:, j * D_MODEL:(j + 1) * D_MODEL] = y[j * nseq:(j + 1) * nseq, :]


def _full(shape):
    n = len(shape)
    return pl.BlockSpec(shape, lambda *_: (0,) * n)


def _gla_masks(seq_len):
    i = np.arange(GLA_CHUNK)
    same = (i[:, None] // seq_len) == (i[None, :] // seq_len)
    causal = same & (i[None, :] <= i[:, None])
    return (jnp.asarray(causal, BF16), jnp.asarray(same, BF16), jnp.asarray(causal, F32))


def kernel(x_prompt, x_sample, state_gla, state_ffn_conv, p_prompt, p_sample, g_mix_pre, w_in, w_gla_gate, b_gla_gate, g_gla_out, g_sgu_ln, b_sgu_ln, w_spatial, b_spatial, w_out, g_mix_post, g_ffn_pre, w_up, w_conv, b_conv, w_down, g_ffn_post, g_ple_in, w_ple_gate, w_ple, g_ple_post):
    depth = x_prompt.ndim - 2
    assert w_in.shape[0] == 1 and depth == 1
    nb, seq, _ = x_prompt.shape
    ns, dseq, _ = x_sample.shape
    assert seq % PROMPT_TILE == 0 and PROMPT_TILE % GMLP_CHUNK == 0
    assert dseq == 4 and (ns * dseq) % SAMPLE_TILE == 0 and SAMPLE_TILE % GMLP_CHUNK == 0

    wi = w_in[0]
    sp = np.cumsum((GLA_KEY, GLA_KEY, GLA_WIDTH, GLA_WIDTH, GLA_GATE_RANK, GMLP_WIDTH, GMLP_WIDTH))
    w_in_p = jnp.concatenate(
        [wi[:, :sp[3]], wi[:, sp[4]:], wi[:, sp[3]:sp[4]],
         jnp.zeros((D_MODEL, LANES - GLA_GATE_RANK), wi.dtype)], axis=1).astype(BF16)
    w_gate_p = jnp.concatenate(
        [w_gla_gate[0], jnp.zeros((LANES - GLA_GATE_RANK, GLA_KEY), w_gla_gate.dtype)], axis=0).astype(BF16)
    row = lambda a: a.reshape(1, -1)
    g_gla_t = jnp.tile(g_gla_out[0], GLA_HEADS).reshape(1, -1)
    w_out_b = w_out[0].astype(BF16)
    w_up_b = w_up[0].astype(BF16)
    w_down_b = w_down[0].astype(BF16)
    w_pg_b = w_ple_gate[0].astype(BF16)
    w_ple_b = w_ple[0].astype(BF16)

    def spatial(c):
        w = w_spatial[0][:, :c, :c] * jnp.tril(jnp.ones((c, c), w_spatial.dtype))
        reps = GMLP_CHUNK // c
        wbd = jnp.einsum('ab,hts->hatbs', jnp.eye(reps, dtype=w.dtype), w).reshape(
            GMLP_HEADS, GMLP_CHUNK, GMLP_CHUNK)
        bias = jnp.tile(jnp.transpose(b_spatial[0][:, :c]), (reps, 1))
        return wbd.astype(BF16), jnp.repeat(bias, GMLP_DC, axis=1)

    mixer_weights = lambda wsp, bsp, masks: (
        row(g_mix_pre), w_in_p, w_gate_p, row(b_gla_gate), g_gla_t, row(g_sgu_ln), row(b_sgu_ln),
        wsp, bsp, w_out_b, row(g_mix_post)) + masks
    mixer_weight_specs = [
        _full((1, D_MODEL)), _full((D_MODEL, IN_COLS_PADDED)), _full((LANES, GLA_KEY)), _full((1, GLA_KEY)),
        _full((1, GLA_WIDTH)), _full((1, GMLP_WIDTH)), _full((1, GMLP_WIDTH)),
        _full((GMLP_HEADS, GMLP_CHUNK, GMLP_CHUNK)), _full((GMLP_CHUNK, GMLP_WIDTH)),
        _full((D_MODEL, D_MODEL)), _full((1, D_MODEL)),
        _full((GLA_CHUNK, GLA_CHUNK)), _full((GLA_CHUNK, GLA_CHUNK)), _full((GLA_CHUNK, GLA_CHUNK))]

    def mixer_scratch(rows):
        return [pltpu.VMEM((rows, GLA_KEY), F32), pltpu.VMEM((rows, GLA_KEY), F32),
                pltpu.VMEM((rows, GLA_KEY), F32), pltpu.VMEM((rows, GLA_WIDTH), F32),
                pltpu.VMEM((rows, GLA_WIDTH), F32)]

    tl = PROMPT_TILE
    wsp_p, bsp_p = spatial(GMLP_CHUNK)
    x1_p, gla_p = pl.pallas_call(
        functools.partial(_mixer_kernel, rows=tl, sample=False),
        grid=(nb, seq // tl),
        in_specs=[pl.BlockSpec((None, tl, D_MODEL), lambda b, t: (b, t, 0))] + mixer_weight_specs,
        out_specs=[pl.BlockSpec((None, tl, D_MODEL), lambda b, t: (b, t, 0)),
                   pl.BlockSpec((None, GLA_HEADS, GLA_DK, GLA_DV), lambda b, t: (b, 0, 0, 0))],
        out_shape=[jax.ShapeDtypeStruct((nb, seq, D_MODEL), F32),
                   jax.ShapeDtypeStruct((nb, GLA_HEADS, GLA_DK, GLA_DV), F32)],
        scratch_shapes=mixer_scratch(tl) + [pltpu.VMEM((HEAD_PAIRS, 2 * GLA_DV, 2 * GLA_DK), F32)],
        compiler_params=pltpu.CompilerParams(
            dimension_semantics=("arbitrary", "arbitrary"), vmem_limit_bytes=VMEM_LIMIT_BYTES),
        name="mixer_prompt",
    )(x_prompt, *mixer_weights(wsp_p, bsp_p, _gla_masks(GLA_CHUNK)))

    ts = SAMPLE_TILE
    seqs_per_tile = ts // dseq
    wsp_s, bsp_s = spatial(dseq)
    x1_s, gla_s, vrows_s = pl.pallas_call(
        functools.partial(_mixer_kernel, rows=ts, sample=True),
        grid=(ns * dseq // ts,),
        in_specs=[pl.BlockSpec((ts, D_MODEL), lambda i: (i, 0)),
                  pl.BlockSpec((seqs_per_tile, HEAD_PAIRS, 2 * GLA_DK, GLA_DV), lambda i: (i, 0, 0, 0))]
        + mixer_weight_specs,
        out_specs=[pl.BlockSpec((ts, D_MODEL), lambda i: (i, 0)),
                   pl.BlockSpec((seqs_per_tile, HEAD_PAIRS, 2 * GLA_DK, GLA_DV), lambda i: (i, 0, 0, 0)),
                   pl.BlockSpec((ts, GMLP_WIDTH), lambda i: (i, 0))],
        out_shape=[jax.ShapeDtypeStruct((ns * dseq, D_MODEL), F32),
                   jax.ShapeDtypeStruct((ns, HEAD_PAIRS, 2 * GLA_DK, GLA_DV), F32),
                   jax.ShapeDtypeStruct((ns * dseq, GMLP_WIDTH), F32)],
        scratch_shapes=mixer_scratch(ts),
        compiler_params=pltpu.CompilerParams(
            dimension_semantics=("arbitrary",), vmem_limit_bytes=VMEM_LIMIT_BYTES),
        name="mixer_sample",
    )(x_sample.reshape(ns * dseq, D_MODEL),
      state_gla[0].reshape(ns, HEAD_PAIRS, 2 * GLA_DK, GLA_DV),
      *mixer_weights(wsp_s, bsp_s, _gla_masks(dseq)))

    ffn_weights = (row(g_ffn_pre), w_up_b, w_conv[0], row(b_conv), w_down_b, row(g_ffn_post),
                   row(g_ple_in), w_pg_b, w_ple_b, row(g_ple_post))
    ffn_weight_specs = [
        _full((1, D_MODEL)), _full((D_MODEL, 2 * D_FF)), _full((CONV_W, 2 * D_FF)), _full((1, 2 * D_FF)),
        _full((D_FF, D_MODEL)), _full((1, D_MODEL)), _full((1, D_MODEL)), _full((D_MODEL, D_MODEL)),
        _full((PLE_DIM, D_MODEL)), _full((1, D_MODEL))]

    y_p, conv_p = pl.pallas_call(
        functools.partial(_ffn_prompt_kernel, rows=tl),
        grid=(nb, seq // tl),
        in_specs=[pl.BlockSpec((None, tl, D_MODEL), lambda b, t: (b, t, 0)),
                  pl.BlockSpec((None, tl, PLE_DIM), lambda b, t: (b, t, 0))] + ffn_weight_specs,
        out_specs=[pl.BlockSpec((None, tl, D_MODEL), lambda b, t: (b, t, 0)),
                   pl.BlockSpec((None, CONV_W - 1, 2 * D_FF), lambda b, t: (b, 0, 0))],
        out_shape=[jax.ShapeDtypeStruct((nb, seq, D_MODEL), F32),
                   jax.ShapeDtypeStruct((nb, CONV_W - 1, 2 * D_FF), F32)],
        scratch_shapes=[pltpu.VMEM((SUBLANES, 2 * D_FF), F32),
                        pltpu.VMEM((tl + SUBLANES, FF_BLOCK), F32)],
        compiler_params=pltpu.CompilerParams(
            dimension_semantics=("arbitrary", "arbitrary"), vmem_limit_bytes=VMEM_LIMIT_BYTES),
        name="ffn_prompt",
    )(x1_p, p_prompt[0], *ffn_weights)

    y_s, conv_s = pl.pallas_call(
        functools.partial(_ffn_sample_kernel, nseq=ns, seq_len=dseq),
        grid=(1,),
        in_specs=[_full((ns, dseq * D_MODEL)), _full((ns, dseq * PLE_DIM)),
                  _full((ns, (CONV_W - 1) * 2 * D_FF))] + ffn_weight_specs,
        out_specs=[_full((ns, dseq * D_MODEL)), _full((ns, (CONV_W - 1) * 2 * D_FF))],
        out_shape=[jax.ShapeDtypeStruct((ns, dseq * D_MODEL), F32),
                   jax.ShapeDtypeStruct((ns, (CONV_W - 1) * 2 * D_FF), F32)],
        compiler_params=pltpu.CompilerParams(
            dimension_semantics=("arbitrary",), vmem_limit_bytes=VMEM_LIMIT_BYTES),
        name="ffn_sample",
    )(x1_s.reshape(ns, dseq * D_MODEL), p_sample[0].reshape(ns, dseq * PLE_DIM),
      state_ffn_conv[0].reshape(ns, (CONV_W - 1) * 2 * D_FF), *ffn_weights)

    return (y_p,
            y_s.reshape(ns, dseq, D_MODEL),
            gla_p[None],
            gla_s.reshape(1, ns, GLA_HEADS, GLA_DK, GLA_DV),
            conv_p[None],
            conv_s.reshape(1, ns, CONV_W - 1, 2 * D_FF),
            vrows_s.reshape(1, ns, dseq, GMLP_WIDTH))
```

```python
import functools

import numpy as np
import jax
import jax.numpy as jnp
from jax import lax
from jax.experimental import pallas as pl
from jax.experimental.pallas import tpu as pltpu

D_MODEL = 1024
GLA_HEADS = 4
GLA_DK = 64
GLA_DV = 128
GLA_KEY = GLA_HEADS * GLA_DK
GLA_WIDTH = GLA_HEADS * GLA_DV
GMLP_HEADS = 4
GMLP_DC = 128
GMLP_WIDTH = GMLP_HEADS * GMLP_DC
GLA_GATE_RANK = 16
GLA_TAU = 16.0
GLA_CHUNK = 64
GMLP_CHUNK = 128
D_FF = 2816
CONV_W = 3
PLE_DIM = 256
EPS = 1e-6

LANES = 128
SUBLANES = 8
HEAD_PAIRS = GLA_HEADS // 2

COL_QK = 0
COL_V = COL_QK + 2 * GLA_KEY
COL_R = COL_V + GLA_WIDTH
COL_U = COL_R + GLA_WIDTH
COL_VS = COL_U + GMLP_WIDTH
COL_GLR = COL_VS + GMLP_WIDTH
IN_COLS_PADDED = COL_GLR + LANES

FF_BLOCK = 256
N_FF_BLOCKS = D_FF // FF_BLOCK

PROMPT_TILE = 256
SAMPLE_TILE = 128

VMEM_LIMIT_BYTES = 56 * 1024 * 1024

BF16 = jnp.bfloat16
F32 = jnp.float32


def _dot(a, b):
    return jnp.dot(a, b, preferred_element_type=F32)


def _dot_nt(a, b):
    return lax.dot_general(a, b, (((1,), (1,)), ((), ())), preferred_element_type=F32)


def _dot_tn(a, b):
    return lax.dot_general(a, b, (((0,), (0,)), ((), ())), preferred_element_type=F32)


def _rms(x, g):
    return x * lax.rsqrt(jnp.mean(x * x, axis=-1, keepdims=True) + EPS) * g


def _gelu(x):
    c = np.sqrt(2.0 / np.pi).astype(np.float32)
    return x * (0.5 * (1.0 + jnp.tanh(c * (x + 0.044715 * (x * x * x)))))


def _sigmoid(x):
    return 1.0 / (1.0 + jnp.exp(-x))


def _log_sigmoid(x):
    return jnp.minimum(x, 0.0) - jnp.log1p(jnp.exp(-jnp.abs(x)))


def _split_bf16(x):
    hi = x.astype(BF16)
    lo = (x - hi.astype(F32)).astype(BF16)
    return hi, lo


def _mixer_kernel(*refs, rows, sample):
    if sample:
        (x_ref, sin_ref, g_pre_ref, w_in_ref, w_gate_ref, b_gate_ref, g_gla_ref, g_ln_ref, b_ln_ref,
         w_sp_ref, b_sp_ref, w_out_ref, g_post_ref, cum_ref, tot_ref, causal_ref,
         y_ref, sout_ref, vn_ref, o_s) = refs
        st_s = None
    else:
        (x_ref, g_pre_ref, w_in_ref, w_gate_ref, b_gate_ref, g_gla_ref, g_ln_ref, b_ln_ref,
         w_sp_ref, b_sp_ref, w_out_ref, g_post_ref, cum_ref, tot_ref, causal_ref,
         y_ref, sout_ref, o_s, st_s) = refs
        t = pl.program_id(1)

        @pl.when(t == 0)
        def _():
            st_s[...] = jnp.zeros_like(st_s)

    x = x_ref[...]
    h = _rms(x, g_pre_ref[...]).astype(BF16)

    qk = _dot(h, w_in_ref[:, COL_QK:COL_V])
    q = qk[:, :GLA_KEY] * (GLA_DK ** -0.5)
    k = qk[:, GLA_KEY:]
    v = _dot(h, w_in_ref[:, COL_V:COL_R])
    v_bf = v.astype(BF16)
    glr = _dot(h, w_in_ref[:, COL_GLR:IN_COLS_PADDED])
    logit = _dot(glr.astype(BF16), w_gate_ref[...]) + b_gate_ref[...]
    g_hi, g_lo = _split_bf16(_log_sigmoid(logit) * (1.0 / GLA_TAU))

    cum = cum_ref[...]
    tot = tot_ref[...]
    b = _dot(cum, g_hi) + _dot(cum, g_lo)
    bl = _dot(tot, g_hi) + _dot(tot, g_lo)
    qb = q * jnp.exp(b)
    kn = k * jnp.exp(-b)
    kd = k * jnp.exp(bl - b)
    causal = causal_ref[...]

    lane = lax.broadcasted_iota(jnp.int32, (rows, LANES), 1)
    head_lane_masks = (lane < GLA_DK, lane >= GLA_DK)

    for p in range(HEAD_PAIRS):
        ls = slice(p * LANES, (p + 1) * LANES)
        qbp = qb[:, ls]
        knp = kn[:, ls].astype(BF16)
        kdp = kd[:, ls]
        o_intra = []
        for hh in range(2):
            hd = 2 * p + hh
            qm = jnp.where(head_lane_masks[hh], qbp, 0.0).astype(BF16)
            s = _dot_nt(qm, knp) * causal
            o_intra.append(_dot(s.astype(BF16), v_bf[:, hd * GLA_DV:(hd + 1) * GLA_DV]))

        if not sample:
            st = st_s[p]
            ri = lax.broadcasted_iota(jnp.int32, st.shape, 0)
            ci = lax.broadcasted_iota(jnp.int32, st.shape, 1)
            same_head = (ri < GLA_DV) == (ci < GLA_DK)
            qbp_bf = qbp.astype(BF16)
            kdp_bf = kdp.astype(BF16)
            for c in range(rows // GLA_CHUNK):
                rws = slice(c * GLA_CHUNK, (c + 1) * GLA_CHUNK)
                o_inter = _dot_nt(qbp_bf[rws, :], st.astype(BF16))
                for hh in range(2):
                    hd = 2 * p + hh
                    o_s[rws, hd * GLA_DV:(hd + 1) * GLA_DV] = (
                        o_intra[hh][rws, :] + o_inter[:, hh * GLA_DV:(hh + 1) * GLA_DV])
                upd = _dot_tn(v_bf[rws, 2 * p * GLA_DV:(2 * p + 2) * GLA_DV], kdp_bf[rws, :])
                dec = jnp.exp(bl[c * GLA_CHUNK:c * GLA_CHUNK + 1, ls])
                st = st * dec + jnp.where(same_head, upd, 0.0)
            st_s[p] = st
        else:
            seq_len = 4
            seqs_per_grp = SUBLANES // seq_len
            kdp_t = kdp.T.astype(BF16)
            dec_t = jnp.exp(bl[:, ls]).T
            sub = lax.broadcasted_iota(jnp.int32, (SUBLANES, LANES), 0)
            crow = lax.broadcasted_iota(jnp.int32, (rows, LANES), 0)
            for grp in range(rows // SUBLANES):
                g0 = grp * SUBLANES
                qg = qbp[g0:g0 + SUBLANES, :]
                o_acc = [o_intra[hh][g0:g0 + SUBLANES, :] for hh in range(2)]
                for j in range(seqs_per_grp):
                    sidx = grp * seqs_per_grp + j
                    s_pair = sin_ref[sidx, p]
                    s_bf = s_pair.astype(BF16)
                    in_seq = (sub >= j * seq_len) & (sub < (j + 1) * seq_len)
                    in_seq_c = (crow >= g0 + j * seq_len) & (crow < g0 + (j + 1) * seq_len)
                    new_rows = []
                    for hh in range(2):
                        hd = 2 * p + hh
                        qsel = jnp.where(in_seq & head_lane_masks[hh][:SUBLANES], qg, 0.0).astype(BF16)
                        o_acc[hh] = o_acc[hh] + _dot(qsel, s_bf)
                        vsel = jnp.where(in_seq_c, v[:, hd * GLA_DV:(hd + 1) * GLA_DV], 0.0).astype(BF16)
                        new_rows.append(_dot(kdp_t[hh * GLA_DK:(hh + 1) * GLA_DK, :], vsel))
                    col = g0 + j * seq_len
                    sout_ref[sidx, p] = s_pair * dec_t[:, col:col + 1] + jnp.concatenate(new_rows, axis=0)
                for hh in range(2):
                    hd = 2 * p + hh
                    o_s[g0:g0 + SUBLANES, hd * GLA_DV:(hd + 1) * GLA_DV] = o_acc[hh]

    if not sample:
        @pl.when(t == pl.num_programs(1) - 1)
        def _():
            for p in range(HEAD_PAIRS):
                st = st_s[p]
                sout_ref[2 * p] = st[:GLA_DV, :].T[:GLA_DK, :]
                sout_ref[2 * p + 1] = st[GLA_DV:, :].T[GLA_DK:, :]

    r = _dot(h, w_in_ref[:, COL_R:COL_U])
    o = o_s[...]
    og = jnp.concatenate(
        [_rms(o[:, i * GLA_DV:(i + 1) * GLA_DV], 1.0) for i in range(GLA_HEADS)], axis=-1)
    og = og * g_gla_ref[...] * (r * _sigmoid(r))
    mix = _dot(og.astype(BF16), w_out_ref[:GLA_WIDTH, :])

    u = _gelu(_dot(h, w_in_ref[:, COL_U:COL_VS]))
    vs = _gelu(_dot(h, w_in_ref[:, COL_VS:COL_GLR]))
    vn_parts = []
    for i in range(GMLP_HEADS):
        vh = vs[:, i * GMLP_DC:(i + 1) * GMLP_DC]
        mu = jnp.mean(vh, axis=-1, keepdims=True)
        d = vh - mu
        var = jnp.mean(d * d, axis=-1, keepdims=True)
        vn_parts.append(d * lax.rsqrt(var + EPS))
    vn = jnp.concatenate(vn_parts, axis=-1) * g_ln_ref[...] + b_ln_ref[...]
    if sample:
        vn_ref[...] = vn
    vn_bf = vn.astype(BF16)
    mixed_parts = []
    for cch in range(rows // GMLP_CHUNK):
        crows = slice(cch * GMLP_CHUNK, (cch + 1) * GMLP_CHUNK)
        mixed_parts.append(jnp.concatenate(
            [_dot(w_sp_ref[i], vn_bf[crows, i * GMLP_DC:(i + 1) * GMLP_DC]) for i in range(GMLP_HEADS)],
            axis=-1) + b_sp_ref[...])
    so = u * jnp.concatenate(mixed_parts, axis=0)
    mix = mix + _dot(so.astype(BF16), w_out_ref[GLA_WIDTH:, :])

    y_ref[...] = x + _rms(mix, g_post_ref[...])


def _ffn_tail(x1, f, p, g_ffn_post, g_ple_in, w_pg_ref, w_ple_ref, g_ple_post):
    x2 = x1 + _rms(f, g_ffn_post)
    hp = _rms(x2, g_ple_in).astype(BF16)
    gate = _sigmoid(_dot(hp, w_pg_ref[...]))
    pe = _dot(p.astype(BF16), w_ple_ref[...])
    return x2 + _rms(pe * gate, g_ple_post)


def _ffn_prompt_kernel(x_ref, p_ref, g_pre_ref, w_up_ref, w_conv_ref, b_conv_ref, w_down_ref,
                       g_post_ref, g_ple_in_ref, w_pg_ref, w_ple_ref, g_ple_post_ref,
                       y_ref, cs_ref,
                       carry_s, up_s, *, rows):
    t = pl.program_id(1)

    @pl.when(t == 0)
    def _():
        carry_s[...] = jnp.zeros_like(carry_s)

    x1 = x_ref[...]
    h = _rms(x1, g_pre_ref[...]).astype(BF16)
    hdr = SUBLANES

    def half_cols(j, half):
        c0 = half * D_FF + j * FF_BLOCK
        return slice(c0, c0 + FF_BLOCK)

    def project(j):
        for half in range(2):
            cols = half_cols(j, half)
            buf = up_s.at[j % 2, half]
            buf[0:hdr, :] = carry_s[:, cols]
            buf[hdr:hdr + rows, :] = _dot(h, w_up_ref[:, cols])
            carry_s[:, cols] = buf[rows:rows + hdr, :]

    def conv(j, half):
        cols = half_cols(j, half)
        buf = up_s.at[j % 2, half]
        w = w_conv_ref[:, cols]
        return (b_conv_ref[:, cols]
                + buf[hdr - 2:hdr - 2 + rows, :] * w[0:1, :]
                + buf[hdr - 1:hdr - 1 + rows, :] * w[1:2, :]
                + buf[hdr:hdr + rows, :] * w[2:3, :])

    project(0)
    f = jnp.zeros((rows, D_MODEL), F32)
    for j in range(N_FF_BLOCKS):
        if j + 1 < N_FF_BLOCKS:
            project(j + 1)
        act = (_gelu(conv(j, 0)) * conv(j, 1)).astype(BF16)
        f = f + _dot(act, w_down_ref[j * FF_BLOCK:(j + 1) * FF_BLOCK, :])

    @pl.when(t == pl.num_programs(1) - 1)
    def _():
        cs_ref[...] = carry_s[SUBLANES - (CONV_W - 1):, :]

    y_ref[...] = _ffn_tail(x1, f, p_ref[...], g_post_ref[...], g_ple_in_ref[...], w_pg_ref,
                           w_ple_ref, g_ple_post_ref[...])


def _ffn_sample_kernel(x_ref, p_ref, cin_ref, g_pre_ref, w_up_ref, w_conv_ref, b_conv_ref, w_down_ref,
                       g_post_ref, g_ple_in_ref, w_pg_ref, w_ple_ref, g_ple_post_ref,
                       y_ref, cs_ref, *, nseq, seq_len):
    x1 = jnp.concatenate([x_ref[:, j * D_MODEL:(j + 1) * D_MODEL] for j in range(seq_len)], axis=0)
    p = jnp.concatenate([p_ref[:, j * PLE_DIM:(j + 1) * PLE_DIM] for j in range(seq_len)], axis=0)
    h = _rms(x1, g_pre_ref[...]).astype(BF16)

    def conv_half(col0):
        cols = slice(col0, col0 + FF_BLOCK)
        up = _dot(h, w_up_ref[:, cols])
        blocks = [cin_ref[:, col0:col0 + FF_BLOCK],
                  cin_ref[:, 2 * D_FF + col0:2 * D_FF + col0 + FF_BLOCK]]
        blocks += [up[j * nseq:(j + 1) * nseq, :] for j in range(seq_len)]
        w = w_conv_ref[:, cols]
        b = b_conv_ref[:, cols]
        conv = jnp.concatenate(
            [b + blocks[j] * w[0:1, :] + blocks[j + 1] * w[1:2, :] + blocks[j + 2] * w[2:3, :]
             for j in range(seq_len)], axis=0)
        cs_ref[:, col0:col0 + FF_BLOCK] = blocks[seq_len]
        cs_ref[:, 2 * D_FF + col0:2 * D_FF + col0 + FF_BLOCK] = blocks[seq_len + 1]
        return conv

    f = jnp.zeros((nseq * seq_len, D_MODEL), F32)
    for j in range(N_FF_BLOCKS):
        gate = conv_half(j * FF_BLOCK)
        val = conv_half(D_FF + j * FF_BLOCK)
        act = (_gelu(gate) * val).astype(BF16)
        f = f + _dot(act, w_down_ref[j * FF_BLOCK:(j + 1) * FF_BLOCK, :])

    y = _ffn_tail(x1, f, p, g_post_ref[...], g_ple_in_ref[...], w_pg_ref, w_ple_ref,
                  g_ple_post_ref[...])
    for j in range(seq_len):
        y_ref[:, j * D_MODEL:(j + 1) * D_MODEL] = y[j * nseq:(j + 1) * nseq, :]


def _full(shape):
    n = len(shape)
    return pl.BlockSpec(shape, lambda *_: (0,) * n)


def _gla_masks(rows, span):
    i = np.arange(rows)
    same = (i[:, None] // span) == (i[None, :] // span)
    causal = same & (i[None, :] <= i[:, None])
    return (jnp.asarray(causal, BF16), jnp.asarray(same, BF16), jnp.asarray(causal, F32))


def kernel(x_prompt, x_sample, state_gla, state_ffn_conv, p_prompt, p_sample, g_mix_pre, w_in, w_gla_gate, b_gla_gate, g_gla_out, g_sgu_ln, b_sgu_ln, w_spatial, b_spatial, w_out, g_mix_post, g_ffn_pre, w_up, w_conv, b_conv, w_down, g_ffn_post, g_ple_in, w_ple_gate, w_ple, g_ple_post):
    depth = x_prompt.ndim - 2
    assert w_in.shape[0] == 1 and depth == 1
    nb, seq, _ = x_prompt.shape
    ns, dseq, _ = x_sample.shape
    assert seq % PROMPT_TILE == 0 and PROMPT_TILE % GMLP_CHUNK == 0
    assert dseq == 4 and (ns * dseq) % SAMPLE_TILE == 0 and SAMPLE_TILE % GMLP_CHUNK == 0

    wi = w_in[0]
    sp = np.cumsum((GLA_KEY, GLA_KEY, GLA_WIDTH, GLA_WIDTH, GLA_GATE_RANK, GMLP_WIDTH, GMLP_WIDTH))
    w_in_p = jnp.concatenate(
        [wi[:, :sp[3]], wi[:, sp[4]:], wi[:, sp[3]:sp[4]],
         jnp.zeros((D_MODEL, LANES - GLA_GATE_RANK), wi.dtype)], axis=1).astype(BF16)
    w_gate_p = jnp.concatenate(
        [w_gla_gate[0], jnp.zeros((LANES - GLA_GATE_RANK, GLA_KEY), w_gla_gate.dtype)], axis=0).astype(BF16)
    row = lambda a: a.reshape(1, -1)
    g_gla_t = jnp.tile(g_gla_out[0], GLA_HEADS).reshape(1, -1)
    w_out_b = w_out[0].astype(BF16)
    w_up_b = w_up[0].astype(BF16)
    w_down_b = w_down[0].astype(BF16)
    w_pg_b = w_ple_gate[0].astype(BF16)
    w_ple_b = w_ple[0].astype(BF16)

    def spatial(c):
        w = w_spatial[0][:, :c, :c] * jnp.tril(jnp.ones((c, c), w_spatial.dtype))
        reps = GMLP_CHUNK // c
        wbd = jnp.einsum('ab,hts->hatbs', jnp.eye(reps, dtype=w.dtype), w).reshape(
            GMLP_HEADS, GMLP_CHUNK, GMLP_CHUNK)
        bias = jnp.tile(jnp.transpose(b_spatial[0][:, :c]), (reps, 1))
        return wbd.astype(BF16), jnp.repeat(bias, GMLP_DC, axis=1)

    mixer_weights = lambda wsp, bsp, masks: (
        row(g_mix_pre), w_in_p, w_gate_p, row(b_gla_gate), g_gla_t, row(g_sgu_ln), row(b_sgu_ln),
        wsp, bsp, w_out_b, row(g_mix_post)) + masks

    def mixer_weight_specs(rows):
        return [
            _full((1, D_MODEL)), _full((D_MODEL, IN_COLS_PADDED)), _full((LANES, GLA_KEY)),
            _full((1, GLA_KEY)), _full((1, GLA_WIDTH)), _full((1, GMLP_WIDTH)), _full((1, GMLP_WIDTH)),
            _full((GMLP_HEADS, GMLP_CHUNK, GMLP_CHUNK)), _full((GMLP_CHUNK, GMLP_WIDTH)),
            _full((D_MODEL, D_MODEL)), _full((1, D_MODEL)),
            _full((rows, rows)), _full((rows, rows)), _full((rows, rows))]

    tl = PROMPT_TILE
    wsp_p, bsp_p = spatial(GMLP_CHUNK)
    x1_p, gla_p = pl.pallas_call(
        functools.partial(_mixer_kernel, rows=tl, sample=False),
        grid=(nb, seq // tl),
        in_specs=[pl.BlockSpec((None, tl, D_MODEL), lambda b, t: (b, t, 0))] + mixer_weight_specs(tl),
        out_specs=[pl.BlockSpec((None, tl, D_MODEL), lambda b, t: (b, t, 0)),
                   pl.BlockSpec((None, GLA_HEADS, GLA_DK, GLA_DV), lambda b, t: (b, 0, 0, 0))],
        out_shape=[jax.ShapeDtypeStruct((nb, seq, D_MODEL), F32),
                   jax.ShapeDtypeStruct((nb, GLA_HEADS, GLA_DK, GLA_DV), F32)],
        scratch_shapes=[pltpu.VMEM((tl, GLA_WIDTH), F32),
                        pltpu.VMEM((HEAD_PAIRS, 2 * GLA_DV, 2 * GLA_DK), F32)],
        compiler_params=pltpu.CompilerParams(
            dimension_semantics=("arbitrary", "arbitrary"), vmem_limit_bytes=VMEM_LIMIT_BYTES),
        name="mixer_prompt",
    )(x_prompt, *mixer_weights(wsp_p, bsp_p, _gla_masks(tl, GLA_CHUNK)))

    ts = SAMPLE_TILE
    seqs_per_tile = ts // dseq
    wsp_s, bsp_s = spatial(dseq)
    x1_s, gla_s, vrows_s = pl.pallas_call(
        functools.partial(_mixer_kernel, rows=ts, sample=True),
        grid=(ns * dseq // ts,),
        in_specs=[pl.BlockSpec((ts, D_MODEL), lambda i: (i, 0)),
                  pl.BlockSpec((seqs_per_tile, HEAD_PAIRS, 2 * GLA_DK, GLA_DV), lambda i: (i, 0, 0, 0))]
        + mixer_weight_specs(ts),
        out_specs=[pl.BlockSpec((ts, D_MODEL), lambda i: (i, 0)),
                   pl.BlockSpec((seqs_per_tile, HEAD_PAIRS, 2 * GLA_DK, GLA_DV), lambda i: (i, 0, 0, 0)),
                   pl.BlockSpec((ts, GMLP_WIDTH), lambda i: (i, 0))],
        out_shape=[jax.ShapeDtypeStruct((ns * dseq, D_MODEL), F32),
                   jax.ShapeDtypeStruct((ns, HEAD_PAIRS, 2 * GLA_DK, GLA_DV), F32),
                   jax.ShapeDtypeStruct((ns * dseq, GMLP_WIDTH), F32)],
        scratch_shapes=[pltpu.VMEM((ts, GLA_WIDTH), F32)],
        compiler_params=pltpu.CompilerParams(
            dimension_semantics=("arbitrary",), vmem_limit_bytes=VMEM_LIMIT_BYTES),
        name="mixer_sample",
    )(x_sample.reshape(ns * dseq, D_MODEL),
      state_gla[0].reshape(ns, HEAD_PAIRS, 2 * GLA_DK, GLA_DV),
      *mixer_weights(wsp_s, bsp_s, _gla_masks(ts, dseq)))

    ffn_weights = (row(g_ffn_pre), w_up_b, w_conv[0], row(b_conv), w_down_b, row(g_ffn_post),
                   row(g_ple_in), w_pg_b, w_ple_b, row(g_ple_post))
    ffn_weight_specs = [
        _full((1, D_MODEL)), _full((D_MODEL, 2 * D_FF)), _full((CONV_W, 2 * D_FF)), _full((1, 2 * D_FF)),
        _full((D_FF, D_MODEL)), _full((1, D_MODEL)), _full((1, D_MODEL)), _full((D_MODEL, D_MODEL)),
        _full((PLE_DIM, D_MODEL)), _full((1, D_MODEL))]

    y_p, conv_p = pl.pallas_call(
        functools.partial(_ffn_prompt_kernel, rows=tl),
        grid=(nb, seq // tl),
        in_specs=[pl.BlockSpec((None, tl, D_MODEL), lambda b, t: (b, t, 0)),
                  pl.BlockSpec((None, tl, PLE_DIM), lambda b, t: (b, t, 0))] + ffn_weight_specs,
        out_specs=[pl.BlockSpec((None, tl, D_MODEL), lambda b, t: (b, t, 0)),
                   pl.BlockSpec((None, CONV_W - 1, 2 * D_FF), lambda b, t: (b, 0, 0))],
        out_shape=[jax.ShapeDtypeStruct((nb, seq, D_MODEL), F32),
                   jax.ShapeDtypeStruct((nb, CONV_W - 1, 2 * D_FF), F32)],
        scratch_shapes=[pltpu.VMEM((SUBLANES, 2 * D_FF), F32),
                        pltpu.VMEM((2, 2, tl + SUBLANES, FF_BLOCK), F32)],
        compiler_params=pltpu.CompilerParams(
            dimension_semantics=("arbitrary", "arbitrary"), vmem_limit_bytes=VMEM_LIMIT_BYTES),
        name="ffn_prompt",
    )(x1_p, p_prompt[0], *ffn_weights)

    y_s, conv_s = pl.pallas_call(
        functools.partial(_ffn_sample_kernel, nseq=ns, seq_len=dseq),
        grid=(1,),
        in_specs=[_full((ns, dseq * D_MODEL)), _full((ns, dseq * PLE_DIM)),
                  _full((ns, (CONV_W - 1) * 2 * D_FF))] + ffn_weight_specs,
        out_specs=[_full((ns, dseq * D_MODEL)), _full((ns, (CONV_W - 1) * 2 * D_FF))],
        out_shape=[jax.ShapeDtypeStruct((ns, dseq * D_MODEL), F32),
                   jax.ShapeDtypeStruct((ns, (CONV_W - 1) * 2 * D_FF), F32)],
        compiler_params=pltpu.CompilerParams(
            dimension_semantics=("arbitrary",), vmem_limit_bytes=VMEM_LIMIT_BYTES),
        name="ffn_sample",
    )(x1_s.reshape(ns, dseq * D_MODEL), p_sample[0].reshape(ns, dseq * PLE_DIM),
      state_ffn_conv[0].reshape(ns, (CONV_W - 1) * 2 * D_FF), *ffn_weights)

    return (y_p,
            y_s.reshape(ns, dseq, D_MODEL),
            gla_p[None],
            gla_s.reshape(1, ns, GLA_HEADS, GLA_DK, GLA_DV),
            conv_p[None],
            conv_s.reshape(1, ns, CONV_W - 1, 2 * D_FF),
            vrows_s.reshape(1, ns, dseq, GMLP_WIDTH))
```

```python
import functools

import numpy as np
import jax
import jax.numpy as jnp
from jax import lax
from jax.experimental import pallas as pl
from jax.experimental.pallas import tpu as pltpu

D_MODEL = 1024
GLA_HEADS = 4
GLA_DK = 64
GLA_DV = 128
GLA_KEY = GLA_HEADS * GLA_DK
GLA_WIDTH = GLA_HEADS * GLA_DV
GMLP_HEADS = 4
GMLP_DC = 128
GMLP_WIDTH = GMLP_HEADS * GMLP_DC
GLA_GATE_RANK = 16
GLA_TAU = 16.0
GLA_CHUNK = 64
GMLP_CHUNK = 128
D_FF = 2816
CONV_W = 3
PLE_DIM = 256
EPS = 1e-6

LANES = 128
SUBLANES = 8
MXU_N = 256
HEAD_PAIRS = GLA_HEADS // 2

COL_QK = 0
COL_V = COL_QK + 2 * GLA_KEY
COL_R = COL_V + GLA_WIDTH
COL_U = COL_R + GLA_WIDTH
COL_VS = COL_U + GMLP_WIDTH
COL_GLR = COL_VS + GMLP_WIDTH
IN_COLS_PADDED = COL_GLR + LANES

FF_BLOCK = 256
N_FF_BLOCKS = D_FF // FF_BLOCK
FF_GROUP = 4

PROMPT_TILE = 256
SAMPLE_TILE = 128

VMEM_LIMIT_BYTES = 56 * 1024 * 1024

BF16 = jnp.bfloat16
F32 = jnp.float32


def _dot(a, b):
    return jnp.dot(a, b, preferred_element_type=F32)


def _dot_wide(a, w_ref, rows=slice(None)):
    return jnp.concatenate(
        [_dot(a, w_ref[rows, c:c + MXU_N]) for c in range(0, D_MODEL, MXU_N)], axis=-1)


def _dot_nt(a, b):
    return lax.dot_general(a, b, (((1,), (1,)), ((), ())), preferred_element_type=F32)


def _dot_tn(a, b):
    return lax.dot_general(a, b, (((0,), (0,)), ((), ())), preferred_element_type=F32)


def _rms(x, g):
    return x * lax.rsqrt(jnp.mean(x * x, axis=-1, keepdims=True) + EPS) * g


def _gelu_gate(x):
    c = -2.0 * np.sqrt(2.0 / np.pi) * np.log2(np.e)
    a1 = np.float32(c)
    a3 = np.float32(c * 0.044715)
    return 1.0 / (1.0 + jnp.exp2(x * (a3 * (x * x) + a1)))


def _gelu(x):
    return x * _gelu_gate(x)


def _sigmoid(x):
    return 1.0 / (1.0 + jnp.exp(-x))


def _log_sigmoid(x):
    return jnp.minimum(x, 0.0) - jnp.log1p(jnp.exp(-jnp.abs(x)))


def _split_bf16(x):
    hi = x.astype(BF16)
    lo = (x - hi.astype(F32)).astype(BF16)
    return hi, lo


def _mixer_kernel(*refs, rows, sample):
    if sample:
        (x_ref, sin_ref, g_pre_ref, w_in_ref, w_gate_ref, b_gate_ref, g_gla_ref, g_ln_ref, b_ln_ref,
         w_sp_ref, b_sp_ref, w_out_ref, g_post_ref, cum_ref, tot_ref, causal_ref,
         y_ref, sout_ref, vn_ref, o_s) = refs
        st_s = None
    else:
        (x_ref, g_pre_ref, w_in_ref, w_gate_ref, b_gate_ref, g_gla_ref, g_ln_ref, b_ln_ref,
         w_sp_ref, b_sp_ref, w_out_ref, g_post_ref, cum_ref, tot_ref, causal_ref,
         y_ref, sout_ref, o_s, st_s) = refs
        t = pl.program_id(1)

        @pl.when(t == 0)
        def _():
            st_s[...] = jnp.zeros_like(st_s)

    x = x_ref[...]
    h = _rms(x, g_pre_ref[...]).astype(BF16)

    qk = _dot(h, w_in_ref[:, COL_QK:COL_V])
    q = qk[:, :GLA_KEY] * (GLA_DK ** -0.5)
    k = qk[:, GLA_KEY:]
    v = _dot(h, w_in_ref[:, COL_V:COL_R])
    v_bf = v.astype(BF16)
    glr = _dot(h, w_in_ref[:, COL_GLR:IN_COLS_PADDED])
    logit = _dot(glr.astype(BF16), w_gate_ref[...]) + b_gate_ref[...]
    g_hi, g_lo = _split_bf16(_log_sigmoid(logit) * (1.0 / GLA_TAU))

    cum = cum_ref[...]
    tot = tot_ref[...]
    b = _dot(cum, g_hi) + _dot(cum, g_lo)
    bl = _dot(tot, g_hi) + _dot(tot, g_lo)
    qb = q * jnp.exp(b)
    kn = k * jnp.exp(-b)
    kd = k * jnp.exp(bl - b)
    causal = causal_ref[...]

    lane = lax.broadcasted_iota(jnp.int32, (rows, LANES), 1)
    head_lane_masks = (lane < GLA_DK, lane >= GLA_DK)

    for p in range(HEAD_PAIRS):
        ls = slice(p * LANES, (p + 1) * LANES)
        qbp = qb[:, ls]
        knp = kn[:, ls].astype(BF16)
        kdp = kd[:, ls]
        o_intra = []
        for hh in range(2):
            hd = 2 * p + hh
            qm = jnp.where(head_lane_masks[hh], qbp, 0.0).astype(BF16)
            s = _dot_nt(qm, knp) * causal
            o_intra.append(_dot(s.astype(BF16), v_bf[:, hd * GLA_DV:(hd + 1) * GLA_DV]))

        if not sample:
            st = st_s[p]
            ri = lax.broadcasted_iota(jnp.int32, st.shape, 0)
            ci = lax.broadcasted_iota(jnp.int32, st.shape, 1)
            same_head = (ri < GLA_DV) == (ci < GLA_DK)
            qbp_bf = qbp.astype(BF16)
            kdp_bf = kdp.astype(BF16)
            for c in range(rows // GLA_CHUNK):
                rws = slice(c * GLA_CHUNK, (c + 1) * GLA_CHUNK)
                o_inter = _dot_nt(qbp_bf[rws, :], st.astype(BF16))
                for hh in range(2):
                    hd = 2 * p + hh
                    o_s[rws, hd * GLA_DV:(hd + 1) * GLA_DV] = (
                        o_intra[hh][rws, :] + o_inter[:, hh * GLA_DV:(hh + 1) * GLA_DV])
                upd = _dot_tn(v_bf[rws, 2 * p * GLA_DV:(2 * p + 2) * GLA_DV], kdp_bf[rws, :])
                dec = jnp.exp(bl[c * GLA_CHUNK:c * GLA_CHUNK + 1, ls])
                st = st * dec + jnp.where(same_head, upd, 0.0)
            st_s[p] = st
        else:
            seq_len = 4
            seqs_per_grp = SUBLANES // seq_len
            kdp_t = kdp.T.astype(BF16)
            dec_t = jnp.exp(bl[:, ls]).T
            sub = lax.broadcasted_iota(jnp.int32, (SUBLANES, LANES), 0)
            crow = lax.broadcasted_iota(jnp.int32, (rows, LANES), 0)
            for grp in range(rows // SUBLANES):
                g0 = grp * SUBLANES
                qg = qbp[g0:g0 + SUBLANES, :]
                o_acc = [o_intra[hh][g0:g0 + SUBLANES, :] for hh in range(2)]
                for j in range(seqs_per_grp):
                    sidx = grp * seqs_per_grp + j
                    s_pair = sin_ref[sidx, p]
                    s_bf = s_pair.astype(BF16)
                    in_seq = (sub >= j * seq_len) & (sub < (j + 1) * seq_len)
                    in_seq_c = (crow >= g0 + j * seq_len) & (crow < g0 + (j + 1) * seq_len)
                    new_rows = []
                    for hh in range(2):
                        hd = 2 * p + hh
                        qsel = jnp.where(in_seq & head_lane_masks[hh][:SUBLANES], qg, 0.0).astype(BF16)
                        o_acc[hh] = o_acc[hh] + _dot(qsel, s_bf)
                        vsel = jnp.where(in_seq_c, v[:, hd * GLA_DV:(hd + 1) * GLA_DV], 0.0).astype(BF16)
                        new_rows.append(_dot(kdp_t[hh * GLA_DK:(hh + 1) * GLA_DK, :], vsel))
                    col = g0 + j * seq_len
                    sout_ref[sidx, p] = s_pair * dec_t[:, col:col + 1] + jnp.concatenate(new_rows, axis=0)
                for hh in range(2):
                    hd = 2 * p + hh
                    o_s[g0:g0 + SUBLANES, hd * GLA_DV:(hd + 1) * GLA_DV] = o_acc[hh]

    if not sample:
        @pl.when(t == pl.num_programs(1) - 1)
        def _():
            for p in range(HEAD_PAIRS):
                st = st_s[p]
                sout_ref[2 * p] = st[:GLA_DV, :].T[:GLA_DK, :]
                sout_ref[2 * p + 1] = st[GLA_DV:, :].T[GLA_DK:, :]

    r = _dot(h, w_in_ref[:, COL_R:COL_U])
    o = o_s[...]
    og = jnp.concatenate(
        [_rms(o[:, i * GLA_DV:(i + 1) * GLA_DV], 1.0) for i in range(GLA_HEADS)], axis=-1)
    og = og * g_gla_ref[...] * (r * _sigmoid(r))
    mix = _dot_wide(og.astype(BF16), w_out_ref, slice(0, GLA_WIDTH))

    u = _gelu(_dot(h, w_in_ref[:, COL_U:COL_VS]))
    vs = _gelu(_dot(h, w_in_ref[:, COL_VS:COL_GLR]))
    vn_parts = []
    for i in range(GMLP_HEADS):
        vh = vs[:, i * GMLP_DC:(i + 1) * GMLP_DC]
        mu = jnp.mean(vh, axis=-1, keepdims=True)
        d = vh - mu
        var = jnp.mean(d * d, axis=-1, keepdims=True)
        vn_parts.append(d * lax.rsqrt(var + EPS))
    vn = jnp.concatenate(vn_parts, axis=-1) * g_ln_ref[...] + b_ln_ref[...]
    if sample:
        vn_ref[...] = vn
    vn_bf = vn.astype(BF16)
    mixed_parts = []
    for cch in range(rows // GMLP_CHUNK):
        crows = slice(cch * GMLP_CHUNK, (cch + 1) * GMLP_CHUNK)
        mixed_parts.append(jnp.concatenate(
            [_dot(w_sp_ref[i], vn_bf[crows, i * GMLP_DC:(i + 1) * GMLP_DC]) for i in range(GMLP_HEADS)],
            axis=-1) + b_sp_ref[...])
    so = u * jnp.concatenate(mixed_parts, axis=0)
    mix = mix + _dot_wide(so.astype(BF16), w_out_ref, slice(GLA_WIDTH, GLA_WIDTH + GMLP_WIDTH))

    y_ref[...] = x + _rms(mix, g_post_ref[...])


def _ffn_tail(x1, f, p, g_ffn_post, g_ple_in, w_pg_ref, w_ple_ref, g_ple_post):
    x2 = x1 + _rms(f, g_ffn_post)
    hp = _rms(x2, g_ple_in).astype(BF16)
    gate = _sigmoid(_dot_wide(hp, w_pg_ref))
    pe = _dot_wide(p.astype(BF16), w_ple_ref)
    return x2 + _rms(pe * gate, g_ple_post)


def _ff_groups():
    return [range(g, min(g + FF_GROUP, N_FF_BLOCKS)) for g in range(0, N_FF_BLOCKS, FF_GROUP)]


def _ffn_prompt_kernel(x_ref, p_ref, g_pre_ref, w_up_ref, w_conv_ref, b_conv_ref, w_down_ref,
                       g_post_ref, g_ple_in_ref, w_pg_ref, w_ple_ref, g_ple_post_ref,
                       y_ref, cs_ref,
                       carry_s, up_s, act_s, *, rows):
    t = pl.program_id(1)
    hdr = SUBLANES

    @pl.when(t == 0)
    def _():
        carry_s[...] = jnp.zeros_like(carry_s)
        up_s[:, :, :, hdr + rows:, :] = jnp.zeros((2, 2, CONV_W, hdr, FF_BLOCK), F32)

    x1 = x_ref[...]
    h = _rms(x1, g_pre_ref[...]).astype(BF16)

    def half_cols(j, half):
        c0 = half * D_FF + j * FF_BLOCK
        return slice(c0, c0 + FF_BLOCK)

    def project(j):
        for half in range(2):
            cols = half_cols(j, half)
            up = _dot(h, w_up_ref[:, cols])
            for s in range(CONV_W):
                buf = up_s.at[j % 2, half, s]
                if s:
                    buf[hdr:2 * hdr, :] = carry_s[s - 1, :, cols]
                buf[hdr + s:hdr + s + rows, :] = up
                if s:
                    carry_s[s - 1, :, cols] = buf[hdr + rows:2 * hdr + rows, :]

    def conv(j, half):
        cols = half_cols(j, half)
        w = w_conv_ref[:, cols]
        taps = [up_s[j % 2, half, s, hdr:hdr + rows, :] for s in range(CONV_W)]
        return (b_conv_ref[:, cols] + taps[2] * w[0:1, :] + taps[1] * w[1:2, :] + taps[0] * w[2:3, :])

    def down(j):
        k_rows = slice(j * FF_BLOCK, (j + 1) * FF_BLOCK)
        return _dot_wide(act_s[:, k_rows], w_down_ref, k_rows)

    project(0)
    f = None
    for j in range(N_FF_BLOCKS):
        if j + 1 < N_FF_BLOCKS:
            project(j + 1)
        if j == 1:
            f = down(0)
        elif j > 1:
            f = f + down(j - 1)
        gate = conv(j, 0)
        act_s[:, j * FF_BLOCK:(j + 1) * FF_BLOCK] = (
            (gate * conv(j, 1)) * _gelu_gate(gate)).astype(BF16)
    f = f + down(N_FF_BLOCKS - 1)

    @pl.when(t == pl.num_programs(1) - 1)
    def _():
        cs_ref[...] = carry_s[CONV_W - 2, 0:CONV_W - 1, :]

    y_ref[...] = _ffn_tail(x1, f, p_ref[...], g_post_ref[...], g_ple_in_ref[...], w_pg_ref,
                           w_ple_ref, g_ple_post_ref[...])


def _ffn_sample_kernel(x_ref, p_ref, cin_ref, g_pre_ref, w_up_ref, w_conv_ref, b_conv_ref, w_down_ref,
                       g_post_ref, g_ple_in_ref, w_pg_ref, w_ple_ref, g_ple_post_ref,
                       y_ref, cs_ref, act_s, *, nseq, seq_len):
    x1 = jnp.concatenate([x_ref[:, j * D_MODEL:(j + 1) * D_MODEL] for j in range(seq_len)], axis=0)
    p = jnp.concatenate([p_ref[:, j * PLE_DIM:(j + 1) * PLE_DIM] for j in range(seq_len)], axis=0)
    h = _rms(x1, g_pre_ref[...]).astype(BF16)

    def conv_half(col0):
        cols = slice(col0, col0 + FF_BLOCK)
        up = _dot(h, w_up_ref[:, cols])
        blocks = [cin_ref[:, col0:col0 + FF_BLOCK],
                  cin_ref[:, 2 * D_FF + col0:2 * D_FF + col0 + FF_BLOCK]]
        blocks += [up[j * nseq:(j + 1) * nseq, :] for j in range(seq_len)]
        w = w_conv_ref[:, cols]
        b = b_conv_ref[:, cols]
        conv = jnp.concatenate(
            [b + blocks[j] * w[0:1, :] + blocks[j + 1] * w[1:2, :] + blocks[j + 2] * w[2:3, :]
             for j in range(seq_len)], axis=0)
        cs_ref[:, col0:col0 + FF_BLOCK] = blocks[seq_len]
        cs_ref[:, 2 * D_FF + col0:2 * D_FF + col0 + FF_BLOCK] = blocks[seq_len + 1]
        return conv

    f = None
    for grp in _ff_groups():
        for j in grp:
            gate = conv_half(j * FF_BLOCK)
            val = conv_half(D_FF + j * FF_BLOCK)
            act_s[:, j * FF_BLOCK:(j + 1) * FF_BLOCK] = ((gate * val) * _gelu_gate(gate)).astype(BF16)
        k_rows = slice(grp[0] * FF_BLOCK, (grp[-1] + 1) * FF_BLOCK)
        part = _dot_wide(act_s[:, k_rows], w_down_ref, k_rows)
        f = part if f is None else f + part

    y = _ffn_tail(x1, f, p, g_post_ref[...], g_ple_in_ref[...], w_pg_ref, w_ple_ref,
                  g_ple_post_ref[...])
    for j in range(seq_len):
        y_ref[:, j * D_MODEL:(j + 1) * D_MODEL] = y[j * nseq:(j + 1) * nseq, :]


def _full(shape):
    n = len(shape)
    return pl.BlockSpec(shape, lambda *_: (0,) * n)


def _gla_masks(rows, span):
    i = np.arange(rows)
    same = (i[:, None] // span) == (i[None, :] // span)
    causal = same & (i[None, :] <= i[:, None])
    return (jnp.asarray(causal, BF16), jnp.asarray(same, BF16), jnp.asarray(causal, F32))


def kernel(x_prompt, x_sample, state_gla, state_ffn_conv, p_prompt, p_sample, g_mix_pre, w_in, w_gla_gate, b_gla_gate, g_gla_out, g_sgu_ln, b_sgu_ln, w_spatial, b_spatial, w_out, g_mix_post, g_ffn_pre, w_up, w_conv, b_conv, w_down, g_ffn_post, g_ple_in, w_ple_gate, w_ple, g_ple_post):
    depth = x_prompt.ndim - 2
    assert w_in.shape[0] == 1 and depth == 1
    nb, seq, _ = x_prompt.shape
    ns, dseq, _ = x_sample.shape
    assert seq % PROMPT_TILE == 0 and PROMPT_TILE % GMLP_CHUNK == 0
    assert dseq == 4 and (ns * dseq) % SAMPLE_TILE == 0 and SAMPLE_TILE % GMLP_CHUNK == 0

    wi = w_in[0]
    sp = np.cumsum((GLA_KEY, GLA_KEY, GLA_WIDTH, GLA_WIDTH, GLA_GATE_RANK, GMLP_WIDTH, GMLP_WIDTH))
    w_in_p = jnp.concatenate(
        [wi[:, :sp[3]], wi[:, sp[4]:], wi[:, sp[3]:sp[4]],
         jnp.zeros((D_MODEL, LANES - GLA_GATE_RANK), wi.dtype)], axis=1).astype(BF16)
    w_gate_p = jnp.concatenate(
        [w_gla_gate[0], jnp.zeros((LANES - GLA_GATE_RANK, GLA_KEY), w_gla_gate.dtype)], axis=0).astype(BF16)
    row = lambda a: a.reshape(1, -1)
    g_gla_t = jnp.tile(g_gla_out[0], GLA_HEADS).reshape(1, -1)
    pad_cols = lambda w: jnp.pad(w.astype(BF16), ((0, 0), (0, LANES)))
    w_out_b = pad_cols(w_out[0])
    w_up_b = w_up[0].astype(BF16)
    w_down_b = pad_cols(w_down[0])
    w_pg_b = pad_cols(w_ple_gate[0])
    w_ple_b = pad_cols(w_ple[0])

    def spatial(c):
        w = w_spatial[0][:, :c, :c] * jnp.tril(jnp.ones((c, c), w_spatial.dtype))
        reps = GMLP_CHUNK // c
        wbd = jnp.einsum('ab,hts->hatbs', jnp.eye(reps, dtype=w.dtype), w).reshape(
            GMLP_HEADS, GMLP_CHUNK, GMLP_CHUNK)
        bias = jnp.tile(jnp.transpose(b_spatial[0][:, :c]), (reps, 1))
        return wbd.astype(BF16), jnp.repeat(bias, GMLP_DC, axis=1)

    mixer_weights = lambda wsp, bsp, masks: (
        row(g_mix_pre), w_in_p, w_gate_p, row(b_gla_gate), g_gla_t, row(g_sgu_ln), row(b_sgu_ln),
        wsp, bsp, w_out_b, row(g_mix_post)) + masks

    def mixer_weight_specs(rows):
        return [
            _full((1, D_MODEL)), _full((D_MODEL, IN_COLS_PADDED)), _full((LANES, GLA_KEY)),
            _full((1, GLA_KEY)), _full((1, GLA_WIDTH)), _full((1, GMLP_WIDTH)), _full((1, GMLP_WIDTH)),
            _full((GMLP_HEADS, GMLP_CHUNK, GMLP_CHUNK)), _full((GMLP_CHUNK, GMLP_WIDTH)),
            _full((D_MODEL, D_MODEL + LANES)), _full((1, D_MODEL)),
            _full((rows, rows)), _full((rows, rows)), _full((rows, rows))]

    tl = PROMPT_TILE
    wsp_p, bsp_p = spatial(GMLP_CHUNK)
    x1_p, gla_p = pl.pallas_call(
        functools.partial(_mixer_kernel, rows=tl, sample=False),
        grid=(nb, seq // tl),
        in_specs=[pl.BlockSpec((None, tl, D_MODEL), lambda b, t: (b, t, 0))] + mixer_weight_specs(tl),
        out_specs=[pl.BlockSpec((None, tl, D_MODEL), lambda b, t: (b, t, 0)),
                   pl.BlockSpec((None, GLA_HEADS, GLA_DK, GLA_DV), lambda b, t: (b, 0, 0, 0))],
        out_shape=[jax.ShapeDtypeStruct((nb, seq, D_MODEL), F32),
                   jax.ShapeDtypeStruct((nb, GLA_HEADS, GLA_DK, GLA_DV), F32)],
        scratch_shapes=[pltpu.VMEM((tl, GLA_WIDTH), F32),
                        pltpu.VMEM((HEAD_PAIRS, 2 * GLA_DV, 2 * GLA_DK), F32)],
        compiler_params=pltpu.CompilerParams(
            dimension_semantics=("arbitrary", "arbitrary"), vmem_limit_bytes=VMEM_LIMIT_BYTES),
        name="mixer_prompt",
    )(x_prompt, *mixer_weights(wsp_p, bsp_p, _gla_masks(tl, GLA_CHUNK)))

    ts = SAMPLE_TILE
    seqs_per_tile = ts // dseq
    wsp_s, bsp_s = spatial(dseq)
    x1_s, gla_s, vrows_s = pl.pallas_call(
        functools.partial(_mixer_kernel, rows=ts, sample=True),
        grid=(ns * dseq // ts,),
        in_specs=[pl.BlockSpec((ts, D_MODEL), lambda i: (i, 0)),
                  pl.BlockSpec((seqs_per_tile, HEAD_PAIRS, 2 * GLA_DK, GLA_DV), lambda i: (i, 0, 0, 0))]
        + mixer_weight_specs(ts),
        out_specs=[pl.BlockSpec((ts, D_MODEL), lambda i: (i, 0)),
                   pl.BlockSpec((seqs_per_tile, HEAD_PAIRS, 2 * GLA_DK, GLA_DV), lambda i: (i, 0, 0, 0)),
                   pl.BlockSpec((ts, GMLP_WIDTH), lambda i: (i, 0))],
        out_shape=[jax.ShapeDtypeStruct((ns * dseq, D_MODEL), F32),
                   jax.ShapeDtypeStruct((ns, HEAD_PAIRS, 2 * GLA_DK, GLA_DV), F32),
                   jax.ShapeDtypeStruct((ns * dseq, GMLP_WIDTH), F32)],
        scratch_shapes=[pltpu.VMEM((ts, GLA_WIDTH), F32)],
        compiler_params=pltpu.CompilerParams(
            dimension_semantics=("arbitrary",), vmem_limit_bytes=VMEM_LIMIT_BYTES),
        name="mixer_sample",
    )(x_sample.reshape(ns * dseq, D_MODEL),
      state_gla[0].reshape(ns, HEAD_PAIRS, 2 * GLA_DK, GLA_DV),
      *mixer_weights(wsp_s, bsp_s, _gla_masks(ts, dseq)))

    ffn_weights = (row(g_ffn_pre), w_up_b, w_conv[0], row(b_conv), w_down_b, row(g_ffn_post),
                   row(g_ple_in), w_pg_b, w_ple_b, row(g_ple_post))
    ffn_weight_specs = [
        _full((1, D_MODEL)), _full((D_MODEL, 2 * D_FF)), _full((CONV_W, 2 * D_FF)), _full((1, 2 * D_FF)),
        _full((D_FF, D_MODEL + LANES)), _full((1, D_MODEL)), _full((1, D_MODEL)),
        _full((D_MODEL, D_MODEL + LANES)), _full((PLE_DIM, D_MODEL + LANES)), _full((1, D_MODEL))]

    y_p, conv_p = pl.pallas_call(
        functools.partial(_ffn_prompt_kernel, rows=tl),
        grid=(nb, seq // tl),
        in_specs=[pl.BlockSpec((None, tl, D_MODEL), lambda b, t: (b, t, 0)),
                  pl.BlockSpec((None, tl, PLE_DIM), lambda b, t: (b, t, 0))] + ffn_weight_specs,
        out_specs=[pl.BlockSpec((None, tl, D_MODEL), lambda b, t: (b, t, 0)),
                   pl.BlockSpec((None, CONV_W - 1, 2 * D_FF), lambda b, t: (b, 0, 0))],
        out_shape=[jax.ShapeDtypeStruct((nb, seq, D_MODEL), F32),
                   jax.ShapeDtypeStruct((nb, CONV_W - 1, 2 * D_FF), F32)],
        scratch_shapes=[pltpu.VMEM((CONV_W - 1, SUBLANES, 2 * D_FF), F32),
                        pltpu.VMEM((2, 2, CONV_W, tl + 2 * SUBLANES, FF_BLOCK), F32),
                        pltpu.VMEM((tl, D_FF), BF16)],
        compiler_params=pltpu.CompilerParams(
            dimension_semantics=("arbitrary", "arbitrary"), vmem_limit_bytes=VMEM_LIMIT_BYTES),
        name="ffn_prompt",
    )(x1_p, p_prompt[0], *ffn_weights)

    y_s, conv_s = pl.pallas_call(
        functools.partial(_ffn_sample_kernel, nseq=ns, seq_len=dseq),
        grid=(1,),
        in_specs=[_full((ns, dseq * D_MODEL)), _full((ns, dseq * PLE_DIM)),
                  _full((ns, (CONV_W - 1) * 2 * D_FF))] + ffn_weight_specs,
        out_specs=[_full((ns, dseq * D_MODEL)), _full((ns, (CONV_W - 1) * 2 * D_FF))],
        out_shape=[jax.ShapeDtypeStruct((ns, dseq * D_MODEL), F32),
                   jax.ShapeDtypeStruct((ns, (CONV_W - 1) * 2 * D_FF), F32)],
        scratch_shapes=[pltpu.VMEM((ns * dseq, D_FF), BF16)],
        compiler_params=pltpu.CompilerParams(
            dimension_semantics=("arbitrary",), vmem_limit_bytes=VMEM_LIMIT_BYTES),
        name="ffn_sample",
    )(x1_s.reshape(ns, dseq * D_MODEL), p_sample[0].reshape(ns, dseq * PLE_DIM),
      state_ffn_conv[0].reshape(ns, (CONV_W - 1) * 2 * D_FF), *ffn_weights)

    return (y_p,
            y_s.reshape(ns, dseq, D_MODEL),
            gla_p[None],
            gla_s.reshape(1, ns, GLA_HEADS, GLA_DK, GLA_DV),
            conv_p[None],
            conv_s.reshape(1, ns, CONV_W - 1, 2 * D_FF),
            vrows_s.reshape(1, ns, dseq, GMLP_WIDTH))
```

```python
import functools

import numpy as np
import jax
import jax.numpy as jnp
from jax import lax
from jax.experimental import pallas as pl
from jax.experimental.pallas import tpu as pltpu

D_MODEL = 1024
GLA_HEADS = 4
GLA_DK = 64
GLA_DV = 128
GLA_KEY = GLA_HEADS * GLA_DK
GLA_WIDTH = GLA_HEADS * GLA_DV
GMLP_HEADS = 4
GMLP_DC = 128
GMLP_WIDTH = GMLP_HEADS * GMLP_DC
GLA_GATE_RANK = 16
GLA_TAU = 16.0
GLA_CHUNK = 64
GMLP_CHUNK = 128
D_FF = 2816
CONV_W = 3
PLE_DIM = 256
EPS = 1e-6

LANES = 128
SUBLANES = 8
MXU_N = 256
HEAD_PAIRS = GLA_HEADS // 2

COL_QK = 0
COL_V = COL_QK + 2 * GLA_KEY
COL_R = COL_V + GLA_WIDTH
IN_COLS_GLA = COL_R + GLA_WIDTH
COL_U = 0
COL_VS = COL_U + GMLP_WIDTH
COL_GLR = COL_VS + GMLP_WIDTH
IN_COLS_SGU = COL_GLR + LANES

FF_BLOCK = 256
N_FF_BLOCKS = D_FF // FF_BLOCK
FF_GROUP = 4

PROMPT_TILE = 256
SAMPLE_TILE = 128

VMEM_LIMIT_BYTES = 56 * 1024 * 1024

BF16 = jnp.bfloat16
F32 = jnp.float32


def _dot(a, b):
    return jnp.dot(a, b, preferred_element_type=F32)


def _dot_wide(a, w_ref, rows=slice(None)):
    return jnp.concatenate(
        [_dot(a, w_ref[rows, c:c + MXU_N]) for c in range(0, D_MODEL, MXU_N)], axis=-1)


def _dot_nt(a, b):
    return lax.dot_general(a, b, (((1,), (1,)), ((), ())), preferred_element_type=F32)


def _dot_tn(a, b):
    return lax.dot_general(a, b, (((0,), (0,)), ((), ())), preferred_element_type=F32)


def _rms(x, g):
    return x * lax.rsqrt(jnp.mean(x * x, axis=-1, keepdims=True) + EPS) * g


def _gelu_gate(x):
    c = -2.0 * np.sqrt(2.0 / np.pi) * np.log2(np.e)
    a1 = np.float32(c)
    a3 = np.float32(c * 0.044715)
    return 1.0 / (1.0 + jnp.exp2(x * (a3 * (x * x) + a1)))


def _gelu(x):
    return x * _gelu_gate(x)


def _sigmoid(x):
    return 1.0 / (1.0 + jnp.exp(-x))


def _log_sigmoid(x):
    return jnp.minimum(x, 0.0) - jnp.log1p(jnp.exp(-jnp.abs(x)))


def _split_bf16(x):
    hi = x.astype(BF16)
    lo = (x - hi.astype(F32)).astype(BF16)
    return hi, lo


def _mixer_kernel(*refs, rows, sample):
    if sample:
        (x_ref, sin_ref, g_pre_ref, w_ga_ref, w_sg_ref, w_gate_ref, b_gate_ref, g_gla_ref, g_ln_ref, b_ln_ref,
         w_sp_ref, b_sp_ref, w_out_ref, g_post_ref, cum_ref, tot_ref, causal_ref,
         y_ref, sout_ref, vn_ref, o_s) = refs
        st_s = None
    else:
        (x_ref, g_pre_ref, w_ga_ref, w_sg_ref, w_gate_ref, b_gate_ref, g_gla_ref, g_ln_ref, b_ln_ref,
         w_sp_ref, b_sp_ref, w_out_ref, g_post_ref, cum_ref, tot_ref, causal_ref,
         y_ref, sout_ref, o_s, st_s) = refs
        t = pl.program_id(1)

        @pl.when(t == 0)
        def _():
            st_s[...] = jnp.zeros_like(st_s)

    x = x_ref[...]
    h = _rms(x, g_pre_ref[...]).astype(BF16)

    u_pre = _dot(h, w_sg_ref[:, COL_U:COL_VS])
    vs_pre = _dot(h, w_sg_ref[:, COL_VS:COL_GLR])
    glr = _dot(h, w_sg_ref[:, COL_GLR:IN_COLS_SGU])
    qk = _dot(h, w_ga_ref[:, COL_QK:COL_V])
    v = _dot(h, w_ga_ref[:, COL_V:COL_R])
    r = _dot(h, w_ga_ref[:, COL_R:IN_COLS_GLA])

    u = _gelu(u_pre)
    vs = _gelu(vs_pre)
    vn_parts = []
    for i in range(GMLP_HEADS):
        vh = vs[:, i * GMLP_DC:(i + 1) * GMLP_DC]
        mu = jnp.mean(vh, axis=-1, keepdims=True)
        d = vh - mu
        var = jnp.mean(d * d, axis=-1, keepdims=True)
        vn_parts.append(d * lax.rsqrt(var + EPS))
    vn = jnp.concatenate(vn_parts, axis=-1) * g_ln_ref[...] + b_ln_ref[...]
    if sample:
        vn_ref[...] = vn
    vn_bf = vn.astype(BF16)
    mixed_parts = []
    for cch in range(rows // GMLP_CHUNK):
        crows = slice(cch * GMLP_CHUNK, (cch + 1) * GMLP_CHUNK)
        mixed_parts.append(jnp.concatenate(
            [_dot(w_sp_ref[i], vn_bf[crows, i * GMLP_DC:(i + 1) * GMLP_DC]) for i in range(GMLP_HEADS)],
            axis=-1) + b_sp_ref[...])
    so = u * jnp.concatenate(mixed_parts, axis=0)
    mix = _dot_wide(so.astype(BF16), w_out_ref, slice(GLA_WIDTH, GLA_WIDTH + GMLP_WIDTH))

    q = qk[:, :GLA_KEY] * (GLA_DK ** -0.5)
    k = qk[:, GLA_KEY:]
    v_bf = v.astype(BF16)
    logit = _dot(glr.astype(BF16), w_gate_ref[...]) + b_gate_ref[...]
    g_hi, g_lo = _split_bf16(_log_sigmoid(logit) * (1.0 / GLA_TAU))

    cum = cum_ref[...]
    tot = tot_ref[...]
    b = _dot(cum, g_hi) + _dot(cum, g_lo)
    bl = _dot(tot, g_hi) + _dot(tot, g_lo)
    qb = q * jnp.exp(b)
    kn = k * jnp.exp(-b)
    kd = k * jnp.exp(bl - b)
    causal = causal_ref[...]

    lane = lax.broadcasted_iota(jnp.int32, (rows, LANES), 1)
    head_lane_masks = (lane < GLA_DK, lane >= GLA_DK)

    for p in range(HEAD_PAIRS):
        ls = slice(p * LANES, (p + 1) * LANES)
        qbp = qb[:, ls]
        knp = kn[:, ls].astype(BF16)
        kdp = kd[:, ls]
        o_intra = []
        for hh in range(2):
            hd = 2 * p + hh
            qm = jnp.where(head_lane_masks[hh], qbp, 0.0).astype(BF16)
            s = _dot_nt(qm, knp) * causal
            o_intra.append(_dot(s.astype(BF16), v_bf[:, hd * GLA_DV:(hd + 1) * GLA_DV]))

        if not sample:
            st = st_s[p]
            ri = lax.broadcasted_iota(jnp.int32, st.shape, 0)
            ci = lax.broadcasted_iota(jnp.int32, st.shape, 1)
            same_head = (ri < GLA_DV) == (ci < GLA_DK)
            qbp_bf = qbp.astype(BF16)
            kdp_bf = kdp.astype(BF16)
            for c in range(rows // GLA_CHUNK):
                rws = slice(c * GLA_CHUNK, (c + 1) * GLA_CHUNK)
                o_inter = _dot_nt(qbp_bf[rws, :], st.astype(BF16))
                for hh in range(2):
                    hd = 2 * p + hh
                    o_s[rws, hd * GLA_DV:(hd + 1) * GLA_DV] = (
                        o_intra[hh][rws, :] + o_inter[:, hh * GLA_DV:(hh + 1) * GLA_DV])
                upd = _dot_tn(v_bf[rws, 2 * p * GLA_DV:(2 * p + 2) * GLA_DV], kdp_bf[rws, :])
                dec = jnp.exp(bl[c * GLA_CHUNK:c * GLA_CHUNK + 1, ls])
                st = st * dec + jnp.where(same_head, upd, 0.0)
            st_s[p] = st
        else:
            seq_len = 4
            seqs_per_grp = SUBLANES // seq_len
            kdp_t = kdp.T.astype(BF16)
            dec_t = jnp.exp(bl[:, ls]).T
            sub = lax.broadcasted_iota(jnp.int32, (SUBLANES, LANES), 0)
            crow = lax.broadcasted_iota(jnp.int32, (rows, LANES), 0)
            for grp in range(rows // SUBLANES):
                g0 = grp * SUBLANES
                qg = qbp[g0:g0 + SUBLANES, :]
                o_acc = [o_intra[hh][g0:g0 + SUBLANES, :] for hh in range(2)]
                for j in range(seqs_per_grp):
                    sidx = grp * seqs_per_grp + j
                    s_pair = sin_ref[sidx, p]
                    s_bf = s_pair.astype(BF16)
                    in_seq = (sub >= j * seq_len) & (sub < (j + 1) * seq_len)
                    in_seq_c = (crow >= g0 + j * seq_len) & (crow < g0 + (j + 1) * seq_len)
                    new_rows = []
                    for hh in range(2):
                        hd = 2 * p + hh
                        qsel = jnp.where(in_seq & head_lane_masks[hh][:SUBLANES], qg, 0.0).astype(BF16)
                        o_acc[hh] = o_acc[hh] + _dot(qsel, s_bf)
                        vsel = jnp.where(in_seq_c, v[:, hd * GLA_DV:(hd + 1) * GLA_DV], 0.0).astype(BF16)
                        new_rows.append(_dot(kdp_t[hh * GLA_DK:(hh + 1) * GLA_DK, :], vsel))
                    col = g0 + j * seq_len
                    sout_ref[sidx, p] = s_pair * dec_t[:, col:col + 1] + jnp.concatenate(new_rows, axis=0)
                for hh in range(2):
                    hd = 2 * p + hh
                    o_s[g0:g0 + SUBLANES, hd * GLA_DV:(hd + 1) * GLA_DV] = o_acc[hh]

    o = o_s[...]
    og = jnp.concatenate(
        [_rms(o[:, i * GLA_DV:(i + 1) * GLA_DV], 1.0) for i in range(GLA_HEADS)], axis=-1)
    og = og * g_gla_ref[...] * (r * _sigmoid(r))
    mix = mix + _dot_wide(og.astype(BF16), w_out_ref, slice(0, GLA_WIDTH))

    y_ref[...] = x + _rms(mix, g_post_ref[...])

    if not sample:
        @pl.when(t == pl.num_programs(1) - 1)
        def _():
            for p in range(HEAD_PAIRS):
                st = st_s[p]
                sout_ref[2 * p] = st[:GLA_DV, :].T[:GLA_DK, :]
                sout_ref[2 * p + 1] = st[GLA_DV:, :].T[GLA_DK:, :]


def _ffn_tail(x1, f, p, g_ffn_post, g_ple_in, w_pg_ref, w_ple_ref, g_ple_post):
    x2 = x1 + _rms(f, g_ffn_post)
    hp = _rms(x2, g_ple_in).astype(BF16)
    gate = _sigmoid(_dot_wide(hp, w_pg_ref))
    pe = _dot_wide(p.astype(BF16), w_ple_ref)
    return x2 + _rms(pe * gate, g_ple_post)


def _ff_groups():
    return [range(g, min(g + FF_GROUP, N_FF_BLOCKS)) for g in range(0, N_FF_BLOCKS, FF_GROUP)]


def _ffn_prompt_kernel(x_ref, p_ref, g_pre_ref, w_up_ref, w_conv_ref, b_conv_ref, w_down_ref,
                       g_post_ref, g_ple_in_ref, w_pg_ref, w_ple_ref, g_ple_post_ref,
                       y_ref, cs_ref,
                       carry_s, up_s, act_s, *, rows):
    t = pl.program_id(1)
    hdr = SUBLANES

    @pl.when(t == 0)
    def _():
        carry_s[...] = jnp.zeros_like(carry_s)
        up_s[:, :, :, hdr + rows:, :] = jnp.zeros((2, 2, CONV_W, hdr, FF_BLOCK), F32)

    x1 = x_ref[...]
    h = _rms(x1, g_pre_ref[...]).astype(BF16)

    def half_cols(j, half):
        c0 = half * D_FF + j * FF_BLOCK
        return slice(c0, c0 + FF_BLOCK)

    def project(j):
        for half in range(2):
            cols = half_cols(j, half)
            up = _dot(h, w_up_ref[:, cols])
            for s in range(CONV_W):
                buf = up_s.at[j % 2, half, s]
                if s:
                    buf[hdr:2 * hdr, :] = carry_s[s - 1, :, cols]
                buf[hdr + s:hdr + s + rows, :] = up
                if s:
                    carry_s[s - 1, :, cols] = buf[hdr + rows:2 * hdr + rows, :]

    def conv(j, half):
        cols = half_cols(j, half)
        w = w_conv_ref[:, cols]
        taps = [up_s[j % 2, half, s, hdr:hdr + rows, :] for s in range(CONV_W)]
        return (b_conv_ref[:, cols] + taps[2] * w[0:1, :] + taps[1] * w[1:2, :] + taps[0] * w[2:3, :])

    def down(j):
        k_rows = slice(j * FF_BLOCK, (j + 1) * FF_BLOCK)
        return _dot_wide(act_s[:, k_rows], w_down_ref, k_rows)

    project(0)
    f = None
    for j in range(N_FF_BLOCKS):
        if j + 1 < N_FF_BLOCKS:
            project(j + 1)
        if j == 1:
            f = down(0)
        elif j > 1:
            f = f + down(j - 1)
        gate = conv(j, 0)
        act_s[:, j * FF_BLOCK:(j + 1) * FF_BLOCK] = (
            (gate * conv(j, 1)) * _gelu_gate(gate)).astype(BF16)
    f = f + down(N_FF_BLOCKS - 1)

    y_ref[...] = _ffn_tail(x1, f, p_ref[...], g_post_ref[...], g_ple_in_ref[...], w_pg_ref,
                           w_ple_ref, g_ple_post_ref[...])

    @pl.when(t == pl.num_programs(1) - 1)
    def _():
        cs_ref[...] = carry_s[CONV_W - 2, 0:CONV_W - 1, :]


def _ffn_sample_kernel(x_ref, p_ref, cin_ref, g_pre_ref, w_up_ref, w_conv_ref, b_conv_ref, w_down_ref,
                       g_post_ref, g_ple_in_ref, w_pg_ref, w_ple_ref, g_ple_post_ref,
                       y_ref, cs_ref, act_s, *, nseq, seq_len):
    x1 = jnp.concatenate([x_ref[:, j * D_MODEL:(j + 1) * D_MODEL] for j in range(seq_len)], axis=0)
    p = jnp.concatenate([p_ref[:, j * PLE_DIM:(j + 1) * PLE_DIM] for j in range(seq_len)], axis=0)
    h = _rms(x1, g_pre_ref[...]).astype(BF16)

    def conv_half(col0):
        cols = slice(col0, col0 + FF_BLOCK)
        up = _dot(h, w_up_ref[:, cols])
        blocks = [cin_ref[:, col0:col0 + FF_BLOCK],
                  cin_ref[:, 2 * D_FF + col0:2 * D_FF + col0 + FF_BLOCK]]
        blocks += [up[j * nseq:(j + 1) * nseq, :] for j in range(seq_len)]
        w = w_conv_ref[:, cols]
        b = b_conv_ref[:, cols]
        conv = jnp.concatenate(
            [b + blocks[j] * w[0:1, :] + blocks[j + 1] * w[1:2, :] + blocks[j + 2] * w[2:3, :]
             for j in range(seq_len)], axis=0)
        cs_ref[:, col0:col0 + FF_BLOCK] = blocks[seq_len]
        cs_ref[:, 2 * D_FF + col0:2 * D_FF + col0 + FF_BLOCK] = blocks[seq_len + 1]
        return conv

    f = None
    for grp in _ff_groups():
        for j in grp:
            gate = conv_half(j * FF_BLOCK)
            val = conv_half(D_FF + j * FF_BLOCK)
            act_s[:, j * FF_BLOCK:(j + 1) * FF_BLOCK] = ((gate * val) * _gelu_gate(gate)).astype(BF16)
        k_rows = slice(grp[0] * FF_BLOCK, (grp[-1] + 1) * FF_BLOCK)
        part = _dot_wide(act_s[:, k_rows], w_down_ref, k_rows)
        f = part if f is None else f + part

    y = _ffn_tail(x1, f, p, g_post_ref[...], g_ple_in_ref[...], w_pg_ref, w_ple_ref,
                  g_ple_post_ref[...])
    for j in range(seq_len):
        y_ref[:, j * D_MODEL:(j + 1) * D_MODEL] = y[j * nseq:(j + 1) * nseq, :]


def _full(shape):
    n = len(shape)
    return pl.BlockSpec(shape, lambda *_: (0,) * n)


def _gla_masks(rows, span):
    i = np.arange(rows)
    same = (i[:, None] // span) == (i[None, :] // span)
    causal = same & (i[None, :] <= i[:, None])
    return (jnp.asarray(causal, BF16), jnp.asarray(same, BF16), jnp.asarray(causal, F32))


def kernel(x_prompt, x_sample, state_gla, state_ffn_conv, p_prompt, p_sample, g_mix_pre, w_in, w_gla_gate, b_gla_gate, g_gla_out, g_sgu_ln, b_sgu_ln, w_spatial, b_spatial, w_out, g_mix_post, g_ffn_pre, w_up, w_conv, b_conv, w_down, g_ffn_post, g_ple_in, w_ple_gate, w_ple, g_ple_post):
    depth = x_prompt.ndim - 2
    assert w_in.shape[0] == 1 and depth == 1
    nb, seq, _ = x_prompt.shape
    ns, dseq, _ = x_sample.shape
    assert seq % PROMPT_TILE == 0 and PROMPT_TILE % GMLP_CHUNK == 0
    assert dseq == 4 and (ns * dseq) % SAMPLE_TILE == 0 and SAMPLE_TILE % GMLP_CHUNK == 0

    wi = w_in[0]
    sp = np.cumsum((GLA_KEY, GLA_KEY, GLA_WIDTH, GLA_WIDTH, GLA_GATE_RANK, GMLP_WIDTH, GMLP_WIDTH))
    w_ga = wi[:, :sp[3]].astype(BF16)
    w_sg = jnp.concatenate(
        [wi[:, sp[4]:], wi[:, sp[3]:sp[4]],
         jnp.zeros((D_MODEL, LANES - GLA_GATE_RANK), wi.dtype)], axis=1).astype(BF16)
    w_gate_p = jnp.concatenate(
        [w_gla_gate[0], jnp.zeros((LANES - GLA_GATE_RANK, GLA_KEY), w_gla_gate.dtype)], axis=0).astype(BF16)
    row = lambda a: a.reshape(1, -1)
    g_gla_t = jnp.tile(g_gla_out[0], GLA_HEADS).reshape(1, -1)
    pad_cols = lambda w: jnp.pad(w, ((0, 0), (0, LANES))).astype(BF16)
    w_out_b = pad_cols(w_out[0])
    w_up_b = w_up[0].astype(BF16)
    w_down_b = pad_cols(w_down[0])
    w_pg_b = pad_cols(w_ple_gate[0])
    w_ple_b = pad_cols(w_ple[0])

    def spatial(c):
        reps = GMLP_CHUNK // c
        i = np.arange(GMLP_CHUNK)
        keep = ((i[:, None] // c) == (i[None, :] // c)) & (i[None, :] <= i[:, None])
        wbd = jnp.tile(w_spatial[0][:, :c, :c], (1, reps, reps)) * jnp.asarray(keep, w_spatial.dtype)
        bias = jnp.tile(jnp.transpose(b_spatial[0][:, :c]), (reps, 1))
        return wbd.astype(BF16), jnp.repeat(bias, GMLP_DC, axis=1)

    mixer_weights = lambda wsp, bsp, masks: (
        row(g_mix_pre), w_ga, w_sg, w_gate_p, row(b_gla_gate), g_gla_t, row(g_sgu_ln), row(b_sgu_ln),
        wsp, bsp, w_out_b, row(g_mix_post)) + masks

    def mixer_weight_specs(rows):
        return [
            _full((1, D_MODEL)), _full((D_MODEL, IN_COLS_GLA)), _full((D_MODEL, IN_COLS_SGU)),
            _full((LANES, GLA_KEY)),
            _full((1, GLA_KEY)), _full((1, GLA_WIDTH)), _full((1, GMLP_WIDTH)), _full((1, GMLP_WIDTH)),
            _full((GMLP_HEADS, GMLP_CHUNK, GMLP_CHUNK)), _full((GMLP_CHUNK, GMLP_WIDTH)),
            _full((D_MODEL, D_MODEL + LANES)), _full((1, D_MODEL)),
            _full((rows, rows)), _full((rows, rows)), _full((rows, rows))]

    tl = PROMPT_TILE
    wsp_p, bsp_p = spatial(GMLP_CHUNK)
    x1_p, gla_p = pl.pallas_call(
        functools.partial(_mixer_kernel, rows=tl, sample=False),
        grid=(nb, seq // tl),
        in_specs=[pl.BlockSpec((None, tl, D_MODEL), lambda b, t: (b, t, 0))] + mixer_weight_specs(tl),
        out_specs=[pl.BlockSpec((None, tl, D_MODEL), lambda b, t: (b, t, 0)),
                   pl.BlockSpec((None, GLA_HEADS, GLA_DK, GLA_DV), lambda b, t: (b, 0, 0, 0))],
        out_shape=[jax.ShapeDtypeStruct((nb, seq, D_MODEL), F32),
                   jax.ShapeDtypeStruct((nb, GLA_HEADS, GLA_DK, GLA_DV), F32)],
        scratch_shapes=[pltpu.VMEM((tl, GLA_WIDTH), F32),
                        pltpu.VMEM((HEAD_PAIRS, 2 * GLA_DV, 2 * GLA_DK), F32)],
        compiler_params=pltpu.CompilerParams(
            dimension_semantics=("arbitrary", "arbitrary"), vmem_limit_bytes=VMEM_LIMIT_BYTES),
        name="mixer_prompt",
    )(x_prompt, *mixer_weights(wsp_p, bsp_p, _gla_masks(tl, GLA_CHUNK)))

    ts = SAMPLE_TILE
    seqs_per_tile = ts // dseq
    wsp_s, bsp_s = spatial(dseq)
    x1_s, gla_s, vrows_s = pl.pallas_call(
        functools.partial(_mixer_kernel, rows=ts, sample=True),
        grid=(ns * dseq // ts,),
        in_specs=[pl.BlockSpec((ts, D_MODEL), lambda i: (i, 0)),
                  pl.BlockSpec((seqs_per_tile, HEAD_PAIRS, 2 * GLA_DK, GLA_DV), lambda i: (i, 0, 0, 0))]
        + mixer_weight_specs(ts),
        out_specs=[pl.BlockSpec((ts, D_MODEL), lambda i: (i, 0)),
                   pl.BlockSpec((seqs_per_tile, HEAD_PAIRS, 2 * GLA_DK, GLA_DV), lambda i: (i, 0, 0, 0)),
                   pl.BlockSpec((ts, GMLP_WIDTH), lambda i: (i, 0))],
        out_shape=[jax.ShapeDtypeStruct((ns * dseq, D_MODEL), F32),
                   jax.ShapeDtypeStruct((ns, HEAD_PAIRS, 2 * GLA_DK, GLA_DV), F32),
                   jax.ShapeDtypeStruct((ns * dseq, GMLP_WIDTH), F32)],
        scratch_shapes=[pltpu.VMEM((ts, GLA_WIDTH), F32)],
        compiler_params=pltpu.CompilerParams(
            dimension_semantics=("arbitrary",), vmem_limit_bytes=VMEM_LIMIT_BYTES),
        name="mixer_sample",
    )(x_sample.reshape(ns * dseq, D_MODEL),
      state_gla[0].reshape(ns, HEAD_PAIRS, 2 * GLA_DK, GLA_DV),
      *mixer_weights(wsp_s, bsp_s, _gla_masks(ts, dseq)))

    ffn_weights = (row(g_ffn_pre), w_up_b, w_conv[0], row(b_conv), w_down_b, row(g_ffn_post),
                   row(g_ple_in), w_pg_b, w_ple_b, row(g_ple_post))
    ffn_weight_specs = [
        _full((1, D_MODEL)), _full((D_MODEL, 2 * D_FF)), _full((CONV_W, 2 * D_FF)), _full((1, 2 * D_FF)),
        _full((D_FF, D_MODEL + LANES)), _full((1, D_MODEL)), _full((1, D_MODEL)),
        _full((D_MODEL, D_MODEL + LANES)), _full((PLE_DIM, D_MODEL + LANES)), _full((1, D_MODEL))]

    y_p, conv_p = pl.pallas_call(
        functools.partial(_ffn_prompt_kernel, rows=tl),
        grid=(nb, seq // tl),
        in_specs=[pl.BlockSpec((None, tl, D_MODEL), lambda b, t: (b, t, 0)),
                  pl.BlockSpec((None, tl, PLE_DIM), lambda b, t: (b, t, 0))] + ffn_weight_specs,
        out_specs=[pl.BlockSpec((None, tl, D_MODEL), lambda b, t: (b, t, 0)),
                   pl.BlockSpec((None, CONV_W - 1, 2 * D_FF), lambda b, t: (b, 0, 0))],
        out_shape=[jax.ShapeDtypeStruct((nb, seq, D_MODEL), F32),
                   jax.ShapeDtypeStruct((nb, CONV_W - 1, 2 * D_FF), F32)],
        scratch_shapes=[pltpu.VMEM((CONV_W - 1, SUBLANES, 2 * D_FF), F32),
                        pltpu.VMEM((2, 2, CONV_W, tl + 2 * SUBLANES, FF_BLOCK), F32),
                        pltpu.VMEM((tl, D_FF), BF16)],
        compiler_params=pltpu.CompilerParams(
            dimension_semantics=("arbitrary", "arbitrary"), vmem_limit_bytes=VMEM_LIMIT_BYTES),
        name="ffn_prompt",
    )(x1_p, p_prompt[0], *ffn_weights)

    y_s, conv_s = pl.pallas_call(
        functools.partial(_ffn_sample_kernel, nseq=ns, seq_len=dseq),
        grid=(1,),
        in_specs=[_full((ns, dseq * D_MODEL)), _full((ns, dseq * PLE_DIM)),
                  _full((ns, (CONV_W - 1) * 2 * D_FF))] + ffn_weight_specs,
        out_specs=[_full((ns, dseq * D_MODEL)), _full((ns, (CONV_W - 1) * 2 * D_FF))],
        out_shape=[jax.ShapeDtypeStruct((ns, dseq * D_MODEL), F32),
                   jax.ShapeDtypeStruct((ns, (CONV_W - 1) * 2 * D_FF), F32)],
        scratch_shapes=[pltpu.VMEM((ns * dseq, D_FF), BF16)],
        compiler_params=pltpu.CompilerParams(
            dimension_semantics=("arbitrary",), vmem_limit_bytes=VMEM_LIMIT_BYTES),
        name="ffn_sample",
    )(x1_s.reshape(ns, dseq * D_MODEL), p_sample[0].reshape(ns, dseq * PLE_DIM),
      state_ffn_conv[0].reshape(ns, (CONV_W - 1) * 2 * D_FF), *ffn_weights)

    return (y_p,
            y_s.reshape(ns, dseq, D_MODEL),
            gla_p[None],
            gla_s.reshape(1, ns, GLA_HEADS, GLA_DK, GLA_DV),
            conv_p[None],
            conv_s.reshape(1, ns, CONV_W - 1, 2 * D_FF),
            vrows_s.reshape(1, ns, dseq, GMLP_WIDTH))
```

```python
import functools

import numpy as np
import jax
import jax.numpy as jnp
from jax import lax
from jax.experimental import pallas as pl
from jax.experimental.pallas import tpu as pltpu

D_MODEL = 1024
GLA_HEADS = 4
GLA_DK = 64
GLA_DV = 128
GLA_KEY = GLA_HEADS * GLA_DK
GLA_WIDTH = GLA_HEADS * GLA_DV
GMLP_HEADS = 4
GMLP_DC = 128
GMLP_WIDTH = GMLP_HEADS * GMLP_DC
GLA_GATE_RANK = 16
GLA_TAU = 16.0
GLA_CHUNK = 64
GMLP_CHUNK = 128
D_FF = 2816
CONV_W = 3
PLE_DIM = 256
EPS = 1e-6

LANES = 128
SUBLANES = 8
MXU_N = 256
HEAD_PAIRS = GLA_HEADS // 2

COL_QK = 0
COL_V = COL_QK + 2 * GLA_KEY
COL_R = COL_V + GLA_WIDTH
IN_COLS_GLA = COL_R + GLA_WIDTH
COL_U = 0
COL_VS = COL_U + GMLP_WIDTH
COL_GLR = COL_VS + GMLP_WIDTH
IN_COLS_SGU = COL_GLR + LANES

FF_BLOCK = 256
N_FF_BLOCKS = D_FF // FF_BLOCK
FF_GROUP = 4

PROMPT_TILE = 256
SAMPLE_TILE = 128

VMEM_LIMIT_BYTES = 56 * 1024 * 1024

BF16 = jnp.bfloat16
F32 = jnp.float32


def _dot(a, b):
    return jnp.dot(a, b, preferred_element_type=F32)


def _dot_wide(a, w_ref, rows=slice(None)):
    return jnp.concatenate(
        [_dot(a, w_ref[rows, c:c + MXU_N]) for c in range(0, D_MODEL, MXU_N)], axis=-1)


def _dot_nt(a, b):
    return lax.dot_general(a, b, (((1,), (1,)), ((), ())), preferred_element_type=F32)


def _dot_tn(a, b):
    return lax.dot_general(a, b, (((0,), (0,)), ((), ())), preferred_element_type=F32)


def _rms(x, g):
    return x * lax.rsqrt(jnp.mean(x * x, axis=-1, keepdims=True) + EPS) * g


def _gelu_gate(x):
    c = -2.0 * np.sqrt(2.0 / np.pi) * np.log2(np.e)
    a1 = np.float32(c)
    a3 = np.float32(c * 0.044715)
    return 1.0 / (1.0 + jnp.exp2(x * (a3 * (x * x) + a1)))


def _gelu(x):
    return x * _gelu_gate(x)


def _sigmoid(x):
    return 1.0 / (1.0 + jnp.exp(-x))


def _log_sigmoid(x):
    return jnp.minimum(x, 0.0) - jnp.log1p(jnp.exp(-jnp.abs(x)))


def _split_bf16(x):
    hi = x.astype(BF16)
    lo = (x - hi.astype(F32)).astype(BF16)
    return hi, lo


def _mixer_kernel(*refs, rows, sample):
    if sample:
        (x_ref, sin_ref, g_pre_ref, w_ga_ref, w_sg_ref, w_gate_ref, b_gate_ref, g_gla_ref, g_ln_ref, b_ln_ref,
         w_sp_ref, b_sp_ref, w_out_ref, g_post_ref, cum_ref, tot_ref, causal_ref,
         y_ref, sout_ref, vn_ref, o_s) = refs
        st_s = None
    else:
        (x_ref, g_pre_ref, w_ga_ref, w_sg_ref, w_gate_ref, b_gate_ref, g_gla_ref, g_ln_ref, b_ln_ref,
         w_sp_ref, b_sp_ref, w_out_ref, g_post_ref, cum_ref, tot_ref, causal_ref,
         y_ref, sout_ref, o_s, st_s) = refs
        t = pl.program_id(1)

        @pl.when(t == 0)
        def _():
            st_s[...] = jnp.zeros_like(st_s)

    x = x_ref[...]
    h = _rms(x, g_pre_ref[...]).astype(BF16)

    u_pre = _dot(h, w_sg_ref[:, COL_U:COL_VS])
    vs_pre = _dot(h, w_sg_ref[:, COL_VS:COL_GLR])
    glr = _dot(h, w_sg_ref[:, COL_GLR:IN_COLS_SGU])
    qk = _dot(h, w_ga_ref[:, COL_QK:COL_V])
    v = _dot(h, w_ga_ref[:, COL_V:COL_R])
    r = _dot(h, w_ga_ref[:, COL_R:IN_COLS_GLA])

    u = _gelu(u_pre)
    vs = _gelu(vs_pre)
    vn_parts = []
    for i in range(GMLP_HEADS):
        vh = vs[:, i * GMLP_DC:(i + 1) * GMLP_DC]
        mu = jnp.mean(vh, axis=-1, keepdims=True)
        d = vh - mu
        var = jnp.mean(d * d, axis=-1, keepdims=True)
        vn_parts.append(d * lax.rsqrt(var + EPS))
    vn = jnp.concatenate(vn_parts, axis=-1) * g_ln_ref[...] + b_ln_ref[...]
    if sample:
        vn_ref[...] = vn
    vn_bf = vn.astype(BF16)
    mixed_parts = []
    for cch in range(rows // GMLP_CHUNK):
        crows = slice(cch * GMLP_CHUNK, (cch + 1) * GMLP_CHUNK)
        mixed_parts.append(jnp.concatenate(
            [_dot(w_sp_ref[i], vn_bf[crows, i * GMLP_DC:(i + 1) * GMLP_DC]) for i in range(GMLP_HEADS)],
            axis=-1) + b_sp_ref[...])
    so = u * jnp.concatenate(mixed_parts, axis=0)
    mix = _dot_wide(so.astype(BF16), w_out_ref, slice(GLA_WIDTH, GLA_WIDTH + GMLP_WIDTH))

    q = qk[:, :GLA_KEY] * (GLA_DK ** -0.5)
    k = qk[:, GLA_KEY:]
    v_bf = v.astype(BF16)
    logit = _dot(glr.astype(BF16), w_gate_ref[...]) + b_gate_ref[...]
    g_hi, g_lo = _split_bf16(_log_sigmoid(logit) * (1.0 / GLA_TAU))

    cum = cum_ref[...]
    tot = tot_ref[...]
    b = _dot(cum, g_hi) + _dot(cum, g_lo)
    bl = _dot(tot, g_hi) + _dot(tot, g_lo)
    qb = q * jnp.exp(b)
    kn = k * jnp.exp(-b)
    kd = k * jnp.exp(bl - b)
    causal = causal_ref[...]

    lane = lax.broadcasted_iota(jnp.int32, (rows, LANES), 1)
    head_lane_masks = (lane < GLA_DK, lane >= GLA_DK)

    for p in range(HEAD_PAIRS):
        ls = slice(p * LANES, (p + 1) * LANES)
        qbp = qb[:, ls]
        knp = kn[:, ls].astype(BF16)
        kdp = kd[:, ls]
        o_intra = []
        for hh in range(2):
            hd = 2 * p + hh
            qm = jnp.where(head_lane_masks[hh], qbp, 0.0).astype(BF16)
            s = _dot_nt(qm, knp) * causal
            o_intra.append(_dot(s.astype(BF16), v_bf[:, hd * GLA_DV:(hd + 1) * GLA_DV]))

        if not sample:
            st = st_s[p]
            ri = lax.broadcasted_iota(jnp.int32, st.shape, 0)
            ci = lax.broadcasted_iota(jnp.int32, st.shape, 1)
            same_head = (ri < GLA_DV) == (ci < GLA_DK)
            qbp_bf = qbp.astype(BF16)
            kdp_bf = kdp.astype(BF16)
            for c in range(rows // GLA_CHUNK):
                rws = slice(c * GLA_CHUNK, (c + 1) * GLA_CHUNK)
                o_inter = _dot_nt(qbp_bf[rws, :], st.astype(BF16))
                for hh in range(2):
                    hd = 2 * p + hh
                    o_s[rws, hd * GLA_DV:(hd + 1) * GLA_DV] = (
                        o_intra[hh][rws, :] + o_inter[:, hh * GLA_DV:(hh + 1) * GLA_DV])
                upd = _dot_tn(v_bf[rws, 2 * p * GLA_DV:(2 * p + 2) * GLA_DV], kdp_bf[rws, :])
                dec = jnp.exp(bl[c * GLA_CHUNK:c * GLA_CHUNK + 1, ls])
                st = st * dec + jnp.where(same_head, upd, 0.0)
            st_s[p] = st
        else:
            seq_len = 4
            seqs_per_grp = SUBLANES // seq_len
            kdp_t = kdp.T.astype(BF16)
            dec_t = jnp.exp(bl[:, ls]).T
            sub = lax.broadcasted_iota(jnp.int32, (SUBLANES, LANES), 0)
            crow = lax.broadcasted_iota(jnp.int32, (rows, LANES), 0)
            for grp in range(rows // SUBLANES):
                g0 = grp * SUBLANES
                qg = qbp[g0:g0 + SUBLANES, :]
                o_acc = [o_intra[hh][g0:g0 + SUBLANES, :] for hh in range(2)]
                for j in range(seqs_per_grp):
                    sidx = grp * seqs_per_grp + j
                    s_pair = sin_ref[sidx, p]
                    s_bf = s_pair.astype(BF16)
                    in_seq = (sub >= j * seq_len) & (sub < (j + 1) * seq_len)
                    in_seq_c = (crow >= g0 + j * seq_len) & (crow < g0 + (j + 1) * seq_len)
                    new_rows = []
                    for hh in range(2):
                        hd = 2 * p + hh
                        qsel = jnp.where(in_seq & head_lane_masks[hh][:SUBLANES], qg, 0.0).astype(BF16)
                        o_acc[hh] = o_acc[hh] + _dot(qsel, s_bf)
                        vsel = jnp.where(in_seq_c, v[:, hd * GLA_DV:(hd + 1) * GLA_DV], 0.0).astype(BF16)
                        new_rows.append(_dot(kdp_t[hh * GLA_DK:(hh + 1) * GLA_DK, :], vsel))
                    col = g0 + j * seq_len
                    sout_ref[sidx, p] = s_pair * dec_t[:, col:col + 1] + jnp.concatenate(new_rows, axis=0)
                for hh in range(2):
                    hd = 2 * p + hh
                    o_s[g0:g0 + SUBLANES, hd * GLA_DV:(hd + 1) * GLA_DV] = o_acc[hh]

    o = o_s[...]
    og = jnp.concatenate(
        [_rms(o[:, i * GLA_DV:(i + 1) * GLA_DV], 1.0) for i in range(GLA_HEADS)], axis=-1)
    og = og * g_gla_ref[...] * (r * _sigmoid(r))
    mix = mix + _dot_wide(og.astype(BF16), w_out_ref, slice(0, GLA_WIDTH))

    y_ref[...] = x + _rms(mix, g_post_ref[...])

    if not sample:
        @pl.when(t == pl.num_programs(1) - 1)
        def _():
            for p in range(HEAD_PAIRS):
                st = st_s[p]
                sout_ref[2 * p] = st[:GLA_DV, :].T[:GLA_DK, :]
                sout_ref[2 * p + 1] = st[GLA_DV:, :].T[GLA_DK:, :]


def _ffn_tail(x1, f, p, g_ffn_post, g_ple_in, w_pg_ref, w_ple_ref, g_ple_post):
    x2 = x1 + _rms(f, g_ffn_post)
    hp = _rms(x2, g_ple_in).astype(BF16)
    gate = _sigmoid(_dot_wide(hp, w_pg_ref))
    pe = _dot_wide(p.astype(BF16), w_ple_ref)
    return x2 + _rms(pe * gate, g_ple_post)


def _ff_groups():
    return [range(g, min(g + FF_GROUP, N_FF_BLOCKS)) for g in range(0, N_FF_BLOCKS, FF_GROUP)]


def _ffn_prompt_kernel(x_ref, p_ref, g_pre_ref, w_up_ref, w_conv_ref, b_conv_ref, w_down_ref,
                       g_post_ref, g_ple_in_ref, w_pg_ref, w_ple_ref, g_ple_post_ref,
                       y_ref, cs_ref,
                       carry_s, up_s, act_s, *, rows):
    t = pl.program_id(1)
    hdr = SUBLANES

    @pl.when(t == 0)
    def _():
        carry_s[...] = jnp.zeros_like(carry_s)
        up_s[:, :, :, hdr + rows:, :] = jnp.zeros((2, 2, CONV_W, hdr, FF_BLOCK), F32)

    x1 = x_ref[...]
    h = _rms(x1, g_pre_ref[...]).astype(BF16)

    def half_cols(j, half):
        c0 = half * D_FF + j * FF_BLOCK
        return slice(c0, c0 + FF_BLOCK)

    def project(j):
        for half in range(2):
            cols = half_cols(j, half)
            up = _dot(h, w_up_ref[:, cols])
            for s in range(CONV_W):
                buf = up_s.at[j % 2, half, s]
                if s:
                    buf[hdr:2 * hdr, :] = carry_s[s - 1, :, cols]
                buf[hdr + s:hdr + s + rows, :] = up
                if s:
                    carry_s[s - 1, :, cols] = buf[hdr + rows:2 * hdr + rows, :]

    def conv(j, half):
        cols = half_cols(j, half)
        w = w_conv_ref[:, cols]
        taps = [up_s[j % 2, half, s, hdr:hdr + rows, :] for s in range(CONV_W)]
        return (b_conv_ref[:, cols] + taps[2] * w[0:1, :] + taps[1] * w[1:2, :] + taps[0] * w[2:3, :])

    def down(j):
        k_rows = slice(j * FF_BLOCK, (j + 1) * FF_BLOCK)
        return _dot_wide(act_s[:, k_rows], w_down_ref, k_rows)

    project(0)
    f = None
    for j in range(N_FF_BLOCKS):
        if j + 1 < N_FF_BLOCKS:
            project(j + 1)
        if j == 1:
            f = down(0)
        elif j > 1:
            f = f + down(j - 1)
        gate = conv(j, 0)
        act_s[:, j * FF_BLOCK:(j + 1) * FF_BLOCK] = (
            (gate * conv(j, 1)) * _gelu_gate(gate)).astype(BF16)
    f = f + down(N_FF_BLOCKS - 1)

    y_ref[...] = _ffn_tail(x1, f, p_ref[...], g_post_ref[...], g_ple_in_ref[...], w_pg_ref,
                           w_ple_ref, g_ple_post_ref[...])

    @pl.when(t == pl.num_programs(1) - 1)
    def _():
        cs_ref[...] = carry_s[CONV_W - 2, 0:CONV_W - 1, :]


def _ffn_sample_kernel(x_ref, p_ref, cin_ref, g_pre_ref, w_up_ref, w_conv_ref, b_conv_ref, w_down_ref,
                       g_post_ref, g_ple_in_ref, w_pg_ref, w_ple_ref, g_ple_post_ref,
                       y_ref, cs_ref, act_s, *, nseq, seq_len):
    x1 = jnp.concatenate([x_ref[:, j, :] for j in range(seq_len)], axis=0)
    p = jnp.concatenate([p_ref[:, j, :] for j in range(seq_len)], axis=0)
    h = _rms(x1, g_pre_ref[...]).astype(BF16)

    def conv_half(col0):
        cols = slice(col0, col0 + FF_BLOCK)
        up = _dot(h, w_up_ref[:, cols])
        blocks = [cin_ref[:, i, cols] for i in range(CONV_W - 1)]
        blocks += [up[j * nseq:(j + 1) * nseq, :] for j in range(seq_len)]
        w = w_conv_ref[:, cols]
        b = b_conv_ref[:, cols]
        conv = jnp.concatenate(
            [b + blocks[j] * w[0:1, :] + blocks[j + 1] * w[1:2, :] + blocks[j + 2] * w[2:3, :]
             for j in range(seq_len)], axis=0)
        for i in range(CONV_W - 1):
            cs_ref[:, i, cols] = blocks[seq_len + i]
        return conv

    f = None
    for grp in _ff_groups():
        for j in grp:
            gate = conv_half(j * FF_BLOCK)
            val = conv_half(D_FF + j * FF_BLOCK)
            act_s[:, j * FF_BLOCK:(j + 1) * FF_BLOCK] = ((gate * val) * _gelu_gate(gate)).astype(BF16)
        k_rows = slice(grp[0] * FF_BLOCK, (grp[-1] + 1) * FF_BLOCK)
        part = _dot_wide(act_s[:, k_rows], w_down_ref, k_rows)
        f = part if f is None else f + part

    y = _ffn_tail(x1, f, p, g_post_ref[...], g_ple_in_ref[...], w_pg_ref, w_ple_ref,
                  g_ple_post_ref[...])
    for j in range(seq_len):
        y_ref[:, j, :] = y[j * nseq:(j + 1) * nseq, :]


def _full(shape):
    n = len(shape)
    return pl.BlockSpec(shape, lambda *_: (0,) * n)


def _gla_masks(rows, span):
    i = np.arange(rows)
    same = (i[:, None] // span) == (i[None, :] // span)
    causal = same & (i[None, :] <= i[:, None])
    return (jnp.asarray(causal, BF16), jnp.asarray(same, BF16), jnp.asarray(causal, F32))


def kernel(x_prompt, x_sample, state_gla, state_ffn_conv, p_prompt, p_sample, g_mix_pre, w_in, w_gla_gate, b_gla_gate, g_gla_out, g_sgu_ln, b_sgu_ln, w_spatial, b_spatial, w_out, g_mix_post, g_ffn_pre, w_up, w_conv, b_conv, w_down, g_ffn_post, g_ple_in, w_ple_gate, w_ple, g_ple_post):
    depth = x_prompt.ndim - 2
    assert w_in.shape[0] == 1 and depth == 1
    nb, seq, _ = x_prompt.shape
    ns, dseq, _ = x_sample.shape
    assert seq % PROMPT_TILE == 0 and PROMPT_TILE % GMLP_CHUNK == 0
    assert dseq == 4 and (ns * dseq) % SAMPLE_TILE == 0 and SAMPLE_TILE % GMLP_CHUNK == 0

    wi = w_in[0]
    sp = np.cumsum((GLA_KEY, GLA_KEY, GLA_WIDTH, GLA_WIDTH, GLA_GATE_RANK, GMLP_WIDTH, GMLP_WIDTH))
    w_ga = wi[:, :sp[3]].astype(BF16)
    w_sg = jnp.concatenate(
        [wi[:, sp[4]:], wi[:, sp[3]:sp[4]],
         jnp.zeros((D_MODEL, LANES - GLA_GATE_RANK), wi.dtype)], axis=1).astype(BF16)
    w_gate_p = jnp.concatenate(
        [w_gla_gate[0], jnp.zeros((LANES - GLA_GATE_RANK, GLA_KEY), w_gla_gate.dtype)], axis=0).astype(BF16)
    row = lambda a: a.reshape(1, -1)
    g_gla_t = jnp.tile(g_gla_out[0], GLA_HEADS).reshape(1, -1)
    pad_cols = lambda w: jnp.pad(w, ((0, 0), (0, LANES))).astype(BF16)
    w_out_b = pad_cols(w_out[0])
    w_up_b = w_up[0].astype(BF16)
    w_down_b = pad_cols(w_down[0])
    w_pg_b = pad_cols(w_ple_gate[0])
    w_ple_b = pad_cols(w_ple[0])

    def spatial(c):
        i = np.arange(GMLP_CHUNK)
        keep = ((i[:, None] // c) == (i[None, :] // c)) & (i[None, :] <= i[:, None])
        e = jnp.asarray((i[:, None] % c) == np.arange(c)[None, :], w_spatial.dtype)
        hi = lax.Precision.HIGHEST
        wbd = jnp.einsum('ia,hab,jb->hij', e, w_spatial[0][:, :c, :c], e, precision=hi)
        wbd = wbd * jnp.asarray(keep, w_spatial.dtype)
        bias = jnp.einsum('ia,ha->ih', e, b_spatial[0][:, :c], precision=hi)
        return wbd.astype(BF16), jnp.repeat(bias, GMLP_DC, axis=1)

    mixer_weights = lambda wsp, bsp, masks: (
        row(g_mix_pre), w_ga, w_sg, w_gate_p, row(b_gla_gate), g_gla_t, row(g_sgu_ln), row(b_sgu_ln),
        wsp, bsp, w_out_b, row(g_mix_post)) + masks

    def mixer_weight_specs(rows):
        return [
            _full((1, D_MODEL)), _full((D_MODEL, IN_COLS_GLA)), _full((D_MODEL, IN_COLS_SGU)),
            _full((LANES, GLA_KEY)),
            _full((1, GLA_KEY)), _full((1, GLA_WIDTH)), _full((1, GMLP_WIDTH)), _full((1, GMLP_WIDTH)),
            _full((GMLP_HEADS, GMLP_CHUNK, GMLP_CHUNK)), _full((GMLP_CHUNK, GMLP_WIDTH)),
            _full((D_MODEL, D_MODEL + LANES)), _full((1, D_MODEL)),
            _full((rows, rows)), _full((rows, rows)), _full((rows, rows))]

    tl = PROMPT_TILE
    wsp_p, bsp_p = spatial(GMLP_CHUNK)
    x1_p, gla_p = pl.pallas_call(
        functools.partial(_mixer_kernel, rows=tl, sample=False),
        grid=(nb, seq // tl),
        in_specs=[pl.BlockSpec((None, tl, D_MODEL), lambda b, t: (b, t, 0))] + mixer_weight_specs(tl),
        out_specs=[pl.BlockSpec((None, tl, D_MODEL), lambda b, t: (b, t, 0)),
                   pl.BlockSpec((None, GLA_HEADS, GLA_DK, GLA_DV), lambda b, t: (b, 0, 0, 0))],
        out_shape=[jax.ShapeDtypeStruct((nb, seq, D_MODEL), F32),
                   jax.ShapeDtypeStruct((nb, GLA_HEADS, GLA_DK, GLA_DV), F32)],
        scratch_shapes=[pltpu.VMEM((tl, GLA_WIDTH), F32),
                        pltpu.VMEM((HEAD_PAIRS, 2 * GLA_DV, 2 * GLA_DK), F32)],
        compiler_params=pltpu.CompilerParams(
            dimension_semantics=("arbitrary", "arbitrary"), vmem_limit_bytes=VMEM_LIMIT_BYTES),
        name="mixer_prompt",
    )(x_prompt, *mixer_weights(wsp_p, bsp_p, _gla_masks(tl, GLA_CHUNK)))

    ts = SAMPLE_TILE
    seqs_per_tile = ts // dseq
    wsp_s, bsp_s = spatial(dseq)
    x1_s, gla_s, vrows_s = pl.pallas_call(
        functools.partial(_mixer_kernel, rows=ts, sample=True),
        grid=(ns * dseq // ts,),
        in_specs=[pl.BlockSpec((ts, D_MODEL), lambda i: (i, 0)),
                  pl.BlockSpec((seqs_per_tile, HEAD_PAIRS, 2 * GLA_DK, GLA_DV), lambda i: (i, 0, 0, 0))]
        + mixer_weight_specs(ts),
        out_specs=[pl.BlockSpec((ts, D_MODEL), lambda i: (i, 0)),
                   pl.BlockSpec((seqs_per_tile, HEAD_PAIRS, 2 * GLA_DK, GLA_DV), lambda i: (i, 0, 0, 0)),
                   pl.BlockSpec((ts, GMLP_WIDTH), lambda i: (i, 0))],
        out_shape=[jax.ShapeDtypeStruct((ns * dseq, D_MODEL), F32),
                   jax.ShapeDtypeStruct((ns, HEAD_PAIRS, 2 * GLA_DK, GLA_DV), F32),
                   jax.ShapeDtypeStruct((ns * dseq, GMLP_WIDTH), F32)],
        scratch_shapes=[pltpu.VMEM((ts, GLA_WIDTH), F32)],
        compiler_params=pltpu.CompilerParams(
            dimension_semantics=("arbitrary",), vmem_limit_bytes=VMEM_LIMIT_BYTES),
        name="mixer_sample",
    )(x_sample.reshape(ns * dseq, D_MODEL),
      state_gla[0].reshape(ns, HEAD_PAIRS, 2 * GLA_DK, GLA_DV),
      *mixer_weights(wsp_s, bsp_s, _gla_masks(ts, dseq)))

    ffn_weights = (row(g_ffn_pre), w_up_b, w_conv[0], row(b_conv), w_down_b, row(g_ffn_post),
                   row(g_ple_in), w_pg_b, w_ple_b, row(g_ple_post))
    ffn_weight_specs = [
        _full((1, D_MODEL)), _full((D_MODEL, 2 * D_FF)), _full((CONV_W, 2 * D_FF)), _full((1, 2 * D_FF)),
        _full((D_FF, D_MODEL + LANES)), _full((1, D_MODEL)), _full((1, D_MODEL)),
        _full((D_MODEL, D_MODEL + LANES)), _full((PLE_DIM, D_MODEL + LANES)), _full((1, D_MODEL))]

    y_p, conv_p = pl.pallas_call(
        functools.partial(_ffn_prompt_kernel, rows=tl),
        grid=(nb, seq // tl),
        in_specs=[pl.BlockSpec((None, tl, D_MODEL), lambda b, t: (b, t, 0)),
                  pl.BlockSpec((None, tl, PLE_DIM), lambda b, t: (b, t, 0))] + ffn_weight_specs,
        out_specs=[pl.BlockSpec((None, tl, D_MODEL), lambda b, t: (b, t, 0)),
                   pl.BlockSpec((None, CONV_W - 1, 2 * D_FF), lambda b, t: (b, 0, 0))],
        out_shape=[jax.ShapeDtypeStruct((nb, seq, D_MODEL), F32),
                   jax.ShapeDtypeStruct((nb, CONV_W - 1, 2 * D_FF), F32)],
        scratch_shapes=[pltpu.VMEM((CONV_W - 1, SUBLANES, 2 * D_FF), F32),
                        pltpu.VMEM((2, 2, CONV_W, tl + 2 * SUBLANES, FF_BLOCK), F32),
                        pltpu.VMEM((tl, D_FF), BF16)],
        compiler_params=pltpu.CompilerParams(
            dimension_semantics=("arbitrary", "arbitrary"), vmem_limit_bytes=VMEM_LIMIT_BYTES),
        name="ffn_prompt",
    )(x1_p, p_prompt[0], *ffn_weights)

    y_s, conv_s = pl.pallas_call(
        functools.partial(_ffn_sample_kernel, nseq=ns, seq_len=dseq),
        grid=(1,),
        in_specs=[_full((ns, dseq, D_MODEL)), _full((ns, dseq, PLE_DIM)),
                  _full((ns, CONV_W - 1, 2 * D_FF))] + ffn_weight_specs,
        out_specs=[_full((ns, dseq, D_MODEL)), _full((ns, CONV_W - 1, 2 * D_FF))],
        out_shape=[jax.ShapeDtypeStruct((ns, dseq, D_MODEL), F32),
                   jax.ShapeDtypeStruct((ns, CONV_W - 1, 2 * D_FF), F32)],
        scratch_shapes=[pltpu.VMEM((ns * dseq, D_FF), BF16)],
        compiler_params=pltpu.CompilerParams(
            dimension_semantics=("arbitrary",), vmem_limit_bytes=VMEM_LIMIT_BYTES),
        name="ffn_sample",
    )(x1_s.reshape(ns, dseq, D_MODEL), p_sample[0], state_ffn_conv[0], *ffn_weights)

    return (y_p,
            y_s,
            gla_p[None],
            gla_s.reshape(1, ns, GLA_HEADS, GLA_DK, GLA_DV),
            conv_p[None],
            conv_s[None],
            vrows_s.reshape(1, ns, dseq, GMLP_WIDTH))
```

```python
import functools

import numpy as np
import jax
import jax.numpy as jnp
from jax import lax
from jax.experimental import pallas as pl
from jax.experimental.pallas import tpu as pltpu

D_MODEL = 1024
GLA_HEADS = 4
GLA_DK = 64
GLA_DV = 128
GLA_KEY = GLA_HEADS * GLA_DK
GLA_WIDTH = GLA_HEADS * GLA_DV
GMLP_HEADS = 4
GMLP_DC = 128
GMLP_WIDTH = GMLP_HEADS * GMLP_DC
GLA_GATE_RANK = 16
GLA_TAU = 16.0
GLA_CHUNK = 64
GMLP_CHUNK = 128
D_FF = 2816
CONV_W = 3
PLE_DIM = 256
EPS = 1e-6

LANES = 128
SUBLANES = 8
MXU_N = 256
HEAD_PAIRS = GLA_HEADS // 2

COL_QK = 0
COL_V = COL_QK + 2 * GLA_KEY
COL_R = COL_V + GLA_WIDTH
IN_COLS_GLA = COL_R + GLA_WIDTH
COL_U = 0
COL_VS = COL_U + GMLP_WIDTH
COL_GLR = COL_VS + GMLP_WIDTH
IN_COLS_SGU = COL_GLR + LANES

FF_BLOCK = 256
N_FF_BLOCKS = D_FF // FF_BLOCK
FF_GROUP = 4

PROMPT_TILE = 256
FFN_PROMPT_TILE = 256
SAMPLE_TILE = 128

VMEM_LIMIT_BYTES = 56 * 1024 * 1024

BF16 = jnp.bfloat16
F32 = jnp.float32


def _dot(a, b):
    return jnp.dot(a, b, preferred_element_type=F32)


def _dot_wide(a, w_ref, rows=slice(None)):
    return jnp.concatenate(
        [_dot(a, w_ref[rows, c:c + MXU_N]) for c in range(0, D_MODEL, MXU_N)], axis=-1)


def _dot_nt(a, b):
    return lax.dot_general(a, b, (((1,), (1,)), ((), ())), preferred_element_type=F32)


def _dot_tn(a, b):
    return lax.dot_general(a, b, (((0,), (0,)), ((), ())), preferred_element_type=F32)


def _rms(x, g):
    return x * lax.rsqrt(jnp.mean(x * x, axis=-1, keepdims=True) + EPS) * g


def _gelu_gate(x):
    c = -2.0 * np.sqrt(2.0 / np.pi) * np.log2(np.e)
    a1 = np.float32(c)
    a3 = np.float32(c * 0.044715)
    return 1.0 / (1.0 + jnp.exp2(x * (a3 * (x * x) + a1)))


def _gelu(x):
    return x * _gelu_gate(x)


def _sigmoid(x):
    return 1.0 / (1.0 + jnp.exp(-x))


def _log_sigmoid(x):
    return jnp.minimum(x, 0.0) - jnp.log1p(jnp.exp(-jnp.abs(x)))


def _split_bf16(x):
    hi = x.astype(BF16)
    lo = (x - hi.astype(F32)).astype(BF16)
    return hi, lo


def _mixer_kernel(*refs, rows, sample):
    if sample:
        (x_ref, sin_ref, g_pre_ref, w_ga_ref, w_sg_ref, w_gate_ref, b_gate_ref, g_gla_ref, g_ln_ref, b_ln_ref,
         w_sp_ref, b_sp_ref, w_out_ref, g_post_ref, cum_ref, tot_ref, causal_ref,
         y_ref, sout_ref, vn_ref, o_s) = refs
        st_s = None
    else:
        (x_ref, g_pre_ref, w_ga_ref, w_sg_ref, w_gate_ref, b_gate_ref, g_gla_ref, g_ln_ref, b_ln_ref,
         w_sp_ref, b_sp_ref, w_out_ref, g_post_ref, cum_ref, tot_ref, causal_ref,
         y_ref, sout_ref, o_s, st_s) = refs
        t = pl.program_id(1)

        @pl.when(t == 0)
        def _():
            st_s[...] = jnp.zeros_like(st_s)

    x = x_ref[...]
    h = _rms(x, g_pre_ref[...]).astype(BF16)

    u_pre = _dot(h, w_sg_ref[:, COL_U:COL_VS])
    vs_pre = _dot(h, w_sg_ref[:, COL_VS:COL_GLR])
    glr = _dot(h, w_sg_ref[:, COL_GLR:IN_COLS_SGU])
    qk = _dot(h, w_ga_ref[:, COL_QK:COL_V])
    v = _dot(h, w_ga_ref[:, COL_V:COL_R])
    r = _dot(h, w_ga_ref[:, COL_R:IN_COLS_GLA])

    u = _gelu(u_pre)
    vs = _gelu(vs_pre)
    vn_parts = []
    for i in range(GMLP_HEADS):
        vh = vs[:, i * GMLP_DC:(i + 1) * GMLP_DC]
        mu = jnp.mean(vh, axis=-1, keepdims=True)
        d = vh - mu
        var = jnp.mean(d * d, axis=-1, keepdims=True)
        vn_parts.append(d * lax.rsqrt(var + EPS))
    vn = jnp.concatenate(vn_parts, axis=-1) * g_ln_ref[...] + b_ln_ref[...]
    if sample:
        vn_ref[...] = vn
    vn_bf = vn.astype(BF16)

    logit = _dot(glr.astype(BF16), w_gate_ref[...]) + b_gate_ref[...]
    g_hi, g_lo = _split_bf16(_log_sigmoid(logit) * (1.0 / GLA_TAU))

    mixed_parts = []
    for cch in range(rows // GMLP_CHUNK):
        crows = slice(cch * GMLP_CHUNK, (cch + 1) * GMLP_CHUNK)
        mixed_parts.append(jnp.concatenate(
            [_dot(w_sp_ref[i], vn_bf[crows, i * GMLP_DC:(i + 1) * GMLP_DC]) for i in range(GMLP_HEADS)],
            axis=-1) + b_sp_ref[...])

    cum = cum_ref[...]
    tot = tot_ref[...]
    b = _dot(cum, g_hi) + _dot(cum, g_lo)
    bl = _dot(tot, g_hi) + _dot(tot, g_lo)

    so = u * jnp.concatenate(mixed_parts, axis=0)
    mix = _dot_wide(so.astype(BF16), w_out_ref, slice(GLA_WIDTH, GLA_WIDTH + GMLP_WIDTH))

    q = qk[:, :GLA_KEY] * (GLA_DK ** -0.5)
    k = qk[:, GLA_KEY:]
    v_bf = v.astype(BF16)
    qb = q * jnp.exp(b)
    kn = k * jnp.exp(-b)
    kd = k * jnp.exp(bl - b)
    causal = causal_ref[...]

    lane = lax.broadcasted_iota(jnp.int32, (rows, LANES), 1)
    head_lane_masks = (lane < GLA_DK, lane >= GLA_DK)

    prompt_pairs = []
    for p in range(HEAD_PAIRS):
        ls = slice(p * LANES, (p + 1) * LANES)
        qbp = qb[:, ls]
        knp = kn[:, ls].astype(BF16)
        kdp = kd[:, ls]
        o_intra = []
        for hh in range(2):
            hd = 2 * p + hh
            qm = jnp.where(head_lane_masks[hh], qbp, 0.0).astype(BF16)
            s = _dot_nt(qm, knp) * causal
            o_intra.append(_dot(s.astype(BF16), v_bf[:, hd * GLA_DV:(hd + 1) * GLA_DV]))

        if not sample:
            st_shape = (2 * GLA_DV, 2 * GLA_DK)
            ri = lax.broadcasted_iota(jnp.int32, st_shape, 0)
            ci = lax.broadcasted_iota(jnp.int32, st_shape, 1)
            same_head = (ri < GLA_DV) == (ci < GLA_DK)
            kdp_bf = kdp.astype(BF16)
            upd = []
            for c in range(rows // GLA_CHUNK):
                rws = slice(c * GLA_CHUNK, (c + 1) * GLA_CHUNK)
                inc = _dot_tn(v_bf[rws, 2 * p * GLA_DV:(2 * p + 2) * GLA_DV], kdp_bf[rws, :])
                upd.append(jnp.where(same_head, inc, 0.0))
            prompt_pairs.append((ls, qbp.astype(BF16), o_intra, upd))
        else:
            seq_len = 4
            seqs_per_grp = SUBLANES // seq_len
            kdp_t = kdp.T.astype(BF16)
            dec_t = jnp.exp(bl[:, ls]).T
            sub = lax.broadcasted_iota(jnp.int32, (SUBLANES, LANES), 0)
            crow = lax.broadcasted_iota(jnp.int32, (rows, LANES), 0)
            for grp in range(rows // SUBLANES):
                g0 = grp * SUBLANES
                qg = qbp[g0:g0 + SUBLANES, :]
                o_acc = [o_intra[hh][g0:g0 + SUBLANES, :] for hh in range(2)]
                for j in range(seqs_per_grp):
                    sidx = grp * seqs_per_grp + j
                    s_pair = sin_ref[sidx, p]
                    s_bf = s_pair.astype(BF16)
                    in_seq = (sub >= j * seq_len) & (sub < (j + 1) * seq_len)
                    in_seq_c = (crow >= g0 + j * seq_len) & (crow < g0 + (j + 1) * seq_len)
                    new_rows = []
                    for hh in range(2):
                        hd = 2 * p + hh
                        qsel = jnp.where(in_seq & head_lane_masks[hh][:SUBLANES], qg, 0.0).astype(BF16)
                        o_acc[hh] = o_acc[hh] + _dot(qsel, s_bf)
                        vsel = jnp.where(in_seq_c, v[:, hd * GLA_DV:(hd + 1) * GLA_DV], 0.0).astype(BF16)
                        new_rows.append(_dot(kdp_t[hh * GLA_DK:(hh + 1) * GLA_DK, :], vsel))
                    col = g0 + j * seq_len
                    sout_ref[sidx, p] = s_pair * dec_t[:, col:col + 1] + jnp.concatenate(new_rows, axis=0)
                for hh in range(2):
                    hd = 2 * p + hh
                    o_s[g0:g0 + SUBLANES, hd * GLA_DV:(hd + 1) * GLA_DV] = o_acc[hh]

    if not sample:
        st = [st_s[p] for p in range(HEAD_PAIRS)]
        for c in range(rows // GLA_CHUNK):
            rws = slice(c * GLA_CHUNK, (c + 1) * GLA_CHUNK)
            for p, (ls, qbp_bf, o_intra, upd) in enumerate(prompt_pairs):
                o_inter = _dot_nt(qbp_bf[rws, :], st[p].astype(BF16))
                for hh in range(2):
                    hd = 2 * p + hh
                    o_s[rws, hd * GLA_DV:(hd + 1) * GLA_DV] = (
                        o_intra[hh][rws, :] + o_inter[:, hh * GLA_DV:(hh + 1) * GLA_DV])
                dec = jnp.exp(bl[c * GLA_CHUNK:c * GLA_CHUNK + 1, ls])
                st[p] = st[p] * dec + upd[c]
        for p in range(HEAD_PAIRS):
            st_s[p] = st[p]

    o = o_s[...]
    og = jnp.concatenate(
        [_rms(o[:, i * GLA_DV:(i + 1) * GLA_DV], 1.0) for i in range(GLA_HEADS)], axis=-1)
    og = og * g_gla_ref[...] * (r * _sigmoid(r))
    mix = mix + _dot_wide(og.astype(BF16), w_out_ref, slice(0, GLA_WIDTH))

    y_ref[...] = x + _rms(mix, g_post_ref[...])

    if not sample:
        @pl.when(t == pl.num_programs(1) - 1)
        def _():
            for p in range(HEAD_PAIRS):
                st = st_s[p]
                sout_ref[2 * p] = st[:GLA_DV, :].T[:GLA_DK, :]
                sout_ref[2 * p + 1] = st[GLA_DV:, :].T[GLA_DK:, :]


def _ffn_tail(x1, f, p, g_ffn_post, g_ple_in, w_pg_ref, w_ple_ref, g_ple_post):
    x2 = x1 + _rms(f, g_ffn_post)
    hp = _rms(x2, g_ple_in).astype(BF16)
    gate = _sigmoid(_dot_wide(hp, w_pg_ref))
    pe = _dot_wide(p.astype(BF16), w_ple_ref)
    return x2 + _rms(pe * gate, g_ple_post)


def _ff_groups():
    return [range(g, min(g + FF_GROUP, N_FF_BLOCKS)) for g in range(0, N_FF_BLOCKS, FF_GROUP)]


def _ffn_prompt_kernel(x_ref, p_ref, g_pre_ref, w_up_ref, w_conv_ref, b_conv_ref, w_down_ref,
                       g_post_ref, g_ple_in_ref, w_pg_ref, w_ple_ref, g_ple_post_ref,
                       y_ref, cs_ref,
                       carry_s, up_s, act_s, *, rows):
    t = pl.program_id(1)
    hdr = SUBLANES

    @pl.when(t == 0)
    def _():
        carry_s[...] = jnp.zeros_like(carry_s)
        up_s[:, :, :, hdr + rows:, :] = jnp.zeros((2, 2, CONV_W, hdr, FF_BLOCK), F32)

    x1 = x_ref[...]
    h = _rms(x1, g_pre_ref[...]).astype(BF16)

    def half_cols(j, half):
        c0 = half * D_FF + j * FF_BLOCK
        return slice(c0, c0 + FF_BLOCK)

    def project(j):
        for half in range(2):
            cols = half_cols(j, half)
            up = _dot(h, w_up_ref[:, cols])
            for s in range(CONV_W):
                buf = up_s.at[j % 2, half, s]
                if s:
                    buf[hdr:2 * hdr, :] = carry_s[s - 1, :, cols]
                buf[hdr + s:hdr + s + rows, :] = up
                if s:
                    carry_s[s - 1, :, cols] = buf[hdr + rows:2 * hdr + rows, :]

    def conv(j, half):
        cols = half_cols(j, half)
        w = w_conv_ref[:, cols]
        taps = [up_s[j % 2, half, s, hdr:hdr + rows, :] for s in range(CONV_W)]
        return (b_conv_ref[:, cols] + taps[2] * w[0:1, :] + taps[1] * w[1:2, :] + taps[0] * w[2:3, :])

    def down(j):
        k_rows = slice(j * FF_BLOCK, (j + 1) * FF_BLOCK)
        return _dot_wide(act_s[:, k_rows], w_down_ref, k_rows)

    project(0)
    f = None
    for j in range(N_FF_BLOCKS):
        if j + 1 < N_FF_BLOCKS:
            project(j + 1)
        if j == 1:
            f = down(0)
        elif j > 1:
            f = f + down(j - 1)
        gate = conv(j, 0)
        act_s[:, j * FF_BLOCK:(j + 1) * FF_BLOCK] = (
            (gate * conv(j, 1)) * _gelu_gate(gate)).astype(BF16)
    f = f + down(N_FF_BLOCKS - 1)

    y_ref[...] = _ffn_tail(x1, f, p_ref[...], g_post_ref[...], g_ple_in_ref[...], w_pg_ref,
                           w_ple_ref, g_ple_post_ref[...])

    @pl.when(t == pl.num_programs(1) - 1)
    def _():
        cs_ref[...] = carry_s[CONV_W - 2, 0:CONV_W - 1, :]


def _ffn_sample_kernel(x_ref, p_ref, cin_ref, g_pre_ref, w_up_ref, w_conv_ref, b_conv_ref, w_down_ref,
                       g_post_ref, g_ple_in_ref, w_pg_ref, w_ple_ref, g_ple_post_ref,
                       y_ref, cs_ref, act_s, *, nseq, seq_len):
    x1 = jnp.concatenate([x_ref[:, j, :] for j in range(seq_len)], axis=0)
    p = jnp.concatenate([p_ref[:, j, :] for j in range(seq_len)], axis=0)
    h = _rms(x1, g_pre_ref[...]).astype(BF16)

    def conv_half(col0):
        cols = slice(col0, col0 + FF_BLOCK)
        up = _dot(h, w_up_ref[:, cols])
        blocks = [cin_ref[:, i, cols] for i in range(CONV_W - 1)]
        blocks += [up[j * nseq:(j + 1) * nseq, :] for j in range(seq_len)]
        w = w_conv_ref[:, cols]
        b = b_conv_ref[:, cols]
        conv = jnp.concatenate(
            [b + blocks[j] * w[0:1, :] + blocks[j + 1] * w[1:2, :] + blocks[j + 2] * w[2:3, :]
             for j in range(seq_len)], axis=0)
        for i in range(CONV_W - 1):
            cs_ref[:, i, cols] = blocks[seq_len + i]
        return conv

    f = None
    for grp in _ff_groups():
        for j in grp:
            gate = conv_half(j * FF_BLOCK)
            val = conv_half(D_FF + j * FF_BLOCK)
            act_s[:, j * FF_BLOCK:(j + 1) * FF_BLOCK] = ((gate * val) * _gelu_gate(gate)).astype(BF16)
        k_rows = slice(grp[0] * FF_BLOCK, (grp[-1] + 1) * FF_BLOCK)
        part = _dot_wide(act_s[:, k_rows], w_down_ref, k_rows)
        f = part if f is None else f + part

    y = _ffn_tail(x1, f, p, g_post_ref[...], g_ple_in_ref[...], w_pg_ref, w_ple_ref,
                  g_ple_post_ref[...])
    for j in range(seq_len):
        y_ref[:, j, :] = y[j * nseq:(j + 1) * nseq, :]


def _whole(shape):
    n = len(shape)
    return pl.BlockSpec(shape, lambda *_: (0,) * n)


def _full(shape):
    n = len(shape)
    return pl.BlockSpec(shape, lambda *_: (0,) * n, pipeline_mode=pl.Buffered(1))


def _gla_masks(rows, span):
    i = np.arange(rows)
    same = (i[:, None] // span) == (i[None, :] // span)
    causal = same & (i[None, :] <= i[:, None])
    return (jnp.asarray(causal, BF16), jnp.asarray(same, BF16), jnp.asarray(causal, F32))


def kernel(x_prompt, x_sample, state_gla, state_ffn_conv, p_prompt, p_sample, g_mix_pre, w_in, w_gla_gate, b_gla_gate, g_gla_out, g_sgu_ln, b_sgu_ln, w_spatial, b_spatial, w_out, g_mix_post, g_ffn_pre, w_up, w_conv, b_conv, w_down, g_ffn_post, g_ple_in, w_ple_gate, w_ple, g_ple_post):
    depth = x_prompt.ndim - 2
    assert w_in.shape[0] == 1 and depth == 1
    nb, seq, _ = x_prompt.shape
    ns, dseq, _ = x_sample.shape
    assert seq % PROMPT_TILE == 0 and PROMPT_TILE % GMLP_CHUNK == 0 and seq % FFN_PROMPT_TILE == 0
    assert dseq == 4 and (ns * dseq) % SAMPLE_TILE == 0 and SAMPLE_TILE % GMLP_CHUNK == 0

    wi = w_in[0]
    sp = np.cumsum((GLA_KEY, GLA_KEY, GLA_WIDTH, GLA_WIDTH, GLA_GATE_RANK, GMLP_WIDTH, GMLP_WIDTH))
    w_ga = wi[:, :sp[3]].astype(BF16)
    w_sg = jnp.concatenate(
        [wi[:, sp[4]:], wi[:, sp[3]:sp[4]],
         jnp.zeros((D_MODEL, LANES - GLA_GATE_RANK), wi.dtype)], axis=1).astype(BF16)
    w_gate_p = jnp.concatenate(
        [w_gla_gate[0], jnp.zeros((LANES - GLA_GATE_RANK, GLA_KEY), w_gla_gate.dtype)], axis=0).astype(BF16)
    row = lambda a: a.reshape(1, -1)
    g_gla_t = jnp.tile(g_gla_out[0], GLA_HEADS).reshape(1, -1)
    pad_cols = lambda w: jnp.pad(w, ((0, 0), (0, LANES))).astype(BF16)
    w_out_b = pad_cols(w_out[0])
    w_up_b = w_up[0].astype(BF16)
    w_down_b = pad_cols(w_down[0])
    w_pg_b = pad_cols(w_ple_gate[0])
    w_ple_b = pad_cols(w_ple[0])

    def spatial(c):
        i = np.arange(GMLP_CHUNK)
        keep = ((i[:, None] // c) == (i[None, :] // c)) & (i[None, :] <= i[:, None])
        e = jnp.asarray((i[:, None] % c) == np.arange(c)[None, :], w_spatial.dtype)
        hi = lax.Precision.HIGHEST
        wbd = jnp.einsum('ia,hab,jb->hij', e, w_spatial[0][:, :c, :c], e, precision=hi)
        wbd = wbd * jnp.asarray(keep, w_spatial.dtype)
        bias = jnp.einsum('ia,ha->ih', e, b_spatial[0][:, :c], precision=hi)
        return wbd.astype(BF16), jnp.repeat(bias, GMLP_DC, axis=1)

    mixer_weights = lambda wsp, bsp, masks: (
        row(g_mix_pre), w_ga, w_sg, w_gate_p, row(b_gla_gate), g_gla_t, row(g_sgu_ln), row(b_sgu_ln),
        wsp, bsp, w_out_b, row(g_mix_post)) + masks

    def mixer_weight_specs(rows):
        return [
            _full((1, D_MODEL)), _full((D_MODEL, IN_COLS_GLA)), _full((D_MODEL, IN_COLS_SGU)),
            _full((LANES, GLA_KEY)),
            _full((1, GLA_KEY)), _full((1, GLA_WIDTH)), _full((1, GMLP_WIDTH)), _full((1, GMLP_WIDTH)),
            _full((GMLP_HEADS, GMLP_CHUNK, GMLP_CHUNK)), _full((GMLP_CHUNK, GMLP_WIDTH)),
            _full((D_MODEL, D_MODEL + LANES)), _full((1, D_MODEL)),
            _full((rows, rows)), _full((rows, rows)), _full((rows, rows))]

    tl = PROMPT_TILE
    wsp_p, bsp_p = spatial(GMLP_CHUNK)
    x1_p, gla_p = pl.pallas_call(
        functools.partial(_mixer_kernel, rows=tl, sample=False),
        grid=(nb, seq // tl),
        in_specs=[pl.BlockSpec((None, tl, D_MODEL), lambda b, t: (b, t, 0))] + mixer_weight_specs(tl),
        out_specs=[pl.BlockSpec((None, tl, D_MODEL), lambda b, t: (b, t, 0)),
                   pl.BlockSpec((None, GLA_HEADS, GLA_DK, GLA_DV), lambda b, t: (b, 0, 0, 0))],
        out_shape=[jax.ShapeDtypeStruct((nb, seq, D_MODEL), F32),
                   jax.ShapeDtypeStruct((nb, GLA_HEADS, GLA_DK, GLA_DV), F32)],
        scratch_shapes=[pltpu.VMEM((tl, GLA_WIDTH), F32),
                        pltpu.VMEM((HEAD_PAIRS, 2 * GLA_DV, 2 * GLA_DK), F32)],
        compiler_params=pltpu.CompilerParams(
            dimension_semantics=("arbitrary", "arbitrary"), vmem_limit_bytes=VMEM_LIMIT_BYTES),
        name="mixer_prompt",
    )(x_prompt, *mixer_weights(wsp_p, bsp_p, _gla_masks(tl, GLA_CHUNK)))

    ts = SAMPLE_TILE
    seqs_per_tile = ts // dseq
    wsp_s, bsp_s = spatial(dseq)
    x1_s, gla_s, vrows_s = pl.pallas_call(
        functools.partial(_mixer_kernel, rows=ts, sample=True),
        grid=(ns * dseq // ts,),
        in_specs=[pl.BlockSpec((ts, D_MODEL), lambda i: (i, 0)),
                  pl.BlockSpec((seqs_per_tile, HEAD_PAIRS, 2 * GLA_DK, GLA_DV), lambda i: (i, 0, 0, 0))]
        + mixer_weight_specs(ts),
        out_specs=[pl.BlockSpec((ts, D_MODEL), lambda i: (i, 0)),
                   pl.BlockSpec((seqs_per_tile, HEAD_PAIRS, 2 * GLA_DK, GLA_DV), lambda i: (i, 0, 0, 0)),
                   pl.BlockSpec((ts, GMLP_WIDTH), lambda i: (i, 0))],
        out_shape=[jax.ShapeDtypeStruct((ns * dseq, D_MODEL), F32),
                   jax.ShapeDtypeStruct((ns, HEAD_PAIRS, 2 * GLA_DK, GLA_DV), F32),
                   jax.ShapeDtypeStruct((ns * dseq, GMLP_WIDTH), F32)],
        scratch_shapes=[pltpu.VMEM((ts, GLA_WIDTH), F32)],
        compiler_params=pltpu.CompilerParams(
            dimension_semantics=("arbitrary",), vmem_limit_bytes=VMEM_LIMIT_BYTES),
        name="mixer_sample",
    )(x_sample.reshape(ns * dseq, D_MODEL),
      state_gla[0].reshape(ns, HEAD_PAIRS, 2 * GLA_DK, GLA_DV),
      *mixer_weights(wsp_s, bsp_s, _gla_masks(ts, dseq)))

    ffn_weights = (row(g_ffn_pre), w_up_b, w_conv[0], row(b_conv), w_down_b, row(g_ffn_post),
                   row(g_ple_in), w_pg_b, w_ple_b, row(g_ple_post))
    ffn_weight_specs = [
        _full((1, D_MODEL)), _full((D_MODEL, 2 * D_FF)), _full((CONV_W, 2 * D_FF)), _full((1, 2 * D_FF)),
        _full((D_FF, D_MODEL + LANES)), _full((1, D_MODEL)), _full((1, D_MODEL)),
        _full((D_MODEL, D_MODEL + LANES)), _full((PLE_DIM, D_MODEL + LANES)), _full((1, D_MODEL))]

    tf = FFN_PROMPT_TILE
    y_p, conv_p = pl.pallas_call(
        functools.partial(_ffn_prompt_kernel, rows=tf),
        grid=(nb, seq // tf),
        in_specs=[pl.BlockSpec((None, tf, D_MODEL), lambda b, t: (b, t, 0)),
                  pl.BlockSpec((None, tf, PLE_DIM), lambda b, t: (b, t, 0))] + ffn_weight_specs,
        out_specs=[pl.BlockSpec((None, tf, D_MODEL), lambda b, t: (b, t, 0)),
                   pl.BlockSpec((None, CONV_W - 1, 2 * D_FF), lambda b, t: (b, 0, 0))],
        out_shape=[jax.ShapeDtypeStruct((nb, seq, D_MODEL), F32),
                   jax.ShapeDtypeStruct((nb, CONV_W - 1, 2 * D_FF), F32)],
        scratch_shapes=[pltpu.VMEM((CONV_W - 1, SUBLANES, 2 * D_FF), F32),
                        pltpu.VMEM((2, 2, CONV_W, tf + 2 * SUBLANES, FF_BLOCK), F32),
                        pltpu.VMEM((tf, D_FF), BF16)],
        compiler_params=pltpu.CompilerParams(
            dimension_semantics=("arbitrary", "arbitrary"), vmem_limit_bytes=VMEM_LIMIT_BYTES),
        name="ffn_prompt",
    )(x1_p, p_prompt[0], *ffn_weights)

    y_s, conv_s = pl.pallas_call(
        functools.partial(_ffn_sample_kernel, nseq=ns, seq_len=dseq),
        grid=(1,),
        in_specs=[_whole((ns, dseq, D_MODEL)), _whole((ns, dseq, PLE_DIM)),
                  _whole((ns, CONV_W - 1, 2 * D_FF))] + ffn_weight_specs,
        out_specs=[_whole((ns, dseq, D_MODEL)), _whole((ns, CONV_W - 1, 2 * D_FF))],
        out_shape=[jax.ShapeDtypeStruct((ns, dseq, D_MODEL), F32),
                   jax.ShapeDtypeStruct((ns, CONV_W - 1, 2 * D_FF), F32)],
        scratch_shapes=[pltpu.VMEM((ns * dseq, D_FF), BF16)],
        compiler_params=pltpu.CompilerParams(
            dimension_semantics=("arbitrary",), vmem_limit_bytes=VMEM_LIMIT_BYTES),
        name="ffn_sample",
    )(x1_s.reshape(ns, dseq, D_MODEL), p_sample[0], state_ffn_conv[0], *ffn_weights)

    return (y_p,
            y_s,
            gla_p[None],
            gla_s.reshape(1, ns, GLA_HEADS, GLA_DK, GLA_DV),
            conv_p[None],
            conv_s[None],
            vrows_s.reshape(1, ns, dseq, GMLP_WIDTH))
```

```python
import functools

import numpy as np
import jax
import jax.numpy as jnp
from jax import lax
from jax.experimental import pallas as pl
from jax.experimental.pallas import tpu as pltpu

D_MODEL = 1024
GLA_HEADS = 4
GLA_DK = 64
GLA_DV = 128
GLA_KEY = GLA_HEADS * GLA_DK
GLA_WIDTH = GLA_HEADS * GLA_DV
GMLP_HEADS = 4
GMLP_DC = 128
GMLP_WIDTH = GMLP_HEADS * GMLP_DC
GLA_GATE_RANK = 16
GLA_TAU = 16.0
GLA_CHUNK = 64
GMLP_CHUNK = 128
D_FF = 2816
CONV_W = 3
PLE_DIM = 256
EPS = 1e-6

LANES = 128
SUBLANES = 8
MXU_N = 256
HEAD_PAIRS = GLA_HEADS // 2

COL_QK = 0
COL_V = COL_QK + 2 * GLA_KEY
COL_R = COL_V + GLA_WIDTH
IN_COLS_GLA = COL_R + GLA_WIDTH
COL_U = 0
COL_VS = COL_U + GMLP_WIDTH
COL_GLR = COL_VS + GMLP_WIDTH
IN_COLS_SGU = COL_GLR + LANES

FF_BLOCK = 256
N_FF_BLOCKS = D_FF // FF_BLOCK
FF_GROUP = 4

PROMPT_TILE = 256
FFN_PROMPT_TILE = 256
SAMPLE_TILE = 128

VMEM_LIMIT_BYTES = 56 * 1024 * 1024

BF16 = jnp.bfloat16
F32 = jnp.float32


def _dot(a, b):
    return jnp.dot(a, b, preferred_element_type=F32)


def _dot_wide(a, w_ref, rows=slice(None)):
    return jnp.concatenate(
        [_dot(a, w_ref[rows, c:c + MXU_N]) for c in range(0, D_MODEL, MXU_N)], axis=-1)


def _dot_nt(a, b):
    return lax.dot_general(a, b, (((1,), (1,)), ((), ())), preferred_element_type=F32)


def _dot_tn(a, b):
    return lax.dot_general(a, b, (((0,), (0,)), ((), ())), preferred_element_type=F32)


def _rms(x, g):
    return x * lax.rsqrt(jnp.mean(x * x, axis=-1, keepdims=True) + EPS) * g


def _gelu_gate(x):
    c = -2.0 * np.sqrt(2.0 / np.pi) * np.log2(np.e)
    a1 = np.float32(c)
    a3 = np.float32(c * 0.044715)
    return 1.0 / (1.0 + jnp.exp2(x * (a3 * (x * x) + a1)))


def _gelu(x):
    return x * _gelu_gate(x)


def _sigmoid(x):
    return 1.0 / (1.0 + jnp.exp(-x))


def _log_sigmoid(x):
    return jnp.minimum(x, 0.0) - jnp.log1p(jnp.exp(-jnp.abs(x)))


def _split_bf16(x):
    hi = x.astype(BF16)
    lo = (x - hi.astype(F32)).astype(BF16)
    return hi, lo


def _mixer_kernel(*refs, rows, sample):
    if sample:
        (x_ref, sin_ref, g_pre_ref, w_ga_ref, w_sg_ref, w_gate_ref, b_gate_ref, g_gla_ref, g_ln_ref, b_ln_ref,
         w_sp_ref, b_sp_ref, w_out_ref, g_post_ref, cum_ref, tot_ref, causal_ref,
         y_ref, sout_ref, vn_ref, o_s) = refs
        st_s = None
    else:
        (x_ref, g_pre_ref, w_ga_ref, w_sg_ref, w_gate_ref, b_gate_ref, g_gla_ref, g_ln_ref, b_ln_ref,
         w_sp_ref, b_sp_ref, w_out_ref, g_post_ref, cum_ref, tot_ref, causal_ref,
         y_ref, sout_ref, o_s, st_s) = refs
        t = pl.program_id(1)

        @pl.when(t == 0)
        def _():
            st_s[...] = jnp.zeros_like(st_s)

    x = x_ref[...]
    h = _rms(x, g_pre_ref[...]).astype(BF16)

    glr = _dot(h, w_sg_ref[:, COL_GLR:IN_COLS_SGU])
    logit = _dot(glr.astype(BF16), w_gate_ref[...]) + b_gate_ref[...]
    u_pre = _dot(h, w_sg_ref[:, COL_U:COL_VS])
    vs_pre = _dot(h, w_sg_ref[:, COL_VS:COL_GLR])
    qk = _dot(h, w_ga_ref[:, COL_QK:COL_V])
    v = _dot(h, w_ga_ref[:, COL_V:COL_R])
    r = _dot(h, w_ga_ref[:, COL_R:IN_COLS_GLA])
    g_hi, g_lo = _split_bf16(_log_sigmoid(logit) * (1.0 / GLA_TAU))

    u = _gelu(u_pre)
    vs = _gelu(vs_pre)
    vn_parts = []
    for i in range(GMLP_HEADS):
        vh = vs[:, i * GMLP_DC:(i + 1) * GMLP_DC]
        mu = jnp.mean(vh, axis=-1, keepdims=True)
        d = vh - mu
        var = jnp.mean(d * d, axis=-1, keepdims=True)
        vn_parts.append(d * lax.rsqrt(var + EPS))
    vn = jnp.concatenate(vn_parts, axis=-1) * g_ln_ref[...] + b_ln_ref[...]
    if sample:
        vn_ref[...] = vn
    vn_bf = vn.astype(BF16)

    cum = cum_ref[...]
    tot = tot_ref[...]
    b = _dot(cum, g_hi) + _dot(cum, g_lo)
    bl = _dot(tot, g_hi) + _dot(tot, g_lo)

    mixed_parts = []
    for cch in range(rows // GMLP_CHUNK):
        crows = slice(cch * GMLP_CHUNK, (cch + 1) * GMLP_CHUNK)
        mixed_parts.append(jnp.concatenate(
            [_dot(w_sp_ref[i], vn_bf[crows, i * GMLP_DC:(i + 1) * GMLP_DC]) for i in range(GMLP_HEADS)],
            axis=-1) + b_sp_ref[...])

    so = u * jnp.concatenate(mixed_parts, axis=0)
    mix = _dot_wide(so.astype(BF16), w_out_ref, slice(GLA_WIDTH, GLA_WIDTH + GMLP_WIDTH))

    q = qk[:, :GLA_KEY] * (GLA_DK ** -0.5)
    k = qk[:, GLA_KEY:]
    v_bf = v.astype(BF16)
    qb = q * jnp.exp(b)
    kn = k * jnp.exp(-b)
    kd = k * jnp.exp(bl - b)
    causal = causal_ref[...]

    lane = lax.broadcasted_iota(jnp.int32, (rows, LANES), 1)
    head_lane_masks = (lane < GLA_DK, lane >= GLA_DK)

    prompt_pairs = []
    for p in range(HEAD_PAIRS):
        ls = slice(p * LANES, (p + 1) * LANES)
        qbp = qb[:, ls]
        knp = kn[:, ls].astype(BF16)
        kdp = kd[:, ls]
        o_intra = []
        for hh in range(2):
            hd = 2 * p + hh
            qm = jnp.where(head_lane_masks[hh], qbp, 0.0).astype(BF16)
            s = _dot_nt(qm, knp) * causal
            o_intra.append(_dot(s.astype(BF16), v_bf[:, hd * GLA_DV:(hd + 1) * GLA_DV]))

        if not sample:
            st_shape = (2 * GLA_DV, 2 * GLA_DK)
            ri = lax.broadcasted_iota(jnp.int32, st_shape, 0)
            ci = lax.broadcasted_iota(jnp.int32, st_shape, 1)
            same_head = (ri < GLA_DV) == (ci < GLA_DK)
            kdp_bf = kdp.astype(BF16)
            upd = []
            for c in range(rows // GLA_CHUNK):
                rws = slice(c * GLA_CHUNK, (c + 1) * GLA_CHUNK)
                inc = _dot_tn(v_bf[rws, 2 * p * GLA_DV:(2 * p + 2) * GLA_DV], kdp_bf[rws, :])
                upd.append(jnp.where(same_head, inc, 0.0))
            prompt_pairs.append((ls, qbp.astype(BF16), o_intra, upd))
        else:
            seq_len = 4
            seqs_per_grp = SUBLANES // seq_len
            kdp_t = kdp.T.astype(BF16)
            dec_t = jnp.exp(bl[:, ls]).T
            sub = lax.broadcasted_iota(jnp.int32, (SUBLANES, LANES), 0)
            crow = lax.broadcasted_iota(jnp.int32, (rows, LANES), 0)
            for grp in range(rows // SUBLANES):
                g0 = grp * SUBLANES
                qg = qbp[g0:g0 + SUBLANES, :]
                o_acc = [o_intra[hh][g0:g0 + SUBLANES, :] for hh in range(2)]
                for j in range(seqs_per_grp):
                    sidx = grp * seqs_per_grp + j
                    s_pair = sin_ref[sidx, p]
                    s_bf = s_pair.astype(BF16)
                    in_seq = (sub >= j * seq_len) & (sub < (j + 1) * seq_len)
                    in_seq_c = (crow >= g0 + j * seq_len) & (crow < g0 + (j + 1) * seq_len)
                    new_rows = []
                    for hh in range(2):
                        hd = 2 * p + hh
                        qsel = jnp.where(in_seq & head_lane_masks[hh][:SUBLANES], qg, 0.0).astype(BF16)
                        o_acc[hh] = o_acc[hh] + _dot(qsel, s_bf)
                        vsel = jnp.where(in_seq_c, v[:, hd * GLA_DV:(hd + 1) * GLA_DV], 0.0).astype(BF16)
                        new_rows.append(_dot(kdp_t[hh * GLA_DK:(hh + 1) * GLA_DK, :], vsel))
                    col = g0 + j * seq_len
                    sout_ref[sidx, p] = s_pair * dec_t[:, col:col + 1] + jnp.concatenate(new_rows, axis=0)
                for hh in range(2):
                    hd = 2 * p + hh
                    o_s[g0:g0 + SUBLANES, hd * GLA_DV:(hd + 1) * GLA_DV] = o_acc[hh]

    if not sample:
        st = [st_s[p] for p in range(HEAD_PAIRS)]
        for c in range(rows // GLA_CHUNK):
            rws = slice(c * GLA_CHUNK, (c + 1) * GLA_CHUNK)
            for p, (ls, qbp_bf, o_intra, upd) in enumerate(prompt_pairs):
                o_inter = _dot_nt(qbp_bf[rws, :], st[p].astype(BF16))
                for hh in range(2):
                    hd = 2 * p + hh
                    o_s[rws, hd * GLA_DV:(hd + 1) * GLA_DV] = (
                        o_intra[hh][rws, :] + o_inter[:, hh * GLA_DV:(hh + 1) * GLA_DV])
                dec = jnp.exp(bl[c * GLA_CHUNK:c * GLA_CHUNK + 1, ls])
                st[p] = st[p] * dec + upd[c]
        for p in range(HEAD_PAIRS):
            st_s[p] = st[p]

    out_gate = r * _sigmoid(r)
    half = rows // 2
    for hr in (slice(0, half), slice(half, rows)):
        o = o_s[hr, :]
        og = jnp.concatenate(
            [_rms(o[:, i * GLA_DV:(i + 1) * GLA_DV], 1.0) for i in range(GLA_HEADS)], axis=-1)
        og = og * g_gla_ref[...] * out_gate[hr, :]
        mix_h = mix[hr, :] + _dot_wide(og.astype(BF16), w_out_ref, slice(0, GLA_WIDTH))
        y_ref[hr, :] = x[hr, :] + _rms(mix_h, g_post_ref[...])

    if not sample:
        @pl.when(t == pl.num_programs(1) - 1)
        def _():
            for p in range(HEAD_PAIRS):
                st = st_s[p]
                sout_ref[2 * p] = st[:GLA_DV, :].T[:GLA_DK, :]
                sout_ref[2 * p + 1] = st[GLA_DV:, :].T[GLA_DK:, :]


def _ffn_tail(x1, f, pe, g_ffn_post, g_ple_in, w_pg_ref, g_ple_post):
    half = x1.shape[0] // 2
    out = []
    for hr in (slice(0, half), slice(half, x1.shape[0])):
        x2 = x1[hr, :] + _rms(f[hr, :], g_ffn_post)
        hp = _rms(x2, g_ple_in).astype(BF16)
        gate = _sigmoid(_dot_wide(hp, w_pg_ref))
        out.append((hr, x2 + _rms(pe[hr, :] * gate, g_ple_post)))
    return out


def _ff_groups():
    return [range(g, min(g + FF_GROUP, N_FF_BLOCKS)) for g in range(0, N_FF_BLOCKS, FF_GROUP)]


def _ffn_prompt_kernel(x_ref, p_ref, g_pre_ref, w_up_ref, w_conv_ref, b_conv_ref, w_down_ref,
                       g_post_ref, g_ple_in_ref, w_pg_ref, w_ple_ref, g_ple_post_ref,
                       y_ref, cs_ref,
                       carry_s, up_s, act_s, *, rows):
    t = pl.program_id(1)
    hdr = SUBLANES

    @pl.when(t == 0)
    def _():
        carry_s[...] = jnp.zeros_like(carry_s)
        up_s[:, :, :, hdr + rows:, :] = jnp.zeros((2, 2, CONV_W, hdr, FF_BLOCK), F32)

    x1 = x_ref[...]
    pe = _dot_wide(p_ref[...].astype(BF16), w_ple_ref)
    h = _rms(x1, g_pre_ref[...]).astype(BF16)

    def half_cols(j, half):
        c0 = half * D_FF + j * FF_BLOCK
        return slice(c0, c0 + FF_BLOCK)

    def project(j):
        for half in range(2):
            cols = half_cols(j, half)
            up = _dot(h, w_up_ref[:, cols])
            for s in range(CONV_W):
                buf = up_s.at[j % 2, half, s]
                if s:
                    buf[hdr:2 * hdr, :] = carry_s[s - 1, :, cols]
                buf[hdr + s:hdr + s + rows, :] = up
                if s:
                    carry_s[s - 1, :, cols] = buf[hdr + rows:2 * hdr + rows, :]

    def conv(j, half):
        cols = half_cols(j, half)
        w = w_conv_ref[:, cols]
        taps = [up_s[j % 2, half, s, hdr:hdr + rows, :] for s in range(CONV_W)]
        return (b_conv_ref[:, cols] + taps[2] * w[0:1, :] + taps[1] * w[1:2, :] + taps[0] * w[2:3, :])

    def down(j):
        k_rows = slice(j * FF_BLOCK, (j + 1) * FF_BLOCK)
        return _dot_wide(act_s[:, k_rows], w_down_ref, k_rows)

    project(0)
    f = None
    for j in range(N_FF_BLOCKS):
        if j + 1 < N_FF_BLOCKS:
            project(j + 1)
        if j == 1:
            f = down(0)
        elif j > 1:
            f = f + down(j - 1)
        gate = conv(j, 0)
        act_s[:, j * FF_BLOCK:(j + 1) * FF_BLOCK] = (
            (gate * conv(j, 1)) * _gelu_gate(gate)).astype(BF16)
    f = f + down(N_FF_BLOCKS - 1)

    for hr, y in _ffn_tail(x1, f, pe, g_post_ref[...], g_ple_in_ref[...], w_pg_ref,
                           g_ple_post_ref[...]):
        y_ref[hr, :] = y

    @pl.when(t == pl.num_programs(1) - 1)
    def _():
        cs_ref[...] = carry_s[CONV_W - 2, 0:CONV_W - 1, :]


def _ffn_sample_kernel(x_ref, p_ref, cin_ref, g_pre_ref, w_up_ref, w_conv_ref, b_conv_ref, w_down_ref,
                       g_post_ref, g_ple_in_ref, w_pg_ref, w_ple_ref, g_ple_post_ref,
                       y_ref, cs_ref, act_s, *, nseq, seq_len):
    x1 = jnp.concatenate([x_ref[:, j, :] for j in range(seq_len)], axis=0)
    p = jnp.concatenate([p_ref[:, j, :] for j in range(seq_len)], axis=0)
    pe = _dot_wide(p.astype(BF16), w_ple_ref)
    h = _rms(x1, g_pre_ref[...]).astype(BF16)

    def conv_half(col0):
        cols = slice(col0, col0 + FF_BLOCK)
        up = _dot(h, w_up_ref[:, cols])
        blocks = [cin_ref[:, i, cols] for i in range(CONV_W - 1)]
        blocks += [up[j * nseq:(j + 1) * nseq, :] for j in range(seq_len)]
        w = w_conv_ref[:, cols]
        b = b_conv_ref[:, cols]
        conv = jnp.concatenate(
            [b + blocks[j] * w[0:1, :] + blocks[j + 1] * w[1:2, :] + blocks[j + 2] * w[2:3, :]
             for j in range(seq_len)], axis=0)
        for i in range(CONV_W - 1):
            cs_ref[:, i, cols] = blocks[seq_len + i]
        return conv

    f = None
    for grp in _ff_groups():
        for j in grp:
            gate = conv_half(j * FF_BLOCK)
            val = conv_half(D_FF + j * FF_BLOCK)
            act_s[:, j * FF_BLOCK:(j + 1) * FF_BLOCK] = ((gate * val) * _gelu_gate(gate)).astype(BF16)
        k_rows = slice(grp[0] * FF_BLOCK, (grp[-1] + 1) * FF_BLOCK)
        part = _dot_wide(act_s[:, k_rows], w_down_ref, k_rows)
        f = part if f is None else f + part

    for hr, y in _ffn_tail(x1, f, pe, g_post_ref[...], g_ple_in_ref[...], w_pg_ref, g_ple_post_ref[...]):
        for j in range(hr.start // nseq, hr.stop // nseq):
            y_ref[:, j, :] = y[j * nseq - hr.start:(j + 1) * nseq - hr.start, :]


def _whole(shape):
    n = len(shape)
    return pl.BlockSpec(shape, lambda *_: (0,) * n)


def _full(shape):
    n = len(shape)
    return pl.BlockSpec(shape, lambda *_: (0,) * n, pipeline_mode=pl.Buffered(1))


def _gla_masks(rows, span):
    i = np.arange(rows)
    same = (i[:, None] // span) == (i[None, :] // span)
    causal = same & (i[None, :] <= i[:, None])
    return (jnp.asarray(causal, BF16), jnp.asarray(same, BF16), jnp.asarray(causal, F32))


def kernel(x_prompt, x_sample, state_gla, state_ffn_conv, p_prompt, p_sample, g_mix_pre, w_in, w_gla_gate, b_gla_gate, g_gla_out, g_sgu_ln, b_sgu_ln, w_spatial, b_spatial, w_out, g_mix_post, g_ffn_pre, w_up, w_conv, b_conv, w_down, g_ffn_post, g_ple_in, w_ple_gate, w_ple, g_ple_post):
    depth = x_prompt.ndim - 2
    assert w_in.shape[0] == 1 and depth == 1
    nb, seq, _ = x_prompt.shape
    ns, dseq, _ = x_sample.shape
    assert seq % PROMPT_TILE == 0 and PROMPT_TILE % GMLP_CHUNK == 0 and seq % FFN_PROMPT_TILE == 0
    assert dseq == 4 and (ns * dseq) % SAMPLE_TILE == 0 and SAMPLE_TILE % GMLP_CHUNK == 0

    wi = w_in[0]
    sp = np.cumsum((GLA_KEY, GLA_KEY, GLA_WIDTH, GLA_WIDTH, GLA_GATE_RANK, GMLP_WIDTH, GMLP_WIDTH))
    w_ga = wi[:, :sp[3]].astype(BF16)
    w_sg = jnp.concatenate(
        [wi[:, sp[4]:], wi[:, sp[3]:sp[4]],
         jnp.zeros((D_MODEL, LANES - GLA_GATE_RANK), wi.dtype)], axis=1).astype(BF16)
    w_gate_p = jnp.concatenate(
        [w_gla_gate[0], jnp.zeros((LANES - GLA_GATE_RANK, GLA_KEY), w_gla_gate.dtype)], axis=0).astype(BF16)
    row = lambda a: a.reshape(1, -1)
    g_gla_t = jnp.tile(g_gla_out[0], GLA_HEADS).reshape(1, -1)
    pad_cols = lambda w: jnp.pad(w, ((0, 0), (0, LANES))).astype(BF16)
    w_out_b = pad_cols(w_out[0])
    w_up_b = w_up[0].astype(BF16)
    w_down_b = pad_cols(w_down[0])
    w_pg_b = pad_cols(w_ple_gate[0])
    w_ple_b = pad_cols(w_ple[0])

    def spatial(c):
        i = np.arange(GMLP_CHUNK)
        keep = ((i[:, None] // c) == (i[None, :] // c)) & (i[None, :] <= i[:, None])
        e = jnp.asarray((i[:, None] % c) == np.arange(c)[None, :], w_spatial.dtype)
        hi = lax.Precision.HIGHEST
        wbd = jnp.einsum('ia,hab,jb->hij', e, w_spatial[0][:, :c, :c], e, precision=hi)
        wbd = wbd * jnp.asarray(keep, w_spatial.dtype)
        bias = jnp.einsum('ia,ha->ih', e, b_spatial[0][:, :c], precision=hi)
        return wbd.astype(BF16), jnp.repeat(bias, GMLP_DC, axis=1)

    mixer_weights = lambda wsp, bsp, masks: (
        row(g_mix_pre), w_ga, w_sg, w_gate_p, row(b_gla_gate), g_gla_t, row(g_sgu_ln), row(b_sgu_ln),
        wsp, bsp, w_out_b, row(g_mix_post)) + masks

    def mixer_weight_specs(rows):
        return [
            _full((1, D_MODEL)), _full((D_MODEL, IN_COLS_GLA)), _full((D_MODEL, IN_COLS_SGU)),
            _full((LANES, GLA_KEY)),
            _full((1, GLA_KEY)), _full((1, GLA_WIDTH)), _full((1, GMLP_WIDTH)), _full((1, GMLP_WIDTH)),
            _full((GMLP_HEADS, GMLP_CHUNK, GMLP_CHUNK)), _full((GMLP_CHUNK, GMLP_WIDTH)),
            _full((D_MODEL, D_MODEL + LANES)), _full((1, D_MODEL)),
            _full((rows, rows)), _full((rows, rows)), _full((rows, rows))]

    tl = PROMPT_TILE
    wsp_p, bsp_p = spatial(GMLP_CHUNK)
    x1_p, gla_p = pl.pallas_call(
        functools.partial(_mixer_kernel, rows=tl, sample=False),
        grid=(nb, seq // tl),
        in_specs=[pl.BlockSpec((None, tl, D_MODEL), lambda b, t: (b, t, 0))] + mixer_weight_specs(tl),
        out_specs=[pl.BlockSpec((None, tl, D_MODEL), lambda b, t: (b, t, 0)),
                   pl.BlockSpec((None, GLA_HEADS, GLA_DK, GLA_DV), lambda b, t: (b, 0, 0, 0))],
        out_shape=[jax.ShapeDtypeStruct((nb, seq, D_MODEL), F32),
                   jax.ShapeDtypeStruct((nb, GLA_HEADS, GLA_DK, GLA_DV), F32)],
        scratch_shapes=[pltpu.VMEM((tl, GLA_WIDTH), F32),
                        pltpu.VMEM((HEAD_PAIRS, 2 * GLA_DV, 2 * GLA_DK), F32)],
        compiler_params=pltpu.CompilerParams(
            dimension_semantics=("arbitrary", "arbitrary"), vmem_limit_bytes=VMEM_LIMIT_BYTES),
        name="mixer_prompt",
    )(x_prompt, *mixer_weights(wsp_p, bsp_p, _gla_masks(tl, GLA_CHUNK)))

    ts = SAMPLE_TILE
    seqs_per_tile = ts // dseq
    wsp_s, bsp_s = spatial(dseq)
    x1_s, gla_s, vrows_s = pl.pallas_call(
        functools.partial(_mixer_kernel, rows=ts, sample=True),
        grid=(ns * dseq // ts,),
        in_specs=[pl.BlockSpec((ts, D_MODEL), lambda i: (i, 0)),
                  pl.BlockSpec((seqs_per_tile, HEAD_PAIRS, 2 * GLA_DK, GLA_DV), lambda i: (i, 0, 0, 0))]
        + mixer_weight_specs(ts),
        out_specs=[pl.BlockSpec((ts, D_MODEL), lambda i: (i, 0)),
                   pl.BlockSpec((seqs_per_tile, HEAD_PAIRS, 2 * GLA_DK, GLA_DV), lambda i: (i, 0, 0, 0)),
                   pl.BlockSpec((ts, GMLP_WIDTH), lambda i: (i, 0))],
        out_shape=[jax.ShapeDtypeStruct((ns * dseq, D_MODEL), F32),
                   jax.ShapeDtypeStruct((ns, HEAD_PAIRS, 2 * GLA_DK, GLA_DV), F32),
                   jax.ShapeDtypeStruct((ns * dseq, GMLP_WIDTH), F32)],
        scratch_shapes=[pltpu.VMEM((ts, GLA_WIDTH), F32)],
        compiler_params=pltpu.CompilerParams(
            dimension_semantics=("arbitrary",), vmem_limit_bytes=VMEM_LIMIT_BYTES),
        name="mixer_sample",
    )(x_sample.reshape(ns * dseq, D_MODEL),
      state_gla[0].reshape(ns, HEAD_PAIRS, 2 * GLA_DK, GLA_DV),
      *mixer_weights(wsp_s, bsp_s, _gla_masks(ts, dseq)))

    ffn_weights = (row(g_ffn_pre), w_up_b, w_conv[0], row(b_conv), w_down_b, row(g_ffn_post),
                   row(g_ple_in), w_pg_b, w_ple_b, row(g_ple_post))
    ffn_weight_specs = [
        _full((1, D_MODEL)), _full((D_MODEL, 2 * D_FF)), _full((CONV_W, 2 * D_FF)), _full((1, 2 * D_FF)),
        _full((D_FF, D_MODEL + LANES)), _full((1, D_MODEL)), _full((1, D_MODEL)),
        _full((D_MODEL, D_MODEL + LANES)), _full((PLE_DIM, D_MODEL + LANES)), _full((1, D_MODEL))]

    tf = FFN_PROMPT_TILE
    y_p, conv_p = pl.pallas_call(
        functools.partial(_ffn_prompt_kernel, rows=tf),
        grid=(nb, seq // tf),
        in_specs=[pl.BlockSpec((None, tf, D_MODEL), lambda b, t: (b, t, 0)),
                  pl.BlockSpec((None, tf, PLE_DIM), lambda b, t: (b, t, 0))] + ffn_weight_specs,
        out_specs=[pl.BlockSpec((None, tf, D_MODEL), lambda b, t: (b, t, 0)),
                   pl.BlockSpec((None, CONV_W - 1, 2 * D_FF), lambda b, t: (b, 0, 0))],
        out_shape=[jax.ShapeDtypeStruct((nb, seq, D_MODEL), F32),
                   jax.ShapeDtypeStruct((nb, CONV_W - 1, 2 * D_FF), F32)],
        scratch_shapes=[pltpu.VMEM((CONV_W - 1, SUBLANES, 2 * D_FF), F32),
                        pltpu.VMEM((2, 2, CONV_W, tf + 2 * SUBLANES, FF_BLOCK), F32),
                        pltpu.VMEM((tf, D_FF), BF16)],
        compiler_params=pltpu.CompilerParams(
            dimension_semantics=("arbitrary", "arbitrary"), vmem_limit_bytes=VMEM_LIMIT_BYTES),
        name="ffn_prompt",
    )(x1_p, p_prompt[0], *ffn_weights)

    y_s, conv_s = pl.pallas_call(
        functools.partial(_ffn_sample_kernel, nseq=ns, seq_len=dseq),
        grid=(1,),
        in_specs=[_whole((ns, dseq, D_MODEL)), _whole((ns, dseq, PLE_DIM)),
                  _whole((ns, CONV_W - 1, 2 * D_FF))] + ffn_weight_specs,
        out_specs=[_whole((ns, dseq, D_MODEL)), _whole((ns, CONV_W - 1, 2 * D_FF))],
        out_shape=[jax.ShapeDtypeStruct((ns, dseq, D_MODEL), F32),
                   jax.ShapeDtypeStruct((ns, CONV_W - 1, 2 * D_FF), F32)],
        scratch_shapes=[pltpu.VMEM((ns * dseq, D_FF), BF16)],
        compiler_params=pltpu.CompilerParams(
            dimension_semantics=("arbitrary",), vmem_limit_bytes=VMEM_LIMIT_BYTES),
        name="ffn_sample",
    )(x1_s.reshape(ns, dseq, D_MODEL), p_sample[0], state_ffn_conv[0], *ffn_weights)

    return (y_p,
            y_s,
            gla_p[None],
            gla_s.reshape(1, ns, GLA_HEADS, GLA_DK, GLA_DV),
            conv_p[None],
            conv_s[None],
            vrows_s.reshape(1, ns, dseq, GMLP_WIDTH))
```

```python
import functools

import numpy as np
import jax
import jax.numpy as jnp
from jax import lax
from jax.experimental import pallas as pl
from jax.experimental.pallas import tpu as pltpu

D_MODEL = 1024
GLA_HEADS = 4
GLA_DK = 64
GLA_DV = 128
GLA_KEY = GLA_HEADS * GLA_DK
GLA_WIDTH = GLA_HEADS * GLA_DV
GMLP_HEADS = 4
GMLP_DC = 128
GMLP_WIDTH = GMLP_HEADS * GMLP_DC
GLA_GATE_RANK = 16
GLA_TAU = 16.0
GLA_CHUNK = 64
GMLP_CHUNK = 128
D_FF = 2816
CONV_W = 3
PLE_DIM = 256
EPS = 1e-6

LANES = 128
SUBLANES = 8
MXU_N = 256
HEAD_PAIRS = GLA_HEADS // 2

COL_QK = 0
COL_V = COL_QK + 2 * GLA_KEY
COL_R = COL_V + GLA_WIDTH
IN_COLS_GLA = COL_R + GLA_WIDTH
COL_U = 0
COL_VS = COL_U + GMLP_WIDTH
COL_GLR = COL_VS + GMLP_WIDTH
IN_COLS_SGU = COL_GLR + LANES

FF_BLOCK = 256
N_FF_BLOCKS = D_FF // FF_BLOCK
FF_GROUP = 4

PROMPT_TILE = 256
FFN_PROMPT_TILE = 256
SAMPLE_TILE = 128
CAST_STEPS = 8

VMEM_LIMIT_BYTES = 56 * 1024 * 1024

BF16 = jnp.bfloat16
F32 = jnp.float32


def _dot(a, b):
    return jnp.dot(a, b, preferred_element_type=F32)


def _dot_wide(a, w_ref, rows=slice(None)):
    return jnp.concatenate(
        [_dot(a, w_ref[rows, c:c + MXU_N]) for c in range(0, D_MODEL, MXU_N)], axis=-1)


def _dot_nt(a, b):
    return lax.dot_general(a, b, (((1,), (1,)), ((), ())), preferred_element_type=F32)


def _dot_tn(a, b):
    return lax.dot_general(a, b, (((0,), (0,)), ((), ())), preferred_element_type=F32)


def _rms(x, g):
    return x * lax.rsqrt(jnp.mean(x * x, axis=-1, keepdims=True) + EPS) * g


def _gelu_gate(x):
    c = -2.0 * np.sqrt(2.0 / np.pi) * np.log2(np.e)
    a1 = np.float32(c)
    a3 = np.float32(c * 0.044715)
    return 1.0 / (1.0 + jnp.exp2(x * (a3 * (x * x) + a1)))


def _gelu(x):
    return x * _gelu_gate(x)


def _sigmoid(x):
    return 1.0 / (1.0 + jnp.exp(-x))


def _log_sigmoid(x):
    return jnp.minimum(x, 0.0) - jnp.log1p(jnp.exp(-jnp.abs(x)))


def _split_bf16(x):
    hi = x.astype(BF16)
    lo = (x - hi.astype(F32)).astype(BF16)
    return hi, lo


def _mixer_kernel(*refs, rows, sample):
    if sample:
        (x_ref, sin_ref, g_pre_ref, w_ga_ref, w_sg_ref, w_gate_ref, b_gate_ref, g_gla_ref, g_ln_ref, b_ln_ref,
         w_sp_ref, b_sp_ref, w_out_ref, g_post_ref, cum_ref, tot_ref, causal_ref,
         y_ref, sout_ref, vn_ref, o_s) = refs
        st_s = None
    else:
        (x_ref, g_pre_ref, w_ga_ref, w_sg_ref, w_gate_ref, b_gate_ref, g_gla_ref, g_ln_ref, b_ln_ref,
         w_sp_ref, b_sp_ref, w_out_ref, g_post_ref, cum_ref, tot_ref, causal_ref,
         y_ref, sout_ref, o_s, st_s) = refs
        t = pl.program_id(1)

        @pl.when(t == 0)
        def _():
            st_s[...] = jnp.zeros_like(st_s)

    x = x_ref[...]
    h = _rms(x, g_pre_ref[...]).astype(BF16)

    glr = _dot(h, w_sg_ref[:, COL_GLR:IN_COLS_SGU])
    logit = _dot(glr.astype(BF16), w_gate_ref[...]) + b_gate_ref[...]
    u_pre = _dot(h, w_sg_ref[:, COL_U:COL_VS])
    vs_pre = _dot(h, w_sg_ref[:, COL_VS:COL_GLR])
    qk = _dot(h, w_ga_ref[:, COL_QK:COL_V])
    v = _dot(h, w_ga_ref[:, COL_V:COL_R])
    r = _dot(h, w_ga_ref[:, COL_R:IN_COLS_GLA])
    g_hi, g_lo = _split_bf16(_log_sigmoid(logit) * (1.0 / GLA_TAU))

    u = _gelu(u_pre)
    vs = _gelu(vs_pre)
    vn_parts = []
    for i in range(GMLP_HEADS):
        vh = vs[:, i * GMLP_DC:(i + 1) * GMLP_DC]
        mu = jnp.mean(vh, axis=-1, keepdims=True)
        d = vh - mu
        var = jnp.mean(d * d, axis=-1, keepdims=True)
        vn_parts.append(d * lax.rsqrt(var + EPS))
    vn = jnp.concatenate(vn_parts, axis=-1) * g_ln_ref[...] + b_ln_ref[...]
    if sample:
        vn_ref[...] = vn
    vn_bf = vn.astype(BF16)

    cum = cum_ref[...]
    tot = tot_ref[...]
    b = _dot(cum, g_hi) + _dot(cum, g_lo)
    bl = _dot(tot, g_hi) + _dot(tot, g_lo)

    mixed_parts = []
    for cch in range(rows // GMLP_CHUNK):
        crows = slice(cch * GMLP_CHUNK, (cch + 1) * GMLP_CHUNK)
        mixed_parts.append(jnp.concatenate(
            [_dot(w_sp_ref[i], vn_bf[crows, i * GMLP_DC:(i + 1) * GMLP_DC]) for i in range(GMLP_HEADS)],
            axis=-1) + b_sp_ref[...])

    so = u * jnp.concatenate(mixed_parts, axis=0)
    mix = _dot_wide(so.astype(BF16), w_out_ref, slice(GLA_WIDTH, GLA_WIDTH + GMLP_WIDTH))

    q = qk[:, :GLA_KEY] * (GLA_DK ** -0.5)
    k = qk[:, GLA_KEY:]
    v_bf = v.astype(BF16)
    qb = q * jnp.exp(b)
    kn = k * jnp.exp(-b)
    kd = k * jnp.exp(bl - b)
    causal = causal_ref[...]

    lane = lax.broadcasted_iota(jnp.int32, (rows, LANES), 1)
    head_lane_masks = (lane < GLA_DK, lane >= GLA_DK)

    prompt_pairs = []
    for p in range(HEAD_PAIRS):
        ls = slice(p * LANES, (p + 1) * LANES)
        qbp = qb[:, ls]
        knp = kn[:, ls].astype(BF16)
        kdp = kd[:, ls]
        o_intra = []
        for hh in range(2):
            hd = 2 * p + hh
            qm = jnp.where(head_lane_masks[hh], qbp, 0.0).astype(BF16)
            s = _dot_nt(qm, knp) * causal
            o_intra.append(_dot(s.astype(BF16), v_bf[:, hd * GLA_DV:(hd + 1) * GLA_DV]))

        if not sample:
            st_shape = (2 * GLA_DV, 2 * GLA_DK)
            ri = lax.broadcasted_iota(jnp.int32, st_shape, 0)
            ci = lax.broadcasted_iota(jnp.int32, st_shape, 1)
            same_head = (ri < GLA_DV) == (ci < GLA_DK)
            kdp_bf = kdp.astype(BF16)
            upd = []
            for c in range(rows // GLA_CHUNK):
                rws = slice(c * GLA_CHUNK, (c + 1) * GLA_CHUNK)
                inc = _dot_tn(v_bf[rws, 2 * p * GLA_DV:(2 * p + 2) * GLA_DV], kdp_bf[rws, :])
                upd.append(jnp.where(same_head, inc, 0.0))
            prompt_pairs.append((ls, qbp.astype(BF16), o_intra, upd))
        else:
            seq_len = 4
            seqs_per_grp = SUBLANES // seq_len
            kdp_t = kdp.T.astype(BF16)
            dec_t = jnp.exp(bl[:, ls]).T
            sub = lax.broadcasted_iota(jnp.int32, (SUBLANES, LANES), 0)
            crow = lax.broadcasted_iota(jnp.int32, (rows, LANES), 0)
            for grp in range(rows // SUBLANES):
                g0 = grp * SUBLANES
                qg = qbp[g0:g0 + SUBLANES, :]
                o_acc = [o_intra[hh][g0:g0 + SUBLANES, :] for hh in range(2)]
                for j in range(seqs_per_grp):
                    sidx = grp * seqs_per_grp + j
                    s_pair = sin_ref[sidx, p]
                    s_bf = s_pair.astype(BF16)
                    in_seq = (sub >= j * seq_len) & (sub < (j + 1) * seq_len)
                    in_seq_c = (crow >= g0 + j * seq_len) & (crow < g0 + (j + 1) * seq_len)
                    new_rows = []
                    for hh in range(2):
                        hd = 2 * p + hh
                        qsel = jnp.where(in_seq & head_lane_masks[hh][:SUBLANES], qg, 0.0).astype(BF16)
                        o_acc[hh] = o_acc[hh] + _dot(qsel, s_bf)
                        vsel = jnp.where(in_seq_c, v[:, hd * GLA_DV:(hd + 1) * GLA_DV], 0.0).astype(BF16)
                        new_rows.append(_dot(kdp_t[hh * GLA_DK:(hh + 1) * GLA_DK, :], vsel))
                    col = g0 + j * seq_len
                    sout_ref[sidx, p] = s_pair * dec_t[:, col:col + 1] + jnp.concatenate(new_rows, axis=0)
                for hh in range(2):
                    hd = 2 * p + hh
                    o_s[g0:g0 + SUBLANES, hd * GLA_DV:(hd + 1) * GLA_DV] = o_acc[hh]

    if not sample:
        st = [st_s[p] for p in range(HEAD_PAIRS)]
        for c in range(rows // GLA_CHUNK):
            rws = slice(c * GLA_CHUNK, (c + 1) * GLA_CHUNK)
            for p, (ls, qbp_bf, o_intra, upd) in enumerate(prompt_pairs):
                o_inter = _dot_nt(qbp_bf[rws, :], st[p].astype(BF16))
                for hh in range(2):
                    hd = 2 * p + hh
                    o_s[rws, hd * GLA_DV:(hd + 1) * GLA_DV] = (
                        o_intra[hh][rws, :] + o_inter[:, hh * GLA_DV:(hh + 1) * GLA_DV])
                dec = jnp.exp(bl[c * GLA_CHUNK:c * GLA_CHUNK + 1, ls])
                st[p] = st[p] * dec + upd[c]
        for p in range(HEAD_PAIRS):
            st_s[p] = st[p]

    out_gate = r * _sigmoid(r)
    half = rows // 2
    for hr in (slice(0, half), slice(half, rows)):
        o = o_s[hr, :]
        og = jnp.concatenate(
            [_rms(o[:, i * GLA_DV:(i + 1) * GLA_DV], 1.0) for i in range(GLA_HEADS)], axis=-1)
        og = og * g_gla_ref[...] * out_gate[hr, :]
        mix_h = mix[hr, :] + _dot_wide(og.astype(BF16), w_out_ref, slice(0, GLA_WIDTH))
        y_ref[hr, :] = x[hr, :] + _rms(mix_h, g_post_ref[...])

    if not sample:
        @pl.when(t == pl.num_programs(1) - 1)
        def _():
            for p in range(HEAD_PAIRS):
                st = st_s[p]
                sout_ref[2 * p] = st[:GLA_DV, :].T[:GLA_DK, :]
                sout_ref[2 * p + 1] = st[GLA_DV:, :].T[GLA_DK:, :]


def _ffn_tail(x1, f, p, g_ffn_post, g_ple_in, w_pg_ref, w_ple_ref, g_ple_post):
    x2 = x1 + _rms(f, g_ffn_post)
    hp = _rms(x2, g_ple_in).astype(BF16)
    gate = _sigmoid(_dot_wide(hp, w_pg_ref))
    pe = _dot_wide(p.astype(BF16), w_ple_ref)
    return x2 + _rms(pe * gate, g_ple_post)


def _ff_groups():
    return [range(g, min(g + FF_GROUP, N_FF_BLOCKS)) for g in range(0, N_FF_BLOCKS, FF_GROUP)]


def _ffn_prompt_kernel(x_ref, p_ref, g_pre_ref, w_up_ref, w_conv_ref, b_conv_ref, w_down_ref,
                       g_post_ref, g_ple_in_ref, w_pg_ref, w_ple_ref, g_ple_post_ref,
                       y_ref, cs_ref,
                       carry_s, up_s, act_s, *, rows):
    t = pl.program_id(1)
    hdr = SUBLANES

    @pl.when(t == 0)
    def _():
        carry_s[...] = jnp.zeros_like(carry_s)
        up_s[:, :, :, hdr + rows:, :] = jnp.zeros((2, 2, CONV_W, hdr, FF_BLOCK), F32)

    x1 = x_ref[...]
    h = _rms(x1, g_pre_ref[...]).astype(BF16)

    def half_cols(j, half):
        c0 = half * D_FF + j * FF_BLOCK
        return slice(c0, c0 + FF_BLOCK)

    def project(j):
        for half in range(2):
            cols = half_cols(j, half)
            up = _dot(h, w_up_ref[:, cols])
            for s in range(CONV_W):
                buf = up_s.at[j % 2, half, s]
                if s:
                    buf[hdr:2 * hdr, :] = carry_s[s - 1, :, cols]
                buf[hdr + s:hdr + s + rows, :] = up
                if s:
                    carry_s[s - 1, :, cols] = buf[hdr + rows:2 * hdr + rows, :]

    def conv(j, half):
        cols = half_cols(j, half)
        w = w_conv_ref[:, cols]
        taps = [up_s[j % 2, half, s, hdr:hdr + rows, :] for s in range(CONV_W)]
        return (b_conv_ref[:, cols] + taps[2] * w[0:1, :] + taps[1] * w[1:2, :] + taps[0] * w[2:3, :])

    def down(j):
        k_rows = slice(j * FF_BLOCK, (j + 1) * FF_BLOCK)
        return _dot_wide(act_s[:, k_rows], w_down_ref, k_rows)

    project(0)
    f = None
    for j in range(N_FF_BLOCKS):
        if j + 1 < N_FF_BLOCKS:
            project(j + 1)
        if j == 1:
            f = down(0)
        elif j > 1:
            f = f + down(j - 1)
        gate = conv(j, 0)
        act_s[:, j * FF_BLOCK:(j + 1) * FF_BLOCK] = (
            (gate * conv(j, 1)) * _gelu_gate(gate)).astype(BF16)
    f = f + down(N_FF_BLOCKS - 1)

    y_ref[...] = _ffn_tail(x1, f, p_ref[...], g_post_ref[...], g_ple_in_ref[...], w_pg_ref,
                           w_ple_ref, g_ple_post_ref[...])

    @pl.when(t == pl.num_programs(1) - 1)
    def _():
        cs_ref[...] = carry_s[CONV_W - 2, 0:CONV_W - 1, :]


def _ffn_sample_kernel(x_ref, p_ref, cin_ref, g_pre_ref, w_up_ref, w_conv_ref, b_conv_ref, w_down_ref,
                       g_post_ref, g_ple_in_ref, w_pg_ref, w_ple_ref, g_ple_post_ref,
                       y_ref, cs_ref, act_s, *, nseq, seq_len):
    x1 = jnp.concatenate([x_ref[:, j, :] for j in range(seq_len)], axis=0)
    p = jnp.concatenate([p_ref[:, j, :] for j in range(seq_len)], axis=0)
    h = _rms(x1, g_pre_ref[...]).astype(BF16)

    def conv_half(col0):
        cols = slice(col0, col0 + FF_BLOCK)
        up = _dot(h, w_up_ref[:, cols])
        blocks = [cin_ref[:, i, cols] for i in range(CONV_W - 1)]
        blocks += [up[j * nseq:(j + 1) * nseq, :] for j in range(seq_len)]
        w = w_conv_ref[:, cols]
        b = b_conv_ref[:, cols]
        conv = jnp.concatenate(
            [b + blocks[j] * w[0:1, :] + blocks[j + 1] * w[1:2, :] + blocks[j + 2] * w[2:3, :]
             for j in range(seq_len)], axis=0)
        for i in range(CONV_W - 1):
            cs_ref[:, i, cols] = blocks[seq_len + i]
        return conv

    f = None
    for grp in _ff_groups():
        for j in grp:
            gate = conv_half(j * FF_BLOCK)
            val = conv_half(D_FF + j * FF_BLOCK)
            act_s[:, j * FF_BLOCK:(j + 1) * FF_BLOCK] = ((gate * val) * _gelu_gate(gate)).astype(BF16)
        k_rows = slice(grp[0] * FF_BLOCK, (grp[-1] + 1) * FF_BLOCK)
        part = _dot_wide(act_s[:, k_rows], w_down_ref, k_rows)
        f = part if f is None else f + part

    y = _ffn_tail(x1, f, p, g_post_ref[...], g_ple_in_ref[...], w_pg_ref, w_ple_ref,
                  g_ple_post_ref[...])
    for j in range(seq_len):
        y_ref[:, j, :] = y[j * nseq:(j + 1) * nseq, :]


def _cast_kernel(*refs):
    n = len(refs) // 2
    for src, dst in zip(refs[:n], refs[n:]):
        cols = src.shape[1]
        dst[:, :cols] = src[...].astype(BF16)
        if dst.shape[1] > cols:
            dst[:, cols:] = jnp.zeros((dst.shape[0], dst.shape[1] - cols), BF16)


def _whole(shape):
    n = len(shape)
    return pl.BlockSpec(shape, lambda *_: (0,) * n)


def _full(shape):
    n = len(shape)
    return pl.BlockSpec(shape, lambda *_: (0,) * n, pipeline_mode=pl.Buffered(1))


def _gla_masks(rows, span):
    i = np.arange(rows)
    same = (i[:, None] // span) == (i[None, :] // span)
    causal = same & (i[None, :] <= i[:, None])
    return (jnp.asarray(causal, BF16), jnp.asarray(same, BF16), jnp.asarray(causal, F32))


def kernel(x_prompt, x_sample, state_gla, state_ffn_conv, p_prompt, p_sample, g_mix_pre, w_in, w_gla_gate, b_gla_gate, g_gla_out, g_sgu_ln, b_sgu_ln, w_spatial, b_spatial, w_out, g_mix_post, g_ffn_pre, w_up, w_conv, b_conv, w_down, g_ffn_post, g_ple_in, w_ple_gate, w_ple, g_ple_post):
    depth = x_prompt.ndim - 2
    assert w_in.shape[0] == 1 and depth == 1
    nb, seq, _ = x_prompt.shape
    ns, dseq, _ = x_sample.shape
    assert seq % PROMPT_TILE == 0 and PROMPT_TILE % GMLP_CHUNK == 0 and seq % FFN_PROMPT_TILE == 0
    assert dseq == 4 and (ns * dseq) % SAMPLE_TILE == 0 and SAMPLE_TILE % GMLP_CHUNK == 0

    wi = w_in[0]
    sp = np.cumsum((GLA_KEY, GLA_KEY, GLA_WIDTH, GLA_WIDTH, GLA_GATE_RANK, GMLP_WIDTH, GMLP_WIDTH))
    w_sg = jnp.concatenate(
        [wi[:, sp[4]:], wi[:, sp[3]:sp[4]],
         jnp.zeros((D_MODEL, LANES - GLA_GATE_RANK), wi.dtype)], axis=1).astype(BF16)
    w_gate_p = jnp.concatenate(
        [w_gla_gate[0], jnp.zeros((LANES - GLA_GATE_RANK, GLA_KEY), w_gla_gate.dtype)], axis=0).astype(BF16)
    row = lambda a: a.reshape(1, -1)
    g_gla_t = jnp.tile(g_gla_out[0], GLA_HEADS).reshape(1, -1)
    n_cast = CAST_STEPS
    cast_src = (wi, w_out[0], w_up[0], w_down[0], w_ple_gate[0], w_ple[0])
    cast_cols_in = (IN_COLS_GLA, D_MODEL, 2 * D_FF, D_MODEL, D_MODEL, D_MODEL)
    cast_cols_out = (IN_COLS_GLA, D_MODEL + LANES, 2 * D_FF, D_MODEL + LANES, D_MODEL + LANES, D_MODEL + LANES)
    cast_rows = tuple(w.shape[0] for w in cast_src)
    assert all(r % (n_cast * 2 * SUBLANES) == 0 for r in cast_rows)
    w_ga, w_out_b, w_up_b, w_down_b, w_pg_b, w_ple_b = pl.pallas_call(
        _cast_kernel,
        grid=(n_cast,),
        in_specs=[pl.BlockSpec((r // n_cast, c), lambda i: (i, 0)) for r, c in zip(cast_rows, cast_cols_in)],
        out_specs=[pl.BlockSpec((r // n_cast, c), lambda i: (i, 0)) for r, c in zip(cast_rows, cast_cols_out)],
        out_shape=[jax.ShapeDtypeStruct((r, c), BF16) for r, c in zip(cast_rows, cast_cols_out)],
        compiler_params=pltpu.CompilerParams(
            dimension_semantics=("arbitrary",), vmem_limit_bytes=VMEM_LIMIT_BYTES),
        name="weight_cast",
    )(*cast_src)

    def spatial(c):
        i = np.arange(GMLP_CHUNK)
        keep = ((i[:, None] // c) == (i[None, :] // c)) & (i[None, :] <= i[:, None])
        e = jnp.asarray((i[:, None] % c) == np.arange(c)[None, :], w_spatial.dtype)
        hi = lax.Precision.HIGHEST
        wbd = jnp.einsum('ia,hab,jb->hij', e, w_spatial[0][:, :c, :c], e, precision=hi)
        wbd = wbd * jnp.asarray(keep, w_spatial.dtype)
        bias = jnp.einsum('ia,ha->ih', e, b_spatial[0][:, :c], precision=hi)
        return wbd.astype(BF16), jnp.repeat(bias, GMLP_DC, axis=1)

    mixer_weights = lambda wsp, bsp, masks: (
        row(g_mix_pre), w_ga, w_sg, w_gate_p, row(b_gla_gate), g_gla_t, row(g_sgu_ln), row(b_sgu_ln),
        wsp, bsp, w_out_b, row(g_mix_post)) + masks

    def mixer_weight_specs(rows):
        return [
            _full((1, D_MODEL)), _full((D_MODEL, IN_COLS_GLA)), _full((D_MODEL, IN_COLS_SGU)),
            _full((LANES, GLA_KEY)),
            _full((1, GLA_KEY)), _full((1, GLA_WIDTH)), _full((1, GMLP_WIDTH)), _full((1, GMLP_WIDTH)),
            _full((GMLP_HEADS, GMLP_CHUNK, GMLP_CHUNK)), _full((GMLP_CHUNK, GMLP_WIDTH)),
            _full((D_MODEL, D_MODEL + LANES)), _full((1, D_MODEL)),
            _full((rows, rows)), _full((rows, rows)), _full((rows, rows))]

    tl = PROMPT_TILE
    wsp_p, bsp_p = spatial(GMLP_CHUNK)
    x1_p, gla_p = pl.pallas_call(
        functools.partial(_mixer_kernel, rows=tl, sample=False),
        grid=(nb, seq // tl),
        in_specs=[pl.BlockSpec((None, tl, D_MODEL), lambda b, t: (b, t, 0))] + mixer_weight_specs(tl),
        out_specs=[pl.BlockSpec((None, tl, D_MODEL), lambda b, t: (b, t, 0)),
                   pl.BlockSpec((None, GLA_HEADS, GLA_DK, GLA_DV), lambda b, t: (b, 0, 0, 0))],
        out_shape=[jax.ShapeDtypeStruct((nb, seq, D_MODEL), F32),
                   jax.ShapeDtypeStruct((nb, GLA_HEADS, GLA_DK, GLA_DV), F32)],
        scratch_shapes=[pltpu.VMEM((tl, GLA_WIDTH), F32),
                        pltpu.VMEM((HEAD_PAIRS, 2 * GLA_DV, 2 * GLA_DK), F32)],
        compiler_params=pltpu.CompilerParams(
            dimension_semantics=("arbitrary", "arbitrary"), vmem_limit_bytes=VMEM_LIMIT_BYTES),
        name="mixer_prompt",
    )(x_prompt, *mixer_weights(wsp_p, bsp_p, _gla_masks(tl, GLA_CHUNK)))

    ts = SAMPLE_TILE
    seqs_per_tile = ts // dseq
    wsp_s, bsp_s = spatial(dseq)
    x1_s, gla_s, vrows_s = pl.pallas_call(
        functools.partial(_mixer_kernel, rows=ts, sample=True),
        grid=(ns * dseq // ts,),
        in_specs=[pl.BlockSpec((ts, D_MODEL), lambda i: (i, 0)),
                  pl.BlockSpec((seqs_per_tile, HEAD_PAIRS, 2 * GLA_DK, GLA_DV), lambda i: (i, 0, 0, 0))]
        + mixer_weight_specs(ts),
        out_specs=[pl.BlockSpec((ts, D_MODEL), lambda i: (i, 0)),
                   pl.BlockSpec((seqs_per_tile, HEAD_PAIRS, 2 * GLA_DK, GLA_DV), lambda i: (i, 0, 0, 0)),
                   pl.BlockSpec((ts, GMLP_WIDTH), lambda i: (i, 0))],
        out_shape=[jax.ShapeDtypeStruct((ns * dseq, D_MODEL), F32),
                   jax.ShapeDtypeStruct((ns, HEAD_PAIRS, 2 * GLA_DK, GLA_DV), F32),
                   jax.ShapeDtypeStruct((ns * dseq, GMLP_WIDTH), F32)],
        scratch_shapes=[pltpu.VMEM((ts, GLA_WIDTH), F32)],
        compiler_params=pltpu.CompilerParams(
            dimension_semantics=("arbitrary",), vmem_limit_bytes=VMEM_LIMIT_BYTES),
        name="mixer_sample",
    )(x_sample.reshape(ns * dseq, D_MODEL),
      state_gla[0].reshape(ns, HEAD_PAIRS, 2 * GLA_DK, GLA_DV),
      *mixer_weights(wsp_s, bsp_s, _gla_masks(ts, dseq)))

    ffn_weights = (row(g_ffn_pre), w_up_b, w_conv[0], row(b_conv), w_down_b, row(g_ffn_post),
                   row(g_ple_in), w_pg_b, w_ple_b, row(g_ple_post))
    ffn_weight_specs = [
        _full((1, D_MODEL)), _full((D_MODEL, 2 * D_FF)), _full((CONV_W, 2 * D_FF)), _full((1, 2 * D_FF)),
        _full((D_FF, D_MODEL + LANES)), _full((1, D_MODEL)), _full((1, D_MODEL)),
        _full((D_MODEL, D_MODEL + LANES)), _full((PLE_DIM, D_MODEL + LANES)), _full((1, D_MODEL))]

    tf = FFN_PROMPT_TILE
    y_p, conv_p = pl.pallas_call(
        functools.partial(_ffn_prompt_kernel, rows=tf),
        grid=(nb, seq // tf),
        in_specs=[pl.BlockSpec((None, tf, D_MODEL), lambda b, t: (b, t, 0)),
                  pl.BlockSpec((None, tf, PLE_DIM), lambda b, t: (b, t, 0))] + ffn_weight_specs,
        out_specs=[pl.BlockSpec((None, tf, D_MODEL), lambda b, t: (b, t, 0)),
                   pl.BlockSpec((None, CONV_W - 1, 2 * D_FF), lambda b, t: (b, 0, 0))],
        out_shape=[jax.ShapeDtypeStruct((nb, seq, D_MODEL), F32),
                   jax.ShapeDtypeStruct((nb, CONV_W - 1, 2 * D_FF), F32)],
        scratch_shapes=[pltpu.VMEM((CONV_W - 1, SUBLANES, 2 * D_FF), F32),
                        pltpu.VMEM((2, 2, CONV_W, tf + 2 * SUBLANES, FF_BLOCK), F32),
                        pltpu.VMEM((tf, D_FF), BF16)],
        compiler_params=pltpu.CompilerParams(
            dimension_semantics=("arbitrary", "arbitrary"), vmem_limit_bytes=VMEM_LIMIT_BYTES),
        name="ffn_prompt",
    )(x1_p, p_prompt[0], *ffn_weights)

    y_s, conv_s = pl.pallas_call(
        functools.partial(_ffn_sample_kernel, nseq=ns, seq_len=dseq),
        grid=(1,),
        in_specs=[_whole((ns, dseq, D_MODEL)), _whole((ns, dseq, PLE_DIM)),
                  _whole((ns, CONV_W - 1, 2 * D_FF))] + ffn_weight_specs,
        out_specs=[_whole((ns, dseq, D_MODEL)), _whole((ns, CONV_W - 1, 2 * D_FF))],
        out_shape=[jax.ShapeDtypeStruct((ns, dseq, D_MODEL), F32),
                   jax.ShapeDtypeStruct((ns, CONV_W - 1, 2 * D_FF), F32)],
        scratch_shapes=[pltpu.VMEM((ns * dseq, D_FF), BF16)],
        compiler_params=pltpu.CompilerParams(
            dimension_semantics=("arbitrary",), vmem_limit_bytes=VMEM_LIMIT_BYTES),
        name="ffn_sample",
    )(x1_s.reshape(ns, dseq, D_MODEL), p_sample[0], state_ffn_conv[0], *ffn_weights)

    return (y_p,
            y_s,
            gla_p[None],
            gla_s.reshape(1, ns, GLA_HEADS, GLA_DK, GLA_DV),
            conv_p[None],
            conv_s[None],
            vrows_s.reshape(1, ns, dseq, GMLP_WIDTH))
```

```python
import functools

import numpy as np
import jax
import jax.numpy as jnp
from jax import lax
from jax.experimental import pallas as pl
from jax.experimental.pallas import tpu as pltpu

D_MODEL = 1024
GLA_HEADS = 4
GLA_DK = 64
GLA_DV = 128
GLA_KEY = GLA_HEADS * GLA_DK
GLA_WIDTH = GLA_HEADS * GLA_DV
GMLP_HEADS = 4
GMLP_DC = 128
GMLP_WIDTH = GMLP_HEADS * GMLP_DC
GLA_GATE_RANK = 16
GLA_TAU = 16.0
GLA_CHUNK = 64
GMLP_CHUNK = 128
D_FF = 2816
CONV_W = 3
PLE_DIM = 256
EPS = 1e-6

LANES = 128
SUBLANES = 8
MXU_N = 256
HEAD_PAIRS = GLA_HEADS // 2

COL_QK = 0
COL_V = COL_QK + 2 * GLA_KEY
COL_R = COL_V + GLA_WIDTH
IN_COLS_GLA = COL_R + GLA_WIDTH
COL_U = 0
COL_VS = COL_U + GMLP_WIDTH
COL_GLR = COL_VS + GMLP_WIDTH
IN_COLS_SGU = COL_GLR + LANES

FF_BLOCK = 256
N_FF_BLOCKS = D_FF // FF_BLOCK
FF_GROUP = 4

PROMPT_TILE = 256
FFN_PROMPT_TILE = 256
SAMPLE_TILE = 128
CAST_STEPS = 8

VMEM_LIMIT_BYTES = 56 * 1024 * 1024

BF16 = jnp.bfloat16
F32 = jnp.float32


def _dot(a, b):
    return jnp.dot(a, b, preferred_element_type=F32)


def _dot_wide(a, w_ref, rows=slice(None)):
    return jnp.concatenate(
        [_dot(a, w_ref[rows, c:c + MXU_N]) for c in range(0, D_MODEL, MXU_N)], axis=-1)


def _dot_nt(a, b):
    return lax.dot_general(a, b, (((1,), (1,)), ((), ())), preferred_element_type=F32)


def _dot_tn(a, b):
    return lax.dot_general(a, b, (((0,), (0,)), ((), ())), preferred_element_type=F32)


def _rms(x, g):
    return x * lax.rsqrt(jnp.mean(x * x, axis=-1, keepdims=True) + EPS) * g


def _gelu_gate(x):
    c = -2.0 * np.sqrt(2.0 / np.pi) * np.log2(np.e)
    a1 = np.float32(c)
    a3 = np.float32(c * 0.044715)
    return 1.0 / (1.0 + jnp.exp2(x * (a3 * (x * x) + a1)))


def _gelu(x):
    return x * _gelu_gate(x)


def _sigmoid(x):
    return 1.0 / (1.0 + jnp.exp(-x))


def _log_sigmoid(x):
    return jnp.minimum(x, 0.0) - jnp.log1p(jnp.exp(-jnp.abs(x)))


def _split_bf16(x):
    hi = x.astype(BF16)
    lo = (x - hi.astype(F32)).astype(BF16)
    return hi, lo


def _mixer_kernel(*refs, rows, sample):
    if sample:
        (x_ref, sin_ref, g_pre_ref, w_ga_ref, w_sg_ref, w_gate_ref, b_gate_ref, g_gla_ref, g_ln_ref, b_ln_ref,
         w_sp_ref, b_sp_ref, w_out_ref, g_post_ref, cum_ref, tot_ref, causal_ref,
         y_ref, sout_ref, vn_ref, o_s) = refs
        st_s = None
    else:
        (x_ref, g_pre_ref, w_ga_ref, w_sg_ref, w_gate_ref, b_gate_ref, g_gla_ref, g_ln_ref, b_ln_ref,
         w_sp_ref, b_sp_ref, w_out_ref, g_post_ref, cum_ref, tot_ref, causal_ref,
         y_ref, sout_ref, o_s, st_s) = refs
        t = pl.program_id(1)

        @pl.when(t == 0)
        def _():
            st_s[...] = jnp.zeros_like(st_s)

    x = x_ref[...]
    h = _rms(x, g_pre_ref[...]).astype(BF16)

    glr = _dot(h, w_sg_ref[:, COL_GLR:IN_COLS_SGU])
    logit = _dot(glr.astype(BF16), w_gate_ref[...]) + b_gate_ref[...]
    u_pre = _dot(h, w_sg_ref[:, COL_U:COL_VS])
    vs_pre = _dot(h, w_sg_ref[:, COL_VS:COL_GLR])
    qk = _dot(h, w_ga_ref[:, COL_QK:COL_V])
    v = _dot(h, w_ga_ref[:, COL_V:COL_R])
    r = _dot(h, w_ga_ref[:, COL_R:IN_COLS_GLA])
    g_hi, g_lo = _split_bf16(_log_sigmoid(logit) * (1.0 / GLA_TAU))

    u = _gelu(u_pre)
    vs = _gelu(vs_pre)
    vn_parts = []
    for i in range(GMLP_HEADS):
        vh = vs[:, i * GMLP_DC:(i + 1) * GMLP_DC]
        mu = jnp.mean(vh, axis=-1, keepdims=True)
        d = vh - mu
        var = jnp.mean(d * d, axis=-1, keepdims=True)
        vn_parts.append(d * lax.rsqrt(var + EPS))
    vn = jnp.concatenate(vn_parts, axis=-1) * g_ln_ref[...] + b_ln_ref[...]
    if sample:
        vn_ref[...] = vn
    vn_bf = vn.astype(BF16)

    cum = cum_ref[...]
    tot = tot_ref[...]
    b = _dot(cum, g_hi) + _dot(cum, g_lo)
    bl = _dot(tot, g_hi) + _dot(tot, g_lo)

    mixed_parts = []
    for cch in range(rows // GMLP_CHUNK):
        crows = slice(cch * GMLP_CHUNK, (cch + 1) * GMLP_CHUNK)
        mixed_parts.append(jnp.concatenate(
            [_dot(w_sp_ref[i], vn_bf[crows, i * GMLP_DC:(i + 1) * GMLP_DC]) for i in range(GMLP_HEADS)],
            axis=-1) + b_sp_ref[...])

    so = u * jnp.concatenate(mixed_parts, axis=0)
    mix = _dot_wide(so.astype(BF16), w_out_ref, slice(GLA_WIDTH, GLA_WIDTH + GMLP_WIDTH))

    q = qk[:, :GLA_KEY] * (GLA_DK ** -0.5)
    k = qk[:, GLA_KEY:]
    v_bf = v.astype(BF16)
    qb = q * jnp.exp(b)
    kn = k * jnp.exp(-b)
    kd = k * jnp.exp(bl - b)
    causal = causal_ref[...]

    lane = lax.broadcasted_iota(jnp.int32, (rows, LANES), 1)
    head_lane_masks = (lane < GLA_DK, lane >= GLA_DK)

    prompt_pairs = []
    for p in range(HEAD_PAIRS):
        ls = slice(p * LANES, (p + 1) * LANES)
        qbp = qb[:, ls]
        knp = kn[:, ls].astype(BF16)
        kdp = kd[:, ls]
        o_intra = []
        for hh in range(2):
            hd = 2 * p + hh
            qm = jnp.where(head_lane_masks[hh], qbp, 0.0).astype(BF16)
            s = _dot_nt(qm, knp) * causal
            o_intra.append(_dot(s.astype(BF16), v_bf[:, hd * GLA_DV:(hd + 1) * GLA_DV]))

        if not sample:
            st_shape = (2 * GLA_DV, 2 * GLA_DK)
            ri = lax.broadcasted_iota(jnp.int32, st_shape, 0)
            ci = lax.broadcasted_iota(jnp.int32, st_shape, 1)
            same_head = (ri < GLA_DV) == (ci < GLA_DK)
            kdp_bf = kdp.astype(BF16)
            upd = []
            for c in range(rows // GLA_CHUNK):
                rws = slice(c * GLA_CHUNK, (c + 1) * GLA_CHUNK)
                inc = _dot_tn(v_bf[rws, 2 * p * GLA_DV:(2 * p + 2) * GLA_DV], kdp_bf[rws, :])
                upd.append(jnp.where(same_head, inc, 0.0))
            prompt_pairs.append((ls, qbp.astype(BF16), o_intra, upd))
        else:
            seq_len = 4
            seqs_per_grp = SUBLANES // seq_len
            kdp_t = kdp.T.astype(BF16)
            dec_t = jnp.exp(bl[:, ls]).T
            sub = lax.broadcasted_iota(jnp.int32, (SUBLANES, LANES), 0)
            crow = lax.broadcasted_iota(jnp.int32, (rows, LANES), 0)
            for grp in range(rows // SUBLANES):
                g0 = grp * SUBLANES
                qg = qbp[g0:g0 + SUBLANES, :]
                o_acc = [o_intra[hh][g0:g0 + SUBLANES, :] for hh in range(2)]
                for j in range(seqs_per_grp):
                    sidx = grp * seqs_per_grp + j
                    s_pair = sin_ref[sidx, p]
                    s_bf = s_pair.astype(BF16)
                    in_seq = (sub >= j * seq_len) & (sub < (j + 1) * seq_len)
                    in_seq_c = (crow >= g0 + j * seq_len) & (crow < g0 + (j + 1) * seq_len)
                    new_rows = []
                    for hh in range(2):
                        hd = 2 * p + hh
                        qsel = jnp.where(in_seq & head_lane_masks[hh][:SUBLANES], qg, 0.0).astype(BF16)
                        o_acc[hh] = o_acc[hh] + _dot(qsel, s_bf)
                        vsel = jnp.where(in_seq_c, v[:, hd * GLA_DV:(hd + 1) * GLA_DV], 0.0).astype(BF16)
                        new_rows.append(_dot(kdp_t[hh * GLA_DK:(hh + 1) * GLA_DK, :], vsel))
                    col = g0 + j * seq_len
                    sout_ref[sidx, p] = s_pair * dec_t[:, col:col + 1] + jnp.concatenate(new_rows, axis=0)
                for hh in range(2):
                    hd = 2 * p + hh
                    o_s[g0:g0 + SUBLANES, hd * GLA_DV:(hd + 1) * GLA_DV] = o_acc[hh]

    if not sample:
        st = [st_s[p] for p in range(HEAD_PAIRS)]
        for c in range(rows // GLA_CHUNK):
            rws = slice(c * GLA_CHUNK, (c + 1) * GLA_CHUNK)
            for p, (ls, qbp_bf, o_intra, upd) in enumerate(prompt_pairs):
                o_inter = _dot_nt(qbp_bf[rws, :], st[p].astype(BF16))
                for hh in range(2):
                    hd = 2 * p + hh
                    o_s[rws, hd * GLA_DV:(hd + 1) * GLA_DV] = (
                        o_intra[hh][rws, :] + o_inter[:, hh * GLA_DV:(hh + 1) * GLA_DV])
                dec = jnp.exp(bl[c * GLA_CHUNK:c * GLA_CHUNK + 1, ls])
                st[p] = st[p] * dec + upd[c]
        for p in range(HEAD_PAIRS):
            st_s[p] = st[p]

    out_gate = r * _sigmoid(r)
    half = rows // 2
    for hr in (slice(0, half), slice(half, rows)):
        o = o_s[hr, :]
        og = jnp.concatenate(
            [_rms(o[:, i * GLA_DV:(i + 1) * GLA_DV], 1.0) for i in range(GLA_HEADS)], axis=-1)
        og = og * g_gla_ref[...] * out_gate[hr, :]
        mix_h = mix[hr, :] + _dot_wide(og.astype(BF16), w_out_ref, slice(0, GLA_WIDTH))
        y_ref[hr, :] = x[hr, :] + _rms(mix_h, g_post_ref[...])

    if not sample:
        @pl.when(t == pl.num_programs(1) - 1)
        def _():
            for p in range(HEAD_PAIRS):
                st = st_s[p]
                sout_ref[2 * p] = st[:GLA_DV, :].T[:GLA_DK, :]
                sout_ref[2 * p + 1] = st[GLA_DV:, :].T[GLA_DK:, :]


def _ffn_tail(x1, f, p, g_ffn_post, g_ple_in, w_pg_ref, w_ple_ref, g_ple_post):
    x2 = x1 + _rms(f, g_ffn_post)
    hp = _rms(x2, g_ple_in).astype(BF16)
    gate = _sigmoid(_dot_wide(hp, w_pg_ref))
    pe = _dot_wide(p.astype(BF16), w_ple_ref)
    return x2 + _rms(pe * gate, g_ple_post)


def _ff_groups():
    return [range(g, min(g + FF_GROUP, N_FF_BLOCKS)) for g in range(0, N_FF_BLOCKS, FF_GROUP)]


def _ffn_prompt_kernel(x_ref, p_ref, g_pre_ref, w_up_ref, w_conv_ref, b_conv_ref, w_down_ref,
                       g_post_ref, g_ple_in_ref, w_pg_ref, w_ple_ref, g_ple_post_ref,
                       y_ref, cs_ref,
                       carry_s, up_s, act_s, *, rows):
    t = pl.program_id(1)
    hdr = SUBLANES

    @pl.when(t == 0)
    def _():
        carry_s[...] = jnp.zeros_like(carry_s)
        up_s[:, :, :, hdr + rows:, :] = jnp.zeros((2, 2, CONV_W, hdr, FF_BLOCK), F32)

    x1 = x_ref[...]
    h = _rms(x1, g_pre_ref[...]).astype(BF16)

    def half_cols(j, half):
        c0 = half * D_FF + j * FF_BLOCK
        return slice(c0, c0 + FF_BLOCK)

    def project(j):
        for half in range(2):
            cols = half_cols(j, half)
            up = _dot(h, w_up_ref[:, cols])
            for s in range(CONV_W):
                buf = up_s.at[j % 2, half, s]
                if s:
                    buf[hdr:2 * hdr, :] = carry_s[s - 1, :, cols]
                buf[hdr + s:hdr + s + rows, :] = up
                if s:
                    carry_s[s - 1, :, cols] = buf[hdr + rows:2 * hdr + rows, :]

    def conv(j, half):
        cols = half_cols(j, half)
        w = w_conv_ref[:, cols]
        taps = [up_s[j % 2, half, s, hdr:hdr + rows, :] for s in range(CONV_W)]
        return (b_conv_ref[:, cols] + taps[2] * w[0:1, :] + taps[1] * w[1:2, :] + taps[0] * w[2:3, :])

    def down(j):
        k_rows = slice(j * FF_BLOCK, (j + 1) * FF_BLOCK)
        return _dot_wide(act_s[:, k_rows], w_down_ref, k_rows)

    project(0)
    f = None
    for j in range(N_FF_BLOCKS):
        if j + 1 < N_FF_BLOCKS:
            project(j + 1)
        if j == 1:
            f = down(0)
        elif j > 1:
            f = f + down(j - 1)
        gate = conv(j, 0)
        act_s[:, j * FF_BLOCK:(j + 1) * FF_BLOCK] = (
            (gate * conv(j, 1)) * _gelu_gate(gate)).astype(BF16)
    f = f + down(N_FF_BLOCKS - 1)

    y_ref[...] = _ffn_tail(x1, f, p_ref[...], g_post_ref[...], g_ple_in_ref[...], w_pg_ref,
                           w_ple_ref, g_ple_post_ref[...])

    @pl.when(t == pl.num_programs(1) - 1)
    def _():
        cs_ref[...] = carry_s[CONV_W - 2, 0:CONV_W - 1, :]


def _ffn_sample_kernel(x_ref, p_ref, cin_ref, g_pre_ref, w_up_ref, w_conv_ref, b_conv_ref, w_down_ref,
                       g_post_ref, g_ple_in_ref, w_pg_ref, w_ple_ref, g_ple_post_ref,
                       y_ref, cs_ref, act_s, *, nseq, seq_len):
    x1 = jnp.concatenate([x_ref[:, j, :] for j in range(seq_len)], axis=0)
    p = jnp.concatenate([p_ref[:, j, :] for j in range(seq_len)], axis=0)
    h = _rms(x1, g_pre_ref[...]).astype(BF16)

    def conv_half(col0):
        cols = slice(col0, col0 + FF_BLOCK)
        up = _dot(h, w_up_ref[:, cols])
        blocks = [cin_ref[:, i, cols] for i in range(CONV_W - 1)]
        blocks += [up[j * nseq:(j + 1) * nseq, :] for j in range(seq_len)]
        w = w_conv_ref[:, cols]
        b = b_conv_ref[:, cols]
        conv = jnp.concatenate(
            [b + blocks[j] * w[0:1, :] + blocks[j + 1] * w[1:2, :] + blocks[j + 2] * w[2:3, :]
             for j in range(seq_len)], axis=0)
        for i in range(CONV_W - 1):
            cs_ref[:, i, cols] = blocks[seq_len + i]
        return conv

    f = None
    for grp in _ff_groups():
        for j in grp:
            gate = conv_half(j * FF_BLOCK)
            val = conv_half(D_FF + j * FF_BLOCK)
            act_s[:, j * FF_BLOCK:(j + 1) * FF_BLOCK] = ((gate * val) * _gelu_gate(gate)).astype(BF16)
        k_rows = slice(grp[0] * FF_BLOCK, (grp[-1] + 1) * FF_BLOCK)
        part = _dot_wide(act_s[:, k_rows], w_down_ref, k_rows)
        f = part if f is None else f + part

    y = _ffn_tail(x1, f, p, g_post_ref[...], g_ple_in_ref[...], w_pg_ref, w_ple_ref,
                  g_ple_post_ref[...])
    for j in range(seq_len):
        y_ref[:, j, :] = y[j * nseq:(j + 1) * nseq, :]


def _cast_kernel(*refs):
    n = (len(refs) - 3) // 2
    w_in_ref, srcs = refs[0], refs[1:1 + n]
    ga_ref, sg_ref, dsts = refs[1 + n], refs[2 + n], refs[3 + n:]

    glr0 = IN_COLS_GLA
    u0 = glr0 + GLA_GATE_RANK
    ga_ref[...] = w_in_ref[:, :glr0].astype(BF16)
    sg_ref[:, :COL_GLR] = w_in_ref[:, u0:u0 + COL_GLR].astype(BF16)
    tile = w_in_ref[:, glr0:glr0 + LANES]
    lane = lax.broadcasted_iota(jnp.int32, tile.shape, 1)
    sg_ref[:, COL_GLR:] = jnp.where(lane < GLA_GATE_RANK, tile, 0.0).astype(BF16)

    for src, dst in zip(srcs, dsts):
        cols = src.shape[1]
        dst[:, :cols] = src[...].astype(BF16)
        if dst.shape[1] > cols:
            dst[:, cols:] = jnp.zeros((dst.shape[0], dst.shape[1] - cols), BF16)


def _whole(shape):
    n = len(shape)
    return pl.BlockSpec(shape, lambda *_: (0,) * n)


def _full(shape):
    n = len(shape)
    return pl.BlockSpec(shape, lambda *_: (0,) * n, pipeline_mode=pl.Buffered(1))


def _gla_masks(rows, span):
    i = np.arange(rows)
    same = (i[:, None] // span) == (i[None, :] // span)
    causal = same & (i[None, :] <= i[:, None])
    return (jnp.asarray(causal, BF16), jnp.asarray(same, BF16), jnp.asarray(causal, F32))


def kernel(x_prompt, x_sample, state_gla, state_ffn_conv, p_prompt, p_sample, g_mix_pre, w_in, w_gla_gate, b_gla_gate, g_gla_out, g_sgu_ln, b_sgu_ln, w_spatial, b_spatial, w_out, g_mix_post, g_ffn_pre, w_up, w_conv, b_conv, w_down, g_ffn_post, g_ple_in, w_ple_gate, w_ple, g_ple_post):
    depth = x_prompt.ndim - 2
    assert w_in.shape[0] == 1 and depth == 1
    nb, seq, _ = x_prompt.shape
    ns, dseq, _ = x_sample.shape
    assert seq % PROMPT_TILE == 0 and PROMPT_TILE % GMLP_CHUNK == 0 and seq % FFN_PROMPT_TILE == 0
    assert dseq == 4 and (ns * dseq) % SAMPLE_TILE == 0 and SAMPLE_TILE % GMLP_CHUNK == 0

    w_gate_p = jnp.concatenate(
        [w_gla_gate[0], jnp.zeros((LANES - GLA_GATE_RANK, GLA_KEY), w_gla_gate.dtype)], axis=0).astype(BF16)
    row = lambda a: a.reshape(1, -1)
    g_gla_t = jnp.tile(g_gla_out[0], GLA_HEADS).reshape(1, -1)
    n_cast = CAST_STEPS
    in_cols = w_in.shape[-1]
    assert in_cols == IN_COLS_GLA + GLA_GATE_RANK + COL_GLR
    cast_src = (w_out[0], w_up[0], w_down[0], w_ple_gate[0], w_ple[0])
    cast_cols_in = (D_MODEL, 2 * D_FF, D_MODEL, D_MODEL, D_MODEL)
    cast_cols_out = (D_MODEL + LANES, 2 * D_FF, D_MODEL + LANES, D_MODEL + LANES, D_MODEL + LANES)
    cast_rows = tuple(w.shape[0] for w in cast_src)
    assert all(r % (n_cast * 2 * SUBLANES) == 0 for r in cast_rows + (D_MODEL,))
    slab = lambda r, c: pl.BlockSpec((r // n_cast, c), lambda i: (i, 0))
    w_ga, w_sg, w_out_b, w_up_b, w_down_b, w_pg_b, w_ple_b = pl.pallas_call(
        _cast_kernel,
        grid=(n_cast,),
        in_specs=[pl.BlockSpec((None, D_MODEL // n_cast, in_cols), lambda i: (0, i, 0))]
        + [slab(r, c) for r, c in zip(cast_rows, cast_cols_in)],
        out_specs=[slab(D_MODEL, IN_COLS_GLA), slab(D_MODEL, IN_COLS_SGU)]
        + [slab(r, c) for r, c in zip(cast_rows, cast_cols_out)],
        out_shape=[jax.ShapeDtypeStruct((D_MODEL, IN_COLS_GLA), BF16),
                   jax.ShapeDtypeStruct((D_MODEL, IN_COLS_SGU), BF16)]
        + [jax.ShapeDtypeStruct((r, c), BF16) for r, c in zip(cast_rows, cast_cols_out)],
        compiler_params=pltpu.CompilerParams(
            dimension_semantics=("arbitrary",), vmem_limit_bytes=VMEM_LIMIT_BYTES),
        name="weight_cast",
    )(w_in, *cast_src)

    def spatial(c):
        i = np.arange(GMLP_CHUNK)
        keep = ((i[:, None] // c) == (i[None, :] // c)) & (i[None, :] <= i[:, None])
        e = jnp.asarray((i[:, None] % c) == np.arange(c)[None, :], w_spatial.dtype)
        hi = lax.Precision.HIGHEST
        wbd = jnp.einsum('ia,hab,jb->hij', e, w_spatial[0][:, :c, :c], e, precision=hi)
        wbd = wbd * jnp.asarray(keep, w_spatial.dtype)
        bias = jnp.einsum('ia,ha->ih', e, b_spatial[0][:, :c], precision=hi)
        return wbd.astype(BF16), jnp.repeat(bias, GMLP_DC, axis=1)

    mixer_weights = lambda wsp, bsp, masks: (
        row(g_mix_pre), w_ga, w_sg, w_gate_p, row(b_gla_gate), g_gla_t, row(g_sgu_ln), row(b_sgu_ln),
        wsp, bsp, w_out_b, row(g_mix_post)) + masks

    def mixer_weight_specs(rows):
        return [
            _full((1, D_MODEL)), _full((D_MODEL, IN_COLS_GLA)), _full((D_MODEL, IN_COLS_SGU)),
            _full((LANES, GLA_KEY)),
            _full((1, GLA_KEY)), _full((1, GLA_WIDTH)), _full((1, GMLP_WIDTH)), _full((1, GMLP_WIDTH)),
            _full((GMLP_HEADS, GMLP_CHUNK, GMLP_CHUNK)), _full((GMLP_CHUNK, GMLP_WIDTH)),
            _full((D_MODEL, D_MODEL + LANES)), _full((1, D_MODEL)),
            _full((rows, rows)), _full((rows, rows)), _full((rows, rows))]

    tl = PROMPT_TILE
    wsp_p, bsp_p = spatial(GMLP_CHUNK)
    x1_p, gla_p = pl.pallas_call(
        functools.partial(_mixer_kernel, rows=tl, sample=False),
        grid=(nb, seq // tl),
        in_specs=[pl.BlockSpec((None, tl, D_MODEL), lambda b, t: (b, t, 0))] + mixer_weight_specs(tl),
        out_specs=[pl.BlockSpec((None, tl, D_MODEL), lambda b, t: (b, t, 0)),
                   pl.BlockSpec((None, GLA_HEADS, GLA_DK, GLA_DV), lambda b, t: (b, 0, 0, 0))],
        out_shape=[jax.ShapeDtypeStruct((nb, seq, D_MODEL), F32),
                   jax.ShapeDtypeStruct((nb, GLA_HEADS, GLA_DK, GLA_DV), F32)],
        scratch_shapes=[pltpu.VMEM((tl, GLA_WIDTH), F32),
                        pltpu.VMEM((HEAD_PAIRS, 2 * GLA_DV, 2 * GLA_DK), F32)],
        compiler_params=pltpu.CompilerParams(
            dimension_semantics=("arbitrary", "arbitrary"), vmem_limit_bytes=VMEM_LIMIT_BYTES),
        name="mixer_prompt",
    )(x_prompt, *mixer_weights(wsp_p, bsp_p, _gla_masks(tl, GLA_CHUNK)))

    ts = SAMPLE_TILE
    seqs_per_tile = ts // dseq
    wsp_s, bsp_s = spatial(dseq)
    x1_s, gla_s, vrows_s = pl.pallas_call(
        functools.partial(_mixer_kernel, rows=ts, sample=True),
        grid=(ns * dseq // ts,),
        in_specs=[pl.BlockSpec((ts, D_MODEL), lambda i: (i, 0)),
                  pl.BlockSpec((seqs_per_tile, HEAD_PAIRS, 2 * GLA_DK, GLA_DV), lambda i: (i, 0, 0, 0))]
        + mixer_weight_specs(ts),
        out_specs=[pl.BlockSpec((ts, D_MODEL), lambda i: (i, 0)),
                   pl.BlockSpec((seqs_per_tile, HEAD_PAIRS, 2 * GLA_DK, GLA_DV), lambda i: (i, 0, 0, 0)),
                   pl.BlockSpec((ts, GMLP_WIDTH), lambda i: (i, 0))],
        out_shape=[jax.ShapeDtypeStruct((ns * dseq, D_MODEL), F32),
                   jax.ShapeDtypeStruct((ns, HEAD_PAIRS, 2 * GLA_DK, GLA_DV), F32),
                   jax.ShapeDtypeStruct((ns * dseq, GMLP_WIDTH), F32)],
        scratch_shapes=[pltpu.VMEM((ts, GLA_WIDTH), F32)],
        compiler_params=pltpu.CompilerParams(
            dimension_semantics=("arbitrary",), vmem_limit_bytes=VMEM_LIMIT_BYTES),
        name="mixer_sample",
    )(x_sample.reshape(ns * dseq, D_MODEL),
      state_gla[0].reshape(ns, HEAD_PAIRS, 2 * GLA_DK, GLA_DV),
      *mixer_weights(wsp_s, bsp_s, _gla_masks(ts, dseq)))

    ffn_weights = (row(g_ffn_pre), w_up_b, w_conv[0], row(b_conv), w_down_b, row(g_ffn_post),
                   row(g_ple_in), w_pg_b, w_ple_b, row(g_ple_post))
    ffn_weight_specs = [
        _full((1, D_MODEL)), _full((D_MODEL, 2 * D_FF)), _full((CONV_W, 2 * D_FF)), _full((1, 2 * D_FF)),
        _full((D_FF, D_MODEL + LANES)), _full((1, D_MODEL)), _full((1, D_MODEL)),
        _full((D_MODEL, D_MODEL + LANES)), _full((PLE_DIM, D_MODEL + LANES)), _full((1, D_MODEL))]

    tf = FFN_PROMPT_TILE
    y_p, conv_p = pl.pallas_call(
        functools.partial(_ffn_prompt_kernel, rows=tf),
        grid=(nb, seq // tf),
        in_specs=[pl.BlockSpec((None, tf, D_MODEL), lambda b, t: (b, t, 0)),
                  pl.BlockSpec((None, tf, PLE_DIM), lambda b, t: (b, t, 0))] + ffn_weight_specs,
        out_specs=[pl.BlockSpec((None, tf, D_MODEL), lambda b, t: (b, t, 0)),
                   pl.BlockSpec((None, CONV_W - 1, 2 * D_FF), lambda b, t: (b, 0, 0))],
        out_shape=[jax.ShapeDtypeStruct((nb, seq, D_MODEL), F32),
                   jax.ShapeDtypeStruct((nb, CONV_W - 1, 2 * D_FF), F32)],
        scratch_shapes=[pltpu.VMEM((CONV_W - 1, SUBLANES, 2 * D_FF), F32),
                        pltpu.VMEM((2, 2, CONV_W, tf + 2 * SUBLANES, FF_BLOCK), F32),
                        pltpu.VMEM((tf, D_FF), BF16)],
        compiler_params=pltpu.CompilerParams(
            dimension_semantics=("arbitrary", "arbitrary"), vmem_limit_bytes=VMEM_LIMIT_BYTES),
        name="ffn_prompt",
    )(x1_p, p_prompt[0], *ffn_weights)

    y_s, conv_s = pl.pallas_call(
        functools.partial(_ffn_sample_kernel, nseq=ns, seq_len=dseq),
        grid=(1,),
        in_specs=[_whole((ns, dseq, D_MODEL)), _whole((ns, dseq, PLE_DIM)),
                  _whole((ns, CONV_W - 1, 2 * D_FF))] + ffn_weight_specs,
        out_specs=[_whole((ns, dseq, D_MODEL)), _whole((ns, CONV_W - 1, 2 * D_FF))],
        out_shape=[jax.ShapeDtypeStruct((ns, dseq, D_MODEL), F32),
                   jax.ShapeDtypeStruct((ns, CONV_W - 1, 2 * D_FF), F32)],
        scratch_shapes=[pltpu.VMEM((ns * dseq, D_FF), BF16)],
        compiler_params=pltpu.CompilerParams(
            dimension_semantics=("arbitrary",), vmem_limit_bytes=VMEM_LIMIT_BYTES),
        name="ffn_sample",
    )(x1_s.reshape(ns, dseq, D_MODEL), p_sample[0], state_ffn_conv[0], *ffn_weights)

    return (y_p,
            y_s,
            gla_p[None],
            gla_s.reshape(1, ns, GLA_HEADS, GLA_DK, GLA_DV),
            conv_p[None],
            conv_s[None],
            vrows_s.reshape(1, ns, dseq, GMLP_WIDTH))
```

```python
import functools

import numpy as np
import jax
import jax.numpy as jnp
from jax import lax
from jax.experimental import pallas as pl
from jax.experimental.pallas import tpu as pltpu

D_MODEL = 1024
GLA_HEADS = 4
GLA_DK = 64
GLA_DV = 128
GLA_KEY = GLA_HEADS * GLA_DK
GLA_WIDTH = GLA_HEADS * GLA_DV
GMLP_HEADS = 4
GMLP_DC = 128
GMLP_WIDTH = GMLP_HEADS * GMLP_DC
GLA_GATE_RANK = 16
GLA_TAU = 16.0
GLA_CHUNK = 64
GMLP_CHUNK = 128
D_FF = 2816
CONV_W = 3
PLE_DIM = 256
EPS = 1e-6

LANES = 128
SUBLANES = 8
MXU_N = 256
HEAD_PAIRS = GLA_HEADS // 2

COL_QK = 0
COL_V = COL_QK + 2 * GLA_KEY
COL_R = COL_V + GLA_WIDTH
IN_COLS_GLA = COL_R + GLA_WIDTH
COL_U = 0
COL_VS = COL_U + GMLP_WIDTH
COL_GLR = COL_VS + GMLP_WIDTH
IN_COLS_SGU = COL_GLR + LANES

FF_BLOCK = 256
N_FF_BLOCKS = D_FF // FF_BLOCK
FF_GROUP = 4

PROMPT_TILE = 256
FFN_PROMPT_TILE = 256
SAMPLE_TILE = 128
CAST_STEPS = 8

VMEM_LIMIT_BYTES = 56 * 1024 * 1024

BF16 = jnp.bfloat16
F32 = jnp.float32


def _dot(a, b):
    return jnp.dot(a, b, preferred_element_type=F32)


def _dot_wide(a, w_ref, rows=slice(None)):
    return jnp.concatenate(
        [_dot(a, w_ref[rows, c:c + MXU_N]) for c in range(0, D_MODEL, MXU_N)], axis=-1)


def _dot_nt(a, b):
    return lax.dot_general(a, b, (((1,), (1,)), ((), ())), preferred_element_type=F32)


def _dot_tn(a, b):
    return lax.dot_general(a, b, (((0,), (0,)), ((), ())), preferred_element_type=F32)


def _rms(x, g):
    return x * lax.rsqrt(jnp.mean(x * x, axis=-1, keepdims=True) + EPS) * g


def _gelu_gate(x):
    c = -2.0 * np.sqrt(2.0 / np.pi) * np.log2(np.e)
    a1 = np.float32(c)
    a3 = np.float32(c * 0.044715)
    return 1.0 / (1.0 + jnp.exp2(x * (a3 * (x * x) + a1)))


def _gelu(x):
    return x * _gelu_gate(x)


def _sigmoid(x):
    return 1.0 / (1.0 + jnp.exp(-x))


def _log_sigmoid(x):
    return jnp.minimum(x, 0.0) - jnp.log1p(jnp.exp(-jnp.abs(x)))


def _split_bf16(x):
    hi = x.astype(BF16)
    lo = (x - hi.astype(F32)).astype(BF16)
    return hi, lo


def _mixer_kernel(*refs, rows, sample):
    if sample:
        (x_ref, sin_ref, g_pre_ref, w_ga_ref, w_sg_ref, w_gate_ref, b_gate_ref, g_gla_ref, g_ln_ref, b_ln_ref,
         w_sp_ref, b_sp_ref, w_out_ref, g_post_ref, cum_ref, tot_ref, causal_ref,
         y_ref, sout_ref, vn_ref, o_s) = refs
        st_s = None
    else:
        (x_ref, g_pre_ref, w_ga_ref, w_sg_ref, w_gate_ref, b_gate_ref, g_gla_ref, g_ln_ref, b_ln_ref,
         w_sp_ref, b_sp_ref, w_out_ref, g_post_ref, cum_ref, tot_ref, causal_ref,
         y_ref, sout_ref, o_s, st_s) = refs
        t = pl.program_id(1)

        @pl.when(t == 0)
        def _():
            st_s[...] = jnp.zeros_like(st_s)

    x = x_ref[...]
    h = _rms(x, g_pre_ref[...]).astype(BF16)

    glr = _dot(h, w_sg_ref[:, COL_GLR:IN_COLS_SGU])
    logit = _dot(glr.astype(BF16), w_gate_ref[...]) + b_gate_ref[...]
    u_pre = _dot(h, w_sg_ref[:, COL_U:COL_VS])
    vs_pre = _dot(h, w_sg_ref[:, COL_VS:COL_GLR])
    qk = _dot(h, w_ga_ref[:, COL_QK:COL_V])
    v = _dot(h, w_ga_ref[:, COL_V:COL_R])
    r = _dot(h, w_ga_ref[:, COL_R:IN_COLS_GLA])
    g_hi, g_lo = _split_bf16(_log_sigmoid(logit) * (1.0 / GLA_TAU))

    u = _gelu(u_pre)
    vs = _gelu(vs_pre)
    vn_parts = []
    for i in range(GMLP_HEADS):
        vh = vs[:, i * GMLP_DC:(i + 1) * GMLP_DC]
        mu = jnp.mean(vh, axis=-1, keepdims=True)
        d = vh - mu
        var = jnp.mean(d * d, axis=-1, keepdims=True)
        vn_parts.append(d * lax.rsqrt(var + EPS))
    vn = jnp.concatenate(vn_parts, axis=-1) * g_ln_ref[...] + b_ln_ref[...]
    if sample:
        vn_ref[...] = vn
    vn_bf = vn.astype(BF16)

    cum = cum_ref[...]
    tot = tot_ref[...]
    b = _dot(cum, g_hi) + _dot(cum, g_lo)
    bl = _dot(tot, g_hi) + _dot(tot, g_lo)

    mixed_parts = []
    for cch in range(rows // GMLP_CHUNK):
        crows = slice(cch * GMLP_CHUNK, (cch + 1) * GMLP_CHUNK)
        mixed_parts.append(jnp.concatenate(
            [_dot(w_sp_ref[i], vn_bf[crows, i * GMLP_DC:(i + 1) * GMLP_DC]) for i in range(GMLP_HEADS)],
            axis=-1) + b_sp_ref[...])

    so = u * jnp.concatenate(mixed_parts, axis=0)
    mix = _dot_wide(so.astype(BF16), w_out_ref, slice(GLA_WIDTH, GLA_WIDTH + GMLP_WIDTH))

    q = qk[:, :GLA_KEY] * (GLA_DK ** -0.5)
    k = qk[:, GLA_KEY:]
    v_bf = v.astype(BF16)
    qb = q * jnp.exp(b)
    kn = k * jnp.exp(-b)
    kd = k * jnp.exp(bl - b)
    causal = causal_ref[...]

    lane = lax.broadcasted_iota(jnp.int32, (rows, LANES), 1)
    head_lane_masks = (lane < GLA_DK, lane >= GLA_DK)

    prompt_pairs = []
    for p in range(HEAD_PAIRS):
        ls = slice(p * LANES, (p + 1) * LANES)
        qbp = qb[:, ls]
        knp = kn[:, ls].astype(BF16)
        kdp = kd[:, ls]
        o_intra = []
        for hh in range(2):
            hd = 2 * p + hh
            qm = jnp.where(head_lane_masks[hh], qbp, 0.0).astype(BF16)
            s = _dot_nt(qm, knp) * causal
            o_intra.append(_dot(s.astype(BF16), v_bf[:, hd * GLA_DV:(hd + 1) * GLA_DV]))

        if not sample:
            st_shape = (2 * GLA_DV, 2 * GLA_DK)
            ri = lax.broadcasted_iota(jnp.int32, st_shape, 0)
            ci = lax.broadcasted_iota(jnp.int32, st_shape, 1)
            same_head = (ri < GLA_DV) == (ci < GLA_DK)
            kdp_bf = kdp.astype(BF16)
            upd = []
            for c in range(rows // GLA_CHUNK):
                rws = slice(c * GLA_CHUNK, (c + 1) * GLA_CHUNK)
                inc = _dot_tn(v_bf[rws, 2 * p * GLA_DV:(2 * p + 2) * GLA_DV], kdp_bf[rws, :])
                upd.append(jnp.where(same_head, inc, 0.0))
            prompt_pairs.append((ls, qbp.astype(BF16), o_intra, upd))
        else:
            seq_len = 4
            seqs_per_grp = SUBLANES // seq_len
            kdp_t = kdp.T.astype(BF16)
            dec_t = jnp.exp(bl[:, ls]).T
            sub = lax.broadcasted_iota(jnp.int32, (SUBLANES, LANES), 0)
            crow = lax.broadcasted_iota(jnp.int32, (rows, LANES), 0)
            for grp in range(rows // SUBLANES):
                g0 = grp * SUBLANES
                qg = qbp[g0:g0 + SUBLANES, :]
                o_acc = [o_intra[hh][g0:g0 + SUBLANES, :] for hh in range(2)]
                for j in range(seqs_per_grp):
                    sidx = grp * seqs_per_grp + j
                    s_pair = sin_ref[sidx, p]
                    s_bf = s_pair.astype(BF16)
                    in_seq = (sub >= j * seq_len) & (sub < (j + 1) * seq_len)
                    in_seq_c = (crow >= g0 + j * seq_len) & (crow < g0 + (j + 1) * seq_len)
                    new_rows = []
                    for hh in range(2):
                        hd = 2 * p + hh
                        qsel = jnp.where(in_seq & head_lane_masks[hh][:SUBLANES], qg, 0.0).astype(BF16)
                        o_acc[hh] = o_acc[hh] + _dot(qsel, s_bf)
                        vsel = jnp.where(in_seq_c, v[:, hd * GLA_DV:(hd + 1) * GLA_DV], 0.0).astype(BF16)
                        new_rows.append(_dot(kdp_t[hh * GLA_DK:(hh + 1) * GLA_DK, :], vsel))
                    col = g0 + j * seq_len
                    sout_ref[sidx, p] = s_pair * dec_t[:, col:col + 1] + jnp.concatenate(new_rows, axis=0)
                for hh in range(2):
                    hd = 2 * p + hh
                    o_s[g0:g0 + SUBLANES, hd * GLA_DV:(hd + 1) * GLA_DV] = o_acc[hh]

    if not sample:
        st = [st_s[p] for p in range(HEAD_PAIRS)]
        for c in range(rows // GLA_CHUNK):
            rws = slice(c * GLA_CHUNK, (c + 1) * GLA_CHUNK)
            for p, (ls, qbp_bf, o_intra, upd) in enumerate(prompt_pairs):
                o_inter = _dot_nt(qbp_bf[rws, :], st[p].astype(BF16))
                for hh in range(2):
                    hd = 2 * p + hh
                    o_s[rws, hd * GLA_DV:(hd + 1) * GLA_DV] = (
                        o_intra[hh][rws, :] + o_inter[:, hh * GLA_DV:(hh + 1) * GLA_DV])
                dec = jnp.exp(bl[c * GLA_CHUNK:c * GLA_CHUNK + 1, ls])
                st[p] = st[p] * dec + upd[c]
        for p in range(HEAD_PAIRS):
            st_s[p] = st[p]

    out_gate = r * _sigmoid(r)
    half = rows // 2
    for hr in (slice(0, half), slice(half, rows)):
        o = o_s[hr, :]
        og = jnp.concatenate(
            [_rms(o[:, i * GLA_DV:(i + 1) * GLA_DV], 1.0) for i in range(GLA_HEADS)], axis=-1)
        og = og * g_gla_ref[...] * out_gate[hr, :]
        mix_h = mix[hr, :] + _dot_wide(og.astype(BF16), w_out_ref, slice(0, GLA_WIDTH))
        y_ref[hr, :] = x[hr, :] + _rms(mix_h, g_post_ref[...])

    if not sample:
        @pl.when(t == pl.num_programs(1) - 1)
        def _():
            for p in range(HEAD_PAIRS):
                st = st_s[p]
                sout_ref[2 * p] = st[:GLA_DV, :].T[:GLA_DK, :]
                sout_ref[2 * p + 1] = st[GLA_DV:, :].T[GLA_DK:, :]


def _ffn_tail(x1, f, p, g_ffn_post, g_ple_in, w_pg_ref, w_ple_ref, g_ple_post):
    x2 = x1 + _rms(f, g_ffn_post)
    hp = _rms(x2, g_ple_in).astype(BF16)
    gate = _sigmoid(_dot_wide(hp, w_pg_ref))
    pe = _dot_wide(p.astype(BF16), w_ple_ref)
    return x2 + _rms(pe * gate, g_ple_post)


def _ff_groups():
    return [range(g, min(g + FF_GROUP, N_FF_BLOCKS)) for g in range(0, N_FF_BLOCKS, FF_GROUP)]


def _ffn_prompt_kernel(x_ref, p_ref, g_pre_ref, w_up_ref, w_conv_ref, b_conv_ref, w_down_ref,
                       g_post_ref, g_ple_in_ref, w_pg_ref, w_ple_ref, g_ple_post_ref,
                       y_ref, cs_ref,
                       carry_s, up_s, act_s, *, rows):
    t = pl.program_id(1)
    hdr = SUBLANES

    @pl.when(t == 0)
    def _():
        carry_s[...] = jnp.zeros_like(carry_s)
        up_s[:, :, :, hdr + rows:, :] = jnp.zeros((2, 2, CONV_W, hdr, FF_BLOCK), F32)

    x1 = x_ref[...]
    h = _rms(x1, g_pre_ref[...]).astype(BF16)

    def half_cols(j, half):
        c0 = half * D_FF + j * FF_BLOCK
        return slice(c0, c0 + FF_BLOCK)

    def project(j):
        for half in range(2):
            cols = half_cols(j, half)
            up = _dot(h, w_up_ref[:, cols])
            for s in range(CONV_W):
                buf = up_s.at[j % 2, half, s]
                if s:
                    buf[hdr:2 * hdr, :] = carry_s[s - 1, :, cols]
                buf[hdr + s:hdr + s + rows, :] = up
                if s:
                    carry_s[s - 1, :, cols] = buf[hdr + rows:2 * hdr + rows, :]

    def conv(j, half):
        cols = half_cols(j, half)
        w = w_conv_ref[:, cols]
        taps = [up_s[j % 2, half, s, hdr:hdr + rows, :] for s in range(CONV_W)]
        return (b_conv_ref[:, cols] + taps[2] * w[0:1, :] + taps[1] * w[1:2, :] + taps[0] * w[2:3, :])

    def down(j):
        k_rows = slice(j * FF_BLOCK, (j + 1) * FF_BLOCK)
        return _dot_wide(act_s[:, k_rows], w_down_ref, k_rows)

    project(0)
    f = None
    for j in range(N_FF_BLOCKS):
        if j + 1 < N_FF_BLOCKS:
            project(j + 1)
        if j == 1:
            f = down(0)
        elif j > 1:
            f = f + down(j - 1)
        gate = conv(j, 0)
        act_s[:, j * FF_BLOCK:(j + 1) * FF_BLOCK] = (
            (gate * conv(j, 1)) * _gelu_gate(gate)).astype(BF16)
    f = f + down(N_FF_BLOCKS - 1)

    y_ref[...] = _ffn_tail(x1, f, p_ref[...], g_post_ref[...], g_ple_in_ref[...], w_pg_ref,
                           w_ple_ref, g_ple_post_ref[...])

    @pl.when(t == pl.num_programs(1) - 1)
    def _():
        cs_ref[...] = carry_s[CONV_W - 2, 0:CONV_W - 1, :]


def _ffn_sample_kernel(x_ref, p_ref, cin_ref, g_pre_ref, w_up_ref, w_conv_ref, b_conv_ref, w_down_ref,
                       g_post_ref, g_ple_in_ref, w_pg_ref, w_ple_ref, g_ple_post_ref,
                       y_ref, cs_ref, act_s, *, nseq, seq_len):
    x1 = jnp.concatenate([x_ref[:, j, :] for j in range(seq_len)], axis=0)
    p = jnp.concatenate([p_ref[:, j, :] for j in range(seq_len)], axis=0)
    h = _rms(x1, g_pre_ref[...]).astype(BF16)

    def conv_half(col0):
        cols = slice(col0, col0 + FF_BLOCK)
        up = _dot(h, w_up_ref[:, cols])
        blocks = [cin_ref[:, i, cols] for i in range(CONV_W - 1)]
        blocks += [up[j * nseq:(j + 1) * nseq, :] for j in range(seq_len)]
        w = w_conv_ref[:, cols]
        b = b_conv_ref[:, cols]
        conv = jnp.concatenate(
            [b + blocks[j] * w[0:1, :] + blocks[j + 1] * w[1:2, :] + blocks[j + 2] * w[2:3, :]
             for j in range(seq_len)], axis=0)
        for i in range(CONV_W - 1):
            cs_ref[:, i, cols] = blocks[seq_len + i]
        return conv

    f = None
    for grp in _ff_groups():
        for j in grp:
            gate = conv_half(j * FF_BLOCK)
            val = conv_half(D_FF + j * FF_BLOCK)
            act_s[:, j * FF_BLOCK:(j + 1) * FF_BLOCK] = ((gate * val) * _gelu_gate(gate)).astype(BF16)
        k_rows = slice(grp[0] * FF_BLOCK, (grp[-1] + 1) * FF_BLOCK)
        part = _dot_wide(act_s[:, k_rows], w_down_ref, k_rows)
        f = part if f is None else f + part

    y = _ffn_tail(x1, f, p, g_post_ref[...], g_ple_in_ref[...], w_pg_ref, w_ple_ref,
                  g_ple_post_ref[...])
    for j in range(seq_len):
        y_ref[:, j, :] = y[j * nseq:(j + 1) * nseq, :]


def _cast_kernel(*refs):
    n = (len(refs) - 3) // 2
    w_in_t_ref, srcs = refs[0], refs[1:1 + n]
    ga_ref, sg_ref, dsts = refs[1 + n], refs[2 + n], refs[3 + n:]

    glr0 = IN_COLS_GLA
    u0 = glr0 + GLA_GATE_RANK
    ga_ref[...] = w_in_t_ref[:glr0, :].T.astype(BF16)
    sg_ref[:, :COL_GLR] = w_in_t_ref[u0:u0 + COL_GLR, :].T.astype(BF16)
    tile = w_in_t_ref[glr0:glr0 + LANES, :].T
    lane = lax.broadcasted_iota(jnp.int32, tile.shape, 1)
    sg_ref[:, COL_GLR:] = jnp.where(lane < GLA_GATE_RANK, tile, 0.0).astype(BF16)

    for src, dst in zip(srcs, dsts):
        cols = src.shape[1]
        dst[:, :cols] = src[...].astype(BF16)
        if dst.shape[1] > cols:
            dst[:, cols:] = jnp.zeros((dst.shape[0], dst.shape[1] - cols), BF16)


def _whole(shape):
    n = len(shape)
    return pl.BlockSpec(shape, lambda *_: (0,) * n)


def _full(shape):
    n = len(shape)
    return pl.BlockSpec(shape, lambda *_: (0,) * n, pipeline_mode=pl.Buffered(1))


def _gla_masks(rows, span):
    i = np.arange(rows)
    same = (i[:, None] // span) == (i[None, :] // span)
    causal = same & (i[None, :] <= i[:, None])
    return (jnp.asarray(causal, BF16), jnp.asarray(same, BF16), jnp.asarray(causal, F32))


def kernel(x_prompt, x_sample, state_gla, state_ffn_conv, p_prompt, p_sample, g_mix_pre, w_in, w_gla_gate, b_gla_gate, g_gla_out, g_sgu_ln, b_sgu_ln, w_spatial, b_spatial, w_out, g_mix_post, g_ffn_pre, w_up, w_conv, b_conv, w_down, g_ffn_post, g_ple_in, w_ple_gate, w_ple, g_ple_post):
    depth = x_prompt.ndim - 2
    assert w_in.shape[0] == 1 and depth == 1
    nb, seq, _ = x_prompt.shape
    ns, dseq, _ = x_sample.shape
    assert seq % PROMPT_TILE == 0 and PROMPT_TILE % GMLP_CHUNK == 0 and seq % FFN_PROMPT_TILE == 0
    assert dseq == 4 and (ns * dseq) % SAMPLE_TILE == 0 and SAMPLE_TILE % GMLP_CHUNK == 0

    w_gate_p = jnp.concatenate(
        [w_gla_gate[0], jnp.zeros((LANES - GLA_GATE_RANK, GLA_KEY), w_gla_gate.dtype)], axis=0).astype(BF16)
    row = lambda a: a.reshape(1, -1)
    g_gla_t = jnp.tile(g_gla_out[0], GLA_HEADS).reshape(1, -1)
    n_cast = CAST_STEPS
    in_cols = w_in.shape[-1]
    assert in_cols == IN_COLS_GLA + GLA_GATE_RANK + COL_GLR
    cast_src = (w_out[0], w_up[0], w_down[0], w_ple_gate[0], w_ple[0])
    cast_cols_in = (D_MODEL, 2 * D_FF, D_MODEL, D_MODEL, D_MODEL)
    cast_cols_out = (D_MODEL + LANES, 2 * D_FF, D_MODEL + LANES, D_MODEL + LANES, D_MODEL + LANES)
    cast_rows = tuple(w.shape[0] for w in cast_src)
    assert all(r % (n_cast * 2 * SUBLANES) == 0 for r in cast_rows + (D_MODEL,))
    slab = lambda r, c: pl.BlockSpec((r // n_cast, c), lambda i: (i, 0))
    w_ga, w_sg, w_out_b, w_up_b, w_down_b, w_pg_b, w_ple_b = pl.pallas_call(
        _cast_kernel,
        grid=(n_cast,),
        in_specs=[pl.BlockSpec((None, in_cols, D_MODEL // n_cast), lambda i: (0, 0, i))]
        + [slab(r, c) for r, c in zip(cast_rows, cast_cols_in)],
        out_specs=[slab(D_MODEL, IN_COLS_GLA), slab(D_MODEL, IN_COLS_SGU)]
        + [slab(r, c) for r, c in zip(cast_rows, cast_cols_out)],
        out_shape=[jax.ShapeDtypeStruct((D_MODEL, IN_COLS_GLA), BF16),
                   jax.ShapeDtypeStruct((D_MODEL, IN_COLS_SGU), BF16)]
        + [jax.ShapeDtypeStruct((r, c), BF16) for r, c in zip(cast_rows, cast_cols_out)],
        compiler_params=pltpu.CompilerParams(
            dimension_semantics=("arbitrary",), vmem_limit_bytes=VMEM_LIMIT_BYTES),
        name="weight_cast",
    )(jnp.swapaxes(w_in, 1, 2), *cast_src)

    def spatial(c):
        i = np.arange(GMLP_CHUNK)
        keep = ((i[:, None] // c) == (i[None, :] // c)) & (i[None, :] <= i[:, None])
        e = jnp.asarray((i[:, None] % c) == np.arange(c)[None, :], w_spatial.dtype)
        hi = lax.Precision.HIGHEST
        wbd = jnp.einsum('ia,hab,jb->hij', e, w_spatial[0][:, :c, :c], e, precision=hi)
        wbd = wbd * jnp.asarray(keep, w_spatial.dtype)
        bias = jnp.einsum('ia,ha->ih', e, b_spatial[0][:, :c], precision=hi)
        return wbd.astype(BF16), jnp.repeat(bias, GMLP_DC, axis=1)

    mixer_weights = lambda wsp, bsp, masks: (
        row(g_mix_pre), w_ga, w_sg, w_gate_p, row(b_gla_gate), g_gla_t, row(g_sgu_ln), row(b_sgu_ln),
        wsp, bsp, w_out_b, row(g_mix_post)) + masks

    def mixer_weight_specs(rows):
        return [
            _full((1, D_MODEL)), _full((D_MODEL, IN_COLS_GLA)), _full((D_MODEL, IN_COLS_SGU)),
            _full((LANES, GLA_KEY)),
            _full((1, GLA_KEY)), _full((1, GLA_WIDTH)), _full((1, GMLP_WIDTH)), _full((1, GMLP_WIDTH)),
            _full((GMLP_HEADS, GMLP_CHUNK, GMLP_CHUNK)), _full((GMLP_CHUNK, GMLP_WIDTH)),
            _full((D_MODEL, D_MODEL + LANES)), _full((1, D_MODEL)),
            _full((rows, rows)), _full((rows, rows)), _full((rows, rows))]

    tl = PROMPT_TILE
    wsp_p, bsp_p = spatial(GMLP_CHUNK)
    x1_p, gla_p = pl.pallas_call(
        functools.partial(_mixer_kernel, rows=tl, sample=False),
        grid=(nb, seq // tl),
        in_specs=[pl.BlockSpec((None, tl, D_MODEL), lambda b, t: (b, t, 0))] + mixer_weight_specs(tl),
        out_specs=[pl.BlockSpec((None, tl, D_MODEL), lambda b, t: (b, t, 0)),
                   pl.BlockSpec((None, GLA_HEADS, GLA_DK, GLA_DV), lambda b, t: (b, 0, 0, 0))],
        out_shape=[jax.ShapeDtypeStruct((nb, seq, D_MODEL), F32),
                   jax.ShapeDtypeStruct((nb, GLA_HEADS, GLA_DK, GLA_DV), F32)],
        scratch_shapes=[pltpu.VMEM((tl, GLA_WIDTH), F32),
                        pltpu.VMEM((HEAD_PAIRS, 2 * GLA_DV, 2 * GLA_DK), F32)],
        compiler_params=pltpu.CompilerParams(
            dimension_semantics=("arbitrary", "arbitrary"), vmem_limit_bytes=VMEM_LIMIT_BYTES),
        name="mixer_prompt",
    )(x_prompt, *mixer_weights(wsp_p, bsp_p, _gla_masks(tl, GLA_CHUNK)))

    ts = SAMPLE_TILE
    seqs_per_tile = ts // dseq
    wsp_s, bsp_s = spatial(dseq)
    x1_s, gla_s, vrows_s = pl.pallas_call(
        functools.partial(_mixer_kernel, rows=ts, sample=True),
        grid=(ns * dseq // ts,),
        in_specs=[pl.BlockSpec((ts, D_MODEL), lambda i: (i, 0)),
                  pl.BlockSpec((seqs_per_tile, HEAD_PAIRS, 2 * GLA_DK, GLA_DV), lambda i: (i, 0, 0, 0))]
        + mixer_weight_specs(ts),
        out_specs=[pl.BlockSpec((ts, D_MODEL), lambda i: (i, 0)),
                   pl.BlockSpec((seqs_per_tile, HEAD_PAIRS, 2 * GLA_DK, GLA_DV), lambda i: (i, 0, 0, 0)),
                   pl.BlockSpec((ts, GMLP_WIDTH), lambda i: (i, 0))],
        out_shape=[jax.ShapeDtypeStruct((ns * dseq, D_MODEL), F32),
                   jax.ShapeDtypeStruct((ns, HEAD_PAIRS, 2 * GLA_DK, GLA_DV), F32),
                   jax.ShapeDtypeStruct((ns * dseq, GMLP_WIDTH), F32)],
        scratch_shapes=[pltpu.VMEM((ts, GLA_WIDTH), F32)],
        compiler_params=pltpu.CompilerParams(
            dimension_semantics=("arbitrary",), vmem_limit_bytes=VMEM_LIMIT_BYTES),
        name="mixer_sample",
    )(x_sample.reshape(ns * dseq, D_MODEL),
      state_gla[0].reshape(ns, HEAD_PAIRS, 2 * GLA_DK, GLA_DV),
      *mixer_weights(wsp_s, bsp_s, _gla_masks(ts, dseq)))

    ffn_weights = (row(g_ffn_pre), w_up_b, w_conv[0], row(b_conv), w_down_b, row(g_ffn_post),
                   row(g_ple_in), w_pg_b, w_ple_b, row(g_ple_post))
    ffn_weight_specs = [
        _full((1, D_MODEL)), _full((D_MODEL, 2 * D_FF)), _full((CONV_W, 2 * D_FF)), _full((1, 2 * D_FF)),
        _full((D_FF, D_MODEL + LANES)), _full((1, D_MODEL)), _full((1, D_MODEL)),
        _full((D_MODEL, D_MODEL + LANES)), _full((PLE_DIM, D_MODEL + LANES)), _full((1, D_MODEL))]

    tf = FFN_PROMPT_TILE
    y_p, conv_p = pl.pallas_call(
        functools.partial(_ffn_prompt_kernel, rows=tf),
        grid=(nb, seq // tf),
        in_specs=[pl.BlockSpec((None, tf, D_MODEL), lambda b, t: (b, t, 0)),
                  pl.BlockSpec((None, tf, PLE_DIM), lambda b, t: (b, t, 0))] + ffn_weight_specs,
        out_specs=[pl.BlockSpec((None, tf, D_MODEL), lambda b, t: (b, t, 0)),
                   pl.BlockSpec((None, CONV_W - 1, 2 * D_FF), lambda b, t: (b, 0, 0))],
        out_shape=[jax.ShapeDtypeStruct((nb, seq, D_MODEL), F32),
                   jax.ShapeDtypeStruct((nb, CONV_W - 1, 2 * D_FF), F32)],
        scratch_shapes=[pltpu.VMEM((CONV_W - 1, SUBLANES, 2 * D_FF), F32),
                        pltpu.VMEM((2, 2, CONV_W, tf + 2 * SUBLANES, FF_BLOCK), F32),
                        pltpu.VMEM((tf, D_FF), BF16)],
        compiler_params=pltpu.CompilerParams(
            dimension_semantics=("arbitrary", "arbitrary"), vmem_limit_bytes=VMEM_LIMIT_BYTES),
        name="ffn_prompt",
    )(x1_p, p_prompt[0], *ffn_weights)

    y_s, conv_s = pl.pallas_call(
        functools.partial(_ffn_sample_kernel, nseq=ns, seq_len=dseq),
        grid=(1,),
        in_specs=[_whole((ns, dseq, D_MODEL)), _whole((ns, dseq, PLE_DIM)),
                  _whole((ns, CONV_W - 1, 2 * D_FF))] + ffn_weight_specs,
        out_specs=[_whole((ns, dseq, D_MODEL)), _whole((ns, CONV_W - 1, 2 * D_FF))],
        out_shape=[jax.ShapeDtypeStruct((ns, dseq, D_MODEL), F32),
                   jax.ShapeDtypeStruct((ns, CONV_W - 1, 2 * D_FF), F32)],
        scratch_shapes=[pltpu.VMEM((ns * dseq, D_FF), BF16)],
        compiler_params=pltpu.CompilerParams(
            dimension_semantics=("arbitrary",), vmem_limit_bytes=VMEM_LIMIT_BYTES),
        name="ffn_sample",
    )(x1_s.reshape(ns, dseq, D_MODEL), p_sample[0], state_ffn_conv[0], *ffn_weights)

    return (y_p,
            y_s,
            gla_p[None],
            gla_s.reshape(1, ns, GLA_HEADS, GLA_DK, GLA_DV),
            conv_p[None],
            conv_s[None],
            vrows_s.reshape(1, ns, dseq, GMLP_WIDTH))
```

```python
import functools

import numpy as np
import jax
import jax.numpy as jnp
from jax import lax
from jax.experimental import pallas as pl
from jax.experimental.pallas import tpu as pltpu

D_MODEL = 1024
GLA_HEADS = 4
GLA_DK = 64
GLA_DV = 128
GLA_KEY = GLA_HEADS * GLA_DK
GLA_WIDTH = GLA_HEADS * GLA_DV
GMLP_HEADS = 4
GMLP_DC = 128
GMLP_WIDTH = GMLP_HEADS * GMLP_DC
GLA_GATE_RANK = 16
GLA_TAU = 16.0
GLA_CHUNK = 64
GMLP_CHUNK = 128
D_FF = 2816
CONV_W = 3
PLE_DIM = 256
EPS = 1e-6

LANES = 128
SUBLANES = 8
MXU_N = 256
HEAD_PAIRS = GLA_HEADS // 2

COL_QK = 0
COL_V = COL_QK + 2 * GLA_KEY
COL_R = COL_V + GLA_WIDTH
IN_COLS_GLA = COL_R + GLA_WIDTH
COL_U = 0
COL_VS = COL_U + GMLP_WIDTH
COL_GLR = COL_VS + GMLP_WIDTH
IN_COLS_SGU = COL_GLR + LANES

FF_BLOCK = 256
N_FF_BLOCKS = D_FF // FF_BLOCK
FF_GROUP = 4

PROMPT_TILE = 256
MIXER_SEQS_PER_STEP = 2
FFN_PROMPT_TILE = 256
SAMPLE_TILE = 128
CAST_STEPS = 8

VMEM_LIMIT_BYTES = 56 * 1024 * 1024

BF16 = jnp.bfloat16
F32 = jnp.float32


def _dot(a, b):
    return jnp.dot(a, b, preferred_element_type=F32)


def _dot_wide(a, w_ref, rows=slice(None)):
    return jnp.concatenate(
        [_dot(a, w_ref[rows, c:c + MXU_N]) for c in range(0, D_MODEL, MXU_N)], axis=-1)


def _dot_nt(a, b):
    return lax.dot_general(a, b, (((1,), (1,)), ((), ())), preferred_element_type=F32)


def _dot_tn(a, b):
    return lax.dot_general(a, b, (((0,), (0,)), ((), ())), preferred_element_type=F32)


def _rms(x, g):
    return x * lax.rsqrt(jnp.mean(x * x, axis=-1, keepdims=True) + EPS) * g


def _gelu_gate(x):
    c = -2.0 * np.sqrt(2.0 / np.pi) * np.log2(np.e)
    a1 = np.float32(c)
    a3 = np.float32(c * 0.044715)
    return 1.0 / (1.0 + jnp.exp2(x * (a3 * (x * x) + a1)))


def _gelu(x):
    return x * _gelu_gate(x)


def _sigmoid(x):
    return 1.0 / (1.0 + jnp.exp(-x))


def _log_sigmoid(x):
    return jnp.minimum(x, 0.0) - jnp.log1p(jnp.exp(-jnp.abs(x)))


def _split_bf16(x):
    hi = x.astype(BF16)
    lo = (x - hi.astype(F32)).astype(BF16)
    return hi, lo


def _mixer_kernel(*refs, rows, sample, n_par=1):
    if sample:
        (x_ref, sin_ref, g_pre_ref, w_ga_ref, w_sg_ref, w_gate_ref, b_gate_ref, g_gla_ref, g_ln_ref, b_ln_ref,
         w_sp_ref, b_sp_ref, w_out_ref, g_post_ref, cum_ref, tot_ref, causal_ref,
         y_ref, sout_ref, vn_ref, o_s) = refs
        xs, ys, os_ = [x_ref], [y_ref], [o_s]
    else:
        (x_ref, g_pre_ref, w_ga_ref, w_sg_ref, w_gate_ref, b_gate_ref, g_gla_ref, g_ln_ref, b_ln_ref,
         w_sp_ref, b_sp_ref, w_out_ref, g_post_ref, cum_ref, tot_ref, causal_ref,
         y_ref, sout_ref, o_s, st_s) = refs
        t = pl.program_id(1)

        @pl.when(t == 0)
        def _():
            st_s[...] = jnp.zeros_like(st_s)

        xs = [x_ref.at[i] for i in range(n_par)]
        ys = [y_ref.at[i] for i in range(n_par)]
        os_ = [o_s.at[i] for i in range(n_par)]

    lane = lax.broadcasted_iota(jnp.int32, (rows, LANES), 1)
    head_lane_masks = (lane < GLA_DK, lane >= GLA_DK)
    tiles = [dict() for _ in range(n_par)]

    def stage_norm(i):
        d = tiles[i]
        d['x'] = xs[i][...]
        d['h'] = _rms(d['x'], g_pre_ref[...]).astype(BF16)

    def stage_project(i):
        d = tiles[i]
        h = d['h']
        glr = _dot(h, w_sg_ref[:, COL_GLR:IN_COLS_SGU])
        d['logit'] = _dot(glr.astype(BF16), w_gate_ref[...]) + b_gate_ref[...]
        d['u_pre'] = _dot(h, w_sg_ref[:, COL_U:COL_VS])
        d['vs_pre'] = _dot(h, w_sg_ref[:, COL_VS:COL_GLR])
        d['qk'] = _dot(h, w_ga_ref[:, COL_QK:COL_V])
        d['v'] = _dot(h, w_ga_ref[:, COL_V:COL_R])
        d['r'] = _dot(h, w_ga_ref[:, COL_R:IN_COLS_GLA])

    def stage_gate_sgu(i):
        d = tiles[i]
        d['g'] = _split_bf16(_log_sigmoid(d.pop('logit')) * (1.0 / GLA_TAU))
        d['u'] = _gelu(d.pop('u_pre'))
        vs = _gelu(d.pop('vs_pre'))
        vn_parts = []
        for j in range(GMLP_HEADS):
            vh = vs[:, j * GMLP_DC:(j + 1) * GMLP_DC]
            mu = jnp.mean(vh, axis=-1, keepdims=True)
            c = vh - mu
            var = jnp.mean(c * c, axis=-1, keepdims=True)
            vn_parts.append(c * lax.rsqrt(var + EPS))
        vn = jnp.concatenate(vn_parts, axis=-1) * g_ln_ref[...] + b_ln_ref[...]
        if sample:
            vn_ref[...] = vn
        d['vn_bf'] = vn.astype(BF16)

    def stage_mix_sgu(i):
        d = tiles[i]
        g_hi, g_lo = d.pop('g')
        cum = cum_ref[...]
        tot = tot_ref[...]
        d['b'] = _dot(cum, g_hi) + _dot(cum, g_lo)
        d['bl'] = _dot(tot, g_hi) + _dot(tot, g_lo)
        vn_bf = d.pop('vn_bf')
        mixed_parts = []
        for cch in range(rows // GMLP_CHUNK):
            crows = slice(cch * GMLP_CHUNK, (cch + 1) * GMLP_CHUNK)
            mixed_parts.append(jnp.concatenate(
                [_dot(w_sp_ref[j], vn_bf[crows, j * GMLP_DC:(j + 1) * GMLP_DC]) for j in range(GMLP_HEADS)],
                axis=-1) + b_sp_ref[...])
        so = d.pop('u') * jnp.concatenate(mixed_parts, axis=0)
        d['mix'] = _dot_wide(so.astype(BF16), w_out_ref, slice(GLA_WIDTH, GLA_WIDTH + GMLP_WIDTH))

    def stage_decay(i):
        d = tiles[i]
        qk = d.pop('qk')
        q = qk[:, :GLA_KEY] * (GLA_DK ** -0.5)
        k = qk[:, GLA_KEY:]
        b, bl = d.pop('b'), d['bl']
        d['v_bf'] = d['v'].astype(BF16)
        d['qb'] = q * jnp.exp(b)
        d['kn'] = k * jnp.exp(-b)
        d['kd'] = k * jnp.exp(bl - b)

    def stage_intra(i):
        d = tiles[i]
        qb, kn, kd, v, v_bf, bl = d.pop('qb'), d.pop('kn'), d.pop('kd'), d.pop('v'), d['v_bf'], d['bl']
        causal = causal_ref[...]
        d['pairs'] = []
        for p in range(HEAD_PAIRS):
            ls = slice(p * LANES, (p + 1) * LANES)
            qbp = qb[:, ls]
            knp = kn[:, ls].astype(BF16)
            kdp = kd[:, ls]
            o_intra = []
            for hh in range(2):
                hd = 2 * p + hh
                qm = jnp.where(head_lane_masks[hh], qbp, 0.0).astype(BF16)
                s = _dot_nt(qm, knp) * causal
                o_intra.append(_dot(s.astype(BF16), v_bf[:, hd * GLA_DV:(hd + 1) * GLA_DV]))

            if not sample:
                st_shape = (2 * GLA_DV, 2 * GLA_DK)
                ri = lax.broadcasted_iota(jnp.int32, st_shape, 0)
                ci = lax.broadcasted_iota(jnp.int32, st_shape, 1)
                same_head = (ri < GLA_DV) == (ci < GLA_DK)
                kdp_bf = kdp.astype(BF16)
                upd = []
                for c in range(rows // GLA_CHUNK):
                    rws = slice(c * GLA_CHUNK, (c + 1) * GLA_CHUNK)
                    inc = _dot_tn(v_bf[rws, 2 * p * GLA_DV:(2 * p + 2) * GLA_DV], kdp_bf[rws, :])
                    upd.append(jnp.where(same_head, inc, 0.0))
                d['pairs'].append((ls, qbp.astype(BF16), o_intra, upd))
            else:
                seq_len = 4
                seqs_per_grp = SUBLANES // seq_len
                kdp_t = kdp.T.astype(BF16)
                dec_t = jnp.exp(bl[:, ls]).T
                sub = lax.broadcasted_iota(jnp.int32, (SUBLANES, LANES), 0)
                crow = lax.broadcasted_iota(jnp.int32, (rows, LANES), 0)
                for grp in range(rows // SUBLANES):
                    g0 = grp * SUBLANES
                    qg = qbp[g0:g0 + SUBLANES, :]
                    o_acc = [o_intra[hh][g0:g0 + SUBLANES, :] for hh in range(2)]
                    for j in range(seqs_per_grp):
                        sidx = grp * seqs_per_grp + j
                        s_pair = sin_ref[sidx, p]
                        s_bf = s_pair.astype(BF16)
                        in_seq = (sub >= j * seq_len) & (sub < (j + 1) * seq_len)
                        in_seq_c = (crow >= g0 + j * seq_len) & (crow < g0 + (j + 1) * seq_len)
                        new_rows = []
                        for hh in range(2):
                            hd = 2 * p + hh
                            qsel = jnp.where(in_seq & head_lane_masks[hh][:SUBLANES], qg, 0.0).astype(BF16)
                            o_acc[hh] = o_acc[hh] + _dot(qsel, s_bf)
                            vsel = jnp.where(in_seq_c, v[:, hd * GLA_DV:(hd + 1) * GLA_DV], 0.0).astype(BF16)
                            new_rows.append(_dot(kdp_t[hh * GLA_DK:(hh + 1) * GLA_DK, :], vsel))
                        col = g0 + j * seq_len
                        sout_ref[sidx, p] = s_pair * dec_t[:, col:col + 1] + jnp.concatenate(new_rows, axis=0)
                    for hh in range(2):
                        hd = 2 * p + hh
                        os_[i][g0:g0 + SUBLANES, hd * GLA_DV:(hd + 1) * GLA_DV] = o_acc[hh]

    def stage_state(i):
        if sample:
            return
        d = tiles[i]
        bl = d.pop('bl')
        st = [st_s[i, p] for p in range(HEAD_PAIRS)]
        for c in range(rows // GLA_CHUNK):
            rws = slice(c * GLA_CHUNK, (c + 1) * GLA_CHUNK)
            for p, (ls, qbp_bf, o_intra, upd) in enumerate(d['pairs']):
                o_inter = _dot_nt(qbp_bf[rws, :], st[p].astype(BF16))
                for hh in range(2):
                    hd = 2 * p + hh
                    os_[i][rws, hd * GLA_DV:(hd + 1) * GLA_DV] = (
                        o_intra[hh][rws, :] + o_inter[:, hh * GLA_DV:(hh + 1) * GLA_DV])
                dec = jnp.exp(bl[c * GLA_CHUNK:c * GLA_CHUNK + 1, ls])
                st[p] = st[p] * dec + upd[c]
        for p in range(HEAD_PAIRS):
            st_s[i, p] = st[p]
        d.pop('pairs')

    def stage_out(i):
        d = tiles[i]
        r = d.pop('r')
        out_gate = r * _sigmoid(r)
        mix, x = d.pop('mix'), d.pop('x')
        half = rows // 2
        for hr in (slice(0, half), slice(half, rows)):
            o = os_[i][hr, :]
            og = jnp.concatenate(
                [_rms(o[:, j * GLA_DV:(j + 1) * GLA_DV], 1.0) for j in range(GLA_HEADS)], axis=-1)
            og = og * g_gla_ref[...] * out_gate[hr, :]
            mix_h = mix[hr, :] + _dot_wide(og.astype(BF16), w_out_ref, slice(0, GLA_WIDTH))
            ys[i][hr, :] = x[hr, :] + _rms(mix_h, g_post_ref[...])

    stages = (stage_norm, stage_project, stage_gate_sgu, stage_mix_sgu, stage_decay, stage_intra,
              stage_state, stage_out)
    for k in range(len(stages) + n_par - 1):
        for i in range(n_par):
            if 0 <= k - i < len(stages):
                stages[k - i](i)

    if not sample:
        @pl.when(t == pl.num_programs(1) - 1)
        def _():
            for i in range(n_par):
                for p in range(HEAD_PAIRS):
                    st = st_s[i, p]
                    sout_ref[i, 2 * p] = st[:GLA_DV, :].T[:GLA_DK, :]
                    sout_ref[i, 2 * p + 1] = st[GLA_DV:, :].T[GLA_DK:, :]


def _ffn_tail(x1, f, p, g_ffn_post, g_ple_in, w_pg_ref, w_ple_ref, g_ple_post):
    x2 = x1 + _rms(f, g_ffn_post)
    hp = _rms(x2, g_ple_in).astype(BF16)
    gate = _sigmoid(_dot_wide(hp, w_pg_ref))
    pe = _dot_wide(p.astype(BF16), w_ple_ref)
    return x2 + _rms(pe * gate, g_ple_post)


def _ff_groups():
    return [range(g, min(g + FF_GROUP, N_FF_BLOCKS)) for g in range(0, N_FF_BLOCKS, FF_GROUP)]


def _ffn_prompt_kernel(x_ref, p_ref, g_pre_ref, w_up_ref, w_conv_ref, b_conv_ref, w_down_ref,
                       g_post_ref, g_ple_in_ref, w_pg_ref, w_ple_ref, g_ple_post_ref,
                       y_ref, cs_ref,
                       carry_s, up_s, act_s, *, rows):
    t = pl.program_id(1)
    hdr = SUBLANES

    @pl.when(t == 0)
    def _():
        carry_s[...] = jnp.zeros_like(carry_s)
        up_s[:, :, :, hdr + rows:, :] = jnp.zeros((2, 2, CONV_W, hdr, FF_BLOCK), F32)

    x1 = x_ref[...]
    h = _rms(x1, g_pre_ref[...]).astype(BF16)

    def half_cols(j, half):
        c0 = half * D_FF + j * FF_BLOCK
        return slice(c0, c0 + FF_BLOCK)

    def project(j):
        for half in range(2):
            cols = half_cols(j, half)
            up = _dot(h, w_up_ref[:, cols])
            for s in range(CONV_W):
                buf = up_s.at[j % 2, half, s]
                if s:
                    buf[hdr:2 * hdr, :] = carry_s[s - 1, :, cols]
                buf[hdr + s:hdr + s + rows, :] = up
                if s:
                    carry_s[s - 1, :, cols] = buf[hdr + rows:2 * hdr + rows, :]

    def conv(j, half):
        cols = half_cols(j, half)
        w = w_conv_ref[:, cols]
        taps = [up_s[j % 2, half, s, hdr:hdr + rows, :] for s in range(CONV_W)]
        return (b_conv_ref[:, cols] + taps[2] * w[0:1, :] + taps[1] * w[1:2, :] + taps[0] * w[2:3, :])

    def down(j):
        k_rows = slice(j * FF_BLOCK, (j + 1) * FF_BLOCK)
        return _dot_wide(act_s[:, k_rows], w_down_ref, k_rows)

    project(0)
    f = None
    for j in range(N_FF_BLOCKS):
        if j + 1 < N_FF_BLOCKS:
            project(j + 1)
        if j == 1:
            f = down(0)
        elif j > 1:
            f = f + down(j - 1)
        gate = conv(j, 0)
        act_s[:, j * FF_BLOCK:(j + 1) * FF_BLOCK] = (
            (gate * conv(j, 1)) * _gelu_gate(gate)).astype(BF16)
    f = f + down(N_FF_BLOCKS - 1)

    y_ref[...] = _ffn_tail(x1, f, p_ref[...], g_post_ref[...], g_ple_in_ref[...], w_pg_ref,
                           w_ple_ref, g_ple_post_ref[...])

    @pl.when(t == pl.num_programs(1) - 1)
    def _():
        cs_ref[...] = carry_s[CONV_W - 2, 0:CONV_W - 1, :]


def _ffn_sample_kernel(x_ref, p_ref, cin_ref, g_pre_ref, w_up_ref, w_conv_ref, b_conv_ref, w_down_ref,
                       g_post_ref, g_ple_in_ref, w_pg_ref, w_ple_ref, g_ple_post_ref,
                       y_ref, cs_ref, act_s, *, nseq, seq_len):
    x1 = jnp.concatenate([x_ref[:, j, :] for j in range(seq_len)], axis=0)
    p = jnp.concatenate([p_ref[:, j, :] for j in range(seq_len)], axis=0)
    h = _rms(x1, g_pre_ref[...]).astype(BF16)

    def conv_half(col0):
        cols = slice(col0, col0 + FF_BLOCK)
        up = _dot(h, w_up_ref[:, cols])
        blocks = [cin_ref[:, i, cols] for i in range(CONV_W - 1)]
        blocks += [up[j * nseq:(j + 1) * nseq, :] for j in range(seq_len)]
        w = w_conv_ref[:, cols]
        b = b_conv_ref[:, cols]
        conv = jnp.concatenate(
            [b + blocks[j] * w[0:1, :] + blocks[j + 1] * w[1:2, :] + blocks[j + 2] * w[2:3, :]
             for j in range(seq_len)], axis=0)
        for i in range(CONV_W - 1):
            cs_ref[:, i, cols] = blocks[seq_len + i]
        return conv

    f = None
    for grp in _ff_groups():
        for j in grp:
            gate = conv_half(j * FF_BLOCK)
            val = conv_half(D_FF + j * FF_BLOCK)
            act_s[:, j * FF_BLOCK:(j + 1) * FF_BLOCK] = ((gate * val) * _gelu_gate(gate)).astype(BF16)
        k_rows = slice(grp[0] * FF_BLOCK, (grp[-1] + 1) * FF_BLOCK)
        part = _dot_wide(act_s[:, k_rows], w_down_ref, k_rows)
        f = part if f is None else f + part

    y = _ffn_tail(x1, f, p, g_post_ref[...], g_ple_in_ref[...], w_pg_ref, w_ple_ref,
                  g_ple_post_ref[...])
    for j in range(seq_len):
        y_ref[:, j, :] = y[j * nseq:(j + 1) * nseq, :]


def _cast_kernel(*refs):
    n = (len(refs) - 3) // 2
    w_in_t_ref, srcs = refs[0], refs[1:1 + n]
    ga_ref, sg_ref, dsts = refs[1 + n], refs[2 + n], refs[3 + n:]

    glr0 = IN_COLS_GLA
    u0 = glr0 + GLA_GATE_RANK
    ga_ref[...] = w_in_t_ref[:glr0, :].T.astype(BF16)
    sg_ref[:, :COL_GLR] = w_in_t_ref[u0:u0 + COL_GLR, :].T.astype(BF16)
    tile = w_in_t_ref[glr0:glr0 + LANES, :].T
    lane = lax.broadcasted_iota(jnp.int32, tile.shape, 1)
    sg_ref[:, COL_GLR:] = jnp.where(lane < GLA_GATE_RANK, tile, 0.0).astype(BF16)

    for src, dst in zip(srcs, dsts):
        cols = src.shape[1]
        dst[:, :cols] = src[...].astype(BF16)
        if dst.shape[1] > cols:
            dst[:, cols:] = jnp.zeros((dst.shape[0], dst.shape[1] - cols), BF16)


def _whole(shape):
    n = len(shape)
    return pl.BlockSpec(shape, lambda *_: (0,) * n)


def _full(shape):
    n = len(shape)
    return pl.BlockSpec(shape, lambda *_: (0,) * n, pipeline_mode=pl.Buffered(1))


def _gla_masks(rows, span):
    i = np.arange(rows)
    same = (i[:, None] // span) == (i[None, :] // span)
    causal = same & (i[None, :] <= i[:, None])
    return (jnp.asarray(causal, BF16), jnp.asarray(same, BF16), jnp.asarray(causal, F32))


def kernel(x_prompt, x_sample, state_gla, state_ffn_conv, p_prompt, p_sample, g_mix_pre, w_in, w_gla_gate, b_gla_gate, g_gla_out, g_sgu_ln, b_sgu_ln, w_spatial, b_spatial, w_out, g_mix_post, g_ffn_pre, w_up, w_conv, b_conv, w_down, g_ffn_post, g_ple_in, w_ple_gate, w_ple, g_ple_post):
    depth = x_prompt.ndim - 2
    assert w_in.shape[0] == 1 and depth == 1
    nb, seq, _ = x_prompt.shape
    ns, dseq, _ = x_sample.shape
    assert seq % PROMPT_TILE == 0 and PROMPT_TILE % GMLP_CHUNK == 0 and seq % FFN_PROMPT_TILE == 0
    assert dseq == 4 and (ns * dseq) % SAMPLE_TILE == 0 and SAMPLE_TILE % GMLP_CHUNK == 0

    w_gate_p = jnp.concatenate(
        [w_gla_gate[0], jnp.zeros((LANES - GLA_GATE_RANK, GLA_KEY), w_gla_gate.dtype)], axis=0).astype(BF16)
    row = lambda a: a.reshape(1, -1)
    g_gla_t = jnp.tile(g_gla_out[0], GLA_HEADS).reshape(1, -1)
    n_cast = CAST_STEPS
    in_cols = w_in.shape[-1]
    assert in_cols == IN_COLS_GLA + GLA_GATE_RANK + COL_GLR
    cast_src = (w_out[0], w_up[0], w_down[0], w_ple_gate[0], w_ple[0])
    cast_cols_in = (D_MODEL, 2 * D_FF, D_MODEL, D_MODEL, D_MODEL)
    cast_cols_out = (D_MODEL + LANES, 2 * D_FF, D_MODEL + LANES, D_MODEL + LANES, D_MODEL + LANES)
    cast_rows = tuple(w.shape[0] for w in cast_src)
    assert all(r % (n_cast * 2 * SUBLANES) == 0 for r in cast_rows + (D_MODEL,))
    slab = lambda r, c: pl.BlockSpec((r // n_cast, c), lambda i: (i, 0))
    w_ga, w_sg, w_out_b, w_up_b, w_down_b, w_pg_b, w_ple_b = pl.pallas_call(
        _cast_kernel,
        grid=(n_cast,),
        in_specs=[pl.BlockSpec((None, in_cols, D_MODEL // n_cast), lambda i: (0, 0, i))]
        + [slab(r, c) for r, c in zip(cast_rows, cast_cols_in)],
        out_specs=[slab(D_MODEL, IN_COLS_GLA), slab(D_MODEL, IN_COLS_SGU)]
        + [slab(r, c) for r, c in zip(cast_rows, cast_cols_out)],
        out_shape=[jax.ShapeDtypeStruct((D_MODEL, IN_COLS_GLA), BF16),
                   jax.ShapeDtypeStruct((D_MODEL, IN_COLS_SGU), BF16)]
        + [jax.ShapeDtypeStruct((r, c), BF16) for r, c in zip(cast_rows, cast_cols_out)],
        compiler_params=pltpu.CompilerParams(
            dimension_semantics=("arbitrary",), vmem_limit_bytes=VMEM_LIMIT_BYTES),
        name="weight_cast",
    )(jnp.swapaxes(w_in, 1, 2), *cast_src)

    def spatial(c):
        i = np.arange(GMLP_CHUNK)
        keep = ((i[:, None] // c) == (i[None, :] // c)) & (i[None, :] <= i[:, None])
        e = jnp.asarray((i[:, None] % c) == np.arange(c)[None, :], w_spatial.dtype)
        hi = lax.Precision.HIGHEST
        wbd = jnp.einsum('ia,hab,jb->hij', e, w_spatial[0][:, :c, :c], e, precision=hi)
        wbd = wbd * jnp.asarray(keep, w_spatial.dtype)
        bias = jnp.einsum('ia,ha->ih', e, b_spatial[0][:, :c], precision=hi)
        return wbd.astype(BF16), jnp.repeat(bias, GMLP_DC, axis=1)

    mixer_weights = lambda wsp, bsp, masks: (
        row(g_mix_pre), w_ga, w_sg, w_gate_p, row(b_gla_gate), g_gla_t, row(g_sgu_ln), row(b_sgu_ln),
        wsp, bsp, w_out_b, row(g_mix_post)) + masks

    def mixer_weight_specs(rows):
        return [
            _full((1, D_MODEL)), _full((D_MODEL, IN_COLS_GLA)), _full((D_MODEL, IN_COLS_SGU)),
            _full((LANES, GLA_KEY)),
            _full((1, GLA_KEY)), _full((1, GLA_WIDTH)), _full((1, GMLP_WIDTH)), _full((1, GMLP_WIDTH)),
            _full((GMLP_HEADS, GMLP_CHUNK, GMLP_CHUNK)), _full((GMLP_CHUNK, GMLP_WIDTH)),
            _full((D_MODEL, D_MODEL + LANES)), _full((1, D_MODEL)),
            _full((rows, rows)), _full((rows, rows)), _full((rows, rows))]

    tl = PROMPT_TILE
    wsp_p, bsp_p = spatial(GMLP_CHUNK)
    npar = MIXER_SEQS_PER_STEP
    assert nb % npar == 0
    x1_p, gla_p = pl.pallas_call(
        functools.partial(_mixer_kernel, rows=tl, sample=False, n_par=npar),
        grid=(nb // npar, seq // tl),
        in_specs=[pl.BlockSpec((npar, None, tl, D_MODEL), lambda b, t: (0, b, t, 0))] + mixer_weight_specs(tl),
        out_specs=[pl.BlockSpec((npar, None, tl, D_MODEL), lambda b, t: (0, b, t, 0)),
                   pl.BlockSpec((npar, None, GLA_HEADS, GLA_DK, GLA_DV), lambda b, t: (0, b, 0, 0, 0))],
        out_shape=[jax.ShapeDtypeStruct((npar, nb // npar, seq, D_MODEL), F32),
                   jax.ShapeDtypeStruct((npar, nb // npar, GLA_HEADS, GLA_DK, GLA_DV), F32)],
        scratch_shapes=[pltpu.VMEM((npar, tl, GLA_WIDTH), F32),
                        pltpu.VMEM((npar, HEAD_PAIRS, 2 * GLA_DV, 2 * GLA_DK), F32)],
        compiler_params=pltpu.CompilerParams(
            dimension_semantics=("arbitrary", "arbitrary"), vmem_limit_bytes=VMEM_LIMIT_BYTES),
        name="mixer_prompt",
    )(x_prompt.reshape(npar, nb // npar, seq, D_MODEL),
      *mixer_weights(wsp_p, bsp_p, _gla_masks(tl, GLA_CHUNK)))
    x1_p = x1_p.reshape(nb, seq, D_MODEL)
    gla_p = gla_p.reshape(nb, GLA_HEADS, GLA_DK, GLA_DV)

    ts = SAMPLE_TILE
    seqs_per_tile = ts // dseq
    wsp_s, bsp_s = spatial(dseq)
    x1_s, gla_s, vrows_s = pl.pallas_call(
        functools.partial(_mixer_kernel, rows=ts, sample=True),
        grid=(ns * dseq // ts,),
        in_specs=[pl.BlockSpec((ts, D_MODEL), lambda i: (i, 0)),
                  pl.BlockSpec((seqs_per_tile, HEAD_PAIRS, 2 * GLA_DK, GLA_DV), lambda i: (i, 0, 0, 0))]
        + mixer_weight_specs(ts),
        out_specs=[pl.BlockSpec((ts, D_MODEL), lambda i: (i, 0)),
                   pl.BlockSpec((seqs_per_tile, HEAD_PAIRS, 2 * GLA_DK, GLA_DV), lambda i: (i, 0, 0, 0)),
                   pl.BlockSpec((ts, GMLP_WIDTH), lambda i: (i, 0))],
        out_shape=[jax.ShapeDtypeStruct((ns * dseq, D_MODEL), F32),
                   jax.ShapeDtypeStruct((ns, HEAD_PAIRS, 2 * GLA_DK, GLA_DV), F32),
                   jax.ShapeDtypeStruct((ns * dseq, GMLP_WIDTH), F32)],
        scratch_shapes=[pltpu.VMEM((ts, GLA_WIDTH), F32)],
        compiler_params=pltpu.CompilerParams(
            dimension_semantics=("arbitrary",), vmem_limit_bytes=VMEM_LIMIT_BYTES),
        name="mixer_sample",
    )(x_sample.reshape(ns * dseq, D_MODEL),
      state_gla[0].reshape(ns, HEAD_PAIRS, 2 * GLA_DK, GLA_DV),
      *mixer_weights(wsp_s, bsp_s, _gla_masks(ts, dseq)))

    ffn_weights = (row(g_ffn_pre), w_up_b, w_conv[0], row(b_conv), w_down_b, row(g_ffn_post),
                   row(g_ple_in), w_pg_b, w_ple_b, row(g_ple_post))
    ffn_weight_specs = [
        _full((1, D_MODEL)), _full((D_MODEL, 2 * D_FF)), _full((CONV_W, 2 * D_FF)), _full((1, 2 * D_FF)),
        _full((D_FF, D_MODEL + LANES)), _full((1, D_MODEL)), _full((1, D_MODEL)),
        _full((D_MODEL, D_MODEL + LANES)), _full((PLE_DIM, D_MODEL + LANES)), _full((1, D_MODEL))]

    tf = FFN_PROMPT_TILE
    y_p, conv_p = pl.pallas_call(
        functools.partial(_ffn_prompt_kernel, rows=tf),
        grid=(nb, seq // tf),
        in_specs=[pl.BlockSpec((None, tf, D_MODEL), lambda b, t: (b, t, 0)),
                  pl.BlockSpec((None, tf, PLE_DIM), lambda b, t: (b, t, 0))] + ffn_weight_specs,
        out_specs=[pl.BlockSpec((None, tf, D_MODEL), lambda b, t: (b, t, 0)),
                   pl.BlockSpec((None, CONV_W - 1, 2 * D_FF), lambda b, t: (b, 0, 0))],
        out_shape=[jax.ShapeDtypeStruct((nb, seq, D_MODEL), F32),
                   jax.ShapeDtypeStruct((nb, CONV_W - 1, 2 * D_FF), F32)],
        scratch_shapes=[pltpu.VMEM((CONV_W - 1, SUBLANES, 2 * D_FF), F32),
                        pltpu.VMEM((2, 2, CONV_W, tf + 2 * SUBLANES, FF_BLOCK), F32),
                        pltpu.VMEM((tf, D_FF), BF16)],
        compiler_params=pltpu.CompilerParams(
            dimension_semantics=("arbitrary", "arbitrary"), vmem_limit_bytes=VMEM_LIMIT_BYTES),
        name="ffn_prompt",
    )(x1_p, p_prompt[0], *ffn_weights)

    y_s, conv_s = pl.pallas_call(
        functools.partial(_ffn_sample_kernel, nseq=ns, seq_len=dseq),
        grid=(1,),
        in_specs=[_whole((ns, dseq, D_MODEL)), _whole((ns, dseq, PLE_DIM)),
                  _whole((ns, CONV_W - 1, 2 * D_FF))] + ffn_weight_specs,
        out_specs=[_whole((ns, dseq, D_MODEL)), _whole((ns, CONV_W - 1, 2 * D_FF))],
        out_shape=[jax.ShapeDtypeStruct((ns, dseq, D_MODEL), F32),
                   jax.ShapeDtypeStruct((ns, CONV_W - 1, 2 * D_FF), F32)],
        scratch_shapes=[pltpu.VMEM((ns * dseq, D_FF), BF16)],
        compiler_params=pltpu.CompilerParams(
            dimension_semantics=("arbitrary",), vmem_limit_bytes=VMEM_LIMIT_BYTES),
        name="ffn_sample",
    )(x1_s.reshape(ns, dseq, D_MODEL), p_sample[0], state_ffn_conv[0], *ffn_weights)

    return (y_p,
            y_s,
            gla_p[None],
            gla_s.reshape(1, ns, GLA_HEADS, GLA_DK, GLA_DV),
            conv_p[None],
            conv_s[None],
            vrows_s.reshape(1, ns, dseq, GMLP_WIDTH))
```

```python
import functools

import numpy as np
import jax
import jax.numpy as jnp
from jax import lax
from jax.experimental import pallas as pl
from jax.experimental.pallas import tpu as pltpu

D_MODEL = 1024
GLA_HEADS = 4
GLA_DK = 64
GLA_DV = 128
GLA_KEY = GLA_HEADS * GLA_DK
GLA_WIDTH = GLA_HEADS * GLA_DV
GMLP_HEADS = 4
GMLP_DC = 128
GMLP_WIDTH = GMLP_HEADS * GMLP_DC
GLA_GATE_RANK = 16
GLA_TAU = 16.0
GLA_CHUNK = 64
GMLP_CHUNK = 128
D_FF = 2816
CONV_W = 3
PLE_DIM = 256
EPS = 1e-6

LANES = 128
SUBLANES = 8
MXU_N = 256
HEAD_PAIRS = GLA_HEADS // 2

COL_QK = 0
COL_V = COL_QK + 2 * GLA_KEY
COL_R = COL_V + GLA_WIDTH
IN_COLS_GLA = COL_R + GLA_WIDTH
COL_U = 0
COL_VS = COL_U + GMLP_WIDTH
COL_GLR = COL_VS + GMLP_WIDTH
IN_COLS_SGU = COL_GLR + LANES

FF_BLOCK = 256
N_FF_BLOCKS = D_FF // FF_BLOCK
FF_GROUP = 4

PROMPT_TILE = 256
MIXER_SEQS_PER_STEP = 2
FFN_PROMPT_TILE = 256
FFN_SEQS_PER_STEP = 2
FFN_STAGE_SKEW = 3
SAMPLE_TILE = 128
CAST_STEPS = 8

VMEM_LIMIT_BYTES = 56 * 1024 * 1024

BF16 = jnp.bfloat16
F32 = jnp.float32


def _dot(a, b):
    return jnp.dot(a, b, preferred_element_type=F32)


def _dot_wide(a, w_ref, rows=slice(None)):
    return jnp.concatenate(
        [_dot(a, w_ref[rows, c:c + MXU_N]) for c in range(0, D_MODEL, MXU_N)], axis=-1)


def _dot_nt(a, b):
    return lax.dot_general(a, b, (((1,), (1,)), ((), ())), preferred_element_type=F32)


def _dot_tn(a, b):
    return lax.dot_general(a, b, (((0,), (0,)), ((), ())), preferred_element_type=F32)


def _rms(x, g):
    return x * lax.rsqrt(jnp.mean(x * x, axis=-1, keepdims=True) + EPS) * g


def _gelu_gate(x):
    c = -2.0 * np.sqrt(2.0 / np.pi) * np.log2(np.e)
    a1 = np.float32(c)
    a3 = np.float32(c * 0.044715)
    return 1.0 / (1.0 + jnp.exp2(x * (a3 * (x * x) + a1)))


def _gelu(x):
    return x * _gelu_gate(x)


def _sigmoid(x):
    return 1.0 / (1.0 + jnp.exp(-x))


def _log_sigmoid(x):
    return jnp.minimum(x, 0.0) - jnp.log1p(jnp.exp(-jnp.abs(x)))


def _split_bf16(x):
    hi = x.astype(BF16)
    lo = (x - hi.astype(F32)).astype(BF16)
    return hi, lo


def _mixer_kernel(*refs, rows, sample, n_par=1):
    if sample:
        (x_ref, sin_ref, g_pre_ref, w_ga_ref, w_sg_ref, w_gate_ref, b_gate_ref, g_gla_ref, g_ln_ref, b_ln_ref,
         w_sp_ref, b_sp_ref, w_out_ref, g_post_ref, cum_ref, tot_ref, causal_ref,
         y_ref, sout_ref, vn_ref, o_s) = refs
        xs, ys, os_ = [x_ref], [y_ref], [o_s]
    else:
        (x_ref, g_pre_ref, w_ga_ref, w_sg_ref, w_gate_ref, b_gate_ref, g_gla_ref, g_ln_ref, b_ln_ref,
         w_sp_ref, b_sp_ref, w_out_ref, g_post_ref, cum_ref, tot_ref, causal_ref,
         y_ref, sout_ref, o_s, st_s) = refs
        t = pl.program_id(1)

        @pl.when(t == 0)
        def _():
            st_s[...] = jnp.zeros_like(st_s)

        xs = [x_ref.at[i] for i in range(n_par)]
        ys = [y_ref.at[i] for i in range(n_par)]
        os_ = [o_s.at[i] for i in range(n_par)]

    lane = lax.broadcasted_iota(jnp.int32, (rows, LANES), 1)
    head_lane_masks = (lane < GLA_DK, lane >= GLA_DK)
    tiles = [dict() for _ in range(n_par)]

    def stage_norm(i):
        d = tiles[i]
        d['x'] = xs[i][...]
        d['h'] = _rms(d['x'], g_pre_ref[...]).astype(BF16)

    def stage_project(i):
        d = tiles[i]
        h = d['h']
        glr = _dot(h, w_sg_ref[:, COL_GLR:IN_COLS_SGU])
        d['logit'] = _dot(glr.astype(BF16), w_gate_ref[...]) + b_gate_ref[...]
        d['u_pre'] = _dot(h, w_sg_ref[:, COL_U:COL_VS])
        d['vs_pre'] = _dot(h, w_sg_ref[:, COL_VS:COL_GLR])
        d['qk'] = _dot(h, w_ga_ref[:, COL_QK:COL_V])
        d['v'] = _dot(h, w_ga_ref[:, COL_V:COL_R])
        d['r'] = _dot(h, w_ga_ref[:, COL_R:IN_COLS_GLA])

    def stage_gate_sgu(i):
        d = tiles[i]
        d['g'] = _split_bf16(_log_sigmoid(d.pop('logit')) * (1.0 / GLA_TAU))
        d['u'] = _gelu(d.pop('u_pre'))
        vs = _gelu(d.pop('vs_pre'))
        vn_parts = []
        for j in range(GMLP_HEADS):
            vh = vs[:, j * GMLP_DC:(j + 1) * GMLP_DC]
            mu = jnp.mean(vh, axis=-1, keepdims=True)
            c = vh - mu
            var = jnp.mean(c * c, axis=-1, keepdims=True)
            vn_parts.append(c * lax.rsqrt(var + EPS))
        vn = jnp.concatenate(vn_parts, axis=-1) * g_ln_ref[...] + b_ln_ref[...]
        if sample:
            vn_ref[...] = vn
        d['vn_bf'] = vn.astype(BF16)

    def stage_mix_sgu(i):
        d = tiles[i]
        g_hi, g_lo = d.pop('g')
        cum = cum_ref[...]
        tot = tot_ref[...]
        d['b'] = _dot(cum, g_hi) + _dot(cum, g_lo)
        d['bl'] = _dot(tot, g_hi) + _dot(tot, g_lo)
        vn_bf = d.pop('vn_bf')
        mixed_parts = []
        for cch in range(rows // GMLP_CHUNK):
            crows = slice(cch * GMLP_CHUNK, (cch + 1) * GMLP_CHUNK)
            mixed_parts.append(jnp.concatenate(
                [_dot(w_sp_ref[j], vn_bf[crows, j * GMLP_DC:(j + 1) * GMLP_DC]) for j in range(GMLP_HEADS)],
                axis=-1) + b_sp_ref[...])
        so = d.pop('u') * jnp.concatenate(mixed_parts, axis=0)
        d['mix'] = _dot_wide(so.astype(BF16), w_out_ref, slice(GLA_WIDTH, GLA_WIDTH + GMLP_WIDTH))

    def stage_decay(i):
        d = tiles[i]
        qk = d.pop('qk')
        q = qk[:, :GLA_KEY] * (GLA_DK ** -0.5)
        k = qk[:, GLA_KEY:]
        b, bl = d.pop('b'), d['bl']
        d['v_bf'] = d['v'].astype(BF16)
        d['qb'] = q * jnp.exp(b)
        d['kn'] = k * jnp.exp(-b)
        d['kd'] = k * jnp.exp(bl - b)

    def stage_intra(i):
        d = tiles[i]
        qb, kn, kd, v, v_bf, bl = d.pop('qb'), d.pop('kn'), d.pop('kd'), d.pop('v'), d['v_bf'], d['bl']
        causal = causal_ref[...]
        d['pairs'] = []
        for p in range(HEAD_PAIRS):
            ls = slice(p * LANES, (p + 1) * LANES)
            qbp = qb[:, ls]
            knp = kn[:, ls].astype(BF16)
            kdp = kd[:, ls]
            o_intra = []
            for hh in range(2):
                hd = 2 * p + hh
                qm = jnp.where(head_lane_masks[hh], qbp, 0.0).astype(BF16)
                s = _dot_nt(qm, knp) * causal
                o_intra.append(_dot(s.astype(BF16), v_bf[:, hd * GLA_DV:(hd + 1) * GLA_DV]))

            if not sample:
                st_shape = (2 * GLA_DV, 2 * GLA_DK)
                ri = lax.broadcasted_iota(jnp.int32, st_shape, 0)
                ci = lax.broadcasted_iota(jnp.int32, st_shape, 1)
                same_head = (ri < GLA_DV) == (ci < GLA_DK)
                kdp_bf = kdp.astype(BF16)
                upd = []
                for c in range(rows // GLA_CHUNK):
                    rws = slice(c * GLA_CHUNK, (c + 1) * GLA_CHUNK)
                    inc = _dot_tn(v_bf[rws, 2 * p * GLA_DV:(2 * p + 2) * GLA_DV], kdp_bf[rws, :])
                    upd.append(jnp.where(same_head, inc, 0.0))
                d['pairs'].append((ls, qbp.astype(BF16), o_intra, upd))
            else:
                seq_len = 4
                seqs_per_grp = SUBLANES // seq_len
                kdp_t = kdp.T.astype(BF16)
                dec_t = jnp.exp(bl[:, ls]).T
                sub = lax.broadcasted_iota(jnp.int32, (SUBLANES, LANES), 0)
                crow = lax.broadcasted_iota(jnp.int32, (rows, LANES), 0)
                for grp in range(rows // SUBLANES):
                    g0 = grp * SUBLANES
                    qg = qbp[g0:g0 + SUBLANES, :]
                    o_acc = [o_intra[hh][g0:g0 + SUBLANES, :] for hh in range(2)]
                    for j in range(seqs_per_grp):
                        sidx = grp * seqs_per_grp + j
                        s_pair = sin_ref[sidx, p]
                        s_bf = s_pair.astype(BF16)
                        in_seq = (sub >= j * seq_len) & (sub < (j + 1) * seq_len)
                        in_seq_c = (crow >= g0 + j * seq_len) & (crow < g0 + (j + 1) * seq_len)
                        new_rows = []
                        for hh in range(2):
                            hd = 2 * p + hh
                            qsel = jnp.where(in_seq & head_lane_masks[hh][:SUBLANES], qg, 0.0).astype(BF16)
                            o_acc[hh] = o_acc[hh] + _dot(qsel, s_bf)
                            vsel = jnp.where(in_seq_c, v[:, hd * GLA_DV:(hd + 1) * GLA_DV], 0.0).astype(BF16)
                            new_rows.append(_dot(kdp_t[hh * GLA_DK:(hh + 1) * GLA_DK, :], vsel))
                        col = g0 + j * seq_len
                        sout_ref[sidx, p] = s_pair * dec_t[:, col:col + 1] + jnp.concatenate(new_rows, axis=0)
                    for hh in range(2):
                        hd = 2 * p + hh
                        os_[i][g0:g0 + SUBLANES, hd * GLA_DV:(hd + 1) * GLA_DV] = o_acc[hh]

    def stage_state(i):
        if sample:
            return
        d = tiles[i]
        bl = d.pop('bl')
        st = [st_s[i, p] for p in range(HEAD_PAIRS)]
        for c in range(rows // GLA_CHUNK):
            rws = slice(c * GLA_CHUNK, (c + 1) * GLA_CHUNK)
            for p, (ls, qbp_bf, o_intra, upd) in enumerate(d['pairs']):
                o_inter = _dot_nt(qbp_bf[rws, :], st[p].astype(BF16))
                for hh in range(2):
                    hd = 2 * p + hh
                    os_[i][rws, hd * GLA_DV:(hd + 1) * GLA_DV] = (
                        o_intra[hh][rws, :] + o_inter[:, hh * GLA_DV:(hh + 1) * GLA_DV])
                dec = jnp.exp(bl[c * GLA_CHUNK:c * GLA_CHUNK + 1, ls])
                st[p] = st[p] * dec + upd[c]
        for p in range(HEAD_PAIRS):
            st_s[i, p] = st[p]
        d.pop('pairs')

    def stage_out(i):
        d = tiles[i]
        r = d.pop('r')
        out_gate = r * _sigmoid(r)
        mix, x = d.pop('mix'), d.pop('x')
        half = rows // 2
        for hr in (slice(0, half), slice(half, rows)):
            o = os_[i][hr, :]
            og = jnp.concatenate(
                [_rms(o[:, j * GLA_DV:(j + 1) * GLA_DV], 1.0) for j in range(GLA_HEADS)], axis=-1)
            og = og * g_gla_ref[...] * out_gate[hr, :]
            mix_h = mix[hr, :] + _dot_wide(og.astype(BF16), w_out_ref, slice(0, GLA_WIDTH))
            ys[i][hr, :] = x[hr, :] + _rms(mix_h, g_post_ref[...])

    stages = (stage_norm, stage_project, stage_gate_sgu, stage_mix_sgu, stage_decay, stage_intra,
              stage_state, stage_out)
    for k in range(len(stages) + n_par - 1):
        for i in range(n_par):
            if 0 <= k - i < len(stages):
                stages[k - i](i)

    if not sample:
        @pl.when(t == pl.num_programs(1) - 1)
        def _():
            for i in range(n_par):
                for p in range(HEAD_PAIRS):
                    st = st_s[i, p]
                    sout_ref[i, 2 * p] = st[:GLA_DV, :].T[:GLA_DK, :]
                    sout_ref[i, 2 * p + 1] = st[GLA_DV:, :].T[GLA_DK:, :]


def _ffn_tail(x1, f, p, g_ffn_post, g_ple_in, w_pg_ref, w_ple_ref, g_ple_post):
    x2 = x1 + _rms(f, g_ffn_post)
    hp = _rms(x2, g_ple_in).astype(BF16)
    gate = _sigmoid(_dot_wide(hp, w_pg_ref))
    pe = _dot_wide(p.astype(BF16), w_ple_ref)
    return x2 + _rms(pe * gate, g_ple_post)


def _ff_groups():
    return [range(g, min(g + FF_GROUP, N_FF_BLOCKS)) for g in range(0, N_FF_BLOCKS, FF_GROUP)]


def _ffn_prompt_kernel(x_ref, p_ref, g_pre_ref, w_up_ref, w_conv_ref, b_conv_ref, w_down_ref,
                       g_post_ref, g_ple_in_ref, w_pg_ref, w_ple_ref, g_ple_post_ref,
                       y_ref, cs_ref,
                       carry_s, up_s, act_s, *, rows, n_par, skew):
    t = pl.program_id(1)
    hdr = SUBLANES

    @pl.when(t == 0)
    def _():
        carry_s[...] = jnp.zeros_like(carry_s)
        up_s[:, :, :, :, hdr + rows:, :] = jnp.zeros((n_par, 2, 2, CONV_W, hdr, FF_BLOCK), F32)

    tiles = [dict() for _ in range(n_par)]

    def half_cols(j, half):
        c0 = half * D_FF + j * FF_BLOCK
        return slice(c0, c0 + FF_BLOCK)

    def project(i, j):
        for half in range(2):
            cols = half_cols(j, half)
            up = _dot(tiles[i]['h'], w_up_ref[:, cols])
            for s in range(CONV_W):
                buf = up_s.at[i, j % 2, half, s]
                if s:
                    buf[hdr:2 * hdr, :] = carry_s[i, s - 1, :, cols]
                buf[hdr + s:hdr + s + rows, :] = up
                if s:
                    carry_s[i, s - 1, :, cols] = buf[hdr + rows:2 * hdr + rows, :]

    def conv(i, j, half):
        cols = half_cols(j, half)
        w = w_conv_ref[:, cols]
        taps = [up_s[i, j % 2, half, s, hdr:hdr + rows, :] for s in range(CONV_W)]
        return (b_conv_ref[:, cols] + taps[2] * w[0:1, :] + taps[1] * w[1:2, :] + taps[0] * w[2:3, :])

    def down(i, j):
        k_rows = slice(j * FF_BLOCK, (j + 1) * FF_BLOCK)
        return _dot_wide(act_s[i, :, k_rows], w_down_ref, k_rows)

    def stage_head(i):
        d = tiles[i]
        d['x1'] = x_ref[i]
        d['h'] = _rms(d['x1'], g_pre_ref[...]).astype(BF16)
        project(i, 0)

    def stage_block(i, j):
        d = tiles[i]
        if j + 1 < N_FF_BLOCKS:
            project(i, j + 1)
        if j == 1:
            d['f'] = down(i, 0)
        elif j > 1:
            d['f'] = d['f'] + down(i, j - 1)
        gate = conv(i, j, 0)
        act_s[i, :, j * FF_BLOCK:(j + 1) * FF_BLOCK] = (
            (gate * conv(i, j, 1)) * _gelu_gate(gate)).astype(BF16)

    def stage_tail(i):
        d = tiles[i]
        f = d.pop('f') + down(i, N_FF_BLOCKS - 1)
        d.pop('h')
        y_ref[i] = _ffn_tail(d.pop('x1'), f, p_ref[i], g_post_ref[...], g_ple_in_ref[...], w_pg_ref,
                             w_ple_ref, g_ple_post_ref[...])

    stages = ([stage_head] + [functools.partial(stage_block, j=j) for j in range(N_FF_BLOCKS)]
              + [stage_tail])
    for k in range(len(stages) + skew * (n_par - 1)):
        for i in range(n_par):
            if 0 <= k - skew * i < len(stages):
                stages[k - skew * i](i)

    @pl.when(t == pl.num_programs(1) - 1)
    def _():
        for i in range(n_par):
            cs_ref[i] = carry_s[i, CONV_W - 2, 0:CONV_W - 1, :]


def _ffn_sample_kernel(x_ref, p_ref, cin_ref, g_pre_ref, w_up_ref, w_conv_ref, b_conv_ref, w_down_ref,
                       g_post_ref, g_ple_in_ref, w_pg_ref, w_ple_ref, g_ple_post_ref,
                       y_ref, cs_ref, act_s, *, nseq, seq_len):
    x1 = jnp.concatenate([x_ref[:, j, :] for j in range(seq_len)], axis=0)
    p = jnp.concatenate([p_ref[:, j, :] for j in range(seq_len)], axis=0)
    h = _rms(x1, g_pre_ref[...]).astype(BF16)

    def conv_half(col0):
        cols = slice(col0, col0 + FF_BLOCK)
        up = _dot(h, w_up_ref[:, cols])
        blocks = [cin_ref[:, i, cols] for i in range(CONV_W - 1)]
        blocks += [up[j * nseq:(j + 1) * nseq, :] for j in range(seq_len)]
        w = w_conv_ref[:, cols]
        b = b_conv_ref[:, cols]
        conv = jnp.concatenate(
            [b + blocks[j] * w[0:1, :] + blocks[j + 1] * w[1:2, :] + blocks[j + 2] * w[2:3, :]
             for j in range(seq_len)], axis=0)
        for i in range(CONV_W - 1):
            cs_ref[:, i, cols] = blocks[seq_len + i]
        return conv

    f = None
    for grp in _ff_groups():
        for j in grp:
            gate = conv_half(j * FF_BLOCK)
            val = conv_half(D_FF + j * FF_BLOCK)
            act_s[:, j * FF_BLOCK:(j + 1) * FF_BLOCK] = ((gate * val) * _gelu_gate(gate)).astype(BF16)
        k_rows = slice(grp[0] * FF_BLOCK, (grp[-1] + 1) * FF_BLOCK)
        part = _dot_wide(act_s[:, k_rows], w_down_ref, k_rows)
        f = part if f is None else f + part

    y = _ffn_tail(x1, f, p, g_post_ref[...], g_ple_in_ref[...], w_pg_ref, w_ple_ref,
                  g_ple_post_ref[...])
    for j in range(seq_len):
        y_ref[:, j, :] = y[j * nseq:(j + 1) * nseq, :]


def _cast_kernel(*refs):
    n = (len(refs) - 3) // 2
    w_in_t_ref, srcs = refs[0], refs[1:1 + n]
    ga_ref, sg_ref, dsts = refs[1 + n], refs[2 + n], refs[3 + n:]

    glr0 = IN_COLS_GLA
    u0 = glr0 + GLA_GATE_RANK
    ga_ref[...] = w_in_t_ref[:glr0, :].T.astype(BF16)
    sg_ref[:, :COL_GLR] = w_in_t_ref[u0:u0 + COL_GLR, :].T.astype(BF16)
    tile = w_in_t_ref[glr0:glr0 + LANES, :].T
    lane = lax.broadcasted_iota(jnp.int32, tile.shape, 1)
    sg_ref[:, COL_GLR:] = jnp.where(lane < GLA_GATE_RANK, tile, 0.0).astype(BF16)

    for src, dst in zip(srcs, dsts):
        cols = src.shape[1]
        dst[:, :cols] = src[...].astype(BF16)
        if dst.shape[1] > cols:
            dst[:, cols:] = jnp.zeros((dst.shape[0], dst.shape[1] - cols), BF16)


def _whole(shape):
    n = len(shape)
    return pl.BlockSpec(shape, lambda *_: (0,) * n)


def _full(shape):
    n = len(shape)
    return pl.BlockSpec(shape, lambda *_: (0,) * n, pipeline_mode=pl.Buffered(1))


def _gla_masks(rows, span):
    i = np.arange(rows)
    same = (i[:, None] // span) == (i[None, :] // span)
    causal = same & (i[None, :] <= i[:, None])
    return (jnp.asarray(causal, BF16), jnp.asarray(same, BF16), jnp.asarray(causal, F32))


def kernel(x_prompt, x_sample, state_gla, state_ffn_conv, p_prompt, p_sample, g_mix_pre, w_in, w_gla_gate, b_gla_gate, g_gla_out, g_sgu_ln, b_sgu_ln, w_spatial, b_spatial, w_out, g_mix_post, g_ffn_pre, w_up, w_conv, b_conv, w_down, g_ffn_post, g_ple_in, w_ple_gate, w_ple, g_ple_post):
    depth = x_prompt.ndim - 2
    assert w_in.shape[0] == 1 and depth == 1
    nb, seq, _ = x_prompt.shape
    ns, dseq, _ = x_sample.shape
    assert seq % PROMPT_TILE == 0 and PROMPT_TILE % GMLP_CHUNK == 0 and seq % FFN_PROMPT_TILE == 0
    assert dseq == 4 and (ns * dseq) % SAMPLE_TILE == 0 and SAMPLE_TILE % GMLP_CHUNK == 0

    w_gate_p = jnp.concatenate(
        [w_gla_gate[0], jnp.zeros((LANES - GLA_GATE_RANK, GLA_KEY), w_gla_gate.dtype)], axis=0).astype(BF16)
    row = lambda a: a.reshape(1, -1)
    g_gla_t = jnp.tile(g_gla_out[0], GLA_HEADS).reshape(1, -1)
    n_cast = CAST_STEPS
    in_cols = w_in.shape[-1]
    assert in_cols == IN_COLS_GLA + GLA_GATE_RANK + COL_GLR
    cast_src = (w_out[0], w_up[0], w_down[0], w_ple_gate[0], w_ple[0])
    cast_cols_in = (D_MODEL, 2 * D_FF, D_MODEL, D_MODEL, D_MODEL)
    cast_cols_out = (D_MODEL + LANES, 2 * D_FF, D_MODEL + LANES, D_MODEL + LANES, D_MODEL + LANES)
    cast_rows = tuple(w.shape[0] for w in cast_src)
    assert all(r % (n_cast * 2 * SUBLANES) == 0 for r in cast_rows + (D_MODEL,))
    slab = lambda r, c: pl.BlockSpec((r // n_cast, c), lambda i: (i, 0))
    w_ga, w_sg, w_out_b, w_up_b, w_down_b, w_pg_b, w_ple_b = pl.pallas_call(
        _cast_kernel,
        grid=(n_cast,),
        in_specs=[pl.BlockSpec((None, in_cols, D_MODEL // n_cast), lambda i: (0, 0, i))]
        + [slab(r, c) for r, c in zip(cast_rows, cast_cols_in)],
        out_specs=[slab(D_MODEL, IN_COLS_GLA), slab(D_MODEL, IN_COLS_SGU)]
        + [slab(r, c) for r, c in zip(cast_rows, cast_cols_out)],
        out_shape=[jax.ShapeDtypeStruct((D_MODEL, IN_COLS_GLA), BF16),
                   jax.ShapeDtypeStruct((D_MODEL, IN_COLS_SGU), BF16)]
        + [jax.ShapeDtypeStruct((r, c), BF16) for r, c in zip(cast_rows, cast_cols_out)],
        compiler_params=pltpu.CompilerParams(
            dimension_semantics=("arbitrary",), vmem_limit_bytes=VMEM_LIMIT_BYTES),
        name="weight_cast",
    )(jnp.swapaxes(w_in, 1, 2), *cast_src)

    def spatial(c):
        i = np.arange(GMLP_CHUNK)
        keep = ((i[:, None] // c) == (i[None, :] // c)) & (i[None, :] <= i[:, None])
        e = jnp.asarray((i[:, None] % c) == np.arange(c)[None, :], w_spatial.dtype)
        hi = lax.Precision.HIGHEST
        wbd = jnp.einsum('ia,hab,jb->hij', e, w_spatial[0][:, :c, :c], e, precision=hi)
        wbd = wbd * jnp.asarray(keep, w_spatial.dtype)
        bias = jnp.einsum('ia,ha->ih', e, b_spatial[0][:, :c], precision=hi)
        return wbd.astype(BF16), jnp.repeat(bias, GMLP_DC, axis=1)

    mixer_weights = lambda wsp, bsp, masks: (
        row(g_mix_pre), w_ga, w_sg, w_gate_p, row(b_gla_gate), g_gla_t, row(g_sgu_ln), row(b_sgu_ln),
        wsp, bsp, w_out_b, row(g_mix_post)) + masks

    def mixer_weight_specs(rows):
        return [
            _full((1, D_MODEL)), _full((D_MODEL, IN_COLS_GLA)), _full((D_MODEL, IN_COLS_SGU)),
            _full((LANES, GLA_KEY)),
            _full((1, GLA_KEY)), _full((1, GLA_WIDTH)), _full((1, GMLP_WIDTH)), _full((1, GMLP_WIDTH)),
            _full((GMLP_HEADS, GMLP_CHUNK, GMLP_CHUNK)), _full((GMLP_CHUNK, GMLP_WIDTH)),
            _full((D_MODEL, D_MODEL + LANES)), _full((1, D_MODEL)),
            _full((rows, rows)), _full((rows, rows)), _full((rows, rows))]

    tl = PROMPT_TILE
    wsp_p, bsp_p = spatial(GMLP_CHUNK)
    npar = MIXER_SEQS_PER_STEP
    assert nb % npar == 0
    x1_p, gla_p = pl.pallas_call(
        functools.partial(_mixer_kernel, rows=tl, sample=False, n_par=npar),
        grid=(nb // npar, seq // tl),
        in_specs=[pl.BlockSpec((npar, None, tl, D_MODEL), lambda b, t: (0, b, t, 0))] + mixer_weight_specs(tl),
        out_specs=[pl.BlockSpec((npar, None, tl, D_MODEL), lambda b, t: (0, b, t, 0)),
                   pl.BlockSpec((npar, None, GLA_HEADS, GLA_DK, GLA_DV), lambda b, t: (0, b, 0, 0, 0))],
        out_shape=[jax.ShapeDtypeStruct((npar, nb // npar, seq, D_MODEL), F32),
                   jax.ShapeDtypeStruct((npar, nb // npar, GLA_HEADS, GLA_DK, GLA_DV), F32)],
        scratch_shapes=[pltpu.VMEM((npar, tl, GLA_WIDTH), F32),
                        pltpu.VMEM((npar, HEAD_PAIRS, 2 * GLA_DV, 2 * GLA_DK), F32)],
        compiler_params=pltpu.CompilerParams(
            dimension_semantics=("arbitrary", "arbitrary"), vmem_limit_bytes=VMEM_LIMIT_BYTES),
        name="mixer_prompt",
    )(x_prompt.reshape(npar, nb // npar, seq, D_MODEL),
      *mixer_weights(wsp_p, bsp_p, _gla_masks(tl, GLA_CHUNK)))
    x1_p = x1_p.reshape(nb, seq, D_MODEL)
    gla_p = gla_p.reshape(nb, GLA_HEADS, GLA_DK, GLA_DV)

    ts = SAMPLE_TILE
    seqs_per_tile = ts // dseq
    wsp_s, bsp_s = spatial(dseq)
    x1_s, gla_s, vrows_s = pl.pallas_call(
        functools.partial(_mixer_kernel, rows=ts, sample=True),
        grid=(ns * dseq // ts,),
        in_specs=[pl.BlockSpec((ts, D_MODEL), lambda i: (i, 0)),
                  pl.BlockSpec((seqs_per_tile, HEAD_PAIRS, 2 * GLA_DK, GLA_DV), lambda i: (i, 0, 0, 0))]
        + mixer_weight_specs(ts),
        out_specs=[pl.BlockSpec((ts, D_MODEL), lambda i: (i, 0)),
                   pl.BlockSpec((seqs_per_tile, HEAD_PAIRS, 2 * GLA_DK, GLA_DV), lambda i: (i, 0, 0, 0)),
                   pl.BlockSpec((ts, GMLP_WIDTH), lambda i: (i, 0))],
        out_shape=[jax.ShapeDtypeStruct((ns * dseq, D_MODEL), F32),
                   jax.ShapeDtypeStruct((ns, HEAD_PAIRS, 2 * GLA_DK, GLA_DV), F32),
                   jax.ShapeDtypeStruct((ns * dseq, GMLP_WIDTH), F32)],
        scratch_shapes=[pltpu.VMEM((ts, GLA_WIDTH), F32)],
        compiler_params=pltpu.CompilerParams(
            dimension_semantics=("arbitrary",), vmem_limit_bytes=VMEM_LIMIT_BYTES),
        name="mixer_sample",
    )(x_sample.reshape(ns * dseq, D_MODEL),
      state_gla[0].reshape(ns, HEAD_PAIRS, 2 * GLA_DK, GLA_DV),
      *mixer_weights(wsp_s, bsp_s, _gla_masks(ts, dseq)))

    ffn_weights = (row(g_ffn_pre), w_up_b, w_conv[0], row(b_conv), w_down_b, row(g_ffn_post),
                   row(g_ple_in), w_pg_b, w_ple_b, row(g_ple_post))
    ffn_weight_specs = [
        _full((1, D_MODEL)), _full((D_MODEL, 2 * D_FF)), _full((CONV_W, 2 * D_FF)), _full((1, 2 * D_FF)),
        _full((D_FF, D_MODEL + LANES)), _full((1, D_MODEL)), _full((1, D_MODEL)),
        _full((D_MODEL, D_MODEL + LANES)), _full((PLE_DIM, D_MODEL + LANES)), _full((1, D_MODEL))]

    tf = FFN_PROMPT_TILE
    fpar = FFN_SEQS_PER_STEP
    assert nb % fpar == 0
    y_p, conv_p = pl.pallas_call(
        functools.partial(_ffn_prompt_kernel, rows=tf, n_par=fpar, skew=FFN_STAGE_SKEW),
        grid=(nb // fpar, seq // tf),
        in_specs=[pl.BlockSpec((fpar, None, tf, D_MODEL), lambda b, t: (0, b, t, 0)),
                  pl.BlockSpec((fpar, None, tf, PLE_DIM), lambda b, t: (0, b, t, 0))] + ffn_weight_specs,
        out_specs=[pl.BlockSpec((fpar, None, tf, D_MODEL), lambda b, t: (0, b, t, 0)),
                   pl.BlockSpec((fpar, None, CONV_W - 1, 2 * D_FF), lambda b, t: (0, b, 0, 0))],
        out_shape=[jax.ShapeDtypeStruct((fpar, nb // fpar, seq, D_MODEL), F32),
                   jax.ShapeDtypeStruct((fpar, nb // fpar, CONV_W - 1, 2 * D_FF), F32)],
        scratch_shapes=[pltpu.VMEM((fpar, CONV_W - 1, SUBLANES, 2 * D_FF), F32),
                        pltpu.VMEM((fpar, 2, 2, CONV_W, tf + 2 * SUBLANES, FF_BLOCK), F32),
                        pltpu.VMEM((fpar, tf, D_FF), BF16)],
        compiler_params=pltpu.CompilerParams(
            dimension_semantics=("arbitrary", "arbitrary"), vmem_limit_bytes=VMEM_LIMIT_BYTES),
        name="ffn_prompt",
    )(x1_p.reshape(fpar, nb // fpar, seq, D_MODEL),
      p_prompt[0].reshape(fpar, nb // fpar, seq, PLE_DIM), *ffn_weights)
    y_p = y_p.reshape(nb, seq, D_MODEL)
    conv_p = conv_p.reshape(nb, CONV_W - 1, 2 * D_FF)

    y_s, conv_s = pl.pallas_call(
        functools.partial(_ffn_sample_kernel, nseq=ns, seq_len=dseq),
        grid=(1,),
        in_specs=[_whole((ns, dseq, D_MODEL)), _whole((ns, dseq, PLE_DIM)),
                  _whole((ns, CONV_W - 1, 2 * D_FF))] + ffn_weight_specs,
        out_specs=[_whole((ns, dseq, D_MODEL)), _whole((ns, CONV_W - 1, 2 * D_FF))],
        out_shape=[jax.ShapeDtypeStruct((ns, dseq, D_MODEL), F32),
                   jax.ShapeDtypeStruct((ns, CONV_W - 1, 2 * D_FF), F32)],
        scratch_shapes=[pltpu.VMEM((ns * dseq, D_FF), BF16)],
        compiler_params=pltpu.CompilerParams(
            dimension_semantics=("arbitrary",), vmem_limit_bytes=VMEM_LIMIT_BYTES),
        name="ffn_sample",
    )(x1_s.reshape(ns, dseq, D_MODEL), p_sample[0], state_ffn_conv[0], *ffn_weights)

    return (y_p,
            y_s,
            gla_p[None],
            gla_s.reshape(1, ns, GLA_HEADS, GLA_DK, GLA_DV),
            conv_p[None],
            conv_s[None],
            vrows_s.reshape(1, ns, dseq, GMLP_WIDTH))
```

```python
import functools

import numpy as np
import jax
import jax.numpy as jnp
from jax import lax
from jax.experimental import pallas as pl
from jax.experimental.pallas import tpu as pltpu

D_MODEL = 1024
GLA_HEADS = 4
GLA_DK = 64
GLA_DV = 128
GLA_KEY = GLA_HEADS * GLA_DK
GLA_WIDTH = GLA_HEADS * GLA_DV
GMLP_HEADS = 4
GMLP_DC = 128
GMLP_WIDTH = GMLP_HEADS * GMLP_DC
GLA_GATE_RANK = 16
GLA_TAU = 16.0
GLA_CHUNK = 64
GMLP_CHUNK = 128
D_FF = 2816
CONV_W = 3
PLE_DIM = 256
EPS = 1e-6

LANES = 128
SUBLANES = 8
MXU_N = 256
HEAD_PAIRS = GLA_HEADS // 2

COL_QK = 0
COL_V = COL_QK + 2 * GLA_KEY
COL_R = COL_V + GLA_WIDTH
IN_COLS_GLA = COL_R + GLA_WIDTH
COL_U = 0
COL_VS = COL_U + GMLP_WIDTH
COL_GLR = COL_VS + GMLP_WIDTH
IN_COLS_SGU = COL_GLR + LANES

FF_BLOCK = 256
N_FF_BLOCKS = D_FF // FF_BLOCK
FF_GROUP = 4

PROMPT_TILE = 256
MIXER_SEQS_PER_STEP = 4
FFN_PROMPT_TILE = 256
SAMPLE_TILE = 128
CAST_STEPS = 8

VMEM_LIMIT_BYTES = 56 * 1024 * 1024

BF16 = jnp.bfloat16
F32 = jnp.float32


def _dot(a, b):
    return jnp.dot(a, b, preferred_element_type=F32)


def _dot_wide(a, w_ref, rows=slice(None)):
    return jnp.concatenate(
        [_dot(a, w_ref[rows, c:c + MXU_N]) for c in range(0, D_MODEL, MXU_N)], axis=-1)


def _dot_nt(a, b):
    return lax.dot_general(a, b, (((1,), (1,)), ((), ())), preferred_element_type=F32)


def _dot_tn(a, b):
    return lax.dot_general(a, b, (((0,), (0,)), ((), ())), preferred_element_type=F32)


def _rms(x, g):
    return x * lax.rsqrt(jnp.mean(x * x, axis=-1, keepdims=True) + EPS) * g


def _gelu_gate(x):
    c = -2.0 * np.sqrt(2.0 / np.pi) * np.log2(np.e)
    a1 = np.float32(c)
    a3 = np.float32(c * 0.044715)
    return 1.0 / (1.0 + jnp.exp2(x * (a3 * (x * x) + a1)))


def _gelu(x):
    return x * _gelu_gate(x)


def _sigmoid(x):
    return 1.0 / (1.0 + jnp.exp(-x))


def _log_sigmoid(x):
    return jnp.minimum(x, 0.0) - jnp.log1p(jnp.exp(-jnp.abs(x)))


def _split_bf16(x):
    hi = x.astype(BF16)
    lo = (x - hi.astype(F32)).astype(BF16)
    return hi, lo


def _mixer_kernel(*refs, rows, sample, n_par=1):
    if sample:
        (x_ref, sin_ref, g_pre_ref, w_ga_ref, w_sg_ref, w_gate_ref, b_gate_ref, g_gla_ref, g_ln_ref, b_ln_ref,
         w_sp_ref, b_sp_ref, w_out_ref, g_post_ref, cum_ref, tot_ref, causal_ref,
         y_ref, sout_ref, vn_ref, o_s) = refs
        xs, ys, os_ = [x_ref], [y_ref], [o_s]
    else:
        (x_ref, g_pre_ref, w_ga_ref, w_sg_ref, w_gate_ref, b_gate_ref, g_gla_ref, g_ln_ref, b_ln_ref,
         w_sp_ref, b_sp_ref, w_out_ref, g_post_ref, cum_ref, tot_ref, causal_ref,
         y_ref, sout_ref, o_s, st_s) = refs
        t = pl.program_id(1)

        @pl.when(t == 0)
        def _():
            st_s[...] = jnp.zeros_like(st_s)

        xs = [x_ref.at[i] for i in range(n_par)]
        ys = [y_ref.at[i] for i in range(n_par)]
        os_ = [o_s.at[i] for i in range(n_par)]

    lane = lax.broadcasted_iota(jnp.int32, (rows, LANES), 1)
    head_lane_masks = (lane < GLA_DK, lane >= GLA_DK)
    tiles = [dict() for _ in range(n_par)]

    def stage_norm(i):
        d = tiles[i]
        d['x'] = xs[i][...]
        d['h'] = _rms(d['x'], g_pre_ref[...]).astype(BF16)

    def stage_project(i):
        d = tiles[i]
        h = d['h']
        glr = _dot(h, w_sg_ref[:, COL_GLR:IN_COLS_SGU])
        d['logit'] = _dot(glr.astype(BF16), w_gate_ref[...]) + b_gate_ref[...]
        d['u_pre'] = _dot(h, w_sg_ref[:, COL_U:COL_VS])
        d['vs_pre'] = _dot(h, w_sg_ref[:, COL_VS:COL_GLR])
        d['qk'] = _dot(h, w_ga_ref[:, COL_QK:COL_V])
        d['v'] = _dot(h, w_ga_ref[:, COL_V:COL_R])
        d['r'] = _dot(h, w_ga_ref[:, COL_R:IN_COLS_GLA])

    def stage_gate_sgu(i):
        d = tiles[i]
        d['g'] = _split_bf16(_log_sigmoid(d.pop('logit')) * (1.0 / GLA_TAU))
        d['u'] = _gelu(d.pop('u_pre'))
        vs = _gelu(d.pop('vs_pre'))
        vn_parts = []
        for j in range(GMLP_HEADS):
            vh = vs[:, j * GMLP_DC:(j + 1) * GMLP_DC]
            mu = jnp.mean(vh, axis=-1, keepdims=True)
            c = vh - mu
            var = jnp.mean(c * c, axis=-1, keepdims=True)
            vn_parts.append(c * lax.rsqrt(var + EPS))
        vn = jnp.concatenate(vn_parts, axis=-1) * g_ln_ref[...] + b_ln_ref[...]
        if sample:
            vn_ref[...] = vn
        d['vn_bf'] = vn.astype(BF16)

    def stage_mix_sgu(i):
        d = tiles[i]
        g_hi, g_lo = d.pop('g')
        cum = cum_ref[...]
        tot = tot_ref[...]
        d['b'] = _dot(cum, g_hi) + _dot(cum, g_lo)
        d['bl'] = _dot(tot, g_hi) + _dot(tot, g_lo)
        vn_bf = d.pop('vn_bf')
        mixed_parts = []
        for cch in range(rows // GMLP_CHUNK):
            crows = slice(cch * GMLP_CHUNK, (cch + 1) * GMLP_CHUNK)
            mixed_parts.append(jnp.concatenate(
                [_dot(w_sp_ref[j], vn_bf[crows, j * GMLP_DC:(j + 1) * GMLP_DC]) for j in range(GMLP_HEADS)],
                axis=-1) + b_sp_ref[...])
        so = d.pop('u') * jnp.concatenate(mixed_parts, axis=0)
        d['mix'] = _dot_wide(so.astype(BF16), w_out_ref, slice(GLA_WIDTH, GLA_WIDTH + GMLP_WIDTH))

    def stage_decay(i):
        d = tiles[i]
        qk = d.pop('qk')
        q = qk[:, :GLA_KEY] * (GLA_DK ** -0.5)
        k = qk[:, GLA_KEY:]
        b, bl = d.pop('b'), d['bl']
        d['v_bf'] = d['v'].astype(BF16)
        d['qb'] = q * jnp.exp(b)
        d['kn'] = k * jnp.exp(-b)
        d['kd'] = k * jnp.exp(bl - b)

    def stage_intra(i):
        d = tiles[i]
        qb, kn, kd, v, v_bf, bl = d.pop('qb'), d.pop('kn'), d.pop('kd'), d.pop('v'), d['v_bf'], d['bl']
        causal = causal_ref[...]
        d['pairs'] = []
        for p in range(HEAD_PAIRS):
            ls = slice(p * LANES, (p + 1) * LANES)
            qbp = qb[:, ls]
            knp = kn[:, ls].astype(BF16)
            kdp = kd[:, ls]
            o_intra = []
            for hh in range(2):
                hd = 2 * p + hh
                qm = jnp.where(head_lane_masks[hh], qbp, 0.0).astype(BF16)
                s = _dot_nt(qm, knp) * causal
                o_intra.append(_dot(s.astype(BF16), v_bf[:, hd * GLA_DV:(hd + 1) * GLA_DV]))

            if not sample:
                st_shape = (2 * GLA_DV, 2 * GLA_DK)
                ri = lax.broadcasted_iota(jnp.int32, st_shape, 0)
                ci = lax.broadcasted_iota(jnp.int32, st_shape, 1)
                same_head = (ri < GLA_DV) == (ci < GLA_DK)
                kdp_bf = kdp.astype(BF16)
                upd = []
                for c in range(rows // GLA_CHUNK):
                    rws = slice(c * GLA_CHUNK, (c + 1) * GLA_CHUNK)
                    inc = _dot_tn(v_bf[rws, 2 * p * GLA_DV:(2 * p + 2) * GLA_DV], kdp_bf[rws, :])
                    upd.append(jnp.where(same_head, inc, 0.0))
                d['pairs'].append((ls, qbp.astype(BF16), o_intra, upd))
            else:
                seq_len = 4
                seqs_per_grp = SUBLANES // seq_len
                kdp_t = kdp.T.astype(BF16)
                dec_t = jnp.exp(bl[:, ls]).T
                sub = lax.broadcasted_iota(jnp.int32, (SUBLANES, LANES), 0)
                crow = lax.broadcasted_iota(jnp.int32, (rows, LANES), 0)
                for grp in range(rows // SUBLANES):
                    g0 = grp * SUBLANES
                    qg = qbp[g0:g0 + SUBLANES, :]
                    o_acc = [o_intra[hh][g0:g0 + SUBLANES, :] for hh in range(2)]
                    for j in range(seqs_per_grp):
                        sidx = grp * seqs_per_grp + j
                        s_pair = sin_ref[sidx, p]
                        s_bf = s_pair.astype(BF16)
                        in_seq = (sub >= j * seq_len) & (sub < (j + 1) * seq_len)
                        in_seq_c = (crow >= g0 + j * seq_len) & (crow < g0 + (j + 1) * seq_len)
                        new_rows = []
                        for hh in range(2):
                            hd = 2 * p + hh
                            qsel = jnp.where(in_seq & head_lane_masks[hh][:SUBLANES], qg, 0.0).astype(BF16)
                            o_acc[hh] = o_acc[hh] + _dot(qsel, s_bf)
                            vsel = jnp.where(in_seq_c, v[:, hd * GLA_DV:(hd + 1) * GLA_DV], 0.0).astype(BF16)
                            new_rows.append(_dot(kdp_t[hh * GLA_DK:(hh + 1) * GLA_DK, :], vsel))
                        col = g0 + j * seq_len
                        sout_ref[sidx, p] = s_pair * dec_t[:, col:col + 1] + jnp.concatenate(new_rows, axis=0)
                    for hh in range(2):
                        hd = 2 * p + hh
                        os_[i][g0:g0 + SUBLANES, hd * GLA_DV:(hd + 1) * GLA_DV] = o_acc[hh]

    def stage_state(i):
        if sample:
            return
        d = tiles[i]
        bl = d.pop('bl')
        st = [st_s[i, p] for p in range(HEAD_PAIRS)]
        for c in range(rows // GLA_CHUNK):
            rws = slice(c * GLA_CHUNK, (c + 1) * GLA_CHUNK)
            for p, (ls, qbp_bf, o_intra, upd) in enumerate(d['pairs']):
                o_inter = _dot_nt(qbp_bf[rws, :], st[p].astype(BF16))
                for hh in range(2):
                    hd = 2 * p + hh
                    os_[i][rws, hd * GLA_DV:(hd + 1) * GLA_DV] = (
                        o_intra[hh][rws, :] + o_inter[:, hh * GLA_DV:(hh + 1) * GLA_DV])
                dec = jnp.exp(bl[c * GLA_CHUNK:c * GLA_CHUNK + 1, ls])
                st[p] = st[p] * dec + upd[c]
        for p in range(HEAD_PAIRS):
            st_s[i, p] = st[p]
        d.pop('pairs')

    def stage_out(i):
        d = tiles[i]
        r = d.pop('r')
        out_gate = r * _sigmoid(r)
        mix, x = d.pop('mix'), d.pop('x')
        half = rows // 2
        for hr in (slice(0, half), slice(half, rows)):
            o = os_[i][hr, :]
            og = jnp.concatenate(
                [_rms(o[:, j * GLA_DV:(j + 1) * GLA_DV], 1.0) for j in range(GLA_HEADS)], axis=-1)
            og = og * g_gla_ref[...] * out_gate[hr, :]
            mix_h = mix[hr, :] + _dot_wide(og.astype(BF16), w_out_ref, slice(0, GLA_WIDTH))
            ys[i][hr, :] = x[hr, :] + _rms(mix_h, g_post_ref[...])

    stages = (stage_norm, stage_project, stage_gate_sgu, stage_mix_sgu, stage_decay, stage_intra,
              stage_state, stage_out)
    for k in range(len(stages) + n_par - 1):
        for i in range(n_par):
            if 0 <= k - i < len(stages):
                stages[k - i](i)

    if not sample:
        @pl.when(t == pl.num_programs(1) - 1)
        def _():
            for i in range(n_par):
                for p in range(HEAD_PAIRS):
                    st = st_s[i, p]
                    sout_ref[i, 2 * p] = st[:GLA_DV, :].T[:GLA_DK, :]
                    sout_ref[i, 2 * p + 1] = st[GLA_DV:, :].T[GLA_DK:, :]


def _ffn_tail(x1, f, p, g_ffn_post, g_ple_in, w_pg_ref, w_ple_ref, g_ple_post):
    x2 = x1 + _rms(f, g_ffn_post)
    hp = _rms(x2, g_ple_in).astype(BF16)
    gate = _sigmoid(_dot_wide(hp, w_pg_ref))
    pe = _dot_wide(p.astype(BF16), w_ple_ref)
    return x2 + _rms(pe * gate, g_ple_post)


def _ff_groups():
    return [range(g, min(g + FF_GROUP, N_FF_BLOCKS)) for g in range(0, N_FF_BLOCKS, FF_GROUP)]


def _ffn_prompt_kernel(x_ref, p_ref, g_pre_ref, w_up_ref, w_conv_ref, b_conv_ref, w_down_ref,
                       g_post_ref, g_ple_in_ref, w_pg_ref, w_ple_ref, g_ple_post_ref,
                       y_ref, cs_ref,
                       carry_s, up_s, act_s, *, rows):
    t = pl.program_id(1)
    hdr = SUBLANES

    @pl.when(t == 0)
    def _():
        carry_s[...] = jnp.zeros_like(carry_s)
        up_s[:, :, :, hdr + rows:, :] = jnp.zeros((2, 2, CONV_W, hdr, FF_BLOCK), F32)

    x1 = x_ref[...]
    h = _rms(x1, g_pre_ref[...]).astype(BF16)

    def half_cols(j, half):
        c0 = half * D_FF + j * FF_BLOCK
        return slice(c0, c0 + FF_BLOCK)

    def project(j):
        for half in range(2):
            cols = half_cols(j, half)
            up = _dot(h, w_up_ref[:, cols])
            for s in range(CONV_W):
                buf = up_s.at[j % 2, half, s]
                if s:
                    buf[hdr:2 * hdr, :] = carry_s[s - 1, :, cols]
                buf[hdr + s:hdr + s + rows, :] = up
                if s:
                    carry_s[s - 1, :, cols] = buf[hdr + rows:2 * hdr + rows, :]

    def conv(j, half):
        cols = half_cols(j, half)
        w = w_conv_ref[:, cols]
        taps = [up_s[j % 2, half, s, hdr:hdr + rows, :] for s in range(CONV_W)]
        return (b_conv_ref[:, cols] + taps[2] * w[0:1, :] + taps[1] * w[1:2, :] + taps[0] * w[2:3, :])

    def down(j):
        k_rows = slice(j * FF_BLOCK, (j + 1) * FF_BLOCK)
        return _dot_wide(act_s[:, k_rows], w_down_ref, k_rows)

    project(0)
    f = None
    for j in range(N_FF_BLOCKS):
        if j + 1 < N_FF_BLOCKS:
            project(j + 1)
        if j == 1:
            f = down(0)
        elif j > 1:
            f = f + down(j - 1)
        gate = conv(j, 0)
        act_s[:, j * FF_BLOCK:(j + 1) * FF_BLOCK] = (
            (gate * conv(j, 1)) * _gelu_gate(gate)).astype(BF16)
    f = f + down(N_FF_BLOCKS - 1)

    y_ref[...] = _ffn_tail(x1, f, p_ref[...], g_post_ref[...], g_ple_in_ref[...], w_pg_ref,
                           w_ple_ref, g_ple_post_ref[...])

    @pl.when(t == pl.num_programs(1) - 1)
    def _():
        cs_ref[...] = carry_s[CONV_W - 2, 0:CONV_W - 1, :]


def _ffn_sample_kernel(x_ref, p_ref, cin_ref, g_pre_ref, w_up_ref, w_conv_ref, b_conv_ref, w_down_ref,
                       g_post_ref, g_ple_in_ref, w_pg_ref, w_ple_ref, g_ple_post_ref,
                       y_ref, cs_ref, act_s, *, nseq, seq_len):
    x1 = jnp.concatenate([x_ref[:, j, :] for j in range(seq_len)], axis=0)
    p = jnp.concatenate([p_ref[:, j, :] for j in range(seq_len)], axis=0)
    h = _rms(x1, g_pre_ref[...]).astype(BF16)

    def conv_half(col0):
        cols = slice(col0, col0 + FF_BLOCK)
        up = _dot(h, w_up_ref[:, cols])
        blocks = [cin_ref[:, i, cols] for i in range(CONV_W - 1)]
        blocks += [up[j * nseq:(j + 1) * nseq, :] for j in range(seq_len)]
        w = w_conv_ref[:, cols]
        b = b_conv_ref[:, cols]
        conv = jnp.concatenate(
            [b + blocks[j] * w[0:1, :] + blocks[j + 1] * w[1:2, :] + blocks[j + 2] * w[2:3, :]
             for j in range(seq_len)], axis=0)
        for i in range(CONV_W - 1):
            cs_ref[:, i, cols] = blocks[seq_len + i]
        return conv

    f = None
    for grp in _ff_groups():
        for j in grp:
            gate = conv_half(j * FF_BLOCK)
            val = conv_half(D_FF + j * FF_BLOCK)
            act_s[:, j * FF_BLOCK:(j + 1) * FF_BLOCK] = ((gate * val) * _gelu_gate(gate)).astype(BF16)
        k_rows = slice(grp[0] * FF_BLOCK, (grp[-1] + 1) * FF_BLOCK)
        part = _dot_wide(act_s[:, k_rows], w_down_ref, k_rows)
        f = part if f is None else f + part

    y = _ffn_tail(x1, f, p, g_post_ref[...], g_ple_in_ref[...], w_pg_ref, w_ple_ref,
                  g_ple_post_ref[...])
    for j in range(seq_len):
        y_ref[:, j, :] = y[j * nseq:(j + 1) * nseq, :]


def _cast_kernel(*refs):
    n = (len(refs) - 3) // 2
    w_in_t_ref, srcs = refs[0], refs[1:1 + n]
    ga_ref, sg_ref, dsts = refs[1 + n], refs[2 + n], refs[3 + n:]

    glr0 = IN_COLS_GLA
    u0 = glr0 + GLA_GATE_RANK
    ga_ref[...] = w_in_t_ref[:glr0, :].T.astype(BF16)
    sg_ref[:, :COL_GLR] = w_in_t_ref[u0:u0 + COL_GLR, :].T.astype(BF16)
    tile = w_in_t_ref[glr0:glr0 + LANES, :].T
    lane = lax.broadcasted_iota(jnp.int32, tile.shape, 1)
    sg_ref[:, COL_GLR:] = jnp.where(lane < GLA_GATE_RANK, tile, 0.0).astype(BF16)

    for src, dst in zip(srcs, dsts):
        cols = src.shape[1]
        dst[:, :cols] = src[...].astype(BF16)
        if dst.shape[1] > cols:
            dst[:, cols:] = jnp.zeros((dst.shape[0], dst.shape[1] - cols), BF16)


def _whole(shape):
    n = len(shape)
    return pl.BlockSpec(shape, lambda *_: (0,) * n)


def _full(shape):
    n = len(shape)
    return pl.BlockSpec(shape, lambda *_: (0,) * n, pipeline_mode=pl.Buffered(1))


def _gla_masks(rows, span):
    i = np.arange(rows)
    same = (i[:, None] // span) == (i[None, :] // span)
    causal = same & (i[None, :] <= i[:, None])
    return (jnp.asarray(causal, BF16), jnp.asarray(same, BF16), jnp.asarray(causal, F32))


def kernel(x_prompt, x_sample, state_gla, state_ffn_conv, p_prompt, p_sample, g_mix_pre, w_in, w_gla_gate, b_gla_gate, g_gla_out, g_sgu_ln, b_sgu_ln, w_spatial, b_spatial, w_out, g_mix_post, g_ffn_pre, w_up, w_conv, b_conv, w_down, g_ffn_post, g_ple_in, w_ple_gate, w_ple, g_ple_post):
    depth = x_prompt.ndim - 2
    assert w_in.shape[0] == 1 and depth == 1
    nb, seq, _ = x_prompt.shape
    ns, dseq, _ = x_sample.shape
    assert seq % PROMPT_TILE == 0 and PROMPT_TILE % GMLP_CHUNK == 0 and seq % FFN_PROMPT_TILE == 0
    assert dseq == 4 and (ns * dseq) % SAMPLE_TILE == 0 and SAMPLE_TILE % GMLP_CHUNK == 0

    w_gate_p = jnp.concatenate(
        [w_gla_gate[0], jnp.zeros((LANES - GLA_GATE_RANK, GLA_KEY), w_gla_gate.dtype)], axis=0).astype(BF16)
    row = lambda a: a.reshape(1, -1)
    g_gla_t = jnp.tile(g_gla_out[0], GLA_HEADS).reshape(1, -1)
    n_cast = CAST_STEPS
    in_cols = w_in.shape[-1]
    assert in_cols == IN_COLS_GLA + GLA_GATE_RANK + COL_GLR
    cast_src = (w_out[0], w_up[0], w_down[0], w_ple_gate[0], w_ple[0])
    cast_cols_in = (D_MODEL, 2 * D_FF, D_MODEL, D_MODEL, D_MODEL)
    cast_cols_out = (D_MODEL + LANES, 2 * D_FF, D_MODEL + LANES, D_MODEL + LANES, D_MODEL + LANES)
    cast_rows = tuple(w.shape[0] for w in cast_src)
    assert all(r % (n_cast * 2 * SUBLANES) == 0 for r in cast_rows + (D_MODEL,))
    slab = lambda r, c: pl.BlockSpec((r // n_cast, c), lambda i: (i, 0))
    w_ga, w_sg, w_out_b, w_up_b, w_down_b, w_pg_b, w_ple_b = pl.pallas_call(
        _cast_kernel,
        grid=(n_cast,),
        in_specs=[pl.BlockSpec((None, in_cols, D_MODEL // n_cast), lambda i: (0, 0, i))]
        + [slab(r, c) for r, c in zip(cast_rows, cast_cols_in)],
        out_specs=[slab(D_MODEL, IN_COLS_GLA), slab(D_MODEL, IN_COLS_SGU)]
        + [slab(r, c) for r, c in zip(cast_rows, cast_cols_out)],
        out_shape=[jax.ShapeDtypeStruct((D_MODEL, IN_COLS_GLA), BF16),
                   jax.ShapeDtypeStruct((D_MODEL, IN_COLS_SGU), BF16)]
        + [jax.ShapeDtypeStruct((r, c), BF16) for r, c in zip(cast_rows, cast_cols_out)],
        compiler_params=pltpu.CompilerParams(
            dimension_semantics=("arbitrary",), vmem_limit_bytes=VMEM_LIMIT_BYTES),
        name="weight_cast",
    )(jnp.swapaxes(w_in, 1, 2), *cast_src)

    def spatial(c):
        i = np.arange(GMLP_CHUNK)
        keep = ((i[:, None] // c) == (i[None, :] // c)) & (i[None, :] <= i[:, None])
        e = jnp.asarray((i[:, None] % c) == np.arange(c)[None, :], w_spatial.dtype)
        hi = lax.Precision.HIGHEST
        wbd = jnp.einsum('ia,hab,jb->hij', e, w_spatial[0][:, :c, :c], e, precision=hi)
        wbd = wbd * jnp.asarray(keep, w_spatial.dtype)
        bias = jnp.einsum('ia,ha->ih', e, b_spatial[0][:, :c], precision=hi)
        return wbd.astype(BF16), jnp.repeat(bias, GMLP_DC, axis=1)

    mixer_weights = lambda wsp, bsp, masks: (
        row(g_mix_pre), w_ga, w_sg, w_gate_p, row(b_gla_gate), g_gla_t, row(g_sgu_ln), row(b_sgu_ln),
        wsp, bsp, w_out_b, row(g_mix_post)) + masks

    def mixer_weight_specs(rows):
        return [
            _full((1, D_MODEL)), _full((D_MODEL, IN_COLS_GLA)), _full((D_MODEL, IN_COLS_SGU)),
            _full((LANES, GLA_KEY)),
            _full((1, GLA_KEY)), _full((1, GLA_WIDTH)), _full((1, GMLP_WIDTH)), _full((1, GMLP_WIDTH)),
            _full((GMLP_HEADS, GMLP_CHUNK, GMLP_CHUNK)), _full((GMLP_CHUNK, GMLP_WIDTH)),
            _full((D_MODEL, D_MODEL + LANES)), _full((1, D_MODEL)),
            _full((rows, rows)), _full((rows, rows)), _full((rows, rows))]

    tl = PROMPT_TILE
    wsp_p, bsp_p = spatial(GMLP_CHUNK)
    npar = MIXER_SEQS_PER_STEP
    assert nb % npar == 0
    x1_p, gla_p = pl.pallas_call(
        functools.partial(_mixer_kernel, rows=tl, sample=False, n_par=npar),
        grid=(nb // npar, seq // tl),
        in_specs=[pl.BlockSpec((npar, None, tl, D_MODEL), lambda b, t: (0, b, t, 0))] + mixer_weight_specs(tl),
        out_specs=[pl.BlockSpec((npar, None, tl, D_MODEL), lambda b, t: (0, b, t, 0)),
                   pl.BlockSpec((npar, None, GLA_HEADS, GLA_DK, GLA_DV), lambda b, t: (0, b, 0, 0, 0))],
        out_shape=[jax.ShapeDtypeStruct((npar, nb // npar, seq, D_MODEL), F32),
                   jax.ShapeDtypeStruct((npar, nb // npar, GLA_HEADS, GLA_DK, GLA_DV), F32)],
        scratch_shapes=[pltpu.VMEM((npar, tl, GLA_WIDTH), F32),
                        pltpu.VMEM((npar, HEAD_PAIRS, 2 * GLA_DV, 2 * GLA_DK), F32)],
        compiler_params=pltpu.CompilerParams(
            dimension_semantics=("arbitrary", "arbitrary"), vmem_limit_bytes=VMEM_LIMIT_BYTES),
        name="mixer_prompt",
    )(x_prompt.reshape(npar, nb // npar, seq, D_MODEL),
      *mixer_weights(wsp_p, bsp_p, _gla_masks(tl, GLA_CHUNK)))
    x1_p = x1_p.reshape(nb, seq, D_MODEL)
    gla_p = gla_p.reshape(nb, GLA_HEADS, GLA_DK, GLA_DV)

    ts = SAMPLE_TILE
    seqs_per_tile = ts // dseq
    wsp_s, bsp_s = spatial(dseq)
    x1_s, gla_s, vrows_s = pl.pallas_call(
        functools.partial(_mixer_kernel, rows=ts, sample=True),
        grid=(ns * dseq // ts,),
        in_specs=[pl.BlockSpec((ts, D_MODEL), lambda i: (i, 0)),
                  pl.BlockSpec((seqs_per_tile, HEAD_PAIRS, 2 * GLA_DK, GLA_DV), lambda i: (i, 0, 0, 0))]
        + mixer_weight_specs(ts),
        out_specs=[pl.BlockSpec((ts, D_MODEL), lambda i: (i, 0)),
                   pl.BlockSpec((seqs_per_tile, HEAD_PAIRS, 2 * GLA_DK, GLA_DV), lambda i: (i, 0, 0, 0)),
                   pl.BlockSpec((ts, GMLP_WIDTH), lambda i: (i, 0))],
        out_shape=[jax.ShapeDtypeStruct((ns * dseq, D_MODEL), F32),
                   jax.ShapeDtypeStruct((ns, HEAD_PAIRS, 2 * GLA_DK, GLA_DV), F32),
                   jax.ShapeDtypeStruct((ns * dseq, GMLP_WIDTH), F32)],
        scratch_shapes=[pltpu.VMEM((ts, GLA_WIDTH), F32)],
        compiler_params=pltpu.CompilerParams(
            dimension_semantics=("arbitrary",), vmem_limit_bytes=VMEM_LIMIT_BYTES),
        name="mixer_sample",
    )(x_sample.reshape(ns * dseq, D_MODEL),
      state_gla[0].reshape(ns, HEAD_PAIRS, 2 * GLA_DK, GLA_DV),
      *mixer_weights(wsp_s, bsp_s, _gla_masks(ts, dseq)))

    ffn_weights = (row(g_ffn_pre), w_up_b, w_conv[0], row(b_conv), w_down_b, row(g_ffn_post),
                   row(g_ple_in), w_pg_b, w_ple_b, row(g_ple_post))
    ffn_weight_specs = [
        _full((1, D_MODEL)), _full((D_MODEL, 2 * D_FF)), _full((CONV_W, 2 * D_FF)), _full((1, 2 * D_FF)),
        _full((D_FF, D_MODEL + LANES)), _full((1, D_MODEL)), _full((1, D_MODEL)),
        _full((D_MODEL, D_MODEL + LANES)), _full((PLE_DIM, D_MODEL + LANES)), _full((1, D_MODEL))]

    tf = FFN_PROMPT_TILE
    y_p, conv_p = pl.pallas_call(
        functools.partial(_ffn_prompt_kernel, rows=tf),
        grid=(nb, seq // tf),
        in_specs=[pl.BlockSpec((None, tf, D_MODEL), lambda b, t: (b, t, 0)),
                  pl.BlockSpec((None, tf, PLE_DIM), lambda b, t: (b, t, 0))] + ffn_weight_specs,
        out_specs=[pl.BlockSpec((None, tf, D_MODEL), lambda b, t: (b, t, 0)),
                   pl.BlockSpec((None, CONV_W - 1, 2 * D_FF), lambda b, t: (b, 0, 0))],
        out_shape=[jax.ShapeDtypeStruct((nb, seq, D_MODEL), F32),
                   jax.ShapeDtypeStruct((nb, CONV_W - 1, 2 * D_FF), F32)],
        scratch_shapes=[pltpu.VMEM((CONV_W - 1, SUBLANES, 2 * D_FF), F32),
                        pltpu.VMEM((2, 2, CONV_W, tf + 2 * SUBLANES, FF_BLOCK), F32),
                        pltpu.VMEM((tf, D_FF), BF16)],
        compiler_params=pltpu.CompilerParams(
            dimension_semantics=("arbitrary", "arbitrary"), vmem_limit_bytes=VMEM_LIMIT_BYTES),
        name="ffn_prompt",
    )(x1_p, p_prompt[0], *ffn_weights)

    y_s, conv_s = pl.pallas_call(
        functools.partial(_ffn_sample_kernel, nseq=ns, seq_len=dseq),
        grid=(1,),
        in_specs=[_whole((ns, dseq, D_MODEL)), _whole((ns, dseq, PLE_DIM)),
                  _whole((ns, CONV_W - 1, 2 * D_FF))] + ffn_weight_specs,
        out_specs=[_whole((ns, dseq, D_MODEL)), _whole((ns, CONV_W - 1, 2 * D_FF))],
        out_shape=[jax.ShapeDtypeStruct((ns, dseq, D_MODEL), F32),
                   jax.ShapeDtypeStruct((ns, CONV_W - 1, 2 * D_FF), F32)],
        scratch_shapes=[pltpu.VMEM((ns * dseq, D_FF), BF16)],
        compiler_params=pltpu.CompilerParams(
            dimension_semantics=("arbitrary",), vmem_limit_bytes=VMEM_LIMIT_BYTES),
        name="ffn_sample",
    )(x1_s.reshape(ns, dseq, D_MODEL), p_sample[0], state_ffn_conv[0], *ffn_weights)

    return (y_p,
            y_s,
            gla_p[None],
            gla_s.reshape(1, ns, GLA_HEADS, GLA_DK, GLA_DV),
            conv_p[None],
            conv_s[None],
            vrows_s.reshape(1, ns, dseq, GMLP_WIDTH))
```

```python
import functools

import numpy as np
import jax
import jax.numpy as jnp
from jax import lax
from jax.experimental import pallas as pl
from jax.experimental.pallas import tpu as pltpu

D_MODEL = 1024
GLA_HEADS = 4
GLA_DK = 64
GLA_DV = 128
GLA_KEY = GLA_HEADS * GLA_DK
GLA_WIDTH = GLA_HEADS * GLA_DV
GMLP_HEADS = 4
GMLP_DC = 128
GMLP_WIDTH = GMLP_HEADS * GMLP_DC
GLA_GATE_RANK = 16
GLA_TAU = 16.0
GLA_CHUNK = 64
GMLP_CHUNK = 128
D_FF = 2816
CONV_W = 3
PLE_DIM = 256
EPS = 1e-6

LANES = 128
SUBLANES = 8
MXU_N = 256
HEAD_PAIRS = GLA_HEADS // 2

COL_QK = 0
COL_V = COL_QK + 2 * GLA_KEY
COL_R = COL_V + GLA_WIDTH
IN_COLS_GLA = COL_R + GLA_WIDTH
COL_U = 0
COL_VS = COL_U + GMLP_WIDTH
COL_GLR = COL_VS + GMLP_WIDTH
IN_COLS_SGU = COL_GLR + LANES

FF_BLOCK = 256
N_FF_BLOCKS = D_FF // FF_BLOCK

PROMPT_TILE = 256
MIXER_SEQS_PER_STEP = 4
FFN_PROMPT_TILE = 256
SAMPLE_TILE = 128
CAST_STEPS = 8

VMEM_LIMIT_BYTES = 56 * 1024 * 1024

BF16 = jnp.bfloat16
F32 = jnp.float32


def _dot(a, b):
    return jnp.dot(a, b, preferred_element_type=F32)


def _dot_wide(a, w_ref, rows=slice(None)):
    return jnp.concatenate(
        [_dot(a, w_ref[rows, c:c + MXU_N]) for c in range(0, D_MODEL, MXU_N)], axis=-1)


def _dot_nt(a, b):
    return lax.dot_general(a, b, (((1,), (1,)), ((), ())), preferred_element_type=F32)


def _dot_tn(a, b):
    return lax.dot_general(a, b, (((0,), (0,)), ((), ())), preferred_element_type=F32)


def _rms(x, g):
    return x * lax.rsqrt(jnp.mean(x * x, axis=-1, keepdims=True) + EPS) * g


def _gelu_gate(x):
    c = -2.0 * np.sqrt(2.0 / np.pi) * np.log2(np.e)
    a1 = np.float32(c)
    a3 = np.float32(c * 0.044715)
    return 1.0 / (1.0 + jnp.exp2(x * (a3 * (x * x) + a1)))


def _gelu(x):
    return x * _gelu_gate(x)


def _sigmoid(x):
    return 1.0 / (1.0 + jnp.exp(-x))


def _log_sigmoid(x):
    return jnp.minimum(x, 0.0) - jnp.log1p(jnp.exp(-jnp.abs(x)))


def _split_bf16(x):
    hi = x.astype(BF16)
    lo = (x - hi.astype(F32)).astype(BF16)
    return hi, lo


def _mixer_kernel(*refs, rows, sample, n_par=1):
    if sample:
        (x_ref, sin_ref, g_pre_ref, w_ga_ref, w_sg_ref, w_gate_ref, b_gate_ref, g_gla_ref, g_ln_ref, b_ln_ref,
         w_sp_ref, b_sp_ref, w_out_ref, g_post_ref, cum_ref, tot_ref, causal_ref,
         y_ref, sout_ref, vn_ref, o_s) = refs
        xs, ys, os_ = [x_ref], [y_ref], [o_s]
    else:
        (x_ref, g_pre_ref, w_ga_ref, w_sg_ref, w_gate_ref, b_gate_ref, g_gla_ref, g_ln_ref, b_ln_ref,
         w_sp_ref, b_sp_ref, w_out_ref, g_post_ref, cum_ref, tot_ref, causal_ref,
         y_ref, sout_ref, o_s, st_s) = refs
        t = pl.program_id(1)

        @pl.when(t == 0)
        def _():
            st_s[...] = jnp.zeros_like(st_s)

        xs = [x_ref.at[i] for i in range(n_par)]
        ys = [y_ref.at[i] for i in range(n_par)]
        os_ = [o_s.at[i] for i in range(n_par)]

    lane = lax.broadcasted_iota(jnp.int32, (rows, LANES), 1)
    head_lane_masks = (lane < GLA_DK, lane >= GLA_DK)
    tiles = [dict() for _ in range(n_par)]

    def stage_norm(i):
        d = tiles[i]
        d['x'] = xs[i][...]
        d['h'] = _rms(d['x'], g_pre_ref[...]).astype(BF16)

    def stage_project(i):
        d = tiles[i]
        h = d['h']
        glr = _dot(h, w_sg_ref[:, COL_GLR:IN_COLS_SGU])
        d['logit'] = _dot(glr.astype(BF16), w_gate_ref[...]) + b_gate_ref[...]
        d['u_pre'] = _dot(h, w_sg_ref[:, COL_U:COL_VS])
        d['vs_pre'] = _dot(h, w_sg_ref[:, COL_VS:COL_GLR])
        d['qk'] = _dot(h, w_ga_ref[:, COL_QK:COL_V])
        d['v'] = _dot(h, w_ga_ref[:, COL_V:COL_R])
        d['r'] = _dot(h, w_ga_ref[:, COL_R:IN_COLS_GLA])

    def stage_gate_sgu(i):
        d = tiles[i]
        d['g'] = _split_bf16(_log_sigmoid(d.pop('logit')) * (1.0 / GLA_TAU))
        d['u'] = _gelu(d.pop('u_pre'))
        vs = _gelu(d.pop('vs_pre'))
        vn_parts = []
        for j in range(GMLP_HEADS):
            vh = vs[:, j * GMLP_DC:(j + 1) * GMLP_DC]
            mu = jnp.mean(vh, axis=-1, keepdims=True)
            c = vh - mu
            var = jnp.mean(c * c, axis=-1, keepdims=True)
            vn_parts.append(c * lax.rsqrt(var + EPS))
        vn = jnp.concatenate(vn_parts, axis=-1) * g_ln_ref[...] + b_ln_ref[...]
        if sample:
            vn_ref[...] = vn
        d['vn_bf'] = vn.astype(BF16)

    def stage_mix_sgu(i):
        d = tiles[i]
        g_hi, g_lo = d.pop('g')
        cum = cum_ref[...]
        tot = tot_ref[...]
        d['b'] = _dot(cum, g_hi) + _dot(cum, g_lo)
        d['bl'] = _dot(tot, g_hi) + _dot(tot, g_lo)
        vn_bf = d.pop('vn_bf')
        mixed_parts = []
        for cch in range(rows // GMLP_CHUNK):
            crows = slice(cch * GMLP_CHUNK, (cch + 1) * GMLP_CHUNK)
            mixed_parts.append(jnp.concatenate(
                [_dot(w_sp_ref[j], vn_bf[crows, j * GMLP_DC:(j + 1) * GMLP_DC]) for j in range(GMLP_HEADS)],
                axis=-1) + b_sp_ref[...])
        so = d.pop('u') * jnp.concatenate(mixed_parts, axis=0)
        d['mix'] = _dot_wide(so.astype(BF16), w_out_ref, slice(GLA_WIDTH, GLA_WIDTH + GMLP_WIDTH))

    def stage_decay(i):
        d = tiles[i]
        qk = d.pop('qk')
        q = qk[:, :GLA_KEY] * (GLA_DK ** -0.5)
        k = qk[:, GLA_KEY:]
        b, bl = d.pop('b'), d['bl']
        d['v_bf'] = d['v'].astype(BF16)
        d['qb'] = q * jnp.exp(b)
        d['kn'] = k * jnp.exp(-b)
        d['kd'] = k * jnp.exp(bl - b)

    def stage_intra(i):
        d = tiles[i]
        qb, kn, kd, v, v_bf, bl = d.pop('qb'), d.pop('kn'), d.pop('kd'), d.pop('v'), d['v_bf'], d['bl']
        causal = causal_ref[...]
        d['pairs'] = []
        for p in range(HEAD_PAIRS):
            ls = slice(p * LANES, (p + 1) * LANES)
            qbp = qb[:, ls]
            knp = kn[:, ls].astype(BF16)
            kdp = kd[:, ls]
            o_intra = []
            for hh in range(2):
                hd = 2 * p + hh
                qm = jnp.where(head_lane_masks[hh], qbp, 0.0).astype(BF16)
                s = _dot_nt(qm, knp) * causal
                o_intra.append(_dot(s.astype(BF16), v_bf[:, hd * GLA_DV:(hd + 1) * GLA_DV]))

            if not sample:
                st_shape = (2 * GLA_DV, 2 * GLA_DK)
                ri = lax.broadcasted_iota(jnp.int32, st_shape, 0)
                ci = lax.broadcasted_iota(jnp.int32, st_shape, 1)
                same_head = (ri < GLA_DV) == (ci < GLA_DK)
                kdp_bf = kdp.astype(BF16)
                upd = []
                for c in range(rows // GLA_CHUNK):
                    rws = slice(c * GLA_CHUNK, (c + 1) * GLA_CHUNK)
                    inc = _dot_tn(v_bf[rws, 2 * p * GLA_DV:(2 * p + 2) * GLA_DV], kdp_bf[rws, :])
                    upd.append(jnp.where(same_head, inc, 0.0))
                d['pairs'].append((ls, qbp.astype(BF16), o_intra, upd))
            else:
                seq_len = 4
                seqs_per_grp = SUBLANES // seq_len
                kdp_t = kdp.T.astype(BF16)
                dec_t = jnp.exp(bl[:, ls]).T
                sub = lax.broadcasted_iota(jnp.int32, (SUBLANES, LANES), 0)
                crow = lax.broadcasted_iota(jnp.int32, (rows, LANES), 0)
                for grp in range(rows // SUBLANES):
                    g0 = grp * SUBLANES
                    qg = qbp[g0:g0 + SUBLANES, :]
                    o_acc = [o_intra[hh][g0:g0 + SUBLANES, :] for hh in range(2)]
                    for j in range(seqs_per_grp):
                        sidx = grp * seqs_per_grp + j
                        s_pair = sin_ref[sidx, p]
                        s_bf = s_pair.astype(BF16)
                        in_seq = (sub >= j * seq_len) & (sub < (j + 1) * seq_len)
                        in_seq_c = (crow >= g0 + j * seq_len) & (crow < g0 + (j + 1) * seq_len)
                        new_rows = []
                        for hh in range(2):
                            hd = 2 * p + hh
                            qsel = jnp.where(in_seq & head_lane_masks[hh][:SUBLANES], qg, 0.0).astype(BF16)
                            o_acc[hh] = o_acc[hh] + _dot(qsel, s_bf)
                            vsel = jnp.where(in_seq_c, v[:, hd * GLA_DV:(hd + 1) * GLA_DV], 0.0).astype(BF16)
                            new_rows.append(_dot(kdp_t[hh * GLA_DK:(hh + 1) * GLA_DK, :], vsel))
                        col = g0 + j * seq_len
                        sout_ref[sidx, p] = s_pair * dec_t[:, col:col + 1] + jnp.concatenate(new_rows, axis=0)
                    for hh in range(2):
                        hd = 2 * p + hh
                        os_[i][g0:g0 + SUBLANES, hd * GLA_DV:(hd + 1) * GLA_DV] = o_acc[hh]

    def stage_state(i):
        if sample:
            return
        d = tiles[i]
        bl = d.pop('bl')
        st = [st_s[i, p] for p in range(HEAD_PAIRS)]
        for c in range(rows // GLA_CHUNK):
            rws = slice(c * GLA_CHUNK, (c + 1) * GLA_CHUNK)
            for p, (ls, qbp_bf, o_intra, upd) in enumerate(d['pairs']):
                o_inter = _dot_nt(qbp_bf[rws, :], st[p].astype(BF16))
                for hh in range(2):
                    hd = 2 * p + hh
                    os_[i][rws, hd * GLA_DV:(hd + 1) * GLA_DV] = (
                        o_intra[hh][rws, :] + o_inter[:, hh * GLA_DV:(hh + 1) * GLA_DV])
                dec = jnp.exp(bl[c * GLA_CHUNK:c * GLA_CHUNK + 1, ls])
                st[p] = st[p] * dec + upd[c]
        for p in range(HEAD_PAIRS):
            st_s[i, p] = st[p]
        d.pop('pairs')

    def stage_out(i):
        d = tiles[i]
        r = d.pop('r')
        out_gate = r * _sigmoid(r)
        mix, x = d.pop('mix'), d.pop('x')
        half = rows // 2
        for hr in (slice(0, half), slice(half, rows)):
            o = os_[i][hr, :]
            og = jnp.concatenate(
                [_rms(o[:, j * GLA_DV:(j + 1) * GLA_DV], 1.0) for j in range(GLA_HEADS)], axis=-1)
            og = og * g_gla_ref[...] * out_gate[hr, :]
            mix_h = mix[hr, :] + _dot_wide(og.astype(BF16), w_out_ref, slice(0, GLA_WIDTH))
            ys[i][hr, :] = x[hr, :] + _rms(mix_h, g_post_ref[...])

    stages = (stage_norm, stage_project, stage_gate_sgu, stage_mix_sgu, stage_decay, stage_intra,
              stage_state, stage_out)
    for k in range(len(stages) + n_par - 1):
        for i in range(n_par):
            if 0 <= k - i < len(stages):
                stages[k - i](i)

    if not sample:
        @pl.when(t == pl.num_programs(1) - 1)
        def _():
            for i in range(n_par):
                for p in range(HEAD_PAIRS):
                    st = st_s[i, p]
                    sout_ref[i, 2 * p] = st[:GLA_DV, :].T[:GLA_DK, :]
                    sout_ref[i, 2 * p + 1] = st[GLA_DV:, :].T[GLA_DK:, :]


def _ffn_tail(x1, f, p, g_ffn_post, g_ple_in, w_pg_ref, w_ple_ref, g_ple_post):
    x2 = x1 + _rms(f, g_ffn_post)
    hp = _rms(x2, g_ple_in).astype(BF16)
    gate = _sigmoid(_dot_wide(hp, w_pg_ref))
    pe = _dot_wide(p.astype(BF16), w_ple_ref)
    return x2 + _rms(pe * gate, g_ple_post)


def _ffn_prompt_kernel(x_ref, p_ref, g_pre_ref, w_up_ref, w_conv_ref, b_conv_ref, w_down_ref,
                       g_post_ref, g_ple_in_ref, w_pg_ref, w_ple_ref, g_ple_post_ref,
                       y_ref, cs_ref,
                       carry_s, up_s, act_s, *, rows):
    t = pl.program_id(1)
    hdr = SUBLANES

    @pl.when(t == 0)
    def _():
        carry_s[...] = jnp.zeros_like(carry_s)
        up_s[:, :, :, hdr + rows:, :] = jnp.zeros((2, 2, CONV_W, hdr, FF_BLOCK), F32)

    x1 = x_ref[...]
    h = _rms(x1, g_pre_ref[...]).astype(BF16)

    def half_cols(j, half):
        c0 = half * D_FF + j * FF_BLOCK
        return slice(c0, c0 + FF_BLOCK)

    def project(j):
        for half in range(2):
            cols = half_cols(j, half)
            up = _dot(h, w_up_ref[:, cols])
            for s in range(CONV_W):
                buf = up_s.at[j % 2, half, s]
                if s:
                    buf[hdr:2 * hdr, :] = carry_s[s - 1, :, cols]
                buf[hdr + s:hdr + s + rows, :] = up
                if s:
                    carry_s[s - 1, :, cols] = buf[hdr + rows:2 * hdr + rows, :]

    def conv(j, half):
        cols = half_cols(j, half)
        w = w_conv_ref[:, cols]
        taps = [up_s[j % 2, half, s, hdr:hdr + rows, :] for s in range(CONV_W)]
        return (b_conv_ref[:, cols] + taps[2] * w[0:1, :] + taps[1] * w[1:2, :] + taps[0] * w[2:3, :])

    def down(j):
        k_rows = slice(j * FF_BLOCK, (j + 1) * FF_BLOCK)
        return _dot_wide(act_s[:, k_rows], w_down_ref, k_rows)

    project(0)
    f = None
    for j in range(N_FF_BLOCKS):
        if j + 1 < N_FF_BLOCKS:
            project(j + 1)
        if j == 1:
            f = down(0)
        elif j > 1:
            f = f + down(j - 1)
        gate = conv(j, 0)
        act_s[:, j * FF_BLOCK:(j + 1) * FF_BLOCK] = (
            (gate * conv(j, 1)) * _gelu_gate(gate)).astype(BF16)
    f = f + down(N_FF_BLOCKS - 1)

    y_ref[...] = _ffn_tail(x1, f, p_ref[...], g_post_ref[...], g_ple_in_ref[...], w_pg_ref,
                           w_ple_ref, g_ple_post_ref[...])

    @pl.when(t == pl.num_programs(1) - 1)
    def _():
        cs_ref[...] = carry_s[CONV_W - 2, 0:CONV_W - 1, :]


def _ffn_sample_kernel(x_ref, p_ref, cin_ref, g_pre_ref, w_up_g_ref, w_up_v_ref, w_conv_g_ref, w_conv_v_ref,
                       b_conv_g_ref, b_conv_v_ref, w_down_ref, g_post_ref, g_ple_in_ref, w_pg_ref, w_ple_ref,
                       g_ple_post_ref, y_ref, cs_ref, h_s, f_s, *, nseq, seq_len):
    j = pl.program_id(0)

    def rows_major(ref):
        return jnp.concatenate([ref[:, i, :] for i in range(seq_len)], axis=0)

    @pl.when(j == 0)
    def _():
        h_s[...] = _rms(rows_major(x_ref), g_pre_ref[...]).astype(BF16)

    h = h_s[...]

    def conv_half(half, w_up_ref, w_conv_ref, b_conv_ref):
        up = _dot(h, w_up_ref[...])
        blocks = [cin_ref[:, i, half, :] for i in range(CONV_W - 1)]
        blocks += [up[i * nseq:(i + 1) * nseq, :] for i in range(seq_len)]
        w = w_conv_ref[...]
        b = b_conv_ref[...]
        conv = jnp.concatenate(
            [b + blocks[i] * w[0:1, :] + blocks[i + 1] * w[1:2, :] + blocks[i + 2] * w[2:3, :]
             for i in range(seq_len)], axis=0)
        for i in range(CONV_W - 1):
            cs_ref[:, i, half, :] = blocks[seq_len + i]
        return conv

    gate = conv_half(0, w_up_g_ref, w_conv_g_ref, b_conv_g_ref)
    val = conv_half(1, w_up_v_ref, w_conv_v_ref, b_conv_v_ref)
    part = _dot_wide(((gate * val) * _gelu_gate(gate)).astype(BF16), w_down_ref)

    @pl.when(j == 0)
    def _():
        f_s[...] = part

    @pl.when(j > 0)
    def _():
        f_s[...] = f_s[...] + part

    @pl.when(j == pl.num_programs(0) - 1)
    def _():
        y = _ffn_tail(rows_major(x_ref), f_s[...], rows_major(p_ref), g_post_ref[...], g_ple_in_ref[...],
                      w_pg_ref, w_ple_ref, g_ple_post_ref[...])
        for i in range(seq_len):
            y_ref[:, i, :] = y[i * nseq:(i + 1) * nseq, :]


def _cast_kernel(*refs):
    n = (len(refs) - 3) // 2
    w_in_t_ref, srcs = refs[0], refs[1:1 + n]
    ga_ref, sg_ref, dsts = refs[1 + n], refs[2 + n], refs[3 + n:]

    glr0 = IN_COLS_GLA
    u0 = glr0 + GLA_GATE_RANK
    ga_ref[...] = w_in_t_ref[:glr0, :].T.astype(BF16)
    sg_ref[:, :COL_GLR] = w_in_t_ref[u0:u0 + COL_GLR, :].T.astype(BF16)
    tile = w_in_t_ref[glr0:glr0 + LANES, :].T
    lane = lax.broadcasted_iota(jnp.int32, tile.shape, 1)
    sg_ref[:, COL_GLR:] = jnp.where(lane < GLA_GATE_RANK, tile, 0.0).astype(BF16)

    for src, dst in zip(srcs, dsts):
        cols = src.shape[1]
        dst[:, :cols] = src[...].astype(BF16)
        if dst.shape[1] > cols:
            dst[:, cols:] = jnp.zeros((dst.shape[0], dst.shape[1] - cols), BF16)


def _whole(shape):
    n = len(shape)
    return pl.BlockSpec(shape, lambda *_: (0,) * n)


def _full(shape):
    n = len(shape)
    return pl.BlockSpec(shape, lambda *_: (0,) * n, pipeline_mode=pl.Buffered(1))


def _gla_masks(rows, span):
    i = np.arange(rows)
    same = (i[:, None] // span) == (i[None, :] // span)
    causal = same & (i[None, :] <= i[:, None])
    return (jnp.asarray(causal, BF16), jnp.asarray(same, BF16), jnp.asarray(causal, F32))


def kernel(x_prompt, x_sample, state_gla, state_ffn_conv, p_prompt, p_sample, g_mix_pre, w_in, w_gla_gate, b_gla_gate, g_gla_out, g_sgu_ln, b_sgu_ln, w_spatial, b_spatial, w_out, g_mix_post, g_ffn_pre, w_up, w_conv, b_conv, w_down, g_ffn_post, g_ple_in, w_ple_gate, w_ple, g_ple_post):
    depth = x_prompt.ndim - 2
    assert w_in.shape[0] == 1 and depth == 1
    nb, seq, _ = x_prompt.shape
    ns, dseq, _ = x_sample.shape
    assert seq % PROMPT_TILE == 0 and PROMPT_TILE % GMLP_CHUNK == 0 and seq % FFN_PROMPT_TILE == 0
    assert dseq == 4 and (ns * dseq) % SAMPLE_TILE == 0 and SAMPLE_TILE % GMLP_CHUNK == 0

    w_gate_p = jnp.concatenate(
        [w_gla_gate[0], jnp.zeros((LANES - GLA_GATE_RANK, GLA_KEY), w_gla_gate.dtype)], axis=0).astype(BF16)
    row = lambda a: a.reshape(1, -1)
    g_gla_t = jnp.tile(g_gla_out[0], GLA_HEADS).reshape(1, -1)
    n_cast = CAST_STEPS
    in_cols = w_in.shape[-1]
    assert in_cols == IN_COLS_GLA + GLA_GATE_RANK + COL_GLR
    cast_src = (w_out[0], w_up[0], w_down[0], w_ple_gate[0], w_ple[0])
    cast_cols_in = (D_MODEL, 2 * D_FF, D_MODEL, D_MODEL, D_MODEL)
    cast_cols_out = (D_MODEL + LANES, 2 * D_FF, D_MODEL + LANES, D_MODEL + LANES, D_MODEL + LANES)
    cast_rows = tuple(w.shape[0] for w in cast_src)
    assert all(r % (n_cast * 2 * SUBLANES) == 0 for r in cast_rows + (D_MODEL,))
    slab = lambda r, c: pl.BlockSpec((r // n_cast, c), lambda i: (i, 0))
    w_ga, w_sg, w_out_b, w_up_b, w_down_b, w_pg_b, w_ple_b = pl.pallas_call(
        _cast_kernel,
        grid=(n_cast,),
        in_specs=[pl.BlockSpec((None, in_cols, D_MODEL // n_cast), lambda i: (0, 0, i))]
        + [slab(r, c) for r, c in zip(cast_rows, cast_cols_in)],
        out_specs=[slab(D_MODEL, IN_COLS_GLA), slab(D_MODEL, IN_COLS_SGU)]
        + [slab(r, c) for r, c in zip(cast_rows, cast_cols_out)],
        out_shape=[jax.ShapeDtypeStruct((D_MODEL, IN_COLS_GLA), BF16),
                   jax.ShapeDtypeStruct((D_MODEL, IN_COLS_SGU), BF16)]
        + [jax.ShapeDtypeStruct((r, c), BF16) for r, c in zip(cast_rows, cast_cols_out)],
        compiler_params=pltpu.CompilerParams(
            dimension_semantics=("arbitrary",), vmem_limit_bytes=VMEM_LIMIT_BYTES),
        name="weight_cast",
    )(jnp.swapaxes(w_in, 1, 2), *cast_src)

    def spatial(c):
        i = np.arange(GMLP_CHUNK)
        keep = ((i[:, None] // c) == (i[None, :] // c)) & (i[None, :] <= i[:, None])
        e = jnp.asarray((i[:, None] % c) == np.arange(c)[None, :], w_spatial.dtype)
        hi = lax.Precision.HIGHEST
        wbd = jnp.einsum('ia,hab,jb->hij', e, w_spatial[0][:, :c, :c], e, precision=hi)
        wbd = wbd * jnp.asarray(keep, w_spatial.dtype)
        bias = jnp.einsum('ia,ha->ih', e, b_spatial[0][:, :c], precision=hi)
        return wbd.astype(BF16), jnp.repeat(bias, GMLP_DC, axis=1)

    mixer_weights = lambda wsp, bsp, masks: (
        row(g_mix_pre), w_ga, w_sg, w_gate_p, row(b_gla_gate), g_gla_t, row(g_sgu_ln), row(b_sgu_ln),
        wsp, bsp, w_out_b, row(g_mix_post)) + masks

    def mixer_weight_specs(rows):
        return [
            _full((1, D_MODEL)), _full((D_MODEL, IN_COLS_GLA)), _full((D_MODEL, IN_COLS_SGU)),
            _full((LANES, GLA_KEY)),
            _full((1, GLA_KEY)), _full((1, GLA_WIDTH)), _full((1, GMLP_WIDTH)), _full((1, GMLP_WIDTH)),
            _full((GMLP_HEADS, GMLP_CHUNK, GMLP_CHUNK)), _full((GMLP_CHUNK, GMLP_WIDTH)),
            _full((D_MODEL, D_MODEL + LANES)), _full((1, D_MODEL)),
            _full((rows, rows)), _full((rows, rows)), _full((rows, rows))]

    tl = PROMPT_TILE
    wsp_p, bsp_p = spatial(GMLP_CHUNK)
    npar = MIXER_SEQS_PER_STEP
    assert nb % npar == 0
    x1_p, gla_p = pl.pallas_call(
        functools.partial(_mixer_kernel, rows=tl, sample=False, n_par=npar),
        grid=(nb // npar, seq // tl),
        in_specs=[pl.BlockSpec((npar, None, tl, D_MODEL), lambda b, t: (0, b, t, 0))] + mixer_weight_specs(tl),
        out_specs=[pl.BlockSpec((npar, None, tl, D_MODEL), lambda b, t: (0, b, t, 0)),
                   pl.BlockSpec((npar, None, GLA_HEADS, GLA_DK, GLA_DV), lambda b, t: (0, b, 0, 0, 0))],
        out_shape=[jax.ShapeDtypeStruct((npar, nb // npar, seq, D_MODEL), F32),
                   jax.ShapeDtypeStruct((npar, nb // npar, GLA_HEADS, GLA_DK, GLA_DV), F32)],
        scratch_shapes=[pltpu.VMEM((npar, tl, GLA_WIDTH), F32),
                        pltpu.VMEM((npar, HEAD_PAIRS, 2 * GLA_DV, 2 * GLA_DK), F32)],
        compiler_params=pltpu.CompilerParams(
            dimension_semantics=("arbitrary", "arbitrary"), vmem_limit_bytes=VMEM_LIMIT_BYTES),
        name="mixer_prompt",
    )(x_prompt.reshape(npar, nb // npar, seq, D_MODEL),
      *mixer_weights(wsp_p, bsp_p, _gla_masks(tl, GLA_CHUNK)))
    x1_p = x1_p.reshape(nb, seq, D_MODEL)
    gla_p = gla_p.reshape(nb, GLA_HEADS, GLA_DK, GLA_DV)

    ts = SAMPLE_TILE
    seqs_per_tile = ts // dseq
    wsp_s, bsp_s = spatial(dseq)
    x1_s, gla_s, vrows_s = pl.pallas_call(
        functools.partial(_mixer_kernel, rows=ts, sample=True),
        grid=(ns * dseq // ts,),
        in_specs=[pl.BlockSpec((ts, D_MODEL), lambda i: (i, 0)),
                  pl.BlockSpec((seqs_per_tile, HEAD_PAIRS, 2 * GLA_DK, GLA_DV), lambda i: (i, 0, 0, 0))]
        + mixer_weight_specs(ts),
        out_specs=[pl.BlockSpec((ts, D_MODEL), lambda i: (i, 0)),
                   pl.BlockSpec((seqs_per_tile, HEAD_PAIRS, 2 * GLA_DK, GLA_DV), lambda i: (i, 0, 0, 0)),
                   pl.BlockSpec((ts, GMLP_WIDTH), lambda i: (i, 0))],
        out_shape=[jax.ShapeDtypeStruct((ns * dseq, D_MODEL), F32),
                   jax.ShapeDtypeStruct((ns, HEAD_PAIRS, 2 * GLA_DK, GLA_DV), F32),
                   jax.ShapeDtypeStruct((ns * dseq, GMLP_WIDTH), F32)],
        scratch_shapes=[pltpu.VMEM((ts, GLA_WIDTH), F32)],
        compiler_params=pltpu.CompilerParams(
            dimension_semantics=("arbitrary",), vmem_limit_bytes=VMEM_LIMIT_BYTES),
        name="mixer_sample",
    )(x_sample.reshape(ns * dseq, D_MODEL),
      state_gla[0].reshape(ns, HEAD_PAIRS, 2 * GLA_DK, GLA_DV),
      *mixer_weights(wsp_s, bsp_s, _gla_masks(ts, dseq)))

    ffn_weights = (row(g_ffn_pre), w_up_b, w_conv[0], row(b_conv), w_down_b, row(g_ffn_post),
                   row(g_ple_in), w_pg_b, w_ple_b, row(g_ple_post))
    ffn_weight_specs = [
        _full((1, D_MODEL)), _full((D_MODEL, 2 * D_FF)), _full((CONV_W, 2 * D_FF)), _full((1, 2 * D_FF)),
        _full((D_FF, D_MODEL + LANES)), _full((1, D_MODEL)), _full((1, D_MODEL)),
        _full((D_MODEL, D_MODEL + LANES)), _full((PLE_DIM, D_MODEL + LANES)), _full((1, D_MODEL))]

    tf = FFN_PROMPT_TILE
    y_p, conv_p = pl.pallas_call(
        functools.partial(_ffn_prompt_kernel, rows=tf),
        grid=(nb, seq // tf),
        in_specs=[pl.BlockSpec((None, tf, D_MODEL), lambda b, t: (b, t, 0)),
                  pl.BlockSpec((None, tf, PLE_DIM), lambda b, t: (b, t, 0))] + ffn_weight_specs,
        out_specs=[pl.BlockSpec((None, tf, D_MODEL), lambda b, t: (b, t, 0)),
                   pl.BlockSpec((None, CONV_W - 1, 2 * D_FF), lambda b, t: (b, 0, 0))],
        out_shape=[jax.ShapeDtypeStruct((nb, seq, D_MODEL), F32),
                   jax.ShapeDtypeStruct((nb, CONV_W - 1, 2 * D_FF), F32)],
        scratch_shapes=[pltpu.VMEM((CONV_W - 1, SUBLANES, 2 * D_FF), F32),
                        pltpu.VMEM((2, 2, CONV_W, tf + 2 * SUBLANES, FF_BLOCK), F32),
                        pltpu.VMEM((tf, D_FF), BF16)],
        compiler_params=pltpu.CompilerParams(
            dimension_semantics=("arbitrary", "arbitrary"), vmem_limit_bytes=VMEM_LIMIT_BYTES),
        name="ffn_prompt",
    )(x1_p, p_prompt[0], *ffn_weights)

    col_g = lambda j: (0, j)
    col_v = lambda j: (0, N_FF_BLOCKS + j)
    state_spec = pl.BlockSpec((ns, CONV_W - 1, 2, FF_BLOCK), lambda j: (0, 0, 0, j))
    y_s, conv_s = pl.pallas_call(
        functools.partial(_ffn_sample_kernel, nseq=ns, seq_len=dseq),
        grid=(N_FF_BLOCKS,),
        in_specs=[_whole((ns, dseq, D_MODEL)), _whole((ns, dseq, PLE_DIM)), state_spec,
                  _full((1, D_MODEL)),
                  pl.BlockSpec((D_MODEL, FF_BLOCK), col_g), pl.BlockSpec((D_MODEL, FF_BLOCK), col_v),
                  pl.BlockSpec((CONV_W, FF_BLOCK), col_g), pl.BlockSpec((CONV_W, FF_BLOCK), col_v),
                  pl.BlockSpec((1, FF_BLOCK), col_g), pl.BlockSpec((1, FF_BLOCK), col_v),
                  pl.BlockSpec((FF_BLOCK, D_MODEL + LANES), lambda j: (j, 0)),
                  _full((1, D_MODEL)), _full((1, D_MODEL)),
                  _full((D_MODEL, D_MODEL + LANES)), _full((PLE_DIM, D_MODEL + LANES)), _full((1, D_MODEL))],
        out_specs=[_whole((ns, dseq, D_MODEL)), state_spec],
        out_shape=[jax.ShapeDtypeStruct((ns, dseq, D_MODEL), F32),
                   jax.ShapeDtypeStruct((ns, CONV_W - 1, 2, D_FF), F32)],
        scratch_shapes=[pltpu.VMEM((ns * dseq, D_MODEL), BF16), pltpu.VMEM((ns * dseq, D_MODEL), F32)],
        compiler_params=pltpu.CompilerParams(
            dimension_semantics=("arbitrary",), vmem_limit_bytes=VMEM_LIMIT_BYTES),
        name="ffn_sample",
    )(x1_s.reshape(ns, dseq, D_MODEL), p_sample[0],
      state_ffn_conv[0].reshape(ns, CONV_W - 1, 2, D_FF),
      row(g_ffn_pre), w_up_b, w_up_b, w_conv[0], w_conv[0], row(b_conv), row(b_conv), w_down_b,
      row(g_ffn_post), row(g_ple_in), w_pg_b, w_ple_b, row(g_ple_post))
    conv_s = conv_s.reshape(ns, CONV_W - 1, 2 * D_FF)

    return (y_p,
            y_s,
            gla_p[None],
            gla_s.reshape(1, ns, GLA_HEADS, GLA_DK, GLA_DV),
            conv_p[None],
            conv_s[None],
            vrows_s.reshape(1, ns, dseq, GMLP_WIDTH))
```

```python
import functools

import numpy as np
import jax
import jax.numpy as jnp
from jax import lax
from jax.experimental import pallas as pl
from jax.experimental.pallas import tpu as pltpu

D_MODEL = 1024
GLA_HEADS = 4
GLA_DK = 64
GLA_DV = 128
GLA_KEY = GLA_HEADS * GLA_DK
GLA_WIDTH = GLA_HEADS * GLA_DV
GMLP_HEADS = 4
GMLP_DC = 128
GMLP_WIDTH = GMLP_HEADS * GMLP_DC
GLA_GATE_RANK = 16
GLA_TAU = 16.0
GLA_CHUNK = 64
GMLP_CHUNK = 128
D_FF = 2816
CONV_W = 3
PLE_DIM = 256
EPS = 1e-6

LANES = 128
SUBLANES = 8
MXU_N = 256
HEAD_PAIRS = GLA_HEADS // 2

COL_QK = 0
COL_V = COL_QK + 2 * GLA_KEY
COL_R = COL_V + GLA_WIDTH
IN_COLS_GLA = COL_R + GLA_WIDTH
COL_U = 0
COL_VS = COL_U + GMLP_WIDTH
COL_GLR = COL_VS + GMLP_WIDTH
IN_COLS_SGU = COL_GLR + LANES

FF_BLOCK = 256
N_FF_BLOCKS = D_FF // FF_BLOCK
FF_GROUP = 4

PROMPT_TILE = 256
MIXER_SEQS_PER_STEP = 4
FFN_PROMPT_TILE = 256
SAMPLE_TILE = 128
CAST_STEPS = 8

VMEM_LIMIT_BYTES = 56 * 1024 * 1024

BF16 = jnp.bfloat16
F32 = jnp.float32


def _dot(a, b):
    return jnp.dot(a, b, preferred_element_type=F32)


def _dot_wide(a, w_ref, rows=slice(None)):
    return jnp.concatenate(
        [_dot(a, w_ref[rows, c:c + MXU_N]) for c in range(0, D_MODEL, MXU_N)], axis=-1)


def _dot_nt(a, b):
    return lax.dot_general(a, b, (((1,), (1,)), ((), ())), preferred_element_type=F32)


def _dot_tn(a, b):
    return lax.dot_general(a, b, (((0,), (0,)), ((), ())), preferred_element_type=F32)


def _rms(x, g):
    return x * lax.rsqrt(jnp.mean(x * x, axis=-1, keepdims=True) + EPS) * g


def _gelu_gate(x):
    c = -2.0 * np.sqrt(2.0 / np.pi) * np.log2(np.e)
    a1 = np.float32(c)
    a3 = np.float32(c * 0.044715)
    return 1.0 / (1.0 + jnp.exp2(x * (a3 * (x * x) + a1)))


def _gelu(x):
    return x * _gelu_gate(x)


def _sigmoid(x):
    return 1.0 / (1.0 + jnp.exp(-x))


def _log_sigmoid(x):
    return jnp.minimum(x, 0.0) - jnp.log1p(jnp.exp(-jnp.abs(x)))


def _split_bf16(x):
    hi = x.astype(BF16)
    lo = (x - hi.astype(F32)).astype(BF16)
    return hi, lo


def _mixer_kernel(*refs, rows, sample, n_par=1):
    if sample:
        (x_ref, sin_ref, g_pre_ref, w_ga_ref, w_sg_ref, w_gate_ref, b_gate_ref, g_gla_ref, g_ln_ref, b_ln_ref,
         w_sp_ref, b_sp_ref, w_out_ref, g_post_ref, cum_ref, tot_ref, causal_ref,
         y_ref, sout_ref, vn_ref, o_s) = refs
        xs, ys, os_ = [x_ref], [y_ref], [o_s]
    else:
        (x_ref, g_pre_ref, w_ga_ref, w_sg_ref, w_gate_ref, b_gate_ref, g_gla_ref, g_ln_ref, b_ln_ref,
         w_sp_ref, b_sp_ref, w_out_ref, g_post_ref, cum_ref, tot_ref, causal_ref,
         y_ref, sout_ref, o_s, st_s) = refs
        t = pl.program_id(1)

        @pl.when(t == 0)
        def _():
            st_s[...] = jnp.zeros_like(st_s)

        xs = [x_ref.at[i] for i in range(n_par)]
        ys = [y_ref.at[i] for i in range(n_par)]
        os_ = [o_s.at[i] for i in range(n_par)]

    lane = lax.broadcasted_iota(jnp.int32, (rows, LANES), 1)
    head_lane_masks = (lane < GLA_DK, lane >= GLA_DK)
    tiles = [dict() for _ in range(n_par)]

    def stage_norm(i):
        d = tiles[i]
        d['x'] = xs[i][...]
        d['h'] = _rms(d['x'], g_pre_ref[...]).astype(BF16)

    def stage_project(i):
        d = tiles[i]
        h = d['h']
        glr = _dot(h, w_sg_ref[:, COL_GLR:IN_COLS_SGU])
        d['logit'] = _dot(glr.astype(BF16), w_gate_ref[...]) + b_gate_ref[...]
        d['u_pre'] = _dot(h, w_sg_ref[:, COL_U:COL_VS])
        d['vs_pre'] = _dot(h, w_sg_ref[:, COL_VS:COL_GLR])
        d['qk'] = _dot(h, w_ga_ref[:, COL_QK:COL_V])
        d['v'] = _dot(h, w_ga_ref[:, COL_V:COL_R])
        d['r'] = _dot(h, w_ga_ref[:, COL_R:IN_COLS_GLA])

    def stage_gate_sgu(i):
        d = tiles[i]
        d['g'] = _split_bf16(_log_sigmoid(d.pop('logit')) * (1.0 / GLA_TAU))
        d['u'] = _gelu(d.pop('u_pre'))
        vs = _gelu(d.pop('vs_pre'))
        vn_parts = []
        for j in range(GMLP_HEADS):
            vh = vs[:, j * GMLP_DC:(j + 1) * GMLP_DC]
            mu = jnp.mean(vh, axis=-1, keepdims=True)
            c = vh - mu
            var = jnp.mean(c * c, axis=-1, keepdims=True)
            vn_parts.append(c * lax.rsqrt(var + EPS))
        vn = jnp.concatenate(vn_parts, axis=-1) * g_ln_ref[...] + b_ln_ref[...]
        if sample:
            vn_ref[...] = vn
        d['vn_bf'] = vn.astype(BF16)

    def stage_mix_sgu(i):
        d = tiles[i]
        g_hi, g_lo = d.pop('g')
        cum = cum_ref[...]
        tot = tot_ref[...]
        d['b'] = _dot(cum, g_hi) + _dot(cum, g_lo)
        d['bl'] = _dot(tot, g_hi) + _dot(tot, g_lo)
        vn_bf = d.pop('vn_bf')
        mixed_parts = []
        for cch in range(rows // GMLP_CHUNK):
            crows = slice(cch * GMLP_CHUNK, (cch + 1) * GMLP_CHUNK)
            mixed_parts.append(jnp.concatenate(
                [_dot(w_sp_ref[j], vn_bf[crows, j * GMLP_DC:(j + 1) * GMLP_DC]) for j in range(GMLP_HEADS)],
                axis=-1) + b_sp_ref[...])
        so = d.pop('u') * jnp.concatenate(mixed_parts, axis=0)
        d['mix'] = _dot_wide(so.astype(BF16), w_out_ref, slice(GLA_WIDTH, GLA_WIDTH + GMLP_WIDTH))

    def stage_decay(i):
        d = tiles[i]
        qk = d.pop('qk')
        q = qk[:, :GLA_KEY] * (GLA_DK ** -0.5)
        k = qk[:, GLA_KEY:]
        b, bl = d.pop('b'), d['bl']
        d['v_bf'] = d['v'].astype(BF16)
        d['qb'] = q * jnp.exp(b)
        d['kn'] = k * jnp.exp(-b)
        d['kd'] = k * jnp.exp(bl - b)

    def stage_intra(i):
        d = tiles[i]
        qb, kn, kd, v, v_bf, bl = d.pop('qb'), d.pop('kn'), d.pop('kd'), d.pop('v'), d['v_bf'], d['bl']
        causal = causal_ref[...]
        d['pairs'] = []
        for p in range(HEAD_PAIRS):
            ls = slice(p * LANES, (p + 1) * LANES)
            qbp = qb[:, ls]
            knp = kn[:, ls].astype(BF16)
            kdp = kd[:, ls]
            o_intra = []
            for hh in range(2):
                hd = 2 * p + hh
                qm = jnp.where(head_lane_masks[hh], qbp, 0.0).astype(BF16)
                s = _dot_nt(qm, knp) * causal
                o_intra.append(_dot(s.astype(BF16), v_bf[:, hd * GLA_DV:(hd + 1) * GLA_DV]))

            if not sample:
                st_shape = (2 * GLA_DV, 2 * GLA_DK)
                ri = lax.broadcasted_iota(jnp.int32, st_shape, 0)
                ci = lax.broadcasted_iota(jnp.int32, st_shape, 1)
                same_head = (ri < GLA_DV) == (ci < GLA_DK)
                kdp_bf = kdp.astype(BF16)
                upd = []
                for c in range(rows // GLA_CHUNK):
                    rws = slice(c * GLA_CHUNK, (c + 1) * GLA_CHUNK)
                    inc = _dot_tn(v_bf[rws, 2 * p * GLA_DV:(2 * p + 2) * GLA_DV], kdp_bf[rws, :])
                    upd.append(jnp.where(same_head, inc, 0.0))
                d['pairs'].append((ls, qbp.astype(BF16), o_intra, upd))
            else:
                seq_len = 4
                seqs_per_grp = SUBLANES // seq_len
                kdp_t = kdp.T.astype(BF16)
                dec_t = jnp.exp(bl[:, ls]).T
                sub = lax.broadcasted_iota(jnp.int32, (SUBLANES, LANES), 0)
                crow = lax.broadcasted_iota(jnp.int32, (rows, LANES), 0)
                for grp in range(rows // SUBLANES):
                    g0 = grp * SUBLANES
                    qg = qbp[g0:g0 + SUBLANES, :]
                    o_acc = [o_intra[hh][g0:g0 + SUBLANES, :] for hh in range(2)]
                    for j in range(seqs_per_grp):
                        sidx = grp * seqs_per_grp + j
                        s_pair = sin_ref[sidx, p]
                        s_bf = s_pair.astype(BF16)
                        in_seq = (sub >= j * seq_len) & (sub < (j + 1) * seq_len)
                        in_seq_c = (crow >= g0 + j * seq_len) & (crow < g0 + (j + 1) * seq_len)
                        new_rows = []
                        for hh in range(2):
                            hd = 2 * p + hh
                            qsel = jnp.where(in_seq & head_lane_masks[hh][:SUBLANES], qg, 0.0).astype(BF16)
                            o_acc[hh] = o_acc[hh] + _dot(qsel, s_bf)
                            vsel = jnp.where(in_seq_c, v[:, hd * GLA_DV:(hd + 1) * GLA_DV], 0.0).astype(BF16)
                            new_rows.append(_dot(kdp_t[hh * GLA_DK:(hh + 1) * GLA_DK, :], vsel))
                        col = g0 + j * seq_len
                        sout_ref[sidx, p] = s_pair * dec_t[:, col:col + 1] + jnp.concatenate(new_rows, axis=0)
                    for hh in range(2):
                        hd = 2 * p + hh
                        os_[i][g0:g0 + SUBLANES, hd * GLA_DV:(hd + 1) * GLA_DV] = o_acc[hh]

    def stage_state(i):
        if sample:
            return
        d = tiles[i]
        bl = d.pop('bl')
        st = [st_s[i, p] for p in range(HEAD_PAIRS)]
        for c in range(rows // GLA_CHUNK):
            rws = slice(c * GLA_CHUNK, (c + 1) * GLA_CHUNK)
            for p, (ls, qbp_bf, o_intra, upd) in enumerate(d['pairs']):
                o_inter = _dot_nt(qbp_bf[rws, :], st[p].astype(BF16))
                for hh in range(2):
                    hd = 2 * p + hh
                    os_[i][rws, hd * GLA_DV:(hd + 1) * GLA_DV] = (
                        o_intra[hh][rws, :] + o_inter[:, hh * GLA_DV:(hh + 1) * GLA_DV])
                dec = jnp.exp(bl[c * GLA_CHUNK:c * GLA_CHUNK + 1, ls])
                st[p] = st[p] * dec + upd[c]
        for p in range(HEAD_PAIRS):
            st_s[i, p] = st[p]
        d.pop('pairs')

    def stage_out(i):
        d = tiles[i]
        r = d.pop('r')
        out_gate = r * _sigmoid(r)
        mix, x = d.pop('mix'), d.pop('x')
        half = rows // 2
        for hr in (slice(0, half), slice(half, rows)):
            o = os_[i][hr, :]
            og = jnp.concatenate(
                [_rms(o[:, j * GLA_DV:(j + 1) * GLA_DV], 1.0) for j in range(GLA_HEADS)], axis=-1)
            og = og * g_gla_ref[...] * out_gate[hr, :]
            mix_h = mix[hr, :] + _dot_wide(og.astype(BF16), w_out_ref, slice(0, GLA_WIDTH))
            ys[i][hr, :] = x[hr, :] + _rms(mix_h, g_post_ref[...])

    stages = (stage_norm, stage_project, stage_gate_sgu, stage_mix_sgu, stage_decay, stage_intra,
              stage_state, stage_out)
    for stage in stages:
        for i in range(n_par):
            stage(i)

    if not sample:
        @pl.when(t == pl.num_programs(1) - 1)
        def _():
            for i in range(n_par):
                for p in range(HEAD_PAIRS):
                    st = st_s[i, p]
                    sout_ref[i, 2 * p] = st[:GLA_DV, :].T[:GLA_DK, :]
                    sout_ref[i, 2 * p + 1] = st[GLA_DV:, :].T[GLA_DK:, :]


def _ffn_tail(x1, f, p, g_ffn_post, g_ple_in, w_pg_ref, w_ple_ref, g_ple_post):
    x2 = x1 + _rms(f, g_ffn_post)
    hp = _rms(x2, g_ple_in).astype(BF16)
    gate = _sigmoid(_dot_wide(hp, w_pg_ref))
    pe = _dot_wide(p.astype(BF16), w_ple_ref)
    return x2 + _rms(pe * gate, g_ple_post)


def _ff_groups():
    return [range(g, min(g + FF_GROUP, N_FF_BLOCKS)) for g in range(0, N_FF_BLOCKS, FF_GROUP)]


def _ffn_prompt_kernel(x_ref, p_ref, g_pre_ref, w_up_ref, w_conv_ref, b_conv_ref, w_down_ref,
                       g_post_ref, g_ple_in_ref, w_pg_ref, w_ple_ref, g_ple_post_ref,
                       y_ref, cs_ref,
                       carry_s, up_s, act_s, *, rows):
    t = pl.program_id(1)
    hdr = SUBLANES

    @pl.when(t == 0)
    def _():
        carry_s[...] = jnp.zeros_like(carry_s)
        up_s[:, :, :, hdr + rows:, :] = jnp.zeros((2, 2, CONV_W, hdr, FF_BLOCK), F32)

    x1 = x_ref[...]
    h = _rms(x1, g_pre_ref[...]).astype(BF16)

    def half_cols(j, half):
        c0 = half * D_FF + j * FF_BLOCK
        return slice(c0, c0 + FF_BLOCK)

    def project(j):
        for half in range(2):
            cols = half_cols(j, half)
            up = _dot(h, w_up_ref[:, cols])
            for s in range(CONV_W):
                buf = up_s.at[j % 2, half, s]
                if s:
                    buf[hdr:2 * hdr, :] = carry_s[s - 1, :, cols]
                buf[hdr + s:hdr + s + rows, :] = up
                if s:
                    carry_s[s - 1, :, cols] = buf[hdr + rows:2 * hdr + rows, :]

    def conv(j, half):
        cols = half_cols(j, half)
        w = w_conv_ref[:, cols]
        taps = [up_s[j % 2, half, s, hdr:hdr + rows, :] for s in range(CONV_W)]
        return (b_conv_ref[:, cols] + taps[2] * w[0:1, :] + taps[1] * w[1:2, :] + taps[0] * w[2:3, :])

    def down(j):
        k_rows = slice(j * FF_BLOCK, (j + 1) * FF_BLOCK)
        return _dot_wide(act_s[:, k_rows], w_down_ref, k_rows)

    project(0)
    f = None
    for j in range(N_FF_BLOCKS):
        if j + 1 < N_FF_BLOCKS:
            project(j + 1)
        if j == 1:
            f = down(0)
        elif j > 1:
            f = f + down(j - 1)
        gate = conv(j, 0)
        act_s[:, j * FF_BLOCK:(j + 1) * FF_BLOCK] = (
            (gate * conv(j, 1)) * _gelu_gate(gate)).astype(BF16)
    f = f + down(N_FF_BLOCKS - 1)

    y_ref[...] = _ffn_tail(x1, f, p_ref[...], g_post_ref[...], g_ple_in_ref[...], w_pg_ref,
                           w_ple_ref, g_ple_post_ref[...])

    @pl.when(t == pl.num_programs(1) - 1)
    def _():
        cs_ref[...] = carry_s[CONV_W - 2, 0:CONV_W - 1, :]


def _ffn_sample_kernel(x_ref, p_ref, cin_ref, g_pre_ref, w_up_ref, w_conv_ref, b_conv_ref, w_down_ref,
                       g_post_ref, g_ple_in_ref, w_pg_ref, w_ple_ref, g_ple_post_ref,
                       y_ref, cs_ref, act_s, *, nseq, seq_len):
    x1 = jnp.concatenate([x_ref[:, j, :] for j in range(seq_len)], axis=0)
    p = jnp.concatenate([p_ref[:, j, :] for j in range(seq_len)], axis=0)
    h = _rms(x1, g_pre_ref[...]).astype(BF16)

    def conv_half(col0):
        cols = slice(col0, col0 + FF_BLOCK)
        up = _dot(h, w_up_ref[:, cols])
        blocks = [cin_ref[:, i, cols] for i in range(CONV_W - 1)]
        blocks += [up[j * nseq:(j + 1) * nseq, :] for j in range(seq_len)]
        w = w_conv_ref[:, cols]
        b = b_conv_ref[:, cols]
        conv = jnp.concatenate(
            [b + blocks[j] * w[0:1, :] + blocks[j + 1] * w[1:2, :] + blocks[j + 2] * w[2:3, :]
             for j in range(seq_len)], axis=0)
        for i in range(CONV_W - 1):
            cs_ref[:, i, cols] = blocks[seq_len + i]
        return conv

    f = None
    for grp in _ff_groups():
        for j in grp:
            gate = conv_half(j * FF_BLOCK)
            val = conv_half(D_FF + j * FF_BLOCK)
            act_s[:, j * FF_BLOCK:(j + 1) * FF_BLOCK] = ((gate * val) * _gelu_gate(gate)).astype(BF16)
        k_rows = slice(grp[0] * FF_BLOCK, (grp[-1] + 1) * FF_BLOCK)
        part = _dot_wide(act_s[:, k_rows], w_down_ref, k_rows)
        f = part if f is None else f + part

    y = _ffn_tail(x1, f, p, g_post_ref[...], g_ple_in_ref[...], w_pg_ref, w_ple_ref,
                  g_ple_post_ref[...])
    for j in range(seq_len):
        y_ref[:, j, :] = y[j * nseq:(j + 1) * nseq, :]


def _cast_kernel(*refs):
    n = (len(refs) - 3) // 2
    w_in_t_ref, srcs = refs[0], refs[1:1 + n]
    ga_ref, sg_ref, dsts = refs[1 + n], refs[2 + n], refs[3 + n:]

    glr0 = IN_COLS_GLA
    u0 = glr0 + GLA_GATE_RANK
    ga_ref[...] = w_in_t_ref[:glr0, :].T.astype(BF16)
    sg_ref[:, :COL_GLR] = w_in_t_ref[u0:u0 + COL_GLR, :].T.astype(BF16)
    tile = w_in_t_ref[glr0:glr0 + LANES, :].T
    lane = lax.broadcasted_iota(jnp.int32, tile.shape, 1)
    sg_ref[:, COL_GLR:] = jnp.where(lane < GLA_GATE_RANK, tile, 0.0).astype(BF16)

    for src, dst in zip(srcs, dsts):
        cols = src.shape[1]
        dst[:, :cols] = src[...].astype(BF16)
        if dst.shape[1] > cols:
            dst[:, cols:] = jnp.zeros((dst.shape[0], dst.shape[1] - cols), BF16)


def _whole(shape):
    n = len(shape)
    return pl.BlockSpec(shape, lambda *_: (0,) * n)


def _full(shape):
    n = len(shape)
    return pl.BlockSpec(shape, lambda *_: (0,) * n, pipeline_mode=pl.Buffered(1))


def _gla_masks(rows, span):
    i = np.arange(rows)
    same = (i[:, None] // span) == (i[None, :] // span)
    causal = same & (i[None, :] <= i[:, None])
    return (jnp.asarray(causal, BF16), jnp.asarray(same, BF16), jnp.asarray(causal, F32))


def kernel(x_prompt, x_sample, state_gla, state_ffn_conv, p_prompt, p_sample, g_mix_pre, w_in, w_gla_gate, b_gla_gate, g_gla_out, g_sgu_ln, b_sgu_ln, w_spatial, b_spatial, w_out, g_mix_post, g_ffn_pre, w_up, w_conv, b_conv, w_down, g_ffn_post, g_ple_in, w_ple_gate, w_ple, g_ple_post):
    depth = x_prompt.ndim - 2
    assert w_in.shape[0] == 1 and depth == 1
    nb, seq, _ = x_prompt.shape
    ns, dseq, _ = x_sample.shape
    assert seq % PROMPT_TILE == 0 and PROMPT_TILE % GMLP_CHUNK == 0 and seq % FFN_PROMPT_TILE == 0
    assert dseq == 4 and (ns * dseq) % SAMPLE_TILE == 0 and SAMPLE_TILE % GMLP_CHUNK == 0

    w_gate_p = jnp.concatenate(
        [w_gla_gate[0], jnp.zeros((LANES - GLA_GATE_RANK, GLA_KEY), w_gla_gate.dtype)], axis=0).astype(BF16)
    row = lambda a: a.reshape(1, -1)
    g_gla_t = jnp.tile(g_gla_out[0], GLA_HEADS).reshape(1, -1)
    n_cast = CAST_STEPS
    in_cols = w_in.shape[-1]
    assert in_cols == IN_COLS_GLA + GLA_GATE_RANK + COL_GLR
    cast_src = (w_out[0], w_up[0], w_down[0], w_ple_gate[0], w_ple[0])
    cast_cols_in = (D_MODEL, 2 * D_FF, D_MODEL, D_MODEL, D_MODEL)
    cast_cols_out = (D_MODEL + LANES, 2 * D_FF, D_MODEL + LANES, D_MODEL + LANES, D_MODEL + LANES)
    cast_rows = tuple(w.shape[0] for w in cast_src)
    assert all(r % (n_cast * 2 * SUBLANES) == 0 for r in cast_rows + (D_MODEL,))
    slab = lambda r, c: pl.BlockSpec((r // n_cast, c), lambda i: (i, 0))
    w_ga, w_sg, w_out_b, w_up_b, w_down_b, w_pg_b, w_ple_b = pl.pallas_call(
        _cast_kernel,
        grid=(n_cast,),
        in_specs=[pl.BlockSpec((None, in_cols, D_MODEL // n_cast), lambda i: (0, 0, i))]
        + [slab(r, c) for r, c in zip(cast_rows, cast_cols_in)],
        out_specs=[slab(D_MODEL, IN_COLS_GLA), slab(D_MODEL, IN_COLS_SGU)]
        + [slab(r, c) for r, c in zip(cast_rows, cast_cols_out)],
        out_shape=[jax.ShapeDtypeStruct((D_MODEL, IN_COLS_GLA), BF16),
                   jax.ShapeDtypeStruct((D_MODEL, IN_COLS_SGU), BF16)]
        + [jax.ShapeDtypeStruct((r, c), BF16) for r, c in zip(cast_rows, cast_cols_out)],
        compiler_params=pltpu.CompilerParams(
            dimension_semantics=("arbitrary",), vmem_limit_bytes=VMEM_LIMIT_BYTES),
        name="weight_cast",
    )(jnp.swapaxes(w_in, 1, 2), *cast_src)

    def spatial(c):
        i = np.arange(GMLP_CHUNK)
        keep = ((i[:, None] // c) == (i[None, :] // c)) & (i[None, :] <= i[:, None])
        e = jnp.asarray((i[:, None] % c) == np.arange(c)[None, :], w_spatial.dtype)
        hi = lax.Precision.HIGHEST
        wbd = jnp.einsum('ia,hab,jb->hij', e, w_spatial[0][:, :c, :c], e, precision=hi)
        wbd = wbd * jnp.asarray(keep, w_spatial.dtype)
        bias = jnp.einsum('ia,ha->ih', e, b_spatial[0][:, :c], precision=hi)
        return wbd.astype(BF16), jnp.repeat(bias, GMLP_DC, axis=1)

    mixer_weights = lambda wsp, bsp, masks: (
        row(g_mix_pre), w_ga, w_sg, w_gate_p, row(b_gla_gate), g_gla_t, row(g_sgu_ln), row(b_sgu_ln),
        wsp, bsp, w_out_b, row(g_mix_post)) + masks

    def mixer_weight_specs(rows):
        return [
            _full((1, D_MODEL)), _full((D_MODEL, IN_COLS_GLA)), _full((D_MODEL, IN_COLS_SGU)),
            _full((LANES, GLA_KEY)),
            _full((1, GLA_KEY)), _full((1, GLA_WIDTH)), _full((1, GMLP_WIDTH)), _full((1, GMLP_WIDTH)),
            _full((GMLP_HEADS, GMLP_CHUNK, GMLP_CHUNK)), _full((GMLP_CHUNK, GMLP_WIDTH)),
            _full((D_MODEL, D_MODEL + LANES)), _full((1, D_MODEL)),
            _full((rows, rows)), _full((rows, rows)), _full((rows, rows))]

    tl = PROMPT_TILE
    wsp_p, bsp_p = spatial(GMLP_CHUNK)
    npar = MIXER_SEQS_PER_STEP
    assert nb % npar == 0
    x1_p, gla_p = pl.pallas_call(
        functools.partial(_mixer_kernel, rows=tl, sample=False, n_par=npar),
        grid=(nb // npar, seq // tl),
        in_specs=[pl.BlockSpec((npar, None, tl, D_MODEL), lambda b, t: (0, b, t, 0))] + mixer_weight_specs(tl),
        out_specs=[pl.BlockSpec((npar, None, tl, D_MODEL), lambda b, t: (0, b, t, 0)),
                   pl.BlockSpec((npar, None, GLA_HEADS, GLA_DK, GLA_DV), lambda b, t: (0, b, 0, 0, 0))],
        out_shape=[jax.ShapeDtypeStruct((npar, nb // npar, seq, D_MODEL), F32),
                   jax.ShapeDtypeStruct((npar, nb // npar, GLA_HEADS, GLA_DK, GLA_DV), F32)],
        scratch_shapes=[pltpu.VMEM((npar, tl, GLA_WIDTH), F32),
                        pltpu.VMEM((npar, HEAD_PAIRS, 2 * GLA_DV, 2 * GLA_DK), F32)],
        compiler_params=pltpu.CompilerParams(
            dimension_semantics=("arbitrary", "arbitrary"), vmem_limit_bytes=VMEM_LIMIT_BYTES),
        name="mixer_prompt",
    )(x_prompt.reshape(npar, nb // npar, seq, D_MODEL),
      *mixer_weights(wsp_p, bsp_p, _gla_masks(tl, GLA_CHUNK)))
    x1_p = x1_p.reshape(nb, seq, D_MODEL)
    gla_p = gla_p.reshape(nb, GLA_HEADS, GLA_DK, GLA_DV)

    ts = SAMPLE_TILE
    seqs_per_tile = ts // dseq
    wsp_s, bsp_s = spatial(dseq)
    x1_s, gla_s, vrows_s = pl.pallas_call(
        functools.partial(_mixer_kernel, rows=ts, sample=True),
        grid=(ns * dseq // ts,),
        in_specs=[pl.BlockSpec((ts, D_MODEL), lambda i: (i, 0)),
                  pl.BlockSpec((seqs_per_tile, HEAD_PAIRS, 2 * GLA_DK, GLA_DV), lambda i: (i, 0, 0, 0))]
        + mixer_weight_specs(ts),
        out_specs=[pl.BlockSpec((ts, D_MODEL), lambda i: (i, 0)),
                   pl.BlockSpec((seqs_per_tile, HEAD_PAIRS, 2 * GLA_DK, GLA_DV), lambda i: (i, 0, 0, 0)),
                   pl.BlockSpec((ts, GMLP_WIDTH), lambda i: (i, 0))],
        out_shape=[jax.ShapeDtypeStruct((ns * dseq, D_MODEL), F32),
                   jax.ShapeDtypeStruct((ns, HEAD_PAIRS, 2 * GLA_DK, GLA_DV), F32),
                   jax.ShapeDtypeStruct((ns * dseq, GMLP_WIDTH), F32)],
        scratch_shapes=[pltpu.VMEM((ts, GLA_WIDTH), F32)],
        compiler_params=pltpu.CompilerParams(
            dimension_semantics=("arbitrary",), vmem_limit_bytes=VMEM_LIMIT_BYTES),
        name="mixer_sample",
    )(x_sample.reshape(ns * dseq, D_MODEL),
      state_gla[0].reshape(ns, HEAD_PAIRS, 2 * GLA_DK, GLA_DV),
      *mixer_weights(wsp_s, bsp_s, _gla_masks(ts, dseq)))

    ffn_weights = (row(g_ffn_pre), w_up_b, w_conv[0], row(b_conv), w_down_b, row(g_ffn_post),
                   row(g_ple_in), w_pg_b, w_ple_b, row(g_ple_post))
    ffn_weight_specs = [
        _full((1, D_MODEL)), _full((D_MODEL, 2 * D_FF)), _full((CONV_W, 2 * D_FF)), _full((1, 2 * D_FF)),
        _full((D_FF, D_MODEL + LANES)), _full((1, D_MODEL)), _full((1, D_MODEL)),
        _full((D_MODEL, D_MODEL + LANES)), _full((PLE_DIM, D_MODEL + LANES)), _full((1, D_MODEL))]

    tf = FFN_PROMPT_TILE
    y_p, conv_p = pl.pallas_call(
        functools.partial(_ffn_prompt_kernel, rows=tf),
        grid=(nb, seq // tf),
        in_specs=[pl.BlockSpec((None, tf, D_MODEL), lambda b, t: (b, t, 0)),
                  pl.BlockSpec((None, tf, PLE_DIM), lambda b, t: (b, t, 0))] + ffn_weight_specs,
        out_specs=[pl.BlockSpec((None, tf, D_MODEL), lambda b, t: (b, t, 0)),
                   pl.BlockSpec((None, CONV_W - 1, 2 * D_FF), lambda b, t: (b, 0, 0))],
        out_shape=[jax.ShapeDtypeStruct((nb, seq, D_MODEL), F32),
                   jax.ShapeDtypeStruct((nb, CONV_W - 1, 2 * D_FF), F32)],
        scratch_shapes=[pltpu.VMEM((CONV_W - 1, SUBLANES, 2 * D_FF), F32),
                        pltpu.VMEM((2, 2, CONV_W, tf + 2 * SUBLANES, FF_BLOCK), F32),
                        pltpu.VMEM((tf, D_FF), BF16)],
        compiler_params=pltpu.CompilerParams(
            dimension_semantics=("arbitrary", "arbitrary"), vmem_limit_bytes=VMEM_LIMIT_BYTES),
        name="ffn_prompt",
    )(x1_p, p_prompt[0], *ffn_weights)

    y_s, conv_s = pl.pallas_call(
        functools.partial(_ffn_sample_kernel, nseq=ns, seq_len=dseq),
        grid=(1,),
        in_specs=[_whole((ns, dseq, D_MODEL)), _whole((ns, dseq, PLE_DIM)),
                  _whole((ns, CONV_W - 1, 2 * D_FF))] + ffn_weight_specs,
        out_specs=[_whole((ns, dseq, D_MODEL)), _whole((ns, CONV_W - 1, 2 * D_FF))],
        out_shape=[jax.ShapeDtypeStruct((ns, dseq, D_MODEL), F32),
                   jax.ShapeDtypeStruct((ns, CONV_W - 1, 2 * D_FF), F32)],
        scratch_shapes=[pltpu.VMEM((ns * dseq, D_FF), BF16)],
        compiler_params=pltpu.CompilerParams(
            dimension_semantics=("arbitrary",), vmem_limit_bytes=VMEM_LIMIT_BYTES),
        name="ffn_sample",
    )(x1_s.reshape(ns, dseq, D_MODEL), p_sample[0], state_ffn_conv[0], *ffn_weights)

    return (y_p,
            y_s,
            gla_p[None],
            gla_s.reshape(1, ns, GLA_HEADS, GLA_DK, GLA_DV),
            conv_p[None],
            conv_s[None],
            vrows_s.reshape(1, ns, dseq, GMLP_WIDTH))
```

```python
import functools

import numpy as np
import jax
import jax.numpy as jnp
from jax import lax
from jax.experimental import pallas as pl
from jax.experimental.pallas import tpu as pltpu

D_MODEL = 1024
GLA_HEADS = 4
GLA_DK = 64
GLA_DV = 128
GLA_KEY = GLA_HEADS * GLA_DK
GLA_WIDTH = GLA_HEADS * GLA_DV
GMLP_HEADS = 4
GMLP_DC = 128
GMLP_WIDTH = GMLP_HEADS * GMLP_DC
GLA_GATE_RANK = 16
GLA_TAU = 16.0
GLA_CHUNK = 64
GMLP_CHUNK = 128
D_FF = 2816
CONV_W = 3
PLE_DIM = 256
EPS = 1e-6

LANES = 128
SUBLANES = 8
MXU_N = 256
HEAD_PAIRS = GLA_HEADS // 2

COL_QK = 0
COL_V = COL_QK + 2 * GLA_KEY
COL_R = COL_V + GLA_WIDTH
IN_COLS_GLA = COL_R + GLA_WIDTH
COL_U = 0
COL_VS = COL_U + GMLP_WIDTH
COL_GLR = COL_VS + GMLP_WIDTH
IN_COLS_SGU = COL_GLR + LANES

FF_BLOCK = 256
N_FF_BLOCKS = D_FF // FF_BLOCK
FF_GROUP = 4

PROMPT_TILE = 256
MIXER_SEQS_PER_STEP = 4
FFN_PROMPT_TILE = 256
FFN_SEQS_PER_STEP = 2
FFN_STAGE_SKEW = 0
SAMPLE_TILE = 128
CAST_STEPS = 8

VMEM_LIMIT_BYTES = 56 * 1024 * 1024

BF16 = jnp.bfloat16
F32 = jnp.float32


def _dot(a, b):
    return jnp.dot(a, b, preferred_element_type=F32)


def _dot_wide(a, w_ref, rows=slice(None)):
    return jnp.concatenate(
        [_dot(a, w_ref[rows, c:c + MXU_N]) for c in range(0, D_MODEL, MXU_N)], axis=-1)


def _dot_nt(a, b):
    return lax.dot_general(a, b, (((1,), (1,)), ((), ())), preferred_element_type=F32)


def _dot_tn(a, b):
    return lax.dot_general(a, b, (((0,), (0,)), ((), ())), preferred_element_type=F32)


def _rms(x, g):
    return x * lax.rsqrt(jnp.mean(x * x, axis=-1, keepdims=True) + EPS) * g


def _gelu_gate(x):
    c = -2.0 * np.sqrt(2.0 / np.pi) * np.log2(np.e)
    a1 = np.float32(c)
    a3 = np.float32(c * 0.044715)
    return 1.0 / (1.0 + jnp.exp2(x * (a3 * (x * x) + a1)))


def _gelu(x):
    return x * _gelu_gate(x)


def _sigmoid(x):
    return 1.0 / (1.0 + jnp.exp(-x))


def _log_sigmoid(x):
    return jnp.minimum(x, 0.0) - jnp.log1p(jnp.exp(-jnp.abs(x)))


def _split_bf16(x):
    hi = x.astype(BF16)
    lo = (x - hi.astype(F32)).astype(BF16)
    return hi, lo


def _mixer_kernel(*refs, rows, sample, n_par=1):
    if sample:
        (x_ref, sin_ref, g_pre_ref, w_ga_ref, w_sg_ref, w_gate_ref, b_gate_ref, g_gla_ref, g_ln_ref, b_ln_ref,
         w_sp_ref, b_sp_ref, w_out_ref, g_post_ref, cum_ref, tot_ref, causal_ref,
         y_ref, sout_ref, vn_ref, o_s) = refs
        xs, ys, os_ = [x_ref], [y_ref], [o_s]
    else:
        (x_ref, g_pre_ref, w_ga_ref, w_sg_ref, w_gate_ref, b_gate_ref, g_gla_ref, g_ln_ref, b_ln_ref,
         w_sp_ref, b_sp_ref, w_out_ref, g_post_ref, cum_ref, tot_ref, causal_ref,
         y_ref, sout_ref, o_s, st_s) = refs
        t = pl.program_id(1)

        @pl.when(t == 0)
        def _():
            st_s[...] = jnp.zeros_like(st_s)

        xs = [x_ref.at[i] for i in range(n_par)]
        ys = [y_ref.at[i] for i in range(n_par)]
        os_ = [o_s.at[i] for i in range(n_par)]

    lane = lax.broadcasted_iota(jnp.int32, (rows, LANES), 1)
    head_lane_masks = (lane < GLA_DK, lane >= GLA_DK)
    tiles = [dict() for _ in range(n_par)]

    def stage_norm(i):
        d = tiles[i]
        d['x'] = xs[i][...]
        d['h'] = _rms(d['x'], g_pre_ref[...]).astype(BF16)

    def stage_project(i):
        d = tiles[i]
        h = d['h']
        glr = _dot(h, w_sg_ref[:, COL_GLR:IN_COLS_SGU])
        d['logit'] = _dot(glr.astype(BF16), w_gate_ref[...]) + b_gate_ref[...]
        d['u_pre'] = _dot(h, w_sg_ref[:, COL_U:COL_VS])
        d['vs_pre'] = _dot(h, w_sg_ref[:, COL_VS:COL_GLR])
        d['qk'] = _dot(h, w_ga_ref[:, COL_QK:COL_V])
        d['v'] = _dot(h, w_ga_ref[:, COL_V:COL_R])
        d['r'] = _dot(h, w_ga_ref[:, COL_R:IN_COLS_GLA])

    def stage_gate_sgu(i):
        d = tiles[i]
        d['g'] = _split_bf16(_log_sigmoid(d.pop('logit')) * (1.0 / GLA_TAU))
        d['u'] = _gelu(d.pop('u_pre'))
        vs = _gelu(d.pop('vs_pre'))
        vn_parts = []
        for j in range(GMLP_HEADS):
            vh = vs[:, j * GMLP_DC:(j + 1) * GMLP_DC]
            mu = jnp.mean(vh, axis=-1, keepdims=True)
            c = vh - mu
            var = jnp.mean(c * c, axis=-1, keepdims=True)
            vn_parts.append(c * lax.rsqrt(var + EPS))
        vn = jnp.concatenate(vn_parts, axis=-1) * g_ln_ref[...] + b_ln_ref[...]
        if sample:
            vn_ref[...] = vn
        d['vn_bf'] = vn.astype(BF16)

    def stage_mix_sgu(i):
        d = tiles[i]
        g_hi, g_lo = d.pop('g')
        cum = cum_ref[...]
        tot = tot_ref[...]
        d['b'] = _dot(cum, g_hi) + _dot(cum, g_lo)
        d['bl'] = _dot(tot, g_hi) + _dot(tot, g_lo)
        vn_bf = d.pop('vn_bf')
        mixed_parts = []
        for cch in range(rows // GMLP_CHUNK):
            crows = slice(cch * GMLP_CHUNK, (cch + 1) * GMLP_CHUNK)
            mixed_parts.append(jnp.concatenate(
                [_dot(w_sp_ref[j], vn_bf[crows, j * GMLP_DC:(j + 1) * GMLP_DC]) for j in range(GMLP_HEADS)],
                axis=-1) + b_sp_ref[...])
        so = d.pop('u') * jnp.concatenate(mixed_parts, axis=0)
        d['mix'] = _dot_wide(so.astype(BF16), w_out_ref, slice(GLA_WIDTH, GLA_WIDTH + GMLP_WIDTH))

    def stage_decay(i):
        d = tiles[i]
        qk = d.pop('qk')
        q = qk[:, :GLA_KEY] * (GLA_DK ** -0.5)
        k = qk[:, GLA_KEY:]
        b, bl = d.pop('b'), d['bl']
        d['v_bf'] = d['v'].astype(BF16)
        d['qb'] = q * jnp.exp(b)
        d['kn'] = k * jnp.exp(-b)
        d['kd'] = k * jnp.exp(bl - b)

    def stage_intra(i):
        d = tiles[i]
        qb, kn, kd, v, v_bf, bl = d.pop('qb'), d.pop('kn'), d.pop('kd'), d.pop('v'), d['v_bf'], d['bl']
        causal = causal_ref[...]
        d['pairs'] = []
        for p in range(HEAD_PAIRS):
            ls = slice(p * LANES, (p + 1) * LANES)
            qbp = qb[:, ls]
            knp = kn[:, ls].astype(BF16)
            kdp = kd[:, ls]
            o_intra = []
            for hh in range(2):
                hd = 2 * p + hh
                qm = jnp.where(head_lane_masks[hh], qbp, 0.0).astype(BF16)
                s = _dot_nt(qm, knp) * causal
                o_intra.append(_dot(s.astype(BF16), v_bf[:, hd * GLA_DV:(hd + 1) * GLA_DV]))

            if not sample:
                st_shape = (2 * GLA_DV, 2 * GLA_DK)
                ri = lax.broadcasted_iota(jnp.int32, st_shape, 0)
                ci = lax.broadcasted_iota(jnp.int32, st_shape, 1)
                same_head = (ri < GLA_DV) == (ci < GLA_DK)
                kdp_bf = kdp.astype(BF16)
                upd = []
                for c in range(rows // GLA_CHUNK):
                    rws = slice(c * GLA_CHUNK, (c + 1) * GLA_CHUNK)
                    inc = _dot_tn(v_bf[rws, 2 * p * GLA_DV:(2 * p + 2) * GLA_DV], kdp_bf[rws, :])
                    upd.append(jnp.where(same_head, inc, 0.0))
                d['pairs'].append((ls, qbp.astype(BF16), o_intra, upd))
            else:
                seq_len = 4
                seqs_per_grp = SUBLANES // seq_len
                kdp_t = kdp.T.astype(BF16)
                dec_t = jnp.exp(bl[:, ls]).T
                sub = lax.broadcasted_iota(jnp.int32, (SUBLANES, LANES), 0)
                crow = lax.broadcasted_iota(jnp.int32, (rows, LANES), 0)
                for grp in range(rows // SUBLANES):
                    g0 = grp * SUBLANES
                    qg = qbp[g0:g0 + SUBLANES, :]
                    o_acc = [o_intra[hh][g0:g0 + SUBLANES, :] for hh in range(2)]
                    for j in range(seqs_per_grp):
                        sidx = grp * seqs_per_grp + j
                        s_pair = sin_ref[sidx, p]
                        s_bf = s_pair.astype(BF16)
                        in_seq = (sub >= j * seq_len) & (sub < (j + 1) * seq_len)
                        in_seq_c = (crow >= g0 + j * seq_len) & (crow < g0 + (j + 1) * seq_len)
                        new_rows = []
                        for hh in range(2):
                            hd = 2 * p + hh
                            qsel = jnp.where(in_seq & head_lane_masks[hh][:SUBLANES], qg, 0.0).astype(BF16)
                            o_acc[hh] = o_acc[hh] + _dot(qsel, s_bf)
                            vsel = jnp.where(in_seq_c, v[:, hd * GLA_DV:(hd + 1) * GLA_DV], 0.0).astype(BF16)
                            new_rows.append(_dot(kdp_t[hh * GLA_DK:(hh + 1) * GLA_DK, :], vsel))
                        col = g0 + j * seq_len
                        sout_ref[sidx, p] = s_pair * dec_t[:, col:col + 1] + jnp.concatenate(new_rows, axis=0)
                    for hh in range(2):
                        hd = 2 * p + hh
                        os_[i][g0:g0 + SUBLANES, hd * GLA_DV:(hd + 1) * GLA_DV] = o_acc[hh]

    def stage_state(i):
        if sample:
            return
        d = tiles[i]
        bl = d.pop('bl')
        st = [st_s[i, p] for p in range(HEAD_PAIRS)]
        for c in range(rows // GLA_CHUNK):
            rws = slice(c * GLA_CHUNK, (c + 1) * GLA_CHUNK)
            for p, (ls, qbp_bf, o_intra, upd) in enumerate(d['pairs']):
                o_inter = _dot_nt(qbp_bf[rws, :], st[p].astype(BF16))
                for hh in range(2):
                    hd = 2 * p + hh
                    os_[i][rws, hd * GLA_DV:(hd + 1) * GLA_DV] = (
                        o_intra[hh][rws, :] + o_inter[:, hh * GLA_DV:(hh + 1) * GLA_DV])
                dec = jnp.exp(bl[c * GLA_CHUNK:c * GLA_CHUNK + 1, ls])
                st[p] = st[p] * dec + upd[c]
        for p in range(HEAD_PAIRS):
            st_s[i, p] = st[p]
        d.pop('pairs')

    def stage_out(i):
        d = tiles[i]
        r = d.pop('r')
        out_gate = r * _sigmoid(r)
        mix, x = d.pop('mix'), d.pop('x')
        half = rows // 2
        for hr in (slice(0, half), slice(half, rows)):
            o = os_[i][hr, :]
            og = jnp.concatenate(
                [_rms(o[:, j * GLA_DV:(j + 1) * GLA_DV], 1.0) for j in range(GLA_HEADS)], axis=-1)
            og = og * g_gla_ref[...] * out_gate[hr, :]
            mix_h = mix[hr, :] + _dot_wide(og.astype(BF16), w_out_ref, slice(0, GLA_WIDTH))
            ys[i][hr, :] = x[hr, :] + _rms(mix_h, g_post_ref[...])

    stages = (stage_norm, stage_project, stage_gate_sgu, stage_mix_sgu, stage_decay, stage_intra,
              stage_state, stage_out)
    for stage in stages:
        for i in range(n_par):
            stage(i)

    if not sample:
        @pl.when(t == pl.num_programs(1) - 1)
        def _():
            for i in range(n_par):
                for p in range(HEAD_PAIRS):
                    st = st_s[i, p]
                    sout_ref[i, 2 * p] = st[:GLA_DV, :].T[:GLA_DK, :]
                    sout_ref[i, 2 * p + 1] = st[GLA_DV:, :].T[GLA_DK:, :]


def _ffn_tail(x1, f, p, g_ffn_post, g_ple_in, w_pg_ref, w_ple_ref, g_ple_post):
    x2 = x1 + _rms(f, g_ffn_post)
    hp = _rms(x2, g_ple_in).astype(BF16)
    gate = _sigmoid(_dot_wide(hp, w_pg_ref))
    pe = _dot_wide(p.astype(BF16), w_ple_ref)
    return x2 + _rms(pe * gate, g_ple_post)


def _ff_groups():
    return [range(g, min(g + FF_GROUP, N_FF_BLOCKS)) for g in range(0, N_FF_BLOCKS, FF_GROUP)]


def _ffn_prompt_kernel(x_ref, p_ref, g_pre_ref, w_up_ref, w_conv_ref, b_conv_ref, w_down_ref,
                       g_post_ref, g_ple_in_ref, w_pg_ref, w_ple_ref, g_ple_post_ref,
                       y_ref, cs_ref,
                       carry_s, up_s, act_s, *, rows, n_par, skew):
    t = pl.program_id(1)
    hdr = SUBLANES

    @pl.when(t == 0)
    def _():
        carry_s[...] = jnp.zeros_like(carry_s)
        up_s[:, :, :, :, hdr + rows:, :] = jnp.zeros((n_par, 2, 2, CONV_W, hdr, FF_BLOCK), F32)

    tiles = [dict() for _ in range(n_par)]

    def half_cols(j, half):
        c0 = half * D_FF + j * FF_BLOCK
        return slice(c0, c0 + FF_BLOCK)

    def project(i, j):
        for half in range(2):
            cols = half_cols(j, half)
            up = _dot(tiles[i]['h'], w_up_ref[:, cols])
            for s in range(CONV_W):
                buf = up_s.at[i, j % 2, half, s]
                if s:
                    buf[hdr:2 * hdr, :] = carry_s[i, s - 1, :, cols]
                buf[hdr + s:hdr + s + rows, :] = up
                if s:
                    carry_s[i, s - 1, :, cols] = buf[hdr + rows:2 * hdr + rows, :]

    def conv(i, j, half):
        cols = half_cols(j, half)
        w = w_conv_ref[:, cols]
        taps = [up_s[i, j % 2, half, s, hdr:hdr + rows, :] for s in range(CONV_W)]
        return (b_conv_ref[:, cols] + taps[2] * w[0:1, :] + taps[1] * w[1:2, :] + taps[0] * w[2:3, :])

    def down(i, j):
        k_rows = slice(j * FF_BLOCK, (j + 1) * FF_BLOCK)
        return _dot_wide(act_s[i, :, k_rows], w_down_ref, k_rows)

    def stage_head(i):
        d = tiles[i]
        d['x1'] = x_ref[i]
        d['h'] = _rms(d['x1'], g_pre_ref[...]).astype(BF16)
        project(i, 0)

    def stage_block(i, j):
        d = tiles[i]
        if j + 1 < N_FF_BLOCKS:
            project(i, j + 1)
        if j == 1:
            d['f'] = down(i, 0)
        elif j > 1:
            d['f'] = d['f'] + down(i, j - 1)
        gate = conv(i, j, 0)
        act_s[i, :, j * FF_BLOCK:(j + 1) * FF_BLOCK] = (
            (gate * conv(i, j, 1)) * _gelu_gate(gate)).astype(BF16)

    def stage_tail(i):
        d = tiles[i]
        f = d.pop('f') + down(i, N_FF_BLOCKS - 1)
        d.pop('h')
        y_ref[i] = _ffn_tail(d.pop('x1'), f, p_ref[i], g_post_ref[...], g_ple_in_ref[...], w_pg_ref,
                             w_ple_ref, g_ple_post_ref[...])

    stages = ([stage_head] + [functools.partial(stage_block, j=j) for j in range(N_FF_BLOCKS)]
              + [stage_tail])
    for k in range(len(stages) + skew * (n_par - 1)):
        for i in range(n_par):
            if 0 <= k - skew * i < len(stages):
                stages[k - skew * i](i)

    @pl.when(t == pl.num_programs(1) - 1)
    def _():
        for i in range(n_par):
            cs_ref[i] = carry_s[i, CONV_W - 2, 0:CONV_W - 1, :]


def _ffn_sample_kernel(x_ref, p_ref, cin_ref, g_pre_ref, w_up_ref, w_conv_ref, b_conv_ref, w_down_ref,
                       g_post_ref, g_ple_in_ref, w_pg_ref, w_ple_ref, g_ple_post_ref,
                       y_ref, cs_ref, act_s, *, nseq, seq_len):
    x1 = jnp.concatenate([x_ref[:, j, :] for j in range(seq_len)], axis=0)
    p = jnp.concatenate([p_ref[:, j, :] for j in range(seq_len)], axis=0)
    h = _rms(x1, g_pre_ref[...]).astype(BF16)

    def conv_half(col0):
        cols = slice(col0, col0 + FF_BLOCK)
        up = _dot(h, w_up_ref[:, cols])
        blocks = [cin_ref[:, i, cols] for i in range(CONV_W - 1)]
        blocks += [up[j * nseq:(j + 1) * nseq, :] for j in range(seq_len)]
        w = w_conv_ref[:, cols]
        b = b_conv_ref[:, cols]
        conv = jnp.concatenate(
            [b + blocks[j] * w[0:1, :] + blocks[j + 1] * w[1:2, :] + blocks[j + 2] * w[2:3, :]
             for j in range(seq_len)], axis=0)
        for i in range(CONV_W - 1):
            cs_ref[:, i, cols] = blocks[seq_len + i]
        return conv

    f = None
    for grp in _ff_groups():
        for j in grp:
            gate = conv_half(j * FF_BLOCK)
            val = conv_half(D_FF + j * FF_BLOCK)
            act_s[:, j * FF_BLOCK:(j + 1) * FF_BLOCK] = ((gate * val) * _gelu_gate(gate)).astype(BF16)
        k_rows = slice(grp[0] * FF_BLOCK, (grp[-1] + 1) * FF_BLOCK)
        part = _dot_wide(act_s[:, k_rows], w_down_ref, k_rows)
        f = part if f is None else f + part

    y = _ffn_tail(x1, f, p, g_post_ref[...], g_ple_in_ref[...], w_pg_ref, w_ple_ref,
                  g_ple_post_ref[...])
    for j in range(seq_len):
        y_ref[:, j, :] = y[j * nseq:(j + 1) * nseq, :]


def _cast_kernel(*refs):
    n = (len(refs) - 3) // 2
    w_in_t_ref, srcs = refs[0], refs[1:1 + n]
    ga_ref, sg_ref, dsts = refs[1 + n], refs[2 + n], refs[3 + n:]

    glr0 = IN_COLS_GLA
    u0 = glr0 + GLA_GATE_RANK
    ga_ref[...] = w_in_t_ref[:glr0, :].T.astype(BF16)
    sg_ref[:, :COL_GLR] = w_in_t_ref[u0:u0 + COL_GLR, :].T.astype(BF16)
    tile = w_in_t_ref[glr0:glr0 + LANES, :].T
    lane = lax.broadcasted_iota(jnp.int32, tile.shape, 1)
    sg_ref[:, COL_GLR:] = jnp.where(lane < GLA_GATE_RANK, tile, 0.0).astype(BF16)

    for src, dst in zip(srcs, dsts):
        cols = src.shape[1]
        dst[:, :cols] = src[...].astype(BF16)
        if dst.shape[1] > cols:
            dst[:, cols:] = jnp.zeros((dst.shape[0], dst.shape[1] - cols), BF16)


def _whole(shape):
    n = len(shape)
    return pl.BlockSpec(shape, lambda *_: (0,) * n)


def _full(shape):
    n = len(shape)
    return pl.BlockSpec(shape, lambda *_: (0,) * n, pipeline_mode=pl.Buffered(1))


def _gla_masks(rows, span):
    i = np.arange(rows)
    same = (i[:, None] // span) == (i[None, :] // span)
    causal = same & (i[None, :] <= i[:, None])
    return (jnp.asarray(causal, BF16), jnp.asarray(same, BF16), jnp.asarray(causal, F32))


def kernel(x_prompt, x_sample, state_gla, state_ffn_conv, p_prompt, p_sample, g_mix_pre, w_in, w_gla_gate, b_gla_gate, g_gla_out, g_sgu_ln, b_sgu_ln, w_spatial, b_spatial, w_out, g_mix_post, g_ffn_pre, w_up, w_conv, b_conv, w_down, g_ffn_post, g_ple_in, w_ple_gate, w_ple, g_ple_post):
    depth = x_prompt.ndim - 2
    assert w_in.shape[0] == 1 and depth == 1
    nb, seq, _ = x_prompt.shape
    ns, dseq, _ = x_sample.shape
    assert seq % PROMPT_TILE == 0 and PROMPT_TILE % GMLP_CHUNK == 0 and seq % FFN_PROMPT_TILE == 0
    assert dseq == 4 and (ns * dseq) % SAMPLE_TILE == 0 and SAMPLE_TILE % GMLP_CHUNK == 0

    w_gate_p = jnp.concatenate(
        [w_gla_gate[0], jnp.zeros((LANES - GLA_GATE_RANK, GLA_KEY), w_gla_gate.dtype)], axis=0).astype(BF16)
    row = lambda a: a.reshape(1, -1)
    g_gla_t = jnp.tile(g_gla_out[0], GLA_HEADS).reshape(1, -1)
    n_cast = CAST_STEPS
    in_cols = w_in.shape[-1]
    assert in_cols == IN_COLS_GLA + GLA_GATE_RANK + COL_GLR
    cast_src = (w_out[0], w_up[0], w_down[0], w_ple_gate[0], w_ple[0])
    cast_cols_in = (D_MODEL, 2 * D_FF, D_MODEL, D_MODEL, D_MODEL)
    cast_cols_out = (D_MODEL + LANES, 2 * D_FF, D_MODEL + LANES, D_MODEL + LANES, D_MODEL + LANES)
    cast_rows = tuple(w.shape[0] for w in cast_src)
    assert all(r % (n_cast * 2 * SUBLANES) == 0 for r in cast_rows + (D_MODEL,))
    slab = lambda r, c: pl.BlockSpec((r // n_cast, c), lambda i: (i, 0))
    w_ga, w_sg, w_out_b, w_up_b, w_down_b, w_pg_b, w_ple_b = pl.pallas_call(
        _cast_kernel,
        grid=(n_cast,),
        in_specs=[pl.BlockSpec((None, in_cols, D_MODEL // n_cast), lambda i: (0, 0, i))]
        + [slab(r, c) for r, c in zip(cast_rows, cast_cols_in)],
        out_specs=[slab(D_MODEL, IN_COLS_GLA), slab(D_MODEL, IN_COLS_SGU)]
        + [slab(r, c) for r, c in zip(cast_rows, cast_cols_out)],
        out_shape=[jax.ShapeDtypeStruct((D_MODEL, IN_COLS_GLA), BF16),
                   jax.ShapeDtypeStruct((D_MODEL, IN_COLS_SGU), BF16)]
        + [jax.ShapeDtypeStruct((r, c), BF16) for r, c in zip(cast_rows, cast_cols_out)],
        compiler_params=pltpu.CompilerParams(
            dimension_semantics=("arbitrary",), vmem_limit_bytes=VMEM_LIMIT_BYTES),
        name="weight_cast",
    )(jnp.swapaxes(w_in, 1, 2), *cast_src)

    def spatial(c):
        i = np.arange(GMLP_CHUNK)
        keep = ((i[:, None] // c) == (i[None, :] // c)) & (i[None, :] <= i[:, None])
        e = jnp.asarray((i[:, None] % c) == np.arange(c)[None, :], w_spatial.dtype)
        hi = lax.Precision.HIGHEST
        wbd = jnp.einsum('ia,hab,jb->hij', e, w_spatial[0][:, :c, :c], e, precision=hi)
        wbd = wbd * jnp.asarray(keep, w_spatial.dtype)
        bias = jnp.einsum('ia,ha->ih', e, b_spatial[0][:, :c], precision=hi)
        return wbd.astype(BF16), jnp.repeat(bias, GMLP_DC, axis=1)

    mixer_weights = lambda wsp, bsp, masks: (
        row(g_mix_pre), w_ga, w_sg, w_gate_p, row(b_gla_gate), g_gla_t, row(g_sgu_ln), row(b_sgu_ln),
        wsp, bsp, w_out_b, row(g_mix_post)) + masks

    def mixer_weight_specs(rows):
        return [
            _full((1, D_MODEL)), _full((D_MODEL, IN_COLS_GLA)), _full((D_MODEL, IN_COLS_SGU)),
            _full((LANES, GLA_KEY)),
            _full((1, GLA_KEY)), _full((1, GLA_WIDTH)), _full((1, GMLP_WIDTH)), _full((1, GMLP_WIDTH)),
            _full((GMLP_HEADS, GMLP_CHUNK, GMLP_CHUNK)), _full((GMLP_CHUNK, GMLP_WIDTH)),
            _full((D_MODEL, D_MODEL + LANES)), _full((1, D_MODEL)),
            _full((rows, rows)), _full((rows, rows)), _full((rows, rows))]

    tl = PROMPT_TILE
    wsp_p, bsp_p = spatial(GMLP_CHUNK)
    npar = MIXER_SEQS_PER_STEP
    assert nb % npar == 0
    x1_p, gla_p = pl.pallas_call(
        functools.partial(_mixer_kernel, rows=tl, sample=False, n_par=npar),
        grid=(nb // npar, seq // tl),
        in_specs=[pl.BlockSpec((npar, None, tl, D_MODEL), lambda b, t: (0, b, t, 0))] + mixer_weight_specs(tl),
        out_specs=[pl.BlockSpec((npar, None, tl, D_MODEL), lambda b, t: (0, b, t, 0)),
                   pl.BlockSpec((npar, None, GLA_HEADS, GLA_DK, GLA_DV), lambda b, t: (0, b, 0, 0, 0))],
        out_shape=[jax.ShapeDtypeStruct((npar, nb // npar, seq, D_MODEL), F32),
                   jax.ShapeDtypeStruct((npar, nb // npar, GLA_HEADS, GLA_DK, GLA_DV), F32)],
        scratch_shapes=[pltpu.VMEM((npar, tl, GLA_WIDTH), F32),
                        pltpu.VMEM((npar, HEAD_PAIRS, 2 * GLA_DV, 2 * GLA_DK), F32)],
        compiler_params=pltpu.CompilerParams(
            dimension_semantics=("arbitrary", "arbitrary"), vmem_limit_bytes=VMEM_LIMIT_BYTES),
        name="mixer_prompt",
    )(x_prompt.reshape(npar, nb // npar, seq, D_MODEL),
      *mixer_weights(wsp_p, bsp_p, _gla_masks(tl, GLA_CHUNK)))
    x1_p = x1_p.reshape(nb, seq, D_MODEL)
    gla_p = gla_p.reshape(nb, GLA_HEADS, GLA_DK, GLA_DV)

    ts = SAMPLE_TILE
    seqs_per_tile = ts // dseq
    wsp_s, bsp_s = spatial(dseq)
    x1_s, gla_s, vrows_s = pl.pallas_call(
        functools.partial(_mixer_kernel, rows=ts, sample=True),
        grid=(ns * dseq // ts,),
        in_specs=[pl.BlockSpec((ts, D_MODEL), lambda i: (i, 0)),
                  pl.BlockSpec((seqs_per_tile, HEAD_PAIRS, 2 * GLA_DK, GLA_DV), lambda i: (i, 0, 0, 0))]
        + mixer_weight_specs(ts),
        out_specs=[pl.BlockSpec((ts, D_MODEL), lambda i: (i, 0)),
                   pl.BlockSpec((seqs_per_tile, HEAD_PAIRS, 2 * GLA_DK, GLA_DV), lambda i: (i, 0, 0, 0)),
                   pl.BlockSpec((ts, GMLP_WIDTH), lambda i: (i, 0))],
        out_shape=[jax.ShapeDtypeStruct((ns * dseq, D_MODEL), F32),
                   jax.ShapeDtypeStruct((ns, HEAD_PAIRS, 2 * GLA_DK, GLA_DV), F32),
                   jax.ShapeDtypeStruct((ns * dseq, GMLP_WIDTH), F32)],
        scratch_shapes=[pltpu.VMEM((ts, GLA_WIDTH), F32)],
        compiler_params=pltpu.CompilerParams(
            dimension_semantics=("arbitrary",), vmem_limit_bytes=VMEM_LIMIT_BYTES),
        name="mixer_sample",
    )(x_sample.reshape(ns * dseq, D_MODEL),
      state_gla[0].reshape(ns, HEAD_PAIRS, 2 * GLA_DK, GLA_DV),
      *mixer_weights(wsp_s, bsp_s, _gla_masks(ts, dseq)))

    ffn_weights = (row(g_ffn_pre), w_up_b, w_conv[0], row(b_conv), w_down_b, row(g_ffn_post),
                   row(g_ple_in), w_pg_b, w_ple_b, row(g_ple_post))
    ffn_weight_specs = [
        _full((1, D_MODEL)), _full((D_MODEL, 2 * D_FF)), _full((CONV_W, 2 * D_FF)), _full((1, 2 * D_FF)),
        _full((D_FF, D_MODEL + LANES)), _full((1, D_MODEL)), _full((1, D_MODEL)),
        _full((D_MODEL, D_MODEL + LANES)), _full((PLE_DIM, D_MODEL + LANES)), _full((1, D_MODEL))]

    tf = FFN_PROMPT_TILE
    fpar = FFN_SEQS_PER_STEP
    assert nb % fpar == 0
    y_p, conv_p = pl.pallas_call(
        functools.partial(_ffn_prompt_kernel, rows=tf, n_par=fpar, skew=FFN_STAGE_SKEW),
        grid=(nb // fpar, seq // tf),
        in_specs=[pl.BlockSpec((fpar, None, tf, D_MODEL), lambda b, t: (0, b, t, 0)),
                  pl.BlockSpec((fpar, None, tf, PLE_DIM), lambda b, t: (0, b, t, 0))] + ffn_weight_specs,
        out_specs=[pl.BlockSpec((fpar, None, tf, D_MODEL), lambda b, t: (0, b, t, 0)),
                   pl.BlockSpec((fpar, None, CONV_W - 1, 2 * D_FF), lambda b, t: (0, b, 0, 0))],
        out_shape=[jax.ShapeDtypeStruct((fpar, nb // fpar, seq, D_MODEL), F32),
                   jax.ShapeDtypeStruct((fpar, nb // fpar, CONV_W - 1, 2 * D_FF), F32)],
        scratch_shapes=[pltpu.VMEM((fpar, CONV_W - 1, SUBLANES, 2 * D_FF), F32),
                        pltpu.VMEM((fpar, 2, 2, CONV_W, tf + 2 * SUBLANES, FF_BLOCK), F32),
                        pltpu.VMEM((fpar, tf, D_FF), BF16)],
        compiler_params=pltpu.CompilerParams(
            dimension_semantics=("arbitrary", "arbitrary"), vmem_limit_bytes=VMEM_LIMIT_BYTES),
        name="ffn_prompt",
    )(x1_p.reshape(fpar, nb // fpar, seq, D_MODEL),
      p_prompt[0].reshape(fpar, nb // fpar, seq, PLE_DIM), *ffn_weights)
    y_p = y_p.reshape(nb, seq, D_MODEL)
    conv_p = conv_p.reshape(nb, CONV_W - 1, 2 * D_FF)

    y_s, conv_s = pl.pallas_call(
        functools.partial(_ffn_sample_kernel, nseq=ns, seq_len=dseq),
        grid=(1,),
        in_specs=[_whole((ns, dseq, D_MODEL)), _whole((ns, dseq, PLE_DIM)),
                  _whole((ns, CONV_W - 1, 2 * D_FF))] + ffn_weight_specs,
        out_specs=[_whole((ns, dseq, D_MODEL)), _whole((ns, CONV_W - 1, 2 * D_FF))],
        out_shape=[jax.ShapeDtypeStruct((ns, dseq, D_MODEL), F32),
                   jax.ShapeDtypeStruct((ns, CONV_W - 1, 2 * D_FF), F32)],
        scratch_shapes=[pltpu.VMEM((ns * dseq, D_FF), BF16)],
        compiler_params=pltpu.CompilerParams(
            dimension_semantics=("arbitrary",), vmem_limit_bytes=VMEM_LIMIT_BYTES),
        name="ffn_sample",
    )(x1_s.reshape(ns, dseq, D_MODEL), p_sample[0], state_ffn_conv[0], *ffn_weights)

    return (y_p,
            y_s,
            gla_p[None],
            gla_s.reshape(1, ns, GLA_HEADS, GLA_DK, GLA_DV),
            conv_p[None],
            conv_s[None],
            vrows_s.reshape(1, ns, dseq, GMLP_WIDTH))
```

```python
import functools

import numpy as np
import jax
import jax.numpy as jnp
from jax import lax
from jax.experimental import pallas as pl
from jax.experimental.pallas import tpu as pltpu

D_MODEL = 1024
GLA_HEADS = 4
GLA_DK = 64
GLA_DV = 128
GLA_KEY = GLA_HEADS * GLA_DK
GLA_WIDTH = GLA_HEADS * GLA_DV
GMLP_HEADS = 4
GMLP_DC = 128
GMLP_WIDTH = GMLP_HEADS * GMLP_DC
GLA_GATE_RANK = 16
GLA_TAU = 16.0
GLA_CHUNK = 64
GMLP_CHUNK = 128
D_FF = 2816
CONV_W = 3
PLE_DIM = 256
EPS = 1e-6

LANES = 128
SUBLANES = 8
MXU_N = 256
HEAD_PAIRS = GLA_HEADS // 2

COL_QK = 0
COL_V = COL_QK + 2 * GLA_KEY
COL_R = COL_V + GLA_WIDTH
IN_COLS_GLA = COL_R + GLA_WIDTH
COL_U = 0
COL_VS = COL_U + GMLP_WIDTH
COL_GLR = COL_VS + GMLP_WIDTH
IN_COLS_SGU = COL_GLR + LANES

FF_BLOCK = 256
N_FF_BLOCKS = D_FF // FF_BLOCK
FF_GROUP = 4

PROMPT_TILE = 256
MIXER_SEQS_PER_STEP = 4
FFN_PROMPT_TILE = 256
FFN_SEQS_PER_STEP = 2
FFN_STAGE_SKEW = 0
SAMPLE_TILE = 128
CAST_STEPS = 8

VMEM_LIMIT_BYTES = 56 * 1024 * 1024

BF16 = jnp.bfloat16
F32 = jnp.float32


def _dot(a, b):
    return jnp.dot(a, b, preferred_element_type=F32)


def _dot_wide(a, w_ref, rows=slice(None)):
    return jnp.concatenate(
        [_dot(a, w_ref[rows, c:c + MXU_N]) for c in range(0, D_MODEL, MXU_N)], axis=-1)


def _dot_nt(a, b):
    return lax.dot_general(a, b, (((1,), (1,)), ((), ())), preferred_element_type=F32)


def _dot_tn(a, b):
    return lax.dot_general(a, b, (((0,), (0,)), ((), ())), preferred_element_type=F32)


def _rms(x, g):
    return x * lax.rsqrt(jnp.mean(x * x, axis=-1, keepdims=True) + EPS) * g


def _gelu_gate(x):
    c = -2.0 * np.sqrt(2.0 / np.pi) * np.log2(np.e)
    a1 = np.float32(c)
    a3 = np.float32(c * 0.044715)
    return 1.0 / (1.0 + jnp.exp2(x * (a3 * (x * x) + a1)))


def _gelu(x):
    return x * _gelu_gate(x)


def _sigmoid(x):
    return 1.0 / (1.0 + jnp.exp(-x))


def _log_sigmoid(x):
    return jnp.minimum(x, 0.0) - jnp.log1p(jnp.exp(-jnp.abs(x)))


def _split_bf16(x):
    hi = x.astype(BF16)
    lo = (x - hi.astype(F32)).astype(BF16)
    return hi, lo


def _mixer_kernel(*refs, rows, sample, n_par=1):
    if sample:
        (x_ref, sin_ref, g_pre_ref, w_ga_ref, w_sg_ref, w_gate_ref, b_gate_ref, g_gla_ref, g_ln_ref, b_ln_ref,
         w_sp_ref, b_sp_ref, w_out_ref, g_post_ref, cum_ref, tot_ref, causal_ref,
         y_ref, sout_ref, vn_ref, o_s) = refs
        xs, ys, os_ = [x_ref], [y_ref], [o_s]
    else:
        (x_ref, g_pre_ref, w_ga_ref, w_sg_ref, w_gate_ref, b_gate_ref, g_gla_ref, g_ln_ref, b_ln_ref,
         w_sp_ref, b_sp_ref, w_out_ref, g_post_ref, cum_ref, tot_ref, causal_ref,
         y_ref, sout_ref, o_s, st_s) = refs
        t = pl.program_id(1)

        @pl.when(t == 0)
        def _():
            st_s[...] = jnp.zeros_like(st_s)

        xs = [x_ref.at[i] for i in range(n_par)]
        ys = [y_ref.at[i] for i in range(n_par)]
        os_ = [o_s.at[i] for i in range(n_par)]

    lane = lax.broadcasted_iota(jnp.int32, (rows, LANES), 1)
    head_lane_masks = (lane < GLA_DK, lane >= GLA_DK)
    tiles = [dict() for _ in range(n_par)]

    def stage_norm(i):
        d = tiles[i]
        d['x'] = xs[i][...].reshape(rows, D_MODEL)
        d['h'] = _rms(d['x'], g_pre_ref[...]).astype(BF16)

    def stage_project(i):
        d = tiles[i]
        h = d['h']
        glr = _dot(h, w_sg_ref[:, COL_GLR:IN_COLS_SGU])
        d['logit'] = _dot(glr.astype(BF16), w_gate_ref[...]) + b_gate_ref[...]
        d['u_pre'] = _dot(h, w_sg_ref[:, COL_U:COL_VS])
        d['vs_pre'] = _dot(h, w_sg_ref[:, COL_VS:COL_GLR])
        d['qk'] = _dot(h, w_ga_ref[:, COL_QK:COL_V])
        d['v'] = _dot(h, w_ga_ref[:, COL_V:COL_R])
        d['r'] = _dot(h, w_ga_ref[:, COL_R:IN_COLS_GLA])

    def stage_gate_sgu(i):
        d = tiles[i]
        d['g'] = _split_bf16(_log_sigmoid(d.pop('logit')) * (1.0 / GLA_TAU))
        d['u'] = _gelu(d.pop('u_pre'))
        vs = _gelu(d.pop('vs_pre'))
        vn_parts = []
        for j in range(GMLP_HEADS):
            vh = vs[:, j * GMLP_DC:(j + 1) * GMLP_DC]
            mu = jnp.mean(vh, axis=-1, keepdims=True)
            c = vh - mu
            var = jnp.mean(c * c, axis=-1, keepdims=True)
            vn_parts.append(c * lax.rsqrt(var + EPS))
        vn = jnp.concatenate(vn_parts, axis=-1) * g_ln_ref[...] + b_ln_ref[...]
        if sample:
            vn_ref[...] = vn.reshape(vn_ref.shape)
        d['vn_bf'] = vn.astype(BF16)

    def stage_mix_sgu(i):
        d = tiles[i]
        g_hi, g_lo = d.pop('g')
        cum = cum_ref[...]
        tot = tot_ref[...]
        d['b'] = _dot(cum, g_hi) + _dot(cum, g_lo)
        d['bl'] = _dot(tot, g_hi) + _dot(tot, g_lo)
        vn_bf = d.pop('vn_bf')
        mixed_parts = []
        for cch in range(rows // GMLP_CHUNK):
            crows = slice(cch * GMLP_CHUNK, (cch + 1) * GMLP_CHUNK)
            mixed_parts.append(jnp.concatenate(
                [_dot(w_sp_ref[j], vn_bf[crows, j * GMLP_DC:(j + 1) * GMLP_DC]) for j in range(GMLP_HEADS)],
                axis=-1) + b_sp_ref[...])
        so = d.pop('u') * jnp.concatenate(mixed_parts, axis=0)
        d['mix'] = _dot_wide(so.astype(BF16), w_out_ref, slice(GLA_WIDTH, GLA_WIDTH + GMLP_WIDTH))

    def stage_decay(i):
        d = tiles[i]
        qk = d.pop('qk')
        q = qk[:, :GLA_KEY] * (GLA_DK ** -0.5)
        k = qk[:, GLA_KEY:]
        b, bl = d.pop('b'), d['bl']
        d['v_bf'] = d['v'].astype(BF16)
        d['qb'] = q * jnp.exp(b)
        d['kn'] = k * jnp.exp(-b)
        d['kd'] = k * jnp.exp(bl - b)

    def stage_intra(i):
        d = tiles[i]
        qb, kn, kd, v, v_bf, bl = d.pop('qb'), d.pop('kn'), d.pop('kd'), d.pop('v'), d['v_bf'], d['bl']
        causal = causal_ref[...]
        d['pairs'] = []
        for p in range(HEAD_PAIRS):
            ls = slice(p * LANES, (p + 1) * LANES)
            qbp = qb[:, ls]
            knp = kn[:, ls].astype(BF16)
            kdp = kd[:, ls]
            o_intra = []
            for hh in range(2):
                hd = 2 * p + hh
                qm = jnp.where(head_lane_masks[hh], qbp, 0.0).astype(BF16)
                s = _dot_nt(qm, knp) * causal
                o_intra.append(_dot(s.astype(BF16), v_bf[:, hd * GLA_DV:(hd + 1) * GLA_DV]))

            if not sample:
                st_shape = (2 * GLA_DV, 2 * GLA_DK)
                ri = lax.broadcasted_iota(jnp.int32, st_shape, 0)
                ci = lax.broadcasted_iota(jnp.int32, st_shape, 1)
                same_head = (ri < GLA_DV) == (ci < GLA_DK)
                kdp_bf = kdp.astype(BF16)
                upd = []
                for c in range(rows // GLA_CHUNK):
                    rws = slice(c * GLA_CHUNK, (c + 1) * GLA_CHUNK)
                    inc = _dot_tn(v_bf[rws, 2 * p * GLA_DV:(2 * p + 2) * GLA_DV], kdp_bf[rws, :])
                    upd.append(jnp.where(same_head, inc, 0.0))
                d['pairs'].append((ls, qbp.astype(BF16), o_intra, upd))
            else:
                seq_len = 4
                seqs_per_grp = SUBLANES // seq_len
                kdp_t = kdp.T.astype(BF16)
                dec_t = jnp.exp(bl[:, ls]).T
                sub = lax.broadcasted_iota(jnp.int32, (SUBLANES, LANES), 0)
                crow = lax.broadcasted_iota(jnp.int32, (rows, LANES), 0)
                for grp in range(rows // SUBLANES):
                    g0 = grp * SUBLANES
                    qg = qbp[g0:g0 + SUBLANES, :]
                    o_acc = [o_intra[hh][g0:g0 + SUBLANES, :] for hh in range(2)]
                    for j in range(seqs_per_grp):
                        sidx = grp * seqs_per_grp + j
                        s_pair = sin_ref[sidx, p]
                        s_bf = s_pair.astype(BF16)
                        in_seq = (sub >= j * seq_len) & (sub < (j + 1) * seq_len)
                        in_seq_c = (crow >= g0 + j * seq_len) & (crow < g0 + (j + 1) * seq_len)
                        new_rows = []
                        for hh in range(2):
                            hd = 2 * p + hh
                            qsel = jnp.where(in_seq & head_lane_masks[hh][:SUBLANES], qg, 0.0).astype(BF16)
                            o_acc[hh] = o_acc[hh] + _dot(qsel, s_bf)
                            vsel = jnp.where(in_seq_c, v[:, hd * GLA_DV:(hd + 1) * GLA_DV], 0.0).astype(BF16)
                            new_rows.append(_dot(kdp_t[hh * GLA_DK:(hh + 1) * GLA_DK, :], vsel))
                        col = g0 + j * seq_len
                        sout_ref[sidx, p] = s_pair * dec_t[:, col:col + 1] + jnp.concatenate(new_rows, axis=0)
                    for hh in range(2):
                        hd = 2 * p + hh
                        os_[i][g0:g0 + SUBLANES, hd * GLA_DV:(hd + 1) * GLA_DV] = o_acc[hh]

    def stage_state(i):
        if sample:
            return
        d = tiles[i]
        bl = d.pop('bl')
        st = [st_s[i, p] for p in range(HEAD_PAIRS)]
        for c in range(rows // GLA_CHUNK):
            rws = slice(c * GLA_CHUNK, (c + 1) * GLA_CHUNK)
            for p, (ls, qbp_bf, o_intra, upd) in enumerate(d['pairs']):
                o_inter = _dot_nt(qbp_bf[rws, :], st[p].astype(BF16))
                for hh in range(2):
                    hd = 2 * p + hh
                    os_[i][rws, hd * GLA_DV:(hd + 1) * GLA_DV] = (
                        o_intra[hh][rws, :] + o_inter[:, hh * GLA_DV:(hh + 1) * GLA_DV])
                dec = jnp.exp(bl[c * GLA_CHUNK:c * GLA_CHUNK + 1, ls])
                st[p] = st[p] * dec + upd[c]
        for p in range(HEAD_PAIRS):
            st_s[i, p] = st[p]
        d.pop('pairs')

    def stage_out(i):
        d = tiles[i]
        r = d.pop('r')
        out_gate = r * _sigmoid(r)
        mix, x = d.pop('mix'), d.pop('x')
        half = rows // 2
        for hr in (slice(0, half), slice(half, rows)):
            o = os_[i][hr, :]
            og = jnp.concatenate(
                [_rms(o[:, j * GLA_DV:(j + 1) * GLA_DV], 1.0) for j in range(GLA_HEADS)], axis=-1)
            og = og * g_gla_ref[...] * out_gate[hr, :]
            mix_h = mix[hr, :] + _dot_wide(og.astype(BF16), w_out_ref, slice(0, GLA_WIDTH))
            y_h = x[hr, :] + _rms(mix_h, g_post_ref[...])
            if sample:
                seq_len = ys[i].shape[1]
                ys[i][hr.start // seq_len:hr.stop // seq_len] = y_h.reshape(-1, seq_len, D_MODEL)
            else:
                ys[i][hr, :] = y_h

    stages = (stage_norm, stage_project, stage_gate_sgu, stage_mix_sgu, stage_decay, stage_intra,
              stage_state, stage_out)
    for stage in stages:
        for i in range(n_par):
            stage(i)

    if not sample:
        @pl.when(t == pl.num_programs(1) - 1)
        def _():
            for i in range(n_par):
                for p in range(HEAD_PAIRS):
                    st = st_s[i, p]
                    sout_ref[i, 2 * p] = st[:GLA_DV, :].T[:GLA_DK, :]
                    sout_ref[i, 2 * p + 1] = st[GLA_DV:, :].T[GLA_DK:, :]


def _ffn_tail(x1, f, p, g_ffn_post, g_ple_in, w_pg_ref, w_ple_ref, g_ple_post):
    x2 = x1 + _rms(f, g_ffn_post)
    hp = _rms(x2, g_ple_in).astype(BF16)
    gate = _sigmoid(_dot_wide(hp, w_pg_ref))
    pe = _dot_wide(p.astype(BF16), w_ple_ref)
    return x2 + _rms(pe * gate, g_ple_post)


def _ff_groups():
    return [range(g, min(g + FF_GROUP, N_FF_BLOCKS)) for g in range(0, N_FF_BLOCKS, FF_GROUP)]


def _ffn_prompt_kernel(x_ref, p_ref, g_pre_ref, w_up_ref, w_conv_ref, b_conv_ref, w_down_ref,
                       g_post_ref, g_ple_in_ref, w_pg_ref, w_ple_ref, g_ple_post_ref,
                       y_ref, cs_ref,
                       carry_s, up_s, act_s, *, rows, n_par, skew):
    t = pl.program_id(1)
    hdr = SUBLANES

    @pl.when(t == 0)
    def _():
        carry_s[...] = jnp.zeros_like(carry_s)
        up_s[:, :, :, :, hdr + rows:, :] = jnp.zeros((n_par, 2, 2, CONV_W, hdr, FF_BLOCK), F32)

    tiles = [dict() for _ in range(n_par)]

    def half_cols(j, half):
        c0 = half * D_FF + j * FF_BLOCK
        return slice(c0, c0 + FF_BLOCK)

    def project(i, j):
        for half in range(2):
            cols = half_cols(j, half)
            up = _dot(tiles[i]['h'], w_up_ref[:, cols])
            for s in range(CONV_W):
                buf = up_s.at[i, j % 2, half, s]
                if s:
                    buf[hdr:2 * hdr, :] = carry_s[i, s - 1, :, cols]
                buf[hdr + s:hdr + s + rows, :] = up
                if s:
                    carry_s[i, s - 1, :, cols] = buf[hdr + rows:2 * hdr + rows, :]

    def conv(i, j, half):
        cols = half_cols(j, half)
        w = w_conv_ref[:, cols]
        taps = [up_s[i, j % 2, half, s, hdr:hdr + rows, :] for s in range(CONV_W)]
        return (b_conv_ref[:, cols] + taps[2] * w[0:1, :] + taps[1] * w[1:2, :] + taps[0] * w[2:3, :])

    def down(i, j):
        k_rows = slice(j * FF_BLOCK, (j + 1) * FF_BLOCK)
        return _dot_wide(act_s[i, :, k_rows], w_down_ref, k_rows)

    def stage_head(i):
        d = tiles[i]
        d['x1'] = x_ref[i]
        d['h'] = _rms(d['x1'], g_pre_ref[...]).astype(BF16)
        project(i, 0)

    def stage_block(i, j):
        d = tiles[i]
        if j + 1 < N_FF_BLOCKS:
            project(i, j + 1)
        if j == 1:
            d['f'] = down(i, 0)
        elif j > 1:
            d['f'] = d['f'] + down(i, j - 1)
        gate = conv(i, j, 0)
        act_s[i, :, j * FF_BLOCK:(j + 1) * FF_BLOCK] = (
            (gate * conv(i, j, 1)) * _gelu_gate(gate)).astype(BF16)

    def stage_tail(i):
        d = tiles[i]
        f = d.pop('f') + down(i, N_FF_BLOCKS - 1)
        d.pop('h')
        y_ref[i] = _ffn_tail(d.pop('x1'), f, p_ref[i], g_post_ref[...], g_ple_in_ref[...], w_pg_ref,
                             w_ple_ref, g_ple_post_ref[...])

    stages = ([stage_head] + [functools.partial(stage_block, j=j) for j in range(N_FF_BLOCKS)]
              + [stage_tail])
    for k in range(len(stages) + skew * (n_par - 1)):
        for i in range(n_par):
            if 0 <= k - skew * i < len(stages):
                stages[k - skew * i](i)

    @pl.when(t == pl.num_programs(1) - 1)
    def _():
        for i in range(n_par):
            cs_ref[i] = carry_s[i, CONV_W - 2, 0:CONV_W - 1, :]


def _ffn_sample_kernel(x_ref, p_ref, cin_ref, g_pre_ref, w_up_ref, w_conv_ref, b_conv_ref, w_down_ref,
                       g_post_ref, g_ple_in_ref, w_pg_ref, w_ple_ref, g_ple_post_ref,
                       y_ref, cs_ref, act_s, *, nseq, seq_len):
    x1 = jnp.concatenate([x_ref[:, j, :] for j in range(seq_len)], axis=0)
    p = jnp.concatenate([p_ref[:, j, :] for j in range(seq_len)], axis=0)
    h = _rms(x1, g_pre_ref[...]).astype(BF16)

    def conv_half(col0):
        cols = slice(col0, col0 + FF_BLOCK)
        up = _dot(h, w_up_ref[:, cols])
        blocks = [cin_ref[:, i, cols] for i in range(CONV_W - 1)]
        blocks += [up[j * nseq:(j + 1) * nseq, :] for j in range(seq_len)]
        w = w_conv_ref[:, cols]
        b = b_conv_ref[:, cols]
        conv = jnp.concatenate(
            [b + blocks[j] * w[0:1, :] + blocks[j + 1] * w[1:2, :] + blocks[j + 2] * w[2:3, :]
             for j in range(seq_len)], axis=0)
        for i in range(CONV_W - 1):
            cs_ref[:, i, cols] = blocks[seq_len + i]
        return conv

    f = None
    for grp in _ff_groups():
        for j in grp:
            gate = conv_half(j * FF_BLOCK)
            val = conv_half(D_FF + j * FF_BLOCK)
            act_s[:, j * FF_BLOCK:(j + 1) * FF_BLOCK] = ((gate * val) * _gelu_gate(gate)).astype(BF16)
        k_rows = slice(grp[0] * FF_BLOCK, (grp[-1] + 1) * FF_BLOCK)
        part = _dot_wide(act_s[:, k_rows], w_down_ref, k_rows)
        f = part if f is None else f + part

    y = _ffn_tail(x1, f, p, g_post_ref[...], g_ple_in_ref[...], w_pg_ref, w_ple_ref,
                  g_ple_post_ref[...])
    for j in range(seq_len):
        y_ref[:, j, :] = y[j * nseq:(j + 1) * nseq, :]


def _cast_kernel(*refs):
    n = (len(refs) - 3) // 2
    w_in_t_ref, srcs = refs[0], refs[1:1 + n]
    ga_ref, sg_ref, dsts = refs[1 + n], refs[2 + n], refs[3 + n:]

    glr0 = IN_COLS_GLA
    u0 = glr0 + GLA_GATE_RANK
    ga_ref[...] = w_in_t_ref[:glr0, :].T.astype(BF16)
    sg_ref[:, :COL_GLR] = w_in_t_ref[u0:u0 + COL_GLR, :].T.astype(BF16)
    tile = w_in_t_ref[glr0:glr0 + LANES, :].T
    lane = lax.broadcasted_iota(jnp.int32, tile.shape, 1)
    sg_ref[:, COL_GLR:] = jnp.where(lane < GLA_GATE_RANK, tile, 0.0).astype(BF16)

    for src, dst in zip(srcs, dsts):
        cols = src.shape[1]
        dst[:, :cols] = src[...].astype(BF16)
        if dst.shape[1] > cols:
            dst[:, cols:] = jnp.zeros((dst.shape[0], dst.shape[1] - cols), BF16)


def _whole(shape):
    n = len(shape)
    return pl.BlockSpec(shape, lambda *_: (0,) * n)


def _full(shape):
    n = len(shape)
    return pl.BlockSpec(shape, lambda *_: (0,) * n, pipeline_mode=pl.Buffered(1))


def _gla_masks(rows, span):
    i = np.arange(rows)
    same = (i[:, None] // span) == (i[None, :] // span)
    causal = same & (i[None, :] <= i[:, None])
    return (jnp.asarray(causal, BF16), jnp.asarray(same, BF16), jnp.asarray(causal, F32))


def kernel(x_prompt, x_sample, state_gla, state_ffn_conv, p_prompt, p_sample, g_mix_pre, w_in, w_gla_gate, b_gla_gate, g_gla_out, g_sgu_ln, b_sgu_ln, w_spatial, b_spatial, w_out, g_mix_post, g_ffn_pre, w_up, w_conv, b_conv, w_down, g_ffn_post, g_ple_in, w_ple_gate, w_ple, g_ple_post):
    depth = x_prompt.ndim - 2
    assert w_in.shape[0] == 1 and depth == 1
    nb, seq, _ = x_prompt.shape
    ns, dseq, _ = x_sample.shape
    assert seq % PROMPT_TILE == 0 and PROMPT_TILE % GMLP_CHUNK == 0 and seq % FFN_PROMPT_TILE == 0
    assert dseq == 4 and (ns * dseq) % SAMPLE_TILE == 0 and SAMPLE_TILE % GMLP_CHUNK == 0

    w_gate_p = jnp.concatenate(
        [w_gla_gate[0], jnp.zeros((LANES - GLA_GATE_RANK, GLA_KEY), w_gla_gate.dtype)], axis=0).astype(BF16)
    row = lambda a: a.reshape(1, -1)
    g_gla_t = jnp.tile(g_gla_out[0], GLA_HEADS).reshape(1, -1)
    n_cast = CAST_STEPS
    in_cols = w_in.shape[-1]
    assert in_cols == IN_COLS_GLA + GLA_GATE_RANK + COL_GLR
    cast_src = (w_out[0], w_up[0], w_down[0], w_ple_gate[0], w_ple[0])
    cast_cols_in = (D_MODEL, 2 * D_FF, D_MODEL, D_MODEL, D_MODEL)
    cast_cols_out = (D_MODEL + LANES, 2 * D_FF, D_MODEL + LANES, D_MODEL + LANES, D_MODEL + LANES)
    cast_rows = tuple(w.shape[0] for w in cast_src)
    assert all(r % (n_cast * 2 * SUBLANES) == 0 for r in cast_rows + (D_MODEL,))
    slab = lambda r, c: pl.BlockSpec((r // n_cast, c), lambda i: (i, 0))
    w_ga, w_sg, w_out_b, w_up_b, w_down_b, w_pg_b, w_ple_b = pl.pallas_call(
        _cast_kernel,
        grid=(n_cast,),
        in_specs=[pl.BlockSpec((None, in_cols, D_MODEL // n_cast), lambda i: (0, 0, i))]
        + [slab(r, c) for r, c in zip(cast_rows, cast_cols_in)],
        out_specs=[slab(D_MODEL, IN_COLS_GLA), slab(D_MODEL, IN_COLS_SGU)]
        + [slab(r, c) for r, c in zip(cast_rows, cast_cols_out)],
        out_shape=[jax.ShapeDtypeStruct((D_MODEL, IN_COLS_GLA), BF16),
                   jax.ShapeDtypeStruct((D_MODEL, IN_COLS_SGU), BF16)]
        + [jax.ShapeDtypeStruct((r, c), BF16) for r, c in zip(cast_rows, cast_cols_out)],
        compiler_params=pltpu.CompilerParams(
            dimension_semantics=("arbitrary",), vmem_limit_bytes=VMEM_LIMIT_BYTES),
        name="weight_cast",
    )(jnp.swapaxes(w_in, 1, 2), *cast_src)

    def spatial(c):
        i = np.arange(GMLP_CHUNK)
        keep = ((i[:, None] // c) == (i[None, :] // c)) & (i[None, :] <= i[:, None])
        e = jnp.asarray((i[:, None] % c) == np.arange(c)[None, :], w_spatial.dtype)
        hi = lax.Precision.HIGHEST
        wbd = jnp.einsum('ia,hab,jb->hij', e, w_spatial[0][:, :c, :c], e, precision=hi)
        wbd = wbd * jnp.asarray(keep, w_spatial.dtype)
        bias = jnp.einsum('ia,ha->ih', e, b_spatial[0][:, :c], precision=hi)
        return wbd.astype(BF16), jnp.repeat(bias, GMLP_DC, axis=1)

    mixer_weights = lambda wsp, bsp, masks: (
        row(g_mix_pre), w_ga, w_sg, w_gate_p, row(b_gla_gate), g_gla_t, row(g_sgu_ln), row(b_sgu_ln),
        wsp, bsp, w_out_b, row(g_mix_post)) + masks

    def mixer_weight_specs(rows):
        return [
            _full((1, D_MODEL)), _full((D_MODEL, IN_COLS_GLA)), _full((D_MODEL, IN_COLS_SGU)),
            _full((LANES, GLA_KEY)),
            _full((1, GLA_KEY)), _full((1, GLA_WIDTH)), _full((1, GMLP_WIDTH)), _full((1, GMLP_WIDTH)),
            _full((GMLP_HEADS, GMLP_CHUNK, GMLP_CHUNK)), _full((GMLP_CHUNK, GMLP_WIDTH)),
            _full((D_MODEL, D_MODEL + LANES)), _full((1, D_MODEL)),
            _full((rows, rows)), _full((rows, rows)), _full((rows, rows))]

    tl = PROMPT_TILE
    wsp_p, bsp_p = spatial(GMLP_CHUNK)
    npar = MIXER_SEQS_PER_STEP
    assert nb % npar == 0
    x1_p, gla_p = pl.pallas_call(
        functools.partial(_mixer_kernel, rows=tl, sample=False, n_par=npar),
        grid=(nb // npar, seq // tl),
        in_specs=[pl.BlockSpec((npar, None, tl, D_MODEL), lambda b, t: (0, b, t, 0))] + mixer_weight_specs(tl),
        out_specs=[pl.BlockSpec((npar, None, tl, D_MODEL), lambda b, t: (0, b, t, 0)),
                   pl.BlockSpec((npar, None, GLA_HEADS, GLA_DK, GLA_DV), lambda b, t: (0, b, 0, 0, 0))],
        out_shape=[jax.ShapeDtypeStruct((npar, nb // npar, seq, D_MODEL), F32),
                   jax.ShapeDtypeStruct((npar, nb // npar, GLA_HEADS, GLA_DK, GLA_DV), F32)],
        scratch_shapes=[pltpu.VMEM((npar, tl, GLA_WIDTH), F32),
                        pltpu.VMEM((npar, HEAD_PAIRS, 2 * GLA_DV, 2 * GLA_DK), F32)],
        compiler_params=pltpu.CompilerParams(
            dimension_semantics=("arbitrary", "arbitrary"), vmem_limit_bytes=VMEM_LIMIT_BYTES),
        name="mixer_prompt",
    )(x_prompt.reshape(npar, nb // npar, seq, D_MODEL),
      *mixer_weights(wsp_p, bsp_p, _gla_masks(tl, GLA_CHUNK)))
    x1_p = x1_p.reshape(nb, seq, D_MODEL)
    gla_p = gla_p.reshape(nb, GLA_HEADS, GLA_DK, GLA_DV)

    ts = SAMPLE_TILE
    seqs_per_tile = ts // dseq
    wsp_s, bsp_s = spatial(dseq)
    x1_s, gla_s, vrows_s = pl.pallas_call(
        functools.partial(_mixer_kernel, rows=ts, sample=True),
        grid=(ns * dseq // ts,),
        in_specs=[pl.BlockSpec((seqs_per_tile, dseq, D_MODEL), lambda i: (i, 0, 0)),
                  pl.BlockSpec((seqs_per_tile, HEAD_PAIRS, 2 * GLA_DK, GLA_DV), lambda i: (i, 0, 0, 0))]
        + mixer_weight_specs(ts),
        out_specs=[pl.BlockSpec((seqs_per_tile, dseq, D_MODEL), lambda i: (i, 0, 0)),
                   pl.BlockSpec((seqs_per_tile, HEAD_PAIRS, 2 * GLA_DK, GLA_DV), lambda i: (i, 0, 0, 0)),
                   pl.BlockSpec((seqs_per_tile, dseq, GMLP_WIDTH), lambda i: (i, 0, 0))],
        out_shape=[jax.ShapeDtypeStruct((ns, dseq, D_MODEL), F32),
                   jax.ShapeDtypeStruct((ns, HEAD_PAIRS, 2 * GLA_DK, GLA_DV), F32),
                   jax.ShapeDtypeStruct((ns, dseq, GMLP_WIDTH), F32)],
        scratch_shapes=[pltpu.VMEM((ts, GLA_WIDTH), F32)],
        compiler_params=pltpu.CompilerParams(
            dimension_semantics=("arbitrary",), vmem_limit_bytes=VMEM_LIMIT_BYTES),
        name="mixer_sample",
    )(x_sample,
      state_gla[0].reshape(ns, HEAD_PAIRS, 2 * GLA_DK, GLA_DV),
      *mixer_weights(wsp_s, bsp_s, _gla_masks(ts, dseq)))

    ffn_weights = (row(g_ffn_pre), w_up_b, w_conv[0], row(b_conv), w_down_b, row(g_ffn_post),
                   row(g_ple_in), w_pg_b, w_ple_b, row(g_ple_post))
    ffn_weight_specs = [
        _full((1, D_MODEL)), _full((D_MODEL, 2 * D_FF)), _full((CONV_W, 2 * D_FF)), _full((1, 2 * D_FF)),
        _full((D_FF, D_MODEL + LANES)), _full((1, D_MODEL)), _full((1, D_MODEL)),
        _full((D_MODEL, D_MODEL + LANES)), _full((PLE_DIM, D_MODEL + LANES)), _full((1, D_MODEL))]

    tf = FFN_PROMPT_TILE
    fpar = FFN_SEQS_PER_STEP
    assert nb % fpar == 0
    y_p, conv_p = pl.pallas_call(
        functools.partial(_ffn_prompt_kernel, rows=tf, n_par=fpar, skew=FFN_STAGE_SKEW),
        grid=(nb // fpar, seq // tf),
        in_specs=[pl.BlockSpec((fpar, None, tf, D_MODEL), lambda b, t: (0, b, t, 0)),
                  pl.BlockSpec((fpar, None, tf, PLE_DIM), lambda b, t: (0, b, t, 0))] + ffn_weight_specs,
        out_specs=[pl.BlockSpec((fpar, None, tf, D_MODEL), lambda b, t: (0, b, t, 0)),
                   pl.BlockSpec((fpar, None, CONV_W - 1, 2 * D_FF), lambda b, t: (0, b, 0, 0))],
        out_shape=[jax.ShapeDtypeStruct((fpar, nb // fpar, seq, D_MODEL), F32),
                   jax.ShapeDtypeStruct((fpar, nb // fpar, CONV_W - 1, 2 * D_FF), F32)],
        scratch_shapes=[pltpu.VMEM((fpar, CONV_W - 1, SUBLANES, 2 * D_FF), F32),
                        pltpu.VMEM((fpar, 2, 2, CONV_W, tf + 2 * SUBLANES, FF_BLOCK), F32),
                        pltpu.VMEM((fpar, tf, D_FF), BF16)],
        compiler_params=pltpu.CompilerParams(
            dimension_semantics=("arbitrary", "arbitrary"), vmem_limit_bytes=VMEM_LIMIT_BYTES),
        name="ffn_prompt",
    )(x1_p.reshape(fpar, nb // fpar, seq, D_MODEL),
      p_prompt[0].reshape(fpar, nb // fpar, seq, PLE_DIM), *ffn_weights)
    y_p = y_p.reshape(nb, seq, D_MODEL)
    conv_p = conv_p.reshape(nb, CONV_W - 1, 2 * D_FF)

    y_s, conv_s = pl.pallas_call(
        functools.partial(_ffn_sample_kernel, nseq=ns, seq_len=dseq),
        grid=(1,),
        in_specs=[_whole((ns, dseq, D_MODEL)), _whole((ns, dseq, PLE_DIM)),
                  _whole((ns, CONV_W - 1, 2 * D_FF))] + ffn_weight_specs,
        out_specs=[_whole((ns, dseq, D_MODEL)), _whole((ns, CONV_W - 1, 2 * D_FF))],
        out_shape=[jax.ShapeDtypeStruct((ns, dseq, D_MODEL), F32),
                   jax.ShapeDtypeStruct((ns, CONV_W - 1, 2 * D_FF), F32)],
        scratch_shapes=[pltpu.VMEM((ns * dseq, D_FF), BF16)],
        compiler_params=pltpu.CompilerParams(
            dimension_semantics=("arbitrary",), vmem_limit_bytes=VMEM_LIMIT_BYTES),
        name="ffn_sample",
    )(x1_s, p_sample[0], state_ffn_conv[0], *ffn_weights)

    return (y_p,
            y_s,
            gla_p[None],
            gla_s.reshape(1, ns, GLA_HEADS, GLA_DK, GLA_DV),
            conv_p[None],
            conv_s[None],
            vrows_s[None])
```

```python
import functools

import numpy as np
import jax
import jax.numpy as jnp
from jax import lax
from jax.experimental import pallas as pl
from jax.experimental.pallas import tpu as pltpu

D_MODEL = 1024
GLA_HEADS = 4
GLA_DK = 64
GLA_DV = 128
GLA_KEY = GLA_HEADS * GLA_DK
GLA_WIDTH = GLA_HEADS * GLA_DV
GMLP_HEADS = 4
GMLP_DC = 128
GMLP_WIDTH = GMLP_HEADS * GMLP_DC
GLA_GATE_RANK = 16
GLA_TAU = 16.0
GLA_CHUNK = 64
GMLP_CHUNK = 128
D_FF = 2816
CONV_W = 3
PLE_DIM = 256
EPS = 1e-6

LANES = 128
SUBLANES = 8
MXU_N = 256
HEAD_PAIRS = GLA_HEADS // 2

COL_QK = 0
COL_V = COL_QK + 2 * GLA_KEY
COL_R = COL_V + GLA_WIDTH
IN_COLS_GLA = COL_R + GLA_WIDTH
COL_U = 0
COL_VS = COL_U + GMLP_WIDTH
COL_GLR = COL_VS + GMLP_WIDTH
IN_COLS_SGU = COL_GLR + LANES

FF_BLOCK = 256
N_FF_BLOCKS = D_FF // FF_BLOCK
FF_GROUP = 4

PROMPT_TILE = 256
MIXER_SEQS_PER_STEP = 4
FFN_PROMPT_TILE = 256
FFN_SEQS_PER_STEP = 2
SAMPLE_TILE = 128
CAST_STEPS = 8

VMEM_LIMIT_BYTES = 56 * 1024 * 1024

BF16 = jnp.bfloat16
F32 = jnp.float32


def _dot(a, b):
    return jnp.dot(a, b, preferred_element_type=F32)


def _dot_wide(a, w_ref, rows=slice(None)):
    return jnp.concatenate(
        [_dot(a, w_ref[rows, c:c + MXU_N]) for c in range(0, D_MODEL, MXU_N)], axis=-1)


def _dot_nt(a, b):
    return lax.dot_general(a, b, (((1,), (1,)), ((), ())), preferred_element_type=F32)


def _dot_tn(a, b):
    return lax.dot_general(a, b, (((0,), (0,)), ((), ())), preferred_element_type=F32)


def _rms(x, g):
    return x * lax.rsqrt(jnp.mean(x * x, axis=-1, keepdims=True) + EPS) * g


def _gelu_gate(x):
    c = -2.0 * np.sqrt(2.0 / np.pi) * np.log2(np.e)
    a1 = np.float32(c)
    a3 = np.float32(c * 0.044715)
    return 1.0 / (1.0 + jnp.exp2(x * (a3 * (x * x) + a1)))


def _gelu(x):
    return x * _gelu_gate(x)


def _sigmoid(x):
    return 1.0 / (1.0 + jnp.exp(-x))


def _log_sigmoid(x):
    return jnp.minimum(x, 0.0) - jnp.log1p(jnp.exp(-jnp.abs(x)))


def _split_bf16(x):
    hi = x.astype(BF16)
    lo = (x - hi.astype(F32)).astype(BF16)
    return hi, lo


def _mixer_kernel(*refs, rows, sample, n_par=1):
    if sample:
        (x_ref, sin_ref, g_pre_ref, w_ga_ref, w_sg_ref, w_gate_ref, b_gate_ref, g_gla_ref, g_ln_ref, b_ln_ref,
         w_sp_ref, b_sp_ref, w_out_ref, g_post_ref, cum_ref, tot_ref, causal_ref,
         y_ref, sout_ref, vn_ref, o_s) = refs
        xs, ys, os_ = [x_ref], [y_ref], [o_s]
    else:
        (x_ref, g_pre_ref, w_ga_ref, w_sg_ref, w_gate_ref, b_gate_ref, g_gla_ref, g_ln_ref, b_ln_ref,
         w_sp_ref, b_sp_ref, w_out_ref, g_post_ref, cum_ref, tot_ref, causal_ref,
         y_ref, sout_ref, o_s, st_s) = refs
        t = pl.program_id(1)

        @pl.when(t == 0)
        def _():
            st_s[...] = jnp.zeros_like(st_s)

        xs = [x_ref.at[i] for i in range(n_par)]
        ys = [y_ref.at[i] for i in range(n_par)]
        os_ = [o_s.at[i] for i in range(n_par)]

    lane = lax.broadcasted_iota(jnp.int32, (rows, LANES), 1)
    head_lane_masks = (lane < GLA_DK, lane >= GLA_DK)
    tiles = [dict() for _ in range(n_par)]

    def stage_norm(i):
        d = tiles[i]
        d['x'] = xs[i][...].reshape(rows, D_MODEL)
        d['h'] = _rms(d['x'], g_pre_ref[...]).astype(BF16)

    def stage_project(i):
        d = tiles[i]
        h = d['h']
        glr = _dot(h, w_sg_ref[:, COL_GLR:IN_COLS_SGU])
        d['logit'] = _dot(glr.astype(BF16), w_gate_ref[...]) + b_gate_ref[...]
        d['u_pre'] = _dot(h, w_sg_ref[:, COL_U:COL_VS])
        d['vs_pre'] = _dot(h, w_sg_ref[:, COL_VS:COL_GLR])
        d['qk'] = _dot(h, w_ga_ref[:, COL_QK:COL_V])
        d['v'] = _dot(h, w_ga_ref[:, COL_V:COL_R])
        d['r'] = _dot(h, w_ga_ref[:, COL_R:IN_COLS_GLA])

    def stage_gate_sgu(i):
        d = tiles[i]
        d['g'] = _split_bf16(_log_sigmoid(d.pop('logit')) * (1.0 / GLA_TAU))
        d['u'] = _gelu(d.pop('u_pre'))
        vs = _gelu(d.pop('vs_pre'))
        vn_parts = []
        for j in range(GMLP_HEADS):
            vh = vs[:, j * GMLP_DC:(j + 1) * GMLP_DC]
            mu = jnp.mean(vh, axis=-1, keepdims=True)
            c = vh - mu
            var = jnp.mean(c * c, axis=-1, keepdims=True)
            vn_parts.append(c * lax.rsqrt(var + EPS))
        vn = jnp.concatenate(vn_parts, axis=-1) * g_ln_ref[...] + b_ln_ref[...]
        if sample:
            vn_ref[...] = vn.reshape(vn_ref.shape)
        d['vn_bf'] = vn.astype(BF16)

    def stage_mix_sgu(i):
        d = tiles[i]
        g_hi, g_lo = d.pop('g')
        cum = cum_ref[...]
        tot = tot_ref[...]
        d['b'] = _dot(cum, g_hi) + _dot(cum, g_lo)
        d['bl'] = _dot(tot, g_hi) + _dot(tot, g_lo)
        vn_bf = d.pop('vn_bf')
        mixed_parts = []
        for cch in range(rows // GMLP_CHUNK):
            crows = slice(cch * GMLP_CHUNK, (cch + 1) * GMLP_CHUNK)
            mixed_parts.append(jnp.concatenate(
                [_dot(w_sp_ref[j], vn_bf[crows, j * GMLP_DC:(j + 1) * GMLP_DC]) for j in range(GMLP_HEADS)],
                axis=-1) + b_sp_ref[...])
        so = d.pop('u') * jnp.concatenate(mixed_parts, axis=0)
        d['mix'] = _dot_wide(so.astype(BF16), w_out_ref, slice(GLA_WIDTH, GLA_WIDTH + GMLP_WIDTH))

    def stage_decay(i):
        d = tiles[i]
        qk = d.pop('qk')
        q = qk[:, :GLA_KEY] * (GLA_DK ** -0.5)
        k = qk[:, GLA_KEY:]
        b, bl = d.pop('b'), d['bl']
        d['v_bf'] = d['v'].astype(BF16)
        d['qb'] = q * jnp.exp(b)
        d['kn'] = k * jnp.exp(-b)
        d['kd'] = k * jnp.exp(bl - b)

    def stage_intra(i):
        d = tiles[i]
        qb, kn, kd, v, v_bf, bl = d.pop('qb'), d.pop('kn'), d.pop('kd'), d.pop('v'), d['v_bf'], d['bl']
        causal = causal_ref[...]
        d['pairs'] = []
        for p in range(HEAD_PAIRS):
            ls = slice(p * LANES, (p + 1) * LANES)
            qbp = qb[:, ls]
            knp = kn[:, ls].astype(BF16)
            kdp = kd[:, ls]
            o_intra = []
            for hh in range(2):
                hd = 2 * p + hh
                qm = jnp.where(head_lane_masks[hh], qbp, 0.0).astype(BF16)
                s = _dot_nt(qm, knp) * causal
                o_intra.append(_dot(s.astype(BF16), v_bf[:, hd * GLA_DV:(hd + 1) * GLA_DV]))

            if not sample:
                st_shape = (2 * GLA_DV, 2 * GLA_DK)
                ri = lax.broadcasted_iota(jnp.int32, st_shape, 0)
                ci = lax.broadcasted_iota(jnp.int32, st_shape, 1)
                same_head = (ri < GLA_DV) == (ci < GLA_DK)
                kdp_bf = kdp.astype(BF16)
                upd = []
                for c in range(rows // GLA_CHUNK):
                    rws = slice(c * GLA_CHUNK, (c + 1) * GLA_CHUNK)
                    inc = _dot_tn(v_bf[rws, 2 * p * GLA_DV:(2 * p + 2) * GLA_DV], kdp_bf[rws, :])
                    upd.append(jnp.where(same_head, inc, 0.0))
                d['pairs'].append((ls, qbp.astype(BF16), o_intra, upd))
            else:
                seq_len = 4
                seqs_per_grp = SUBLANES // seq_len
                kdp_t = kdp.T.astype(BF16)
                dec_t = jnp.exp(bl[:, ls]).T
                sub = lax.broadcasted_iota(jnp.int32, (SUBLANES, LANES), 0)
                crow = lax.broadcasted_iota(jnp.int32, (rows, LANES), 0)
                for grp in range(rows // SUBLANES):
                    g0 = grp * SUBLANES
                    qg = qbp[g0:g0 + SUBLANES, :]
                    o_acc = [o_intra[hh][g0:g0 + SUBLANES, :] for hh in range(2)]
                    for j in range(seqs_per_grp):
                        sidx = grp * seqs_per_grp + j
                        s_pair = sin_ref[sidx, p]
                        s_bf = s_pair.astype(BF16)
                        in_seq = (sub >= j * seq_len) & (sub < (j + 1) * seq_len)
                        in_seq_c = (crow >= g0 + j * seq_len) & (crow < g0 + (j + 1) * seq_len)
                        new_rows = []
                        for hh in range(2):
                            hd = 2 * p + hh
                            qsel = jnp.where(in_seq & head_lane_masks[hh][:SUBLANES], qg, 0.0).astype(BF16)
                            o_acc[hh] = o_acc[hh] + _dot(qsel, s_bf)
                            vsel = jnp.where(in_seq_c, v[:, hd * GLA_DV:(hd + 1) * GLA_DV], 0.0).astype(BF16)
                            new_rows.append(_dot(kdp_t[hh * GLA_DK:(hh + 1) * GLA_DK, :], vsel))
                        col = g0 + j * seq_len
                        sout_ref[sidx, p] = s_pair * dec_t[:, col:col + 1] + jnp.concatenate(new_rows, axis=0)
                    for hh in range(2):
                        hd = 2 * p + hh
                        os_[i][g0:g0 + SUBLANES, hd * GLA_DV:(hd + 1) * GLA_DV] = o_acc[hh]

    def stage_state(i):
        if sample:
            return
        d = tiles[i]
        bl = d.pop('bl')
        st = [st_s[i, p] for p in range(HEAD_PAIRS)]
        for c in range(rows // GLA_CHUNK):
            rws = slice(c * GLA_CHUNK, (c + 1) * GLA_CHUNK)
            for p, (ls, qbp_bf, o_intra, upd) in enumerate(d['pairs']):
                o_inter = _dot_nt(qbp_bf[rws, :], st[p].astype(BF16))
                for hh in range(2):
                    hd = 2 * p + hh
                    os_[i][rws, hd * GLA_DV:(hd + 1) * GLA_DV] = (
                        o_intra[hh][rws, :] + o_inter[:, hh * GLA_DV:(hh + 1) * GLA_DV])
                dec = jnp.exp(bl[c * GLA_CHUNK:c * GLA_CHUNK + 1, ls])
                st[p] = st[p] * dec + upd[c]
        for p in range(HEAD_PAIRS):
            st_s[i, p] = st[p]
        d.pop('pairs')

    def stage_out(i):
        d = tiles[i]
        r = d.pop('r')
        out_gate = r * _sigmoid(r)
        mix, x = d.pop('mix'), d.pop('x')
        half = rows // 2
        for hr in (slice(0, half), slice(half, rows)):
            o = os_[i][hr, :]
            og = jnp.concatenate(
                [_rms(o[:, j * GLA_DV:(j + 1) * GLA_DV], 1.0) for j in range(GLA_HEADS)], axis=-1)
            og = og * g_gla_ref[...] * out_gate[hr, :]
            mix_h = mix[hr, :] + _dot_wide(og.astype(BF16), w_out_ref, slice(0, GLA_WIDTH))
            y_h = x[hr, :] + _rms(mix_h, g_post_ref[...])
            if sample:
                seq_len = ys[i].shape[1]
                ys[i][hr.start // seq_len:hr.stop // seq_len] = y_h.reshape(-1, seq_len, D_MODEL)
            else:
                ys[i][hr, :] = y_h

    stages = (stage_norm, stage_project, stage_gate_sgu, stage_mix_sgu, stage_decay, stage_intra,
              stage_state, stage_out)
    for stage in stages:
        for i in range(n_par):
            stage(i)

    if not sample:
        @pl.when(t == pl.num_programs(1) - 1)
        def _():
            for i in range(n_par):
                for p in range(HEAD_PAIRS):
                    st = st_s[i, p]
                    sout_ref[i, 2 * p] = st[:GLA_DV, :].T[:GLA_DK, :]
                    sout_ref[i, 2 * p + 1] = st[GLA_DV:, :].T[GLA_DK:, :]


def _ffn_tail(x1, f, p, g_ffn_post, g_ple_in, w_pg_ref, w_ple_ref, g_ple_post):
    x2 = x1 + _rms(f, g_ffn_post)
    hp = _rms(x2, g_ple_in).astype(BF16)
    gate = _sigmoid(_dot_wide(hp, w_pg_ref))
    pe = _dot_wide(p.astype(BF16), w_ple_ref)
    return x2 + _rms(pe * gate, g_ple_post)


def _ff_groups():
    return [range(g, min(g + FF_GROUP, N_FF_BLOCKS)) for g in range(0, N_FF_BLOCKS, FF_GROUP)]


def _ffn_prompt_kernel(x_ref, p_ref, g_pre_ref, w_up_ref, w_conv_ref, b_conv_ref, w_down_ref,
                       g_post_ref, g_ple_in_ref, w_pg_ref, w_ple_ref, g_ple_post_ref,
                       y_ref, cs_ref,
                       carry_s, up_s, act_s, *, rows, n_par):
    t = pl.program_id(1)
    hdr = SUBLANES

    @pl.when(t == 0)
    def _():
        carry_s[...] = jnp.zeros_like(carry_s)
        up_s[:, :, :, :, hdr + rows:, :] = jnp.zeros((n_par, 2, 2, CONV_W, hdr, FF_BLOCK), F32)

    tiles = [dict() for _ in range(n_par)]

    def half_cols(j, half):
        c0 = half * D_FF + j * FF_BLOCK
        return slice(c0, c0 + FF_BLOCK)

    def project(i, j):
        for half in range(2):
            cols = half_cols(j, half)
            up = _dot(tiles[i]['h'], w_up_ref[:, cols])
            for s in range(CONV_W):
                buf = up_s.at[i, j % 2, half, s]
                if s:
                    buf[hdr:2 * hdr, :] = carry_s[i, s - 1, :, cols]
                buf[hdr + s:hdr + s + rows, :] = up
                if s:
                    carry_s[i, s - 1, :, cols] = buf[hdr + rows:2 * hdr + rows, :]

    def conv(i, j, half):
        cols = half_cols(j, half)
        w = w_conv_ref[:, cols]
        taps = [up_s[i, j % 2, half, s, hdr:hdr + rows, :] for s in range(CONV_W)]
        return (b_conv_ref[:, cols] + taps[2] * w[0:1, :] + taps[1] * w[1:2, :] + taps[0] * w[2:3, :])

    def down(i, j):
        k_rows = slice(j * FF_BLOCK, (j + 1) * FF_BLOCK)
        return _dot_wide(act_s[i, :, k_rows], w_down_ref, k_rows)

    def stage_head(i):
        d = tiles[i]
        d['x1'] = x_ref[i]
        d['h'] = _rms(d['x1'], g_pre_ref[...]).astype(BF16)
        project(i, 0)

    def stage_block(i, j):
        d = tiles[i]
        if j + 1 < N_FF_BLOCKS:
            project(i, j + 1)
        if j == 1:
            d['f'] = down(i, 0)
        elif j > 1:
            d['f'] = d['f'] + down(i, j - 1)
        gate = conv(i, j, 0)
        act_s[i, :, j * FF_BLOCK:(j + 1) * FF_BLOCK] = (
            (gate * conv(i, j, 1)) * _gelu_gate(gate)).astype(BF16)

    def stage_tail(i):
        d = tiles[i]
        f = d.pop('f') + down(i, N_FF_BLOCKS - 1)
        d.pop('h')
        y_ref[i] = _ffn_tail(d.pop('x1'), f, p_ref[i], g_post_ref[...], g_ple_in_ref[...], w_pg_ref,
                             w_ple_ref, g_ple_post_ref[...])

    stages = ([stage_head] + [functools.partial(stage_block, j=j) for j in range(N_FF_BLOCKS)]
              + [stage_tail])
    for stage in stages:
        for i in range(n_par):
            stage(i)

    @pl.when(t == pl.num_programs(1) - 1)
    def _():
        for i in range(n_par):
            cs_ref[i] = carry_s[i, CONV_W - 2, 0:CONV_W - 1, :]


def _ffn_sample_kernel(x_ref, p_ref, cin_ref, g_pre_ref, w_up_ref, w_conv_ref, b_conv_ref, w_down_ref,
                       g_post_ref, g_ple_in_ref, w_pg_ref, w_ple_ref, g_ple_post_ref,
                       y_ref, cs_ref, act_s, *, nseq, seq_len):
    x1 = jnp.concatenate([x_ref[:, j, :] for j in range(seq_len)], axis=0)
    p = jnp.concatenate([p_ref[:, j, :] for j in range(seq_len)], axis=0)
    h = _rms(x1, g_pre_ref[...]).astype(BF16)

    def conv_half(col0):
        cols = slice(col0, col0 + FF_BLOCK)
        up = _dot(h, w_up_ref[:, cols])
        blocks = [cin_ref[:, i, cols] for i in range(CONV_W - 1)]
        blocks += [up[j * nseq:(j + 1) * nseq, :] for j in range(seq_len)]
        w = w_conv_ref[:, cols]
        b = b_conv_ref[:, cols]
        conv = jnp.concatenate(
            [b + blocks[j] * w[0:1, :] + blocks[j + 1] * w[1:2, :] + blocks[j + 2] * w[2:3, :]
             for j in range(seq_len)], axis=0)
        for i in range(CONV_W - 1):
            cs_ref[:, i, cols] = blocks[seq_len + i]
        return conv

    f = None
    for grp in _ff_groups():
        for j in grp:
            gate = conv_half(j * FF_BLOCK)
            val = conv_half(D_FF + j * FF_BLOCK)
            act_s[:, j * FF_BLOCK:(j + 1) * FF_BLOCK] = ((gate * val) * _gelu_gate(gate)).astype(BF16)
        k_rows = slice(grp[0] * FF_BLOCK, (grp[-1] + 1) * FF_BLOCK)
        part = _dot_wide(act_s[:, k_rows], w_down_ref, k_rows)
        f = part if f is None else f + part

    y = _ffn_tail(x1, f, p, g_post_ref[...], g_ple_in_ref[...], w_pg_ref, w_ple_ref,
                  g_ple_post_ref[...])
    for j in range(seq_len):
        y_ref[:, j, :] = y[j * nseq:(j + 1) * nseq, :]


def _cast_kernel(*refs):
    n = (len(refs) - 3) // 2
    w_in_t_ref, srcs = refs[0], refs[1:1 + n]
    ga_ref, sg_ref, dsts = refs[1 + n], refs[2 + n], refs[3 + n:]

    glr0 = IN_COLS_GLA
    u0 = glr0 + GLA_GATE_RANK
    ga_ref[...] = w_in_t_ref[:glr0, :].T.astype(BF16)
    sg_ref[:, :COL_GLR] = w_in_t_ref[u0:u0 + COL_GLR, :].T.astype(BF16)
    tile = w_in_t_ref[glr0:glr0 + LANES, :].T
    lane = lax.broadcasted_iota(jnp.int32, tile.shape, 1)
    sg_ref[:, COL_GLR:] = jnp.where(lane < GLA_GATE_RANK, tile, 0.0).astype(BF16)

    for src, dst in zip(srcs, dsts):
        cols = src.shape[1]
        dst[:, :cols] = src[...].astype(BF16)
        if dst.shape[1] > cols:
            dst[:, cols:] = jnp.zeros((dst.shape[0], dst.shape[1] - cols), BF16)


def _whole(shape):
    n = len(shape)
    return pl.BlockSpec(shape, lambda *_: (0,) * n)


def _full(shape):
    n = len(shape)
    return pl.BlockSpec(shape, lambda *_: (0,) * n, pipeline_mode=pl.Buffered(1))


def _gla_masks(rows, span):
    i = np.arange(rows)
    same = (i[:, None] // span) == (i[None, :] // span)
    causal = same & (i[None, :] <= i[:, None])
    return (jnp.asarray(causal, BF16), jnp.asarray(same, BF16), jnp.asarray(causal, F32))


def kernel(x_prompt, x_sample, state_gla, state_ffn_conv, p_prompt, p_sample, g_mix_pre, w_in, w_gla_gate, b_gla_gate, g_gla_out, g_sgu_ln, b_sgu_ln, w_spatial, b_spatial, w_out, g_mix_post, g_ffn_pre, w_up, w_conv, b_conv, w_down, g_ffn_post, g_ple_in, w_ple_gate, w_ple, g_ple_post):
    depth = x_prompt.ndim - 2
    assert w_in.shape[0] == 1 and depth == 1
    nb, seq, _ = x_prompt.shape
    ns, dseq, _ = x_sample.shape
    assert seq % PROMPT_TILE == 0 and PROMPT_TILE % GMLP_CHUNK == 0 and seq % FFN_PROMPT_TILE == 0
    assert dseq == 4 and (ns * dseq) % SAMPLE_TILE == 0 and SAMPLE_TILE % GMLP_CHUNK == 0

    w_gate_p = jnp.concatenate(
        [w_gla_gate[0], jnp.zeros((LANES - GLA_GATE_RANK, GLA_KEY), w_gla_gate.dtype)], axis=0).astype(BF16)
    row = lambda a: a.reshape(1, -1)
    g_gla_t = jnp.tile(g_gla_out[0], GLA_HEADS).reshape(1, -1)
    n_cast = CAST_STEPS
    in_cols = w_in.shape[-1]
    assert in_cols == IN_COLS_GLA + GLA_GATE_RANK + COL_GLR
    cast_src = (w_out[0], w_up[0], w_down[0], w_ple_gate[0], w_ple[0])
    cast_cols_in = (D_MODEL, 2 * D_FF, D_MODEL, D_MODEL, D_MODEL)
    cast_cols_out = (D_MODEL + LANES, 2 * D_FF, D_MODEL + LANES, D_MODEL + LANES, D_MODEL + LANES)
    cast_rows = tuple(w.shape[0] for w in cast_src)
    assert all(r % (n_cast * 2 * SUBLANES) == 0 for r in cast_rows + (D_MODEL,))
    slab = lambda r, c: pl.BlockSpec((r // n_cast, c), lambda i: (i, 0))
    w_ga, w_sg, w_out_b, w_up_b, w_down_b, w_pg_b, w_ple_b = pl.pallas_call(
        _cast_kernel,
        grid=(n_cast,),
        in_specs=[pl.BlockSpec((None, in_cols, D_MODEL // n_cast), lambda i: (0, 0, i))]
        + [slab(r, c) for r, c in zip(cast_rows, cast_cols_in)],
        out_specs=[slab(D_MODEL, IN_COLS_GLA), slab(D_MODEL, IN_COLS_SGU)]
        + [slab(r, c) for r, c in zip(cast_rows, cast_cols_out)],
        out_shape=[jax.ShapeDtypeStruct((D_MODEL, IN_COLS_GLA), BF16),
                   jax.ShapeDtypeStruct((D_MODEL, IN_COLS_SGU), BF16)]
        + [jax.ShapeDtypeStruct((r, c), BF16) for r, c in zip(cast_rows, cast_cols_out)],
        compiler_params=pltpu.CompilerParams(
            dimension_semantics=("arbitrary",), vmem_limit_bytes=VMEM_LIMIT_BYTES),
        name="weight_cast",
    )(jnp.swapaxes(w_in, 1, 2), *cast_src)

    def spatial(c):
        i = np.arange(GMLP_CHUNK)
        keep = ((i[:, None] // c) == (i[None, :] // c)) & (i[None, :] <= i[:, None])
        e = jnp.asarray((i[:, None] % c) == np.arange(c)[None, :], w_spatial.dtype)
        hi = lax.Precision.HIGHEST
        wbd = jnp.einsum('ia,hab,jb->hij', e, w_spatial[0][:, :c, :c], e, precision=hi)
        wbd = wbd * jnp.asarray(keep, w_spatial.dtype)
        bias = jnp.einsum('ia,ha->ih', e, b_spatial[0][:, :c], precision=hi)
        return wbd.astype(BF16), jnp.repeat(bias, GMLP_DC, axis=1)

    mixer_weights = lambda wsp, bsp, masks: (
        row(g_mix_pre), w_ga, w_sg, w_gate_p, row(b_gla_gate), g_gla_t, row(g_sgu_ln), row(b_sgu_ln),
        wsp, bsp, w_out_b, row(g_mix_post)) + masks

    def mixer_weight_specs(rows):
        return [
            _full((1, D_MODEL)), _full((D_MODEL, IN_COLS_GLA)), _full((D_MODEL, IN_COLS_SGU)),
            _full((LANES, GLA_KEY)),
            _full((1, GLA_KEY)), _full((1, GLA_WIDTH)), _full((1, GMLP_WIDTH)), _full((1, GMLP_WIDTH)),
            _full((GMLP_HEADS, GMLP_CHUNK, GMLP_CHUNK)), _full((GMLP_CHUNK, GMLP_WIDTH)),
            _full((D_MODEL, D_MODEL + LANES)), _full((1, D_MODEL)),
            _full((rows, rows)), _full((rows, rows)), _full((rows, rows))]

    tl = PROMPT_TILE
    wsp_p, bsp_p = spatial(GMLP_CHUNK)
    npar = MIXER_SEQS_PER_STEP
    assert nb % npar == 0
    x1_p, gla_p = pl.pallas_call(
        functools.partial(_mixer_kernel, rows=tl, sample=False, n_par=npar),
        grid=(nb // npar, seq // tl),
        in_specs=[pl.BlockSpec((npar, None, tl, D_MODEL), lambda b, t: (0, b, t, 0))] + mixer_weight_specs(tl),
        out_specs=[pl.BlockSpec((npar, None, tl, D_MODEL), lambda b, t: (0, b, t, 0)),
                   pl.BlockSpec((npar, None, GLA_HEADS, GLA_DK, GLA_DV), lambda b, t: (0, b, 0, 0, 0))],
        out_shape=[jax.ShapeDtypeStruct((npar, nb // npar, seq, D_MODEL), F32),
                   jax.ShapeDtypeStruct((npar, nb // npar, GLA_HEADS, GLA_DK, GLA_DV), F32)],
        scratch_shapes=[pltpu.VMEM((npar, tl, GLA_WIDTH), F32),
                        pltpu.VMEM((npar, HEAD_PAIRS, 2 * GLA_DV, 2 * GLA_DK), F32)],
        compiler_params=pltpu.CompilerParams(
            dimension_semantics=("arbitrary", "arbitrary"), vmem_limit_bytes=VMEM_LIMIT_BYTES),
        name="mixer_prompt",
    )(x_prompt.reshape(npar, nb // npar, seq, D_MODEL),
      *mixer_weights(wsp_p, bsp_p, _gla_masks(tl, GLA_CHUNK)))
    x1_p = x1_p.reshape(nb, seq, D_MODEL)
    gla_p = gla_p.reshape(nb, GLA_HEADS, GLA_DK, GLA_DV)

    ts = SAMPLE_TILE
    seqs_per_tile = ts // dseq
    wsp_s, bsp_s = spatial(dseq)
    x1_s, gla_s, vrows_s = pl.pallas_call(
        functools.partial(_mixer_kernel, rows=ts, sample=True),
        grid=(ns * dseq // ts,),
        in_specs=[pl.BlockSpec((seqs_per_tile, dseq, D_MODEL), lambda i: (i, 0, 0)),
                  pl.BlockSpec((seqs_per_tile, HEAD_PAIRS, 2 * GLA_DK, GLA_DV), lambda i: (i, 0, 0, 0))]
        + mixer_weight_specs(ts),
        out_specs=[pl.BlockSpec((seqs_per_tile, dseq, D_MODEL), lambda i: (i, 0, 0)),
                   pl.BlockSpec((seqs_per_tile, HEAD_PAIRS, 2 * GLA_DK, GLA_DV), lambda i: (i, 0, 0, 0)),
                   pl.BlockSpec((seqs_per_tile, dseq, GMLP_WIDTH), lambda i: (i, 0, 0))],
        out_shape=[jax.ShapeDtypeStruct((ns, dseq, D_MODEL), F32),
                   jax.ShapeDtypeStruct((ns, HEAD_PAIRS, 2 * GLA_DK, GLA_DV), F32),
                   jax.ShapeDtypeStruct((ns, dseq, GMLP_WIDTH), F32)],
        scratch_shapes=[pltpu.VMEM((ts, GLA_WIDTH), F32)],
        compiler_params=pltpu.CompilerParams(
            dimension_semantics=("arbitrary",), vmem_limit_bytes=VMEM_LIMIT_BYTES),
        name="mixer_sample",
    )(x_sample,
      state_gla[0].reshape(ns, HEAD_PAIRS, 2 * GLA_DK, GLA_DV),
      *mixer_weights(wsp_s, bsp_s, _gla_masks(ts, dseq)))

    ffn_weights = (row(g_ffn_pre), w_up_b, w_conv[0], row(b_conv), w_down_b, row(g_ffn_post),
                   row(g_ple_in), w_pg_b, w_ple_b, row(g_ple_post))
    ffn_weight_specs = [
        _full((1, D_MODEL)), _full((D_MODEL, 2 * D_FF)), _full((CONV_W, 2 * D_FF)), _full((1, 2 * D_FF)),
        _full((D_FF, D_MODEL + LANES)), _full((1, D_MODEL)), _full((1, D_MODEL)),
        _full((D_MODEL, D_MODEL + LANES)), _full((PLE_DIM, D_MODEL + LANES)), _full((1, D_MODEL))]

    tf = FFN_PROMPT_TILE
    fpar = FFN_SEQS_PER_STEP
    assert nb % fpar == 0
    y_p, conv_p = pl.pallas_call(
        functools.partial(_ffn_prompt_kernel, rows=tf, n_par=fpar),
        grid=(nb // fpar, seq // tf),
        in_specs=[pl.BlockSpec((fpar, None, tf, D_MODEL), lambda b, t: (0, b, t, 0)),
                  pl.BlockSpec((fpar, None, tf, PLE_DIM), lambda b, t: (0, b, t, 0))] + ffn_weight_specs,
        out_specs=[pl.BlockSpec((fpar, None, tf, D_MODEL), lambda b, t: (0, b, t, 0)),
                   pl.BlockSpec((fpar, None, CONV_W - 1, 2 * D_FF), lambda b, t: (0, b, 0, 0))],
        out_shape=[jax.ShapeDtypeStruct((fpar, nb // fpar, seq, D_MODEL), F32),
                   jax.ShapeDtypeStruct((fpar, nb // fpar, CONV_W - 1, 2 * D_FF), F32)],
        scratch_shapes=[pltpu.VMEM((fpar, CONV_W - 1, SUBLANES, 2 * D_FF), F32),
                        pltpu.VMEM((fpar, 2, 2, CONV_W, tf + 2 * SUBLANES, FF_BLOCK), F32),
                        pltpu.VMEM((fpar, tf, D_FF), BF16)],
        compiler_params=pltpu.CompilerParams(
            dimension_semantics=("arbitrary", "arbitrary"), vmem_limit_bytes=VMEM_LIMIT_BYTES),
        name="ffn_prompt",
    )(x1_p.reshape(fpar, nb // fpar, seq, D_MODEL),
      p_prompt[0].reshape(fpar, nb // fpar, seq, PLE_DIM), *ffn_weights)
    y_p = y_p.reshape(nb, seq, D_MODEL)
    conv_p = conv_p.reshape(nb, CONV_W - 1, 2 * D_FF)

    y_s, conv_s = pl.pallas_call(
        functools.partial(_ffn_sample_kernel, nseq=ns, seq_len=dseq),
        grid=(1,),
        in_specs=[_whole((ns, dseq, D_MODEL)), _whole((ns, dseq, PLE_DIM)),
                  _whole((ns, CONV_W - 1, 2 * D_FF))] + ffn_weight_specs,
        out_specs=[_whole((ns, dseq, D_MODEL)), _whole((ns, CONV_W - 1, 2 * D_FF))],
        out_shape=[jax.ShapeDtypeStruct((ns, dseq, D_MODEL), F32),
                   jax.ShapeDtypeStruct((ns, CONV_W - 1, 2 * D_FF), F32)],
        scratch_shapes=[pltpu.VMEM((ns * dseq, D_FF), BF16)],
        compiler_params=pltpu.CompilerParams(
            dimension_semantics=("arbitrary",), vmem_limit_bytes=VMEM_LIMIT_BYTES),
        name="ffn_sample",
    )(x1_s, p_sample[0], state_ffn_conv[0], *ffn_weights)

    return (y_p,
            y_s,
            gla_p[None],
            gla_s.reshape(1, ns, GLA_HEADS, GLA_DK, GLA_DV),
            conv_p[None],
            conv_s[None],
            vrows_s[None])
```

```python
import functools

import numpy as np
import jax
import jax.numpy as jnp
from jax import lax
from jax.experimental import pallas as pl
from jax.experimental.pallas import tpu as pltpu

D_MODEL = 1024
GLA_HEADS = 4
GLA_DK = 64
GLA_DV = 128
GLA_KEY = GLA_HEADS * GLA_DK
GLA_WIDTH = GLA_HEADS * GLA_DV
GMLP_HEADS = 4
GMLP_DC = 128
GMLP_WIDTH = GMLP_HEADS * GMLP_DC
GLA_GATE_RANK = 16
GLA_TAU = 16.0
GLA_CHUNK = 64
GMLP_CHUNK = 128
D_FF = 2816
CONV_W = 3
PLE_DIM = 256
EPS = 1e-6

LANES = 128
SUBLANES = 8
MXU_N = 256
HEAD_PAIRS = GLA_HEADS // 2

COL_QK = 0
COL_V = COL_QK + 2 * GLA_KEY
COL_R = COL_V + GLA_WIDTH
IN_COLS_GLA = COL_R + GLA_WIDTH
COL_U = 0
COL_VS = COL_U + GMLP_WIDTH
COL_GLR = COL_VS + GMLP_WIDTH
IN_COLS_SGU = COL_GLR + LANES

FF_BLOCK = 256
N_FF_BLOCKS = D_FF // FF_BLOCK
FF_GROUP = 4

PROMPT_TILE = 256
MIXER_SEQS_PER_STEP = 4
FFN_PROMPT_TILE = 256
FFN_SEQS_PER_STEP = 2
SAMPLE_TILE = 128
CAST_STEPS = 8

VMEM_LIMIT_BYTES = 56 * 1024 * 1024

BF16 = jnp.bfloat16
F32 = jnp.float32


def _dot(a, b):
    return jnp.dot(a, b, preferred_element_type=F32)


def _dot_wide(a, w_ref, rows=slice(None)):
    return jnp.concatenate(
        [_dot(a, w_ref[rows, c:c + MXU_N]) for c in range(0, D_MODEL, MXU_N)], axis=-1)


def _dot_nt(a, b):
    return lax.dot_general(a, b, (((1,), (1,)), ((), ())), preferred_element_type=F32)


def _dot_tn(a, b):
    return lax.dot_general(a, b, (((0,), (0,)), ((), ())), preferred_element_type=F32)


def _rms(x, g):
    return x * lax.rsqrt(jnp.mean(x * x, axis=-1, keepdims=True) + EPS) * g


def _gelu_gate(x):
    c = -2.0 * np.sqrt(2.0 / np.pi) * np.log2(np.e)
    a1 = np.float32(c)
    a3 = np.float32(c * 0.044715)
    return 1.0 / (1.0 + jnp.exp2(x * (a3 * (x * x) + a1)))


def _gelu(x):
    return x * _gelu_gate(x)


def _sigmoid(x):
    return 1.0 / (1.0 + jnp.exp(-x))


def _log_sigmoid(x):
    return jnp.minimum(x, 0.0) - jnp.log1p(jnp.exp(-jnp.abs(x)))


def _split_bf16(x):
    hi = x.astype(BF16)
    lo = (x - hi.astype(F32)).astype(BF16)
    return hi, lo


def _mixer_kernel(*refs, rows, sample, n_par=1):
    if sample:
        (x_ref, sin_ref, g_pre_ref, w_ga_ref, w_sg_ref, w_gate_ref, b_gate_ref, g_gla_ref, g_ln_ref, b_ln_ref,
         w_sp_ref, b_sp_ref, w_out_ref, g_post_ref, cum_ref, tot_ref, causal_ref,
         y_ref, sout_ref, vn_ref, o_s) = refs
        xs, ys, os_ = [x_ref], [y_ref], [o_s]
    else:
        (x_ref, g_pre_ref, w_ga_ref, w_sg_ref, w_gate_ref, b_gate_ref, g_gla_ref, g_ln_ref, b_ln_ref,
         w_sp_ref, b_sp_ref, w_out_ref, g_post_ref, cum_ref, tot_ref, causal_ref,
         y_ref, sout_ref, o_s, st_s) = refs
        t = pl.program_id(1)

        @pl.when(t == 0)
        def _():
            st_s[...] = jnp.zeros_like(st_s)

        xs = [x_ref.at[i] for i in range(n_par)]
        ys = [y_ref.at[i] for i in range(n_par)]
        os_ = [o_s.at[i] for i in range(n_par)]

    lane = lax.broadcasted_iota(jnp.int32, (rows, LANES), 1)
    head_lane_masks = (lane < GLA_DK, lane >= GLA_DK)
    tiles = [dict() for _ in range(n_par)]

    def stage_norm(i):
        d = tiles[i]
        d['x'] = xs[i][...].reshape(rows, D_MODEL)
        d['h'] = _rms(d['x'], g_pre_ref[...]).astype(BF16)

    def stage_project(i):
        d = tiles[i]
        h = d['h']
        glr = _dot(h, w_sg_ref[:, COL_GLR:IN_COLS_SGU])
        d['logit'] = _dot(glr.astype(BF16), w_gate_ref[...]) + b_gate_ref[...]
        d['u_pre'] = _dot(h, w_sg_ref[:, COL_U:COL_VS])
        d['vs_pre'] = _dot(h, w_sg_ref[:, COL_VS:COL_GLR])
        d['qk'] = _dot(h, w_ga_ref[:, COL_QK:COL_V])
        d['v'] = _dot(h, w_ga_ref[:, COL_V:COL_R])
        d['r'] = _dot(h, w_ga_ref[:, COL_R:IN_COLS_GLA])

    def stage_gate_sgu(i):
        d = tiles[i]
        d['g'] = _split_bf16(_log_sigmoid(d.pop('logit')) * (1.0 / GLA_TAU))
        d['u'] = _gelu(d.pop('u_pre'))
        vs = _gelu(d.pop('vs_pre'))
        vn_parts = []
        for j in range(GMLP_HEADS):
            vh = vs[:, j * GMLP_DC:(j + 1) * GMLP_DC]
            mu = jnp.mean(vh, axis=-1, keepdims=True)
            c = vh - mu
            var = jnp.mean(c * c, axis=-1, keepdims=True)
            vn_parts.append(c * lax.rsqrt(var + EPS))
        vn = jnp.concatenate(vn_parts, axis=-1) * g_ln_ref[...] + b_ln_ref[...]
        if sample:
            vn_ref[...] = vn.reshape(vn_ref.shape)
        d['vn_bf'] = vn.astype(BF16)

    def stage_mix_sgu(i):
        d = tiles[i]
        g_hi, g_lo = d.pop('g')
        cum = cum_ref[...]
        tot = tot_ref[...]
        d['b'] = _dot(cum, g_hi) + _dot(cum, g_lo)
        d['bl'] = _dot(tot, g_hi) + _dot(tot, g_lo)
        vn_bf = d.pop('vn_bf')
        mixed_parts = []
        for cch in range(rows // GMLP_CHUNK):
            crows = slice(cch * GMLP_CHUNK, (cch + 1) * GMLP_CHUNK)
            mixed_parts.append(jnp.concatenate(
                [_dot(w_sp_ref[j], vn_bf[crows, j * GMLP_DC:(j + 1) * GMLP_DC]) for j in range(GMLP_HEADS)],
                axis=-1) + b_sp_ref[...])
        so = d.pop('u') * jnp.concatenate(mixed_parts, axis=0)
        d['mix'] = _dot_wide(so.astype(BF16), w_out_ref, slice(GLA_WIDTH, GLA_WIDTH + GMLP_WIDTH))

    def stage_decay(i):
        d = tiles[i]
        qk = d.pop('qk')
        q = qk[:, :GLA_KEY] * (GLA_DK ** -0.5)
        k = qk[:, GLA_KEY:]
        b, bl = d.pop('b'), d['bl']
        d['v_bf'] = d['v'].astype(BF16)
        d['qb'] = q * jnp.exp(b)
        d['kd'] = k * jnp.exp(bl - b)
        if sample:
            d['qs'] = d['qb']
            d['kn'] = k * jnp.exp(-b)
        else:
            mid = jnp.concatenate(
                [jnp.broadcast_to(b[c * GLA_CHUNK + GLA_CHUNK // 2 - 1:c * GLA_CHUNK + GLA_CHUNK // 2, :],
                                  (GLA_CHUNK, GLA_KEY)) for c in range(rows // GLA_CHUNK)], axis=0)
            d['qs'] = q * jnp.exp(b - mid)
            d['kn'] = k * jnp.exp(mid - b)

    def stage_intra(i):
        d = tiles[i]
        qb, kn, kd, v, v_bf, bl = d.pop('qb'), d.pop('kn'), d.pop('kd'), d.pop('v'), d['v_bf'], d['bl']
        qs = d.pop('qs')
        causal = causal_ref[...]
        d['pairs'] = []
        for p in range(HEAD_PAIRS):
            ls = slice(p * LANES, (p + 1) * LANES)
            qbp = qb[:, ls]
            qsp = qs[:, ls]
            knp = kn[:, ls].astype(BF16)
            kdp = kd[:, ls]
            o_intra = []
            for hh in range(2):
                hd = 2 * p + hh
                qm = jnp.where(head_lane_masks[hh], qsp, 0.0).astype(BF16)
                s = _dot_nt(qm, knp) * causal
                o_intra.append(_dot(s.astype(BF16), v_bf[:, hd * GLA_DV:(hd + 1) * GLA_DV]))

            if not sample:
                st_shape = (2 * GLA_DV, 2 * GLA_DK)
                ri = lax.broadcasted_iota(jnp.int32, st_shape, 0)
                ci = lax.broadcasted_iota(jnp.int32, st_shape, 1)
                same_head = (ri < GLA_DV) == (ci < GLA_DK)
                kdp_bf = kdp.astype(BF16)
                upd = []
                for c in range(rows // GLA_CHUNK):
                    rws = slice(c * GLA_CHUNK, (c + 1) * GLA_CHUNK)
                    inc = _dot_tn(v_bf[rws, 2 * p * GLA_DV:(2 * p + 2) * GLA_DV], kdp_bf[rws, :])
                    upd.append(jnp.where(same_head, inc, 0.0))
                d['pairs'].append((ls, qbp.astype(BF16), o_intra, upd))
            else:
                seq_len = 4
                seqs_per_grp = SUBLANES // seq_len
                kdp_t = kdp.T.astype(BF16)
                dec_t = jnp.exp(bl[:, ls]).T
                sub = lax.broadcasted_iota(jnp.int32, (SUBLANES, LANES), 0)
                crow = lax.broadcasted_iota(jnp.int32, (rows, LANES), 0)
                for grp in range(rows // SUBLANES):
                    g0 = grp * SUBLANES
                    qg = qbp[g0:g0 + SUBLANES, :]
                    o_acc = [o_intra[hh][g0:g0 + SUBLANES, :] for hh in range(2)]
                    for j in range(seqs_per_grp):
                        sidx = grp * seqs_per_grp + j
                        s_pair = sin_ref[sidx, p]
                        s_bf = s_pair.astype(BF16)
                        in_seq = (sub >= j * seq_len) & (sub < (j + 1) * seq_len)
                        in_seq_c = (crow >= g0 + j * seq_len) & (crow < g0 + (j + 1) * seq_len)
                        new_rows = []
                        for hh in range(2):
                            hd = 2 * p + hh
                            qsel = jnp.where(in_seq & head_lane_masks[hh][:SUBLANES], qg, 0.0).astype(BF16)
                            o_acc[hh] = o_acc[hh] + _dot(qsel, s_bf)
                            vsel = jnp.where(in_seq_c, v[:, hd * GLA_DV:(hd + 1) * GLA_DV], 0.0).astype(BF16)
                            new_rows.append(_dot(kdp_t[hh * GLA_DK:(hh + 1) * GLA_DK, :], vsel))
                        col = g0 + j * seq_len
                        sout_ref[sidx, p] = s_pair * dec_t[:, col:col + 1] + jnp.concatenate(new_rows, axis=0)
                    for hh in range(2):
                        hd = 2 * p + hh
                        os_[i][g0:g0 + SUBLANES, hd * GLA_DV:(hd + 1) * GLA_DV] = o_acc[hh]

    def stage_state(i):
        if sample:
            return
        d = tiles[i]
        bl = d.pop('bl')
        st = [st_s[i, p] for p in range(HEAD_PAIRS)]
        for c in range(rows // GLA_CHUNK):
            rws = slice(c * GLA_CHUNK, (c + 1) * GLA_CHUNK)
            for p, (ls, qbp_bf, o_intra, upd) in enumerate(d['pairs']):
                o_inter = _dot_nt(qbp_bf[rws, :], st[p].astype(BF16))
                for hh in range(2):
                    hd = 2 * p + hh
                    os_[i][rws, hd * GLA_DV:(hd + 1) * GLA_DV] = (
                        o_intra[hh][rws, :] + o_inter[:, hh * GLA_DV:(hh + 1) * GLA_DV])
                dec = jnp.exp(bl[c * GLA_CHUNK:c * GLA_CHUNK + 1, ls])
                st[p] = st[p] * dec + upd[c]
        for p in range(HEAD_PAIRS):
            st_s[i, p] = st[p]
        d.pop('pairs')

    def stage_out(i):
        d = tiles[i]
        r = d.pop('r')
        out_gate = r * _sigmoid(r)
        mix, x = d.pop('mix'), d.pop('x')
        half = rows // 2
        for hr in (slice(0, half), slice(half, rows)):
            o = os_[i][hr, :]
            og = jnp.concatenate(
                [_rms(o[:, j * GLA_DV:(j + 1) * GLA_DV], 1.0) for j in range(GLA_HEADS)], axis=-1)
            og = og * g_gla_ref[...] * out_gate[hr, :]
            mix_h = mix[hr, :] + _dot_wide(og.astype(BF16), w_out_ref, slice(0, GLA_WIDTH))
            y_h = x[hr, :] + _rms(mix_h, g_post_ref[...])
            if sample:
                seq_len = ys[i].shape[1]
                ys[i][hr.start // seq_len:hr.stop // seq_len] = y_h.reshape(-1, seq_len, D_MODEL)
            else:
                ys[i][hr, :] = y_h

    stages = (stage_norm, stage_project, stage_gate_sgu, stage_mix_sgu, stage_decay, stage_intra,
              stage_state, stage_out)
    for stage in stages:
        for i in range(n_par):
            stage(i)

    if not sample:
        @pl.when(t == pl.num_programs(1) - 1)
        def _():
            for i in range(n_par):
                for p in range(HEAD_PAIRS):
                    st = st_s[i, p]
                    sout_ref[i, 2 * p] = st[:GLA_DV, :].T[:GLA_DK, :]
                    sout_ref[i, 2 * p + 1] = st[GLA_DV:, :].T[GLA_DK:, :]


def _ffn_tail(x1, f, p, g_ffn_post, g_ple_in, w_pg_ref, w_ple_ref, g_ple_post):
    x2 = x1 + _rms(f, g_ffn_post)
    hp = _rms(x2, g_ple_in).astype(BF16)
    gate = _sigmoid(_dot_wide(hp, w_pg_ref))
    pe = _dot_wide(p.astype(BF16), w_ple_ref)
    return x2 + _rms(pe * gate, g_ple_post)


def _ff_groups():
    return [range(g, min(g + FF_GROUP, N_FF_BLOCKS)) for g in range(0, N_FF_BLOCKS, FF_GROUP)]


def _ffn_prompt_kernel(x_ref, p_ref, g_pre_ref, w_up_ref, w_conv_ref, b_conv_ref, w_down_ref,
                       g_post_ref, g_ple_in_ref, w_pg_ref, w_ple_ref, g_ple_post_ref,
                       y_ref, cs_ref,
                       carry_s, up_s, act_s, *, rows, n_par):
    t = pl.program_id(1)
    hdr = SUBLANES

    @pl.when(t == 0)
    def _():
        carry_s[...] = jnp.zeros_like(carry_s)
        up_s[:, :, :, :, hdr + rows:, :] = jnp.zeros((n_par, 2, 2, CONV_W, hdr, FF_BLOCK), F32)

    tiles = [dict() for _ in range(n_par)]

    def half_cols(j, half):
        c0 = half * D_FF + j * FF_BLOCK
        return slice(c0, c0 + FF_BLOCK)

    def project(i, j):
        for half in range(2):
            cols = half_cols(j, half)
            up = _dot(tiles[i]['h'], w_up_ref[:, cols])
            for s in range(CONV_W):
                buf = up_s.at[i, j % 2, half, s]
                if s:
                    buf[hdr:2 * hdr, :] = carry_s[i, s - 1, :, cols]
                buf[hdr + s:hdr + s + rows, :] = up
                if s:
                    carry_s[i, s - 1, :, cols] = buf[hdr + rows:2 * hdr + rows, :]

    def conv(i, j, half):
        cols = half_cols(j, half)
        w = w_conv_ref[:, cols]
        taps = [up_s[i, j % 2, half, s, hdr:hdr + rows, :] for s in range(CONV_W)]
        return (b_conv_ref[:, cols] + taps[2] * w[0:1, :] + taps[1] * w[1:2, :] + taps[0] * w[2:3, :])

    def down(i, j):
        k_rows = slice(j * FF_BLOCK, (j + 1) * FF_BLOCK)
        return _dot_wide(act_s[i, :, k_rows], w_down_ref, k_rows)

    def stage_head(i):
        d = tiles[i]
        d['x1'] = x_ref[i]
        d['h'] = _rms(d['x1'], g_pre_ref[...]).astype(BF16)
        project(i, 0)

    def stage_block(i, j):
        d = tiles[i]
        if j + 1 < N_FF_BLOCKS:
            project(i, j + 1)
        if j == 1:
            d['f'] = down(i, 0)
        elif j > 1:
            d['f'] = d['f'] + down(i, j - 1)
        gate = conv(i, j, 0)
        act_s[i, :, j * FF_BLOCK:(j + 1) * FF_BLOCK] = (
            (gate * conv(i, j, 1)) * _gelu_gate(gate)).astype(BF16)

    def stage_tail(i):
        d = tiles[i]
        f = d.pop('f') + down(i, N_FF_BLOCKS - 1)
        d.pop('h')
        y_ref[i] = _ffn_tail(d.pop('x1'), f, p_ref[i], g_post_ref[...], g_ple_in_ref[...], w_pg_ref,
                             w_ple_ref, g_ple_post_ref[...])

    stages = ([stage_head] + [functools.partial(stage_block, j=j) for j in range(N_FF_BLOCKS)]
              + [stage_tail])
    for stage in stages:
        for i in range(n_par):
            stage(i)

    @pl.when(t == pl.num_programs(1) - 1)
    def _():
        for i in range(n_par):
            cs_ref[i] = carry_s[i, CONV_W - 2, 0:CONV_W - 1, :]


def _ffn_sample_kernel(x_ref, p_ref, cin_ref, g_pre_ref, w_up_ref, w_conv_ref, b_conv_ref, w_down_ref,
                       g_post_ref, g_ple_in_ref, w_pg_ref, w_ple_ref, g_ple_post_ref,
                       y_ref, cs_ref, act_s, *, nseq, seq_len):
    x1 = jnp.concatenate([x_ref[:, j, :] for j in range(seq_len)], axis=0)
    p = jnp.concatenate([p_ref[:, j, :] for j in range(seq_len)], axis=0)
    h = _rms(x1, g_pre_ref[...]).astype(BF16)

    def conv_half(col0):
        cols = slice(col0, col0 + FF_BLOCK)
        up = _dot(h, w_up_ref[:, cols])
        blocks = [cin_ref[:, i, cols] for i in range(CONV_W - 1)]
        blocks += [up[j * nseq:(j + 1) * nseq, :] for j in range(seq_len)]
        w = w_conv_ref[:, cols]
        b = b_conv_ref[:, cols]
        conv = jnp.concatenate(
            [b + blocks[j] * w[0:1, :] + blocks[j + 1] * w[1:2, :] + blocks[j + 2] * w[2:3, :]
             for j in range(seq_len)], axis=0)
        for i in range(CONV_W - 1):
            cs_ref[:, i, cols] = blocks[seq_len + i]
        return conv

    f = None
    for grp in _ff_groups():
        for j in grp:
            gate = conv_half(j * FF_BLOCK)
            val = conv_half(D_FF + j * FF_BLOCK)
            act_s[:, j * FF_BLOCK:(j + 1) * FF_BLOCK] = ((gate * val) * _gelu_gate(gate)).astype(BF16)
        k_rows = slice(grp[0] * FF_BLOCK, (grp[-1] + 1) * FF_BLOCK)
        part = _dot_wide(act_s[:, k_rows], w_down_ref, k_rows)
        f = part if f is None else f + part

    y = _ffn_tail(x1, f, p, g_post_ref[...], g_ple_in_ref[...], w_pg_ref, w_ple_ref,
                  g_ple_post_ref[...])
    for j in range(seq_len):
        y_ref[:, j, :] = y[j * nseq:(j + 1) * nseq, :]


def _cast_kernel(*refs):
    n = (len(refs) - 3) // 2
    w_in_t_ref, srcs = refs[0], refs[1:1 + n]
    ga_ref, sg_ref, dsts = refs[1 + n], refs[2 + n], refs[3 + n:]

    glr0 = IN_COLS_GLA
    u0 = glr0 + GLA_GATE_RANK
    ga_ref[...] = w_in_t_ref[:glr0, :].T.astype(BF16)
    sg_ref[:, :COL_GLR] = w_in_t_ref[u0:u0 + COL_GLR, :].T.astype(BF16)
    tile = w_in_t_ref[glr0:glr0 + LANES, :].T
    lane = lax.broadcasted_iota(jnp.int32, tile.shape, 1)
    sg_ref[:, COL_GLR:] = jnp.where(lane < GLA_GATE_RANK, tile, 0.0).astype(BF16)

    for src, dst in zip(srcs, dsts):
        cols = src.shape[1]
        dst[:, :cols] = src[...].astype(BF16)
        if dst.shape[1] > cols:
            dst[:, cols:] = jnp.zeros((dst.shape[0], dst.shape[1] - cols), BF16)


def _whole(shape):
    n = len(shape)
    return pl.BlockSpec(shape, lambda *_: (0,) * n)


def _full(shape):
    n = len(shape)
    return pl.BlockSpec(shape, lambda *_: (0,) * n, pipeline_mode=pl.Buffered(1))


def _gla_masks(rows, span):
    i = np.arange(rows)
    same = (i[:, None] // span) == (i[None, :] // span)
    causal = same & (i[None, :] <= i[:, None])
    return (jnp.asarray(causal, BF16), jnp.asarray(same, BF16), jnp.asarray(causal, F32))


def kernel(x_prompt, x_sample, state_gla, state_ffn_conv, p_prompt, p_sample, g_mix_pre, w_in, w_gla_gate, b_gla_gate, g_gla_out, g_sgu_ln, b_sgu_ln, w_spatial, b_spatial, w_out, g_mix_post, g_ffn_pre, w_up, w_conv, b_conv, w_down, g_ffn_post, g_ple_in, w_ple_gate, w_ple, g_ple_post):
    depth = x_prompt.ndim - 2
    assert w_in.shape[0] == 1 and depth == 1
    nb, seq, _ = x_prompt.shape
    ns, dseq, _ = x_sample.shape
    assert seq % PROMPT_TILE == 0 and PROMPT_TILE % GMLP_CHUNK == 0 and seq % FFN_PROMPT_TILE == 0
    assert dseq == 4 and (ns * dseq) % SAMPLE_TILE == 0 and SAMPLE_TILE % GMLP_CHUNK == 0

    w_gate_p = jnp.concatenate(
        [w_gla_gate[0], jnp.zeros((LANES - GLA_GATE_RANK, GLA_KEY), w_gla_gate.dtype)], axis=0).astype(BF16)
    row = lambda a: a.reshape(1, -1)
    g_gla_t = jnp.tile(g_gla_out[0], GLA_HEADS).reshape(1, -1)
    n_cast = CAST_STEPS
    in_cols = w_in.shape[-1]
    assert in_cols == IN_COLS_GLA + GLA_GATE_RANK + COL_GLR
    cast_src = (w_out[0], w_up[0], w_down[0], w_ple_gate[0], w_ple[0])
    cast_cols_in = (D_MODEL, 2 * D_FF, D_MODEL, D_MODEL, D_MODEL)
    cast_cols_out = (D_MODEL + LANES, 2 * D_FF, D_MODEL + LANES, D_MODEL + LANES, D_MODEL + LANES)
    cast_rows = tuple(w.shape[0] for w in cast_src)
    assert all(r % (n_cast * 2 * SUBLANES) == 0 for r in cast_rows + (D_MODEL,))
    slab = lambda r, c: pl.BlockSpec((r // n_cast, c), lambda i: (i, 0))
    w_ga, w_sg, w_out_b, w_up_b, w_down_b, w_pg_b, w_ple_b = pl.pallas_call(
        _cast_kernel,
        grid=(n_cast,),
        in_specs=[pl.BlockSpec((None, in_cols, D_MODEL // n_cast), lambda i: (0, 0, i))]
        + [slab(r, c) for r, c in zip(cast_rows, cast_cols_in)],
        out_specs=[slab(D_MODEL, IN_COLS_GLA), slab(D_MODEL, IN_COLS_SGU)]
        + [slab(r, c) for r, c in zip(cast_rows, cast_cols_out)],
        out_shape=[jax.ShapeDtypeStruct((D_MODEL, IN_COLS_GLA), BF16),
                   jax.ShapeDtypeStruct((D_MODEL, IN_COLS_SGU), BF16)]
        + [jax.ShapeDtypeStruct((r, c), BF16) for r, c in zip(cast_rows, cast_cols_out)],
        compiler_params=pltpu.CompilerParams(
            dimension_semantics=("arbitrary",), vmem_limit_bytes=VMEM_LIMIT_BYTES),
        name="weight_cast",
    )(jnp.swapaxes(w_in, 1, 2), *cast_src)

    def spatial(c):
        i = np.arange(GMLP_CHUNK)
        keep = ((i[:, None] // c) == (i[None, :] // c)) & (i[None, :] <= i[:, None])
        e = jnp.asarray((i[:, None] % c) == np.arange(c)[None, :], w_spatial.dtype)
        hi = lax.Precision.HIGHEST
        wbd = jnp.einsum('ia,hab,jb->hij', e, w_spatial[0][:, :c, :c], e, precision=hi)
        wbd = wbd * jnp.asarray(keep, w_spatial.dtype)
        bias = jnp.einsum('ia,ha->ih', e, b_spatial[0][:, :c], precision=hi)
        return wbd.astype(BF16), jnp.repeat(bias, GMLP_DC, axis=1)

    mixer_weights = lambda wsp, bsp, masks: (
        row(g_mix_pre), w_ga, w_sg, w_gate_p, row(b_gla_gate), g_gla_t, row(g_sgu_ln), row(b_sgu_ln),
        wsp, bsp, w_out_b, row(g_mix_post)) + masks

    def mixer_weight_specs(rows):
        return [
            _full((1, D_MODEL)), _full((D_MODEL, IN_COLS_GLA)), _full((D_MODEL, IN_COLS_SGU)),
            _full((LANES, GLA_KEY)),
            _full((1, GLA_KEY)), _full((1, GLA_WIDTH)), _full((1, GMLP_WIDTH)), _full((1, GMLP_WIDTH)),
            _full((GMLP_HEADS, GMLP_CHUNK, GMLP_CHUNK)), _full((GMLP_CHUNK, GMLP_WIDTH)),
            _full((D_MODEL, D_MODEL + LANES)), _full((1, D_MODEL)),
            _full((rows, rows)), _full((rows, rows)), _full((rows, rows))]

    tl = PROMPT_TILE
    wsp_p, bsp_p = spatial(GMLP_CHUNK)
    npar = MIXER_SEQS_PER_STEP
    assert nb % npar == 0
    x1_p, gla_p = pl.pallas_call(
        functools.partial(_mixer_kernel, rows=tl, sample=False, n_par=npar),
        grid=(nb // npar, seq // tl),
        in_specs=[pl.BlockSpec((npar, None, tl, D_MODEL), lambda b, t: (0, b, t, 0))] + mixer_weight_specs(tl),
        out_specs=[pl.BlockSpec((npar, None, tl, D_MODEL), lambda b, t: (0, b, t, 0)),
                   pl.BlockSpec((npar, None, GLA_HEADS, GLA_DK, GLA_DV), lambda b, t: (0, b, 0, 0, 0))],
        out_shape=[jax.ShapeDtypeStruct((npar, nb // npar, seq, D_MODEL), F32),
                   jax.ShapeDtypeStruct((npar, nb // npar, GLA_HEADS, GLA_DK, GLA_DV), F32)],
        scratch_shapes=[pltpu.VMEM((npar, tl, GLA_WIDTH), F32),
                        pltpu.VMEM((npar, HEAD_PAIRS, 2 * GLA_DV, 2 * GLA_DK), F32)],
        compiler_params=pltpu.CompilerParams(
            dimension_semantics=("arbitrary", "arbitrary"), vmem_limit_bytes=VMEM_LIMIT_BYTES),
        name="mixer_prompt",
    )(x_prompt.reshape(npar, nb // npar, seq, D_MODEL),
      *mixer_weights(wsp_p, bsp_p, _gla_masks(tl, GLA_CHUNK)))
    x1_p = x1_p.reshape(nb, seq, D_MODEL)
    gla_p = gla_p.reshape(nb, GLA_HEADS, GLA_DK, GLA_DV)

    ts = SAMPLE_TILE
    seqs_per_tile = ts // dseq
    wsp_s, bsp_s = spatial(dseq)
    x1_s, gla_s, vrows_s = pl.pallas_call(
        functools.partial(_mixer_kernel, rows=ts, sample=True),
        grid=(ns * dseq // ts,),
        in_specs=[pl.BlockSpec((seqs_per_tile, dseq, D_MODEL), lambda i: (i, 0, 0)),
                  pl.BlockSpec((seqs_per_tile, HEAD_PAIRS, 2 * GLA_DK, GLA_DV), lambda i: (i, 0, 0, 0))]
        + mixer_weight_specs(ts),
        out_specs=[pl.BlockSpec((seqs_per_tile, dseq, D_MODEL), lambda i: (i, 0, 0)),
                   pl.BlockSpec((seqs_per_tile, HEAD_PAIRS, 2 * GLA_DK, GLA_DV), lambda i: (i, 0, 0, 0)),
                   pl.BlockSpec((seqs_per_tile, dseq, GMLP_WIDTH), lambda i: (i, 0, 0))],
        out_shape=[jax.ShapeDtypeStruct((ns, dseq, D_MODEL), F32),
                   jax.ShapeDtypeStruct((ns, HEAD_PAIRS, 2 * GLA_DK, GLA_DV), F32),
                   jax.ShapeDtypeStruct((ns, dseq, GMLP_WIDTH), F32)],
        scratch_shapes=[pltpu.VMEM((ts, GLA_WIDTH), F32)],
        compiler_params=pltpu.CompilerParams(
            dimension_semantics=("arbitrary",), vmem_limit_bytes=VMEM_LIMIT_BYTES),
        name="mixer_sample",
    )(x_sample,
      state_gla[0].reshape(ns, HEAD_PAIRS, 2 * GLA_DK, GLA_DV),
      *mixer_weights(wsp_s, bsp_s, _gla_masks(ts, dseq)))

    ffn_weights = (row(g_ffn_pre), w_up_b, w_conv[0], row(b_conv), w_down_b, row(g_ffn_post),
                   row(g_ple_in), w_pg_b, w_ple_b, row(g_ple_post))
    ffn_weight_specs = [
        _full((1, D_MODEL)), _full((D_MODEL, 2 * D_FF)), _full((CONV_W, 2 * D_FF)), _full((1, 2 * D_FF)),
        _full((D_FF, D_MODEL + LANES)), _full((1, D_MODEL)), _full((1, D_MODEL)),
        _full((D_MODEL, D_MODEL + LANES)), _full((PLE_DIM, D_MODEL + LANES)), _full((1, D_MODEL))]

    tf = FFN_PROMPT_TILE
    fpar = FFN_SEQS_PER_STEP
    assert nb % fpar == 0
    y_p, conv_p = pl.pallas_call(
        functools.partial(_ffn_prompt_kernel, rows=tf, n_par=fpar),
        grid=(nb // fpar, seq // tf),
        in_specs=[pl.BlockSpec((fpar, None, tf, D_MODEL), lambda b, t: (0, b, t, 0)),
                  pl.BlockSpec((fpar, None, tf, PLE_DIM), lambda b, t: (0, b, t, 0))] + ffn_weight_specs,
        out_specs=[pl.BlockSpec((fpar, None, tf, D_MODEL), lambda b, t: (0, b, t, 0)),
                   pl.BlockSpec((fpar, None, CONV_W - 1, 2 * D_FF), lambda b, t: (0, b, 0, 0))],
        out_shape=[jax.ShapeDtypeStruct((fpar, nb // fpar, seq, D_MODEL), F32),
                   jax.ShapeDtypeStruct((fpar, nb // fpar, CONV_W - 1, 2 * D_FF), F32)],
        scratch_shapes=[pltpu.VMEM((fpar, CONV_W - 1, SUBLANES, 2 * D_FF), F32),
                        pltpu.VMEM((fpar, 2, 2, CONV_W, tf + 2 * SUBLANES, FF_BLOCK), F32),
                        pltpu.VMEM((fpar, tf, D_FF), BF16)],
        compiler_params=pltpu.CompilerParams(
            dimension_semantics=("arbitrary", "arbitrary"), vmem_limit_bytes=VMEM_LIMIT_BYTES),
        name="ffn_prompt",
    )(x1_p.reshape(fpar, nb // fpar, seq, D_MODEL),
      p_prompt[0].reshape(fpar, nb // fpar, seq, PLE_DIM), *ffn_weights)
    y_p = y_p.reshape(nb, seq, D_MODEL)
    conv_p = conv_p.reshape(nb, CONV_W - 1, 2 * D_FF)

    y_s, conv_s = pl.pallas_call(
        functools.partial(_ffn_sample_kernel, nseq=ns, seq_len=dseq),
        grid=(1,),
        in_specs=[_whole((ns, dseq, D_MODEL)), _whole((ns, dseq, PLE_DIM)),
                  _whole((ns, CONV_W - 1, 2 * D_FF))] + ffn_weight_specs,
        out_specs=[_whole((ns, dseq, D_MODEL)), _whole((ns, CONV_W - 1, 2 * D_FF))],
        out_shape=[jax.ShapeDtypeStruct((ns, dseq, D_MODEL), F32),
                   jax.ShapeDtypeStruct((ns, CONV_W - 1, 2 * D_FF), F32)],
        scratch_shapes=[pltpu.VMEM((ns * dseq, D_FF), BF16)],
        compiler_params=pltpu.CompilerParams(
            dimension_semantics=("arbitrary",), vmem_limit_bytes=VMEM_LIMIT_BYTES),
        name="ffn_sample",
    )(x1_s, p_sample[0], state_ffn_conv[0], *ffn_weights)

    return (y_p,
            y_s,
            gla_p[None],
            gla_s.reshape(1, ns, GLA_HEADS, GLA_DK, GLA_DV),
            conv_p[None],
            conv_s[None],
            vrows_s[None])
```

```python
import functools

import numpy as np
import jax
import jax.numpy as jnp
from jax import lax
from jax.experimental import pallas as pl
from jax.experimental.pallas import tpu as pltpu

D_MODEL = 1024
GLA_HEADS = 4
GLA_DK = 64
GLA_DV = 128
GLA_KEY = GLA_HEADS * GLA_DK
GLA_WIDTH = GLA_HEADS * GLA_DV
GMLP_HEADS = 4
GMLP_DC = 128
GMLP_WIDTH = GMLP_HEADS * GMLP_DC
GLA_GATE_RANK = 16
GLA_TAU = 16.0
GLA_CHUNK = 64
GMLP_CHUNK = 128
D_FF = 2816
CONV_W = 3
PLE_DIM = 256
EPS = 1e-6

LANES = 128
SUBLANES = 8
MXU_N = 256
HEAD_PAIRS = GLA_HEADS // 2

COL_QK = 0
COL_V = COL_QK + 2 * GLA_KEY
COL_R = COL_V + GLA_WIDTH
IN_COLS_GLA = COL_R + GLA_WIDTH
COL_U = 0
COL_VS = COL_U + GMLP_WIDTH
COL_GLR = COL_VS + GMLP_WIDTH
IN_COLS_SGU = COL_GLR + LANES

FF_BLOCK = 256
N_FF_BLOCKS = D_FF // FF_BLOCK
FF_GROUP = 4

PROMPT_TILE = 256
MIXER_SEQS_PER_STEP = 4
FFN_PROMPT_TILE = 256
FFN_SEQS_PER_STEP = 2
SAMPLE_TILE = 128
CAST_STEPS = 8

VMEM_LIMIT_BYTES = 56 * 1024 * 1024

BF16 = jnp.bfloat16
F32 = jnp.float32


def _dot(a, b):
    return jnp.dot(a, b, preferred_element_type=F32)


def _dot_wide(a, w_ref, rows=slice(None)):
    return jnp.concatenate(
        [_dot(a, w_ref[rows, c:c + MXU_N]) for c in range(0, D_MODEL, MXU_N)], axis=-1)


def _dot_nt(a, b):
    return lax.dot_general(a, b, (((1,), (1,)), ((), ())), preferred_element_type=F32)


def _dot_tn(a, b):
    return lax.dot_general(a, b, (((0,), (0,)), ((), ())), preferred_element_type=F32)


def _rms(x, g):
    return x * lax.rsqrt(jnp.mean(x * x, axis=-1, keepdims=True) + EPS) * g


def _gelu_gate(x):
    c = -2.0 * np.sqrt(2.0 / np.pi) * np.log2(np.e)
    a1 = np.float32(c)
    a3 = np.float32(c * 0.044715)
    return 1.0 / (1.0 + jnp.exp2(x * (a3 * (x * x) + a1)))


def _gelu(x):
    return x * _gelu_gate(x)


def _sigmoid(x):
    return 1.0 / (1.0 + jnp.exp(-x))


def _log_sigmoid(x):
    return jnp.minimum(x, 0.0) - jnp.log1p(jnp.exp(-jnp.abs(x)))


def _split_bf16(x):
    hi = x.astype(BF16)
    lo = (x - hi.astype(F32)).astype(BF16)
    return hi, lo


def _mixer_kernel(*refs, rows, sample, n_par=1):
    if sample:
        (x_ref, sin_ref, g_pre_ref, w_ga_ref, w_sg_ref, w_gate_ref, b_gate_ref, g_gla_ref, g_ln_ref, b_ln_ref,
         w_sp_ref, b_sp_ref, w_out_ref, g_post_ref, cum_ref, tot_ref, causal_ref,
         y_ref, sout_ref, vn_ref, o_s) = refs
        xs, ys, os_ = [x_ref], [y_ref], [o_s]
    else:
        (x_ref, g_pre_ref, w_ga_ref, w_sg_ref, w_gate_ref, b_gate_ref, g_gla_ref, g_ln_ref, b_ln_ref,
         w_sp_ref, b_sp_ref, w_out_ref, g_post_ref, cum_ref, tot_ref, causal_ref,
         y_ref, sout_ref, o_s, st_s) = refs
        t = pl.program_id(1)

        @pl.when(t == 0)
        def _():
            st_s[...] = jnp.zeros_like(st_s)

        xs = [x_ref.at[i] for i in range(n_par)]
        ys = [y_ref.at[i] for i in range(n_par)]
        os_ = [o_s.at[i] for i in range(n_par)]

    lane = lax.broadcasted_iota(jnp.int32, (rows, LANES), 1)
    head_lane_masks = (lane < GLA_DK, lane >= GLA_DK)
    tiles = [dict() for _ in range(n_par)]

    def stage_norm(i):
        d = tiles[i]
        d['x'] = xs[i][...].reshape(rows, D_MODEL)
        d['h'] = _rms(d['x'], g_pre_ref[...]).astype(BF16)

    def stage_project(i):
        d = tiles[i]
        h = d['h']
        glr = _dot(h, w_sg_ref[:, COL_GLR:IN_COLS_SGU])
        d['logit'] = _dot(glr.astype(BF16), w_gate_ref[...]) + b_gate_ref[...]
        d['u_pre'] = _dot(h, w_sg_ref[:, COL_U:COL_VS])
        d['vs_pre'] = _dot(h, w_sg_ref[:, COL_VS:COL_GLR])
        d['qk'] = _dot(h, w_ga_ref[:, COL_QK:COL_V])
        d['v'] = _dot(h, w_ga_ref[:, COL_V:COL_R])
        d['r'] = _dot(h, w_ga_ref[:, COL_R:IN_COLS_GLA])

    def stage_gate_sgu(i):
        d = tiles[i]
        d['g'] = _split_bf16(_log_sigmoid(d.pop('logit')) * (1.0 / GLA_TAU))
        d['u'] = _gelu(d.pop('u_pre'))
        vs = _gelu(d.pop('vs_pre'))
        vn_parts = []
        for j in range(GMLP_HEADS):
            vh = vs[:, j * GMLP_DC:(j + 1) * GMLP_DC]
            mu = jnp.mean(vh, axis=-1, keepdims=True)
            c = vh - mu
            var = jnp.mean(c * c, axis=-1, keepdims=True)
            vn_parts.append(c * lax.rsqrt(var + EPS))
        vn = jnp.concatenate(vn_parts, axis=-1) * g_ln_ref[...] + b_ln_ref[...]
        if sample:
            vn_ref[...] = vn.reshape(vn_ref.shape)
        d['vn_bf'] = vn.astype(BF16)

    def stage_mix_sgu(i):
        d = tiles[i]
        g_hi, g_lo = d.pop('g')
        cum = cum_ref[...]
        tot = tot_ref[...]
        d['b'] = _dot(cum, g_hi) + _dot(cum, g_lo)
        d['bl'] = _dot(tot, g_hi) + _dot(tot, g_lo)
        vn_bf = d.pop('vn_bf')
        mixed_parts = []
        for cch in range(rows // GMLP_CHUNK):
            crows = slice(cch * GMLP_CHUNK, (cch + 1) * GMLP_CHUNK)
            mixed_parts.append(jnp.concatenate(
                [_dot(w_sp_ref[j], vn_bf[crows, j * GMLP_DC:(j + 1) * GMLP_DC]) for j in range(GMLP_HEADS)],
                axis=-1) + b_sp_ref[...])
        so = d.pop('u') * jnp.concatenate(mixed_parts, axis=0)
        d['mix'] = _dot_wide(so.astype(BF16), w_out_ref, slice(GLA_WIDTH, GLA_WIDTH + GMLP_WIDTH))

    def stage_decay(i):
        d = tiles[i]
        qk = d.pop('qk')
        q = qk[:, :GLA_KEY] * (GLA_DK ** -0.5)
        k = qk[:, GLA_KEY:]
        b, bl = d.pop('b'), d['bl']
        d['v_bf'] = d['v'].astype(BF16)
        d['qb'] = q * jnp.exp(b)
        d['kd'] = k * jnp.exp(bl - b)
        if sample:
            d['qs'] = d['qb']
            d['kn'] = k * jnp.exp(-b)
        else:
            mid = jnp.concatenate(
                [jnp.broadcast_to(b[c * GLA_CHUNK + GLA_CHUNK // 2 - 1:c * GLA_CHUNK + GLA_CHUNK // 2, :],
                                  (GLA_CHUNK, GLA_KEY)) for c in range(rows // GLA_CHUNK)], axis=0)
            d['qs'] = q * jnp.exp(b - mid)
            d['kn'] = k * jnp.exp(mid - b)

    def stage_intra(i):
        d = tiles[i]
        qb, kn, kd, v, v_bf, bl = d.pop('qb'), d.pop('kn'), d.pop('kd'), d.pop('v'), d['v_bf'], d['bl']
        qs = d.pop('qs')
        causal = causal_ref[...]
        d['pairs'] = []
        for p in range(HEAD_PAIRS):
            ls = slice(p * LANES, (p + 1) * LANES)
            qbp = qb[:, ls]
            qsp = qs[:, ls]
            knp = kn[:, ls].astype(BF16)
            kdp = kd[:, ls]
            o_intra = []
            for hh in range(2):
                hd = 2 * p + hh
                qm = jnp.where(head_lane_masks[hh], qsp, 0.0).astype(BF16)
                s = _dot_nt(qm, knp) * causal
                o_intra.append(_dot(s.astype(BF16), v_bf[:, hd * GLA_DV:(hd + 1) * GLA_DV]))

            if not sample:
                st_shape = (2 * GLA_DV, 2 * GLA_DK)
                ri = lax.broadcasted_iota(jnp.int32, st_shape, 0)
                ci = lax.broadcasted_iota(jnp.int32, st_shape, 1)
                same_head = (ri < GLA_DV) == (ci < GLA_DK)
                kdp_bf = kdp.astype(BF16)
                upd = []
                for c in range(rows // GLA_CHUNK):
                    rws = slice(c * GLA_CHUNK, (c + 1) * GLA_CHUNK)
                    inc = _dot_tn(v_bf[rws, 2 * p * GLA_DV:(2 * p + 2) * GLA_DV], kdp_bf[rws, :])
                    upd.append(jnp.where(same_head, inc, 0.0))
                d['pairs'].append((ls, qbp.astype(BF16), o_intra, upd))
            else:
                seq_len = 4
                seqs_per_grp = SUBLANES // seq_len
                kdp_t = kdp.T.astype(BF16)
                dec_t = jnp.exp(bl[:, ls]).T
                sub = lax.broadcasted_iota(jnp.int32, (SUBLANES, LANES), 0)
                crow = lax.broadcasted_iota(jnp.int32, (rows, LANES), 0)
                for grp in range(rows // SUBLANES):
                    g0 = grp * SUBLANES
                    qg = qbp[g0:g0 + SUBLANES, :]
                    o_acc = [o_intra[hh][g0:g0 + SUBLANES, :] for hh in range(2)]
                    for j in range(seqs_per_grp):
                        sidx = grp * seqs_per_grp + j
                        s_pair = sin_ref[sidx, p]
                        s_bf = s_pair.astype(BF16)
                        in_seq = (sub >= j * seq_len) & (sub < (j + 1) * seq_len)
                        in_seq_c = (crow >= g0 + j * seq_len) & (crow < g0 + (j + 1) * seq_len)
                        new_rows = []
                        for hh in range(2):
                            hd = 2 * p + hh
                            qsel = jnp.where(in_seq & head_lane_masks[hh][:SUBLANES], qg, 0.0).astype(BF16)
                            o_acc[hh] = o_acc[hh] + _dot(qsel, s_bf)
                            vsel = jnp.where(in_seq_c, v[:, hd * GLA_DV:(hd + 1) * GLA_DV], 0.0).astype(BF16)
                            new_rows.append(_dot(kdp_t[hh * GLA_DK:(hh + 1) * GLA_DK, :], vsel))
                        col = g0 + j * seq_len
                        sout_ref[sidx, p] = s_pair * dec_t[:, col:col + 1] + jnp.concatenate(new_rows, axis=0)
                    for hh in range(2):
                        hd = 2 * p + hh
                        os_[i][g0:g0 + SUBLANES, hd * GLA_DV:(hd + 1) * GLA_DV] = o_acc[hh]

    def stage_state(i):
        if sample:
            return
        d = tiles[i]
        bl = d.pop('bl')
        st = [st_s[i, p] for p in range(HEAD_PAIRS)]
        for c in range(rows // GLA_CHUNK):
            rws = slice(c * GLA_CHUNK, (c + 1) * GLA_CHUNK)
            for p, (ls, qbp_bf, o_intra, upd) in enumerate(d['pairs']):
                o_inter = _dot_nt(qbp_bf[rws, :], st[p].astype(BF16))
                for hh in range(2):
                    hd = 2 * p + hh
                    os_[i][rws, hd * GLA_DV:(hd + 1) * GLA_DV] = (
                        o_intra[hh][rws, :] + o_inter[:, hh * GLA_DV:(hh + 1) * GLA_DV])
                dec = jnp.exp(bl[c * GLA_CHUNK:c * GLA_CHUNK + 1, ls])
                st[p] = st[p] * dec + upd[c]
        for p in range(HEAD_PAIRS):
            st_s[i, p] = st[p]
        d.pop('pairs')

    def stage_out(i):
        d = tiles[i]
        r = d.pop('r')
        out_gate = r * _sigmoid(r)
        mix, x = d.pop('mix'), d.pop('x')
        half = rows // 2
        for hr in (slice(0, half), slice(half, rows)):
            o = os_[i][hr, :]
            og = jnp.concatenate(
                [_rms(o[:, j * GLA_DV:(j + 1) * GLA_DV], 1.0) for j in range(GLA_HEADS)], axis=-1)
            og = og * g_gla_ref[...] * out_gate[hr, :]
            mix_h = mix[hr, :] + _dot_wide(og.astype(BF16), w_out_ref, slice(0, GLA_WIDTH))
            y_h = x[hr, :] + _rms(mix_h, g_post_ref[...])
            if sample:
                seq_len = ys[i].shape[1]
                ys[i][hr.start // seq_len:hr.stop // seq_len] = y_h.reshape(-1, seq_len, D_MODEL)
            else:
                ys[i][hr, :] = y_h

    stages = (stage_norm, stage_project, stage_gate_sgu, stage_mix_sgu, stage_decay, stage_intra,
              stage_state, stage_out)
    for stage in stages:
        for i in range(n_par):
            stage(i)

    if not sample:
        @pl.when(t == pl.num_programs(1) - 1)
        def _():
            for i in range(n_par):
                for p in range(HEAD_PAIRS):
                    st = st_s[i, p]
                    sout_ref[i, 2 * p] = st[:GLA_DV, :].T[:GLA_DK, :]
                    sout_ref[i, 2 * p + 1] = st[GLA_DV:, :].T[GLA_DK:, :]


def _ffn_tail(x1, f, p, g_ffn_post, g_ple_in, w_pg_ref, w_ple_ref, g_ple_post):
    x2 = x1 + _rms(f, g_ffn_post)
    hp = _rms(x2, g_ple_in).astype(BF16)
    gate = _sigmoid(_dot_wide(hp, w_pg_ref))
    pe = _dot_wide(p.astype(BF16), w_ple_ref)
    return x2 + _rms(pe * gate, g_ple_post)


def _ff_groups():
    return [range(g, min(g + FF_GROUP, N_FF_BLOCKS)) for g in range(0, N_FF_BLOCKS, FF_GROUP)]


def _ffn_prompt_kernel(x_ref, p_ref, g_pre_ref, w_up_ref, w_conv_ref, b_conv_ref, w_down_ref,
                       g_post_ref, g_ple_in_ref, w_pg_ref, w_ple_ref, g_ple_post_ref,
                       y_ref, cs_ref,
                       carry_s, up_s, act_s, *, rows, n_par):
    t = pl.program_id(1)
    hdr = SUBLANES

    @pl.when(t == 0)
    def _():
        carry_s[...] = jnp.zeros_like(carry_s)
        up_s[:, :, :, :, hdr + rows:, :] = jnp.zeros((n_par, 2, 2, CONV_W, hdr, FF_BLOCK), F32)

    tiles = [dict() for _ in range(n_par)]

    def half_cols(j, half):
        c0 = half * D_FF + j * FF_BLOCK
        return slice(c0, c0 + FF_BLOCK)

    def project(i, j):
        for half in range(2):
            cols = half_cols(j, half)
            up = _dot(tiles[i]['h'], w_up_ref[:, cols])
            for s in range(CONV_W):
                buf = up_s.at[i, j % 2, half, s]
                if s:
                    buf[hdr:2 * hdr, :] = carry_s[i, s - 1, :, cols]
                buf[hdr + s:hdr + s + rows, :] = up
                if s:
                    carry_s[i, s - 1, :, cols] = buf[hdr + rows:2 * hdr + rows, :]

    def conv(i, j, half):
        cols = half_cols(j, half)
        w = w_conv_ref[:, cols]
        taps = [up_s[i, j % 2, half, s, hdr:hdr + rows, :] for s in range(CONV_W)]
        return (b_conv_ref[:, cols] + taps[2] * w[0:1, :] + taps[1] * w[1:2, :] + taps[0] * w[2:3, :])

    def down(i, j):
        k_rows = slice(j * FF_BLOCK, (j + 1) * FF_BLOCK)
        return _dot_wide(act_s[i, :, k_rows], w_down_ref, k_rows)

    def stage_head(i):
        d = tiles[i]
        d['x1'] = x_ref[i]
        d['h'] = _rms(d['x1'], g_pre_ref[...]).astype(BF16)
        project(i, 0)

    def stage_block(i, j):
        d = tiles[i]
        if j + 1 < N_FF_BLOCKS:
            project(i, j + 1)
        if j == 1:
            d['f'] = down(i, 0)
        elif j > 1:
            d['f'] = d['f'] + down(i, j - 1)
        gate = conv(i, j, 0)
        act_s[i, :, j * FF_BLOCK:(j + 1) * FF_BLOCK] = (
            (gate * conv(i, j, 1)) * _gelu_gate(gate)).astype(BF16)

    def stage_tail(i):
        d = tiles[i]
        f = d.pop('f') + down(i, N_FF_BLOCKS - 1)
        d.pop('h')
        y_ref[i] = _ffn_tail(d.pop('x1'), f, p_ref[i], g_post_ref[...], g_ple_in_ref[...], w_pg_ref,
                             w_ple_ref, g_ple_post_ref[...])

    stages = ([stage_head] + [functools.partial(stage_block, j=j) for j in range(N_FF_BLOCKS)]
              + [stage_tail])
    for stage in stages:
        for i in range(n_par):
            stage(i)

    @pl.when(t == pl.num_programs(1) - 1)
    def _():
        for i in range(n_par):
            cs_ref[i] = carry_s[i, CONV_W - 2, 0:CONV_W - 1, :]


def _ffn_sample_kernel(x_ref, p_ref, cin_ref, g_pre_ref, w_up_ref, w_conv_ref, b_conv_ref, w_down_ref,
                       g_post_ref, g_ple_in_ref, w_pg_ref, w_ple_ref, g_ple_post_ref,
                       y_ref, cs_ref, act_s, *, nseq, seq_len):
    x1 = jnp.concatenate([x_ref[:, j, :] for j in range(seq_len)], axis=0)
    p = jnp.concatenate([p_ref[:, j, :] for j in range(seq_len)], axis=0)
    h = _rms(x1, g_pre_ref[...]).astype(BF16)

    def conv_half(col0):
        cols = slice(col0, col0 + FF_BLOCK)
        up = _dot(h, w_up_ref[:, cols])
        blocks = [cin_ref[:, i, cols] for i in range(CONV_W - 1)]
        blocks += [up[j * nseq:(j + 1) * nseq, :] for j in range(seq_len)]
        w = w_conv_ref[:, cols]
        b = b_conv_ref[:, cols]
        conv = jnp.concatenate(
            [b + blocks[j] * w[0:1, :] + blocks[j + 1] * w[1:2, :] + blocks[j + 2] * w[2:3, :]
             for j in range(seq_len)], axis=0)
        for i in range(CONV_W - 1):
            cs_ref[:, i, cols] = blocks[seq_len + i]
        return conv

    f = None
    for grp in _ff_groups():
        for j in grp:
            gate = conv_half(j * FF_BLOCK)
            val = conv_half(D_FF + j * FF_BLOCK)
            act_s[:, j * FF_BLOCK:(j + 1) * FF_BLOCK] = ((gate * val) * _gelu_gate(gate)).astype(BF16)
        k_rows = slice(grp[0] * FF_BLOCK, (grp[-1] + 1) * FF_BLOCK)
        part = _dot_wide(act_s[:, k_rows], w_down_ref, k_rows)
        f = part if f is None else f + part

    y = _ffn_tail(x1, f, p, g_post_ref[...], g_ple_in_ref[...], w_pg_ref, w_ple_ref,
                  g_ple_post_ref[...])
    for j in range(seq_len):
        y_ref[:, j, :] = y[j * nseq:(j + 1) * nseq, :]


def _cast_kernel(*refs):
    n = (len(refs) - 3) // 2
    w_in_t_ref, srcs = refs[0], refs[1:1 + n]
    ga_ref, sg_ref, dsts = refs[1 + n], refs[2 + n], refs[3 + n:]

    glr0 = IN_COLS_GLA
    u0 = glr0 + GLA_GATE_RANK
    ga_ref[...] = w_in_t_ref[:glr0, :].T.astype(BF16)
    sg_ref[:, :COL_GLR] = w_in_t_ref[u0:u0 + COL_GLR, :].T.astype(BF16)
    tile = w_in_t_ref[glr0:glr0 + LANES, :].T
    lane = lax.broadcasted_iota(jnp.int32, tile.shape, 1)
    sg_ref[:, COL_GLR:] = jnp.where(lane < GLA_GATE_RANK, tile, 0.0).astype(BF16)

    for src, dst in zip(srcs, dsts):
        cols = src.shape[1]
        dst[:, :cols] = src[...].astype(BF16)
        if dst.shape[1] > cols:
            dst[:, cols:] = jnp.zeros((dst.shape[0], dst.shape[1] - cols), BF16)


def _whole(shape):
    n = len(shape)
    return pl.BlockSpec(shape, lambda *_: (0,) * n)


def _full(shape):
    n = len(shape)
    return pl.BlockSpec(shape, lambda *_: (0,) * n, pipeline_mode=pl.Buffered(1))


def _gla_masks(rows, span):
    i = np.arange(rows)
    same = (i[:, None] // span) == (i[None, :] // span)
    causal = same & (i[None, :] <= i[:, None])
    return (jnp.asarray(causal, BF16), jnp.asarray(same, BF16), jnp.asarray(causal, F32))


def kernel(x_prompt, x_sample, state_gla, state_ffn_conv, p_prompt, p_sample, g_mix_pre, w_in, w_gla_gate, b_gla_gate, g_gla_out, g_sgu_ln, b_sgu_ln, w_spatial, b_spatial, w_out, g_mix_post, g_ffn_pre, w_up, w_conv, b_conv, w_down, g_ffn_post, g_ple_in, w_ple_gate, w_ple, g_ple_post):
    depth = x_prompt.ndim - 2
    assert w_in.shape[0] == 1 and depth == 1
    nb, seq, _ = x_prompt.shape
    ns, dseq, _ = x_sample.shape
    assert seq % PROMPT_TILE == 0 and PROMPT_TILE % GMLP_CHUNK == 0 and seq % FFN_PROMPT_TILE == 0
    assert dseq == 4 and (ns * dseq) % SAMPLE_TILE == 0 and SAMPLE_TILE % GMLP_CHUNK == 0

    w_gate_p = jnp.concatenate(
        [w_gla_gate[0], jnp.zeros((LANES - GLA_GATE_RANK, GLA_KEY), w_gla_gate.dtype)], axis=0).astype(BF16)
    row = lambda a: a.reshape(1, -1)
    g_gla_t = jnp.tile(g_gla_out[0], GLA_HEADS).reshape(1, -1)
    n_cast = CAST_STEPS
    in_cols = w_in.shape[-1]
    assert in_cols == IN_COLS_GLA + GLA_GATE_RANK + COL_GLR
    cast_src = (w_out[0], w_up[0], w_down[0], w_ple_gate[0], w_ple[0])
    cast_cols_in = (D_MODEL, 2 * D_FF, D_MODEL, D_MODEL, D_MODEL)
    cast_cols_out = (D_MODEL + LANES, 2 * D_FF, D_MODEL + LANES, D_MODEL + LANES, D_MODEL + LANES)
    cast_rows = tuple(w.shape[0] for w in cast_src)
    assert all(r % (n_cast * 2 * SUBLANES) == 0 for r in cast_rows + (D_MODEL,))
    slab = lambda r, c: pl.BlockSpec((r // n_cast, c), lambda i: (i, 0))
    w_ga, w_sg, w_out_b, w_up_b, w_down_b, w_pg_b, w_ple_b = pl.pallas_call(
        _cast_kernel,
        grid=(n_cast,),
        in_specs=[pl.BlockSpec((None, in_cols, D_MODEL // n_cast), lambda i: (0, 0, i))]
        + [slab(r, c) for r, c in zip(cast_rows, cast_cols_in)],
        out_specs=[slab(D_MODEL, IN_COLS_GLA), slab(D_MODEL, IN_COLS_SGU)]
        + [slab(r, c) for r, c in zip(cast_rows, cast_cols_out)],
        out_shape=[jax.ShapeDtypeStruct((D_MODEL, IN_COLS_GLA), BF16),
                   jax.ShapeDtypeStruct((D_MODEL, IN_COLS_SGU), BF16)]
        + [jax.ShapeDtypeStruct((r, c), BF16) for r, c in zip(cast_rows, cast_cols_out)],
        compiler_params=pltpu.CompilerParams(
            dimension_semantics=("arbitrary",), vmem_limit_bytes=VMEM_LIMIT_BYTES),
        name="weight_cast",
    )(jnp.swapaxes(w_in, 1, 2), *cast_src)

    def spatial(c):
        i = np.arange(GMLP_CHUNK)
        keep = ((i[:, None] // c) == (i[None, :] // c)) & (i[None, :] <= i[:, None])
        if c == GMLP_CHUNK:
            wbd = w_spatial[0]
            bias = jnp.transpose(b_spatial[0])
        else:
            e = jnp.asarray((i[:, None] % c) == np.arange(c)[None, :], w_spatial.dtype)
            hi = lax.Precision.HIGHEST
            wbd = jnp.einsum('ia,hab,jb->hij', e, w_spatial[0][:, :c, :c], e, precision=hi)
            bias = jnp.einsum('ia,ha->ih', e, b_spatial[0][:, :c], precision=hi)
        wbd = wbd * jnp.asarray(keep, w_spatial.dtype)
        return wbd.astype(BF16), jnp.repeat(bias, GMLP_DC, axis=1)

    mixer_weights = lambda wsp, bsp, masks: (
        row(g_mix_pre), w_ga, w_sg, w_gate_p, row(b_gla_gate), g_gla_t, row(g_sgu_ln), row(b_sgu_ln),
        wsp, bsp, w_out_b, row(g_mix_post)) + masks

    def mixer_weight_specs(rows):
        return [
            _full((1, D_MODEL)), _full((D_MODEL, IN_COLS_GLA)), _full((D_MODEL, IN_COLS_SGU)),
            _full((LANES, GLA_KEY)),
            _full((1, GLA_KEY)), _full((1, GLA_WIDTH)), _full((1, GMLP_WIDTH)), _full((1, GMLP_WIDTH)),
            _full((GMLP_HEADS, GMLP_CHUNK, GMLP_CHUNK)), _full((GMLP_CHUNK, GMLP_WIDTH)),
            _full((D_MODEL, D_MODEL + LANES)), _full((1, D_MODEL)),
            _full((rows, rows)), _full((rows, rows)), _full((rows, rows))]

    tl = PROMPT_TILE
    wsp_p, bsp_p = spatial(GMLP_CHUNK)
    npar = MIXER_SEQS_PER_STEP
    assert nb % npar == 0
    x1_p, gla_p = pl.pallas_call(
        functools.partial(_mixer_kernel, rows=tl, sample=False, n_par=npar),
        grid=(nb // npar, seq // tl),
        in_specs=[pl.BlockSpec((npar, None, tl, D_MODEL), lambda b, t: (0, b, t, 0))] + mixer_weight_specs(tl),
        out_specs=[pl.BlockSpec((npar, None, tl, D_MODEL), lambda b, t: (0, b, t, 0)),
                   pl.BlockSpec((npar, None, GLA_HEADS, GLA_DK, GLA_DV), lambda b, t: (0, b, 0, 0, 0))],
        out_shape=[jax.ShapeDtypeStruct((npar, nb // npar, seq, D_MODEL), F32),
                   jax.ShapeDtypeStruct((npar, nb // npar, GLA_HEADS, GLA_DK, GLA_DV), F32)],
        scratch_shapes=[pltpu.VMEM((npar, tl, GLA_WIDTH), F32),
                        pltpu.VMEM((npar, HEAD_PAIRS, 2 * GLA_DV, 2 * GLA_DK), F32)],
        compiler_params=pltpu.CompilerParams(
            dimension_semantics=("arbitrary", "arbitrary"), vmem_limit_bytes=VMEM_LIMIT_BYTES),
        name="mixer_prompt",
    )(x_prompt.reshape(npar, nb // npar, seq, D_MODEL),
      *mixer_weights(wsp_p, bsp_p, _gla_masks(tl, GLA_CHUNK)))
    x1_p = x1_p.reshape(nb, seq, D_MODEL)
    gla_p = gla_p.reshape(nb, GLA_HEADS, GLA_DK, GLA_DV)

    ts = SAMPLE_TILE
    seqs_per_tile = ts // dseq
    wsp_s, bsp_s = spatial(dseq)
    x1_s, gla_s, vrows_s = pl.pallas_call(
        functools.partial(_mixer_kernel, rows=ts, sample=True),
        grid=(ns * dseq // ts,),
        in_specs=[pl.BlockSpec((seqs_per_tile, dseq, D_MODEL), lambda i: (i, 0, 0)),
                  pl.BlockSpec((seqs_per_tile, HEAD_PAIRS, 2 * GLA_DK, GLA_DV), lambda i: (i, 0, 0, 0))]
        + mixer_weight_specs(ts),
        out_specs=[pl.BlockSpec((seqs_per_tile, dseq, D_MODEL), lambda i: (i, 0, 0)),
                   pl.BlockSpec((seqs_per_tile, HEAD_PAIRS, 2 * GLA_DK, GLA_DV), lambda i: (i, 0, 0, 0)),
                   pl.BlockSpec((seqs_per_tile, dseq, GMLP_WIDTH), lambda i: (i, 0, 0))],
        out_shape=[jax.ShapeDtypeStruct((ns, dseq, D_MODEL), F32),
                   jax.ShapeDtypeStruct((ns, HEAD_PAIRS, 2 * GLA_DK, GLA_DV), F32),
                   jax.ShapeDtypeStruct((ns, dseq, GMLP_WIDTH), F32)],
        scratch_shapes=[pltpu.VMEM((ts, GLA_WIDTH), F32)],
        compiler_params=pltpu.CompilerParams(
            dimension_semantics=("arbitrary",), vmem_limit_bytes=VMEM_LIMIT_BYTES),
        name="mixer_sample",
    )(x_sample,
      state_gla[0].reshape(ns, HEAD_PAIRS, 2 * GLA_DK, GLA_DV),
      *mixer_weights(wsp_s, bsp_s, _gla_masks(ts, dseq)))

    ffn_weights = (row(g_ffn_pre), w_up_b, w_conv[0], row(b_conv), w_down_b, row(g_ffn_post),
                   row(g_ple_in), w_pg_b, w_ple_b, row(g_ple_post))
    ffn_weight_specs = [
        _full((1, D_MODEL)), _full((D_MODEL, 2 * D_FF)), _full((CONV_W, 2 * D_FF)), _full((1, 2 * D_FF)),
        _full((D_FF, D_MODEL + LANES)), _full((1, D_MODEL)), _full((1, D_MODEL)),
        _full((D_MODEL, D_MODEL + LANES)), _full((PLE_DIM, D_MODEL + LANES)), _full((1, D_MODEL))]

    tf = FFN_PROMPT_TILE
    fpar = FFN_SEQS_PER_STEP
    assert nb % fpar == 0
    y_p, conv_p = pl.pallas_call(
        functools.partial(_ffn_prompt_kernel, rows=tf, n_par=fpar),
        grid=(nb // fpar, seq // tf),
        in_specs=[pl.BlockSpec((fpar, None, tf, D_MODEL), lambda b, t: (0, b, t, 0)),
                  pl.BlockSpec((fpar, None, tf, PLE_DIM), lambda b, t: (0, b, t, 0))] + ffn_weight_specs,
        out_specs=[pl.BlockSpec((fpar, None, tf, D_MODEL), lambda b, t: (0, b, t, 0)),
                   pl.BlockSpec((fpar, None, CONV_W - 1, 2 * D_FF), lambda b, t: (0, b, 0, 0))],
        out_shape=[jax.ShapeDtypeStruct((fpar, nb // fpar, seq, D_MODEL), F32),
                   jax.ShapeDtypeStruct((fpar, nb // fpar, CONV_W - 1, 2 * D_FF), F32)],
        scratch_shapes=[pltpu.VMEM((fpar, CONV_W - 1, SUBLANES, 2 * D_FF), F32),
                        pltpu.VMEM((fpar, 2, 2, CONV_W, tf + 2 * SUBLANES, FF_BLOCK), F32),
                        pltpu.VMEM((fpar, tf, D_FF), BF16)],
        compiler_params=pltpu.CompilerParams(
            dimension_semantics=("arbitrary", "arbitrary"), vmem_limit_bytes=VMEM_LIMIT_BYTES),
        name="ffn_prompt",
    )(x1_p.reshape(fpar, nb // fpar, seq, D_MODEL),
      p_prompt[0].reshape(fpar, nb // fpar, seq, PLE_DIM), *ffn_weights)
    y_p = y_p.reshape(nb, seq, D_MODEL)
    conv_p = conv_p.reshape(nb, CONV_W - 1, 2 * D_FF)

    y_s, conv_s = pl.pallas_call(
        functools.partial(_ffn_sample_kernel, nseq=ns, seq_len=dseq),
        grid=(1,),
        in_specs=[_whole((ns, dseq, D_MODEL)), _whole((ns, dseq, PLE_DIM)),
                  _whole((ns, CONV_W - 1, 2 * D_FF))] + ffn_weight_specs,
        out_specs=[_whole((ns, dseq, D_MODEL)), _whole((ns, CONV_W - 1, 2 * D_FF))],
        out_shape=[jax.ShapeDtypeStruct((ns, dseq, D_MODEL), F32),
                   jax.ShapeDtypeStruct((ns, CONV_W - 1, 2 * D_FF), F32)],
        scratch_shapes=[pltpu.VMEM((ns * dseq, D_FF), BF16)],
        compiler_params=pltpu.CompilerParams(
            dimension_semantics=("arbitrary",), vmem_limit_bytes=VMEM_LIMIT_BYTES),
        name="ffn_sample",
    )(x1_s, p_sample[0], state_ffn_conv[0], *ffn_weights)

    return (y_p,
            y_s,
            gla_p[None],
            gla_s.reshape(1, ns, GLA_HEADS, GLA_DK, GLA_DV),
            conv_p[None],
            conv_s[None],
            vrows_s[None])
```
